```python
import math
import numpy as np
import jax
import jax.numpy as jnp
from jax import lax

D_MODEL = 2048
BATCH = 8
SEQ = 2048
DEPTH = 2

CTX_LEN = 256
GRID_W = 64

NA_HEADS = 8
NA_HEAD_DIM = 64
NA_WIN_H = 8
NA_WIN_W = 16
NA_WIDTH = NA_HEADS * NA_HEAD_DIM
DA_HEADS = 4
DA_HEAD_DIM = 64
DA_WIDTH = DA_HEADS * 2 * DA_HEAD_DIM
DA_Q_BLOCK = 128
ROPE_BASE = 10000.0
SSM_D_INNER = 1024
SSM_HEAD_DIM = 64
SSM_HEADS = SSM_D_INNER // SSM_HEAD_DIM
SSM_GROUPS = 2
SSM_STATE = 128
SSM_CHUNK = 128
SSM_CONV_W = 5
SSM_CONV_DIM = SSM_D_INNER + 2 * SSM_GROUPS * SSM_STATE
FFN_DIM = 5632
N_EXPERTS = 8
TOP_K = 2
EXPERT_DIM = 5632
MOE_BLOCK = 256
IN_SIZES = (NA_WIDTH, NA_WIDTH, NA_WIDTH, DA_WIDTH, DA_WIDTH, DA_WIDTH,
            SSM_D_INNER, SSM_CONV_DIM, 2 * SSM_HEADS, 3 * D_MODEL)
IN_WIDTH = sum(IN_SIZES)
EPS = 1e-6
NEG_INF = -1e30
F32 = jnp.float32

kernel_name = 'hybrid_diffusion_na_diff_ssd_moe'


def rms_norm(x, g):
    xf = x.astype(F32)
    y = xf * lax.rsqrt(jnp.mean(xf * xf, axis=-1, keepdims=True) + EPS)
    return (y * g.astype(F32)).astype(x.dtype)


def modulate(h, shift, scale):
    return h * (1.0 + scale) + shift


def project_in(u, w_in):
    split_at = np.cumsum(IN_SIZES)[:-1].tolist()
    return jnp.split(u @ w_in, split_at, axis=-1)


def axial_rope_tables(L, dim):
    t = jnp.arange(L)
    pos = jnp.stack([t // GRID_W, t % GRID_W], axis=-1).astype(F32)
    n_freq = dim // 4
    inv_freq = ROPE_BASE ** (-jnp.arange(n_freq, dtype=F32) / n_freq)
    ang = pos[:, :, None] * inv_freq
    return jnp.cos(ang), jnp.sin(ang)


def apply_axial_rope(x, cos, sin):
    shp = x.shape
    xr = x.astype(F32).reshape(shp[:-1] + (2, 2, shp[-1] // 4))
    x1, x2 = xr[..., 0, :], xr[..., 1, :]
    cb = cos[None, :, None, None]
    sb = sin[None, :, None, None]
    out = jnp.stack([x1 * cb - x2 * sb, x2 * cb + x1 * sb], axis=-2)
    return out.reshape(shp).astype(x.dtype)


def dense_attn(q, k, v):
    s = jnp.einsum('bqhd,bkhd->bhqk', q, k).astype(F32) * (q.shape[-1] ** -0.5)
    p = jax.nn.softmax(s, axis=-1).astype(v.dtype)
    return jnp.einsum('bhqk,bkhd->bqhd', p, v)


def neighbourhood_attn_latent(q, k, v, k_ctx, v_ctx, rpb):
    Bn, L, H, dh = q.shape
    rows = L // GRID_W
    wh = min(NA_WIN_H, rows)
    ww = NA_WIN_W
    ncb = GRID_W // ww
    band = 2 * ww
    qcol = np.arange(GRID_W).reshape(ncb, ww)
    band_start = np.clip(np.arange(ncb) * ww - ww // 2, 0, GRID_W - band)
    kcol = band_start[:, None] + np.arange(band)
    win_start = np.clip(qcol - ww // 2, 0, GRID_W - ww)
    col_ok = (kcol[:, None, :] >= win_start[..., None]) & (kcol[:, None, :] < win_start[..., None] + ww)
    dcol_idx = np.clip(kcol[:, None, :] - qcol[:, :, None] + NA_WIN_W - 1, 0, 2 * NA_WIN_W - 2)
    qg = (q * (dh ** -0.5)).reshape(Bn, rows, ncb, ww, H, dh)
    kg = k.reshape(Bn, rows, GRID_W, H, dh)
    vg = v.reshape(Bn, rows, GRID_W, H, dh)
    n_lat = wh * band

    def one_row(r):
        rs = jnp.clip(r - wh // 2, 0, rows - wh)
        k_band = lax.dynamic_slice_in_dim(kg, rs, wh, axis=1)[:, :, kcol]
        v_band = lax.dynamic_slice_in_dim(vg, rs, wh, axis=1)[:, :, kcol]
        q_r = lax.dynamic_index_in_dim(qg, r, axis=1, keepdims=False)
        s_lat = jnp.einsum('bjqhd,bijkhd->bhjqik', q_r, k_band).astype(F32)
        drow = rs + jnp.arange(wh) - r + (NA_WIN_H - 1)
        bias = rpb[:, drow][:, :, dcol_idx].transpose(0, 2, 3, 1, 4)
        s_lat = jnp.where(col_ok[None, None, :, :, None, :], s_lat + bias[None].astype(F32), NEG_INF)
        s_ctx = jnp.einsum('bjqhd,bchd->bhjqc', q_r, k_ctx).astype(F32)
        p = jax.nn.softmax(jnp.concatenate([s_lat.reshape(Bn, H, ncb, ww, n_lat), s_ctx], axis=-1), axis=-1)
        p = p.astype(v.dtype)
        p_lat = p[..., :n_lat].reshape(Bn, H, ncb, ww, wh, band)
        o = (jnp.einsum('bhjqik,bijkhd->bjqhd', p_lat, v_band)
             + jnp.einsum('bhjqc,bchd->bjqhd', p[..., n_lat:], v_ctx))
        return o.reshape(Bn, GRID_W, H, dh)

    out = lax.map(one_row, jnp.arange(rows))
    return out.transpose(1, 0, 2, 3, 4).reshape(Bn, L, H * dh)


def diff_attn(q, k, v, lam):
    s = jnp.einsum('bqhmd,bkhmd->bhmqk', q, k).astype(F32) * (q.shape[-1] ** -0.5)
    p = jax.nn.softmax(s, axis=-1)
    a = (p[:, :, 0] - lam * p[:, :, 1]).astype(v.dtype)
    return jnp.einsum('bhqk,bkhe->bqhe', a, v)


def diff_attn_latent(q, k_all, v_all, lam):
    Bn, L, H, M, dh = q.shape
    nb = L // DA_Q_BLOCK
    qb = q.reshape(Bn, nb, DA_Q_BLOCK, H, M, dh).swapaxes(0, 1)
    ob = lax.map(lambda qq: diff_attn(qq, k_all, v_all, lam), qb)
    return ob.swapaxes(0, 1).reshape(Bn, L, H, -1)


def centred_dwconv(u, w, b):
    k = w.shape[0]
    out = lax.conv_general_dilated(u, w[:, None, :].astype(u.dtype), window_strides=(1,),
                                   padding=[(k // 2, k // 2)],
                                   dimension_numbers=('NWC', 'WIO', 'NWC'),
                                   feature_group_count=u.shape[-1])
    return out + b


def segsum(a):
    T = a.shape[-1]
    cs = jnp.cumsum(a, axis=-1)
    seg = cs[..., :, None] - cs[..., None, :]
    return jnp.where(jnp.tril(jnp.ones((T, T), dtype=bool)), seg, -jnp.inf)


def ssd_chunked(x, a, b, c, init_state):
    Bn, L, H, P = x.shape
    N = b.shape[-1]
    nc = L // SSM_CHUNK
    x = x.reshape(Bn, nc, SSM_CHUNK, H, P)
    b = b.reshape(Bn, nc, SSM_CHUNK, H, N)
    c = c.reshape(Bn, nc, SSM_CHUNK, H, N)
    a = a.reshape(Bn, nc, SSM_CHUNK, H).transpose(0, 3, 1, 2)
    a_cum = jnp.cumsum(a, axis=-1)
    cb = jnp.einsum('bzlhn,bzshn->bhzls', c, b) * jnp.exp(segsum(a))
    y_diag = jnp.einsum('bhzls,bzshp->bzlhp', cb, x)
    decay_to_end = jnp.exp(a_cum[..., -1:] - a_cum)
    chunk_states = jnp.einsum('bzlhn,bhzl,bzlhp->bzhpn', b, decay_to_end, x)
    states = jnp.concatenate([init_state[:, None], chunk_states], axis=1)
    chunk_decay = jnp.exp(segsum(jnp.pad(a_cum[..., -1], ((0, 0), (0, 0), (1, 0)))))
    states = jnp.einsum('bhzy,byhpn->bzhpn', chunk_decay, states)
    y_off = jnp.einsum('bzlhn,bzhpn,bhzl->bzlhp', c, states[:, :-1], jnp.exp(a_cum))
    return (y_diag + y_off).reshape(Bn, L, H, P), states[:, -1]


def ssm_scan_bidir(xbc, dt_raw, conv_w, conv_b, dt_bias, a_log, d_skip, init_fwd, init_bwd):
    Bn, L, _ = xbc.shape
    u = jax.nn.silu(centred_dwconv(xbc, conv_w, conv_b)).astype(F32)
    xs, bs, cs = jnp.split(u, [SSM_D_INNER, SSM_D_INNER + SSM_GROUPS * SSM_STATE], axis=-1)
    rep = SSM_HEADS // SSM_GROUPS
    xs = xs.reshape(Bn, L, SSM_HEADS, SSM_HEAD_DIM)
    bs = jnp.repeat(bs.reshape(Bn, L, SSM_GROUPS, SSM_STATE), rep, axis=2)
    cs = jnp.repeat(cs.reshape(Bn, L, SSM_GROUPS, SSM_STATE), rep, axis=2)
    dt = jax.nn.softplus(dt_raw.astype(F32).reshape(Bn, L, 2, SSM_HEADS) + dt_bias.astype(F32))
    a = -jnp.exp(a_log.astype(F32))
    y_f, s_f = ssd_chunked(xs * dt[:, :, 0, :, None], dt[:, :, 0] * a[0], bs, cs, init_fwd)
    rev = lambda t: jnp.flip(t, axis=1)
    y_b, s_b = ssd_chunked(rev(xs * dt[:, :, 1, :, None]), rev(dt[:, :, 1] * a[1]), rev(bs), rev(cs), init_bwd)
    skip = (d_skip[0] + d_skip[1]).astype(F32)[:, None]
    return y_f + rev(y_b) + skip * xs, s_f, s_b


def ssm_gated_norm(y, z, g):
    Bn, L = y.shape[:2]
    yz = y.reshape(Bn, L, SSM_D_INNER) * jax.nn.silu(z.astype(F32))
    yg = yz.reshape(Bn, L, SSM_GROUPS, -1)
    yg = yg * lax.rsqrt(jnp.mean(yg * yg, axis=-1, keepdims=True) + EPS)
    return (yg.reshape(Bn, L, SSM_D_INNER) * g.astype(F32)).astype(z.dtype)


def merge_branches(o_a, o_b, o_c, gate_logit, w_ba, w_bb, w_bc, w_o):
    g_a, g_b, g_c = jnp.split(jax.nn.sigmoid(gate_logit), 3, axis=-1)
    return (g_a * (o_a @ w_ba) + g_b * (o_b @ w_bb) + g_c * (o_c @ w_bc)) @ w_o


def hybrid_mixer(u_lat, u_ctx, w_in, rpb, lam_vec, subln_g, conv_w, conv_b, dt_bias, a_log, d_skip,
                 ssm_norm_g, w_ba, w_bb, w_bc, w_o, lam_init, rope_cos, rope_sin, with_ctx_out):
    Bn, L, _ = u_lat.shape
    Lc = u_ctx.shape[1]
    qa, ka, va, qb, kb, vb, z, xbc, dt_raw, gate_logit = project_in(u_lat, w_in)
    qa_c, ka_c, va_c, qb_c, kb_c, vb_c, z_c, xbc_c, dt_raw_c, gate_logit_c = project_in(u_ctx, w_in)
    heads_a = lambda t, n: t.reshape(Bn, n, NA_HEADS, NA_HEAD_DIM)
    heads_qk = lambda t, n: t.reshape(Bn, n, DA_HEADS, 2, DA_HEAD_DIM)
    heads_v = lambda t, n: t.reshape(Bn, n, DA_HEADS, 2 * DA_HEAD_DIM)
    ka_ctx, va_ctx = heads_a(ka_c, Lc), heads_a(va_c, Lc)
    o_a = neighbourhood_attn_latent(heads_a(qa, L), heads_a(ka, L), heads_a(va, L), ka_ctx, va_ctx, rpb)
    lv = lam_vec.astype(F32)
    lam = jnp.exp(jnp.sum(lv[0] * lv[1])) - jnp.exp(jnp.sum(lv[2] * lv[3])) + lam_init
    kb_ctx, vb_ctx = heads_qk(kb_c, Lc), heads_v(vb_c, Lc)
    k_all = jnp.concatenate([apply_axial_rope(heads_qk(kb, L), rope_cos, rope_sin), kb_ctx], axis=1)
    v_all = jnp.concatenate([heads_v(vb, L), vb_ctx], axis=1)
    o_b = diff_attn_latent(apply_axial_rope(heads_qk(qb, L), rope_cos, rope_sin), k_all, v_all, lam)
    o_b = (rms_norm(o_b, subln_g) * (1.0 - lam_init)).reshape(Bn, L, DA_WIDTH)
    zero_state = jnp.zeros((Bn, SSM_HEADS, SSM_HEAD_DIM, SSM_STATE), F32)
    y_ctx, s_fwd, s_bwd = ssm_scan_bidir(xbc_c, dt_raw_c, conv_w, conv_b, dt_bias, a_log, d_skip,
                                         zero_state, zero_state)
    y_lat, _, _ = ssm_scan_bidir(xbc, dt_raw, conv_w, conv_b, dt_bias, a_log, d_skip, s_fwd, s_bwd)
    o_c = ssm_gated_norm(y_lat, z, ssm_norm_g)
    out_lat = merge_branches(o_a, o_b, o_c, gate_logit, w_ba, w_bb, w_bc, w_o)
    if not with_ctx_out:
        return out_lat, None
    o_a_c = dense_attn(heads_a(qa_c, Lc), ka_ctx, va_ctx).reshape(Bn, Lc, NA_WIDTH)
    o_b_c = diff_attn(heads_qk(qb_c, Lc), kb_ctx, vb_ctx, lam)
    o_b_c = (rms_norm(o_b_c, subln_g) * (1.0 - lam_init)).reshape(Bn, Lc, DA_WIDTH)
    o_c_c = ssm_gated_norm(y_ctx, z_c, ssm_norm_g)
    out_ctx = merge_branches(o_a_c, o_b_c, o_c_c, gate_logit_c, w_ba, w_bb, w_bc, w_o)
    return out_lat, out_ctx


def swiglu(t, w1, w3, w2):
    return (jax.nn.silu(t @ w1) * (t @ w3)) @ w2


def moe_swiglu(t, router_w, router_b, w1, w3, w2):
    N, D = t.shape
    E = router_w.shape[-1]
    logits = t.astype(F32) @ router_w.astype(F32) + router_b.astype(F32)
    top_logit, top_idx = lax.top_k(logits, TOP_K)
    gate = jax.nn.softmax(top_logit, axis=-1)
    n_assign = N * TOP_K
    e_flat = top_idx.reshape(-1)
    tok_flat = jnp.repeat(jnp.arange(N, dtype=jnp.int32), TOP_K)
    order = jnp.argsort(e_flat)
    e_sorted = e_flat[order]
    tok_sorted = tok_flat[order]
    g_sorted = gate.reshape(-1)[order]
    counts = jnp.bincount(e_flat, length=E)
    padded = (counts + MOE_BLOCK - 1) // MOE_BLOCK * MOE_BLOCK
    pad_end = jnp.cumsum(padded)
    pad_start = pad_end - padded
    start = jnp.cumsum(counts) - counts
    dest = pad_start[e_sorted] + jnp.arange(n_assign) - start[e_sorted]
    n_blocks = -(-n_assign // MOE_BLOCK) + E
    cap = n_blocks * MOE_BLOCK
    slot_tok = jnp.full((cap,), N, dtype=jnp.int32).at[dest].set(tok_sorted)
    t_pad = jnp.concatenate([t, jnp.zeros((1, D), t.dtype)], axis=0)
    x_blocks = t_pad[slot_tok].reshape(n_blocks, MOE_BLOCK, D)
    block_expert = jnp.clip(jnp.searchsorted(pad_end, jnp.arange(n_blocks) * MOE_BLOCK, side='right'), 0, E - 1)
    y_blocks = lax.map(lambda args: swiglu(args[0], w1[args[1]], w3[args[1]], w2[args[1]]),
                       (x_blocks, block_expert))
    y_sorted = y_blocks.reshape(cap, D)[dest]
    return jnp.zeros((N, D), t.dtype).at[tok_sorted].add(y_sorted * g_sorted[:, None].astype(t.dtype))


def setup_inputs(seed: int = 0) -> dict:
    key = jax.random.key(seed)
    ks = jax.random.split(key, 32)
    D = D_MODEL
    n_dense = (DEPTH + 1) // 2
    n_moe = DEPTH // 2

    def nrm(i, shape, s):
        return s * jax.random.normal(ks[i], shape, F32)

    dt0 = jnp.exp(jax.random.uniform(ks[14], (DEPTH, 2, SSM_HEADS), F32,
                                     minval=math.log(1e-3), maxval=math.log(1e-1)))
    return {
        'x': nrm(0, (BATCH, SEQ, D), 1.0),
        'c': nrm(1, (BATCH, D), 1.0),
        'ctx': nrm(2, (BATCH, CTX_LEN, D), 1.0),
        'c_ctx': nrm(3, (D,), 1.0),
        'ada_w': nrm(4, (DEPTH, D, 6 * D), D ** -0.5),
        'ada_b': nrm(5, (DEPTH, 6 * D), 0.02),
        'norm_mix_g': 1.0 + nrm(6, (DEPTH, D), 0.02),
        'norm_ffn_g': 1.0 + nrm(7, (DEPTH, D), 0.02),
        'w_in': nrm(8, (DEPTH, D, IN_WIDTH), D ** -0.5),
        'na_rpb': nrm(9, (DEPTH, NA_HEADS, 2 * NA_WIN_H - 1, 2 * NA_WIN_W - 1), 0.1),
        'da_lambda': nrm(10, (DEPTH, 4, DA_HEAD_DIM), 0.1),
        'da_subln_g': 1.0 + nrm(11, (DEPTH, 2 * DA_HEAD_DIM), 0.02),
        'ssm_conv_w': nrm(12, (DEPTH, SSM_CONV_W, SSM_CONV_DIM), SSM_CONV_W ** -0.5),
        'ssm_conv_b': nrm(13, (DEPTH, SSM_CONV_DIM), 0.02),
        'ssm_dt_bias': dt0 + jnp.log(-jnp.expm1(-dt0)),
        'ssm_a_log': jnp.log(jax.random.uniform(ks[15], (DEPTH, 2, SSM_HEADS), F32, minval=1.0, maxval=16.0)),
        'ssm_d': 1.0 + nrm(16, (DEPTH, 2, SSM_HEADS), 0.1),
        'ssm_norm_g': 1.0 + nrm(17, (DEPTH, SSM_D_INNER), 0.02),
        'w_branch_a': nrm(18, (DEPTH, NA_WIDTH, D), NA_WIDTH ** -0.5),
        'w_branch_b': nrm(19, (DEPTH, DA_WIDTH, D), DA_WIDTH ** -0.5),
        'w_branch_c': nrm(20, (DEPTH, SSM_D_INNER, D), SSM_D_INNER ** -0.5),
        'w_out': nrm(21, (DEPTH, D, D), D ** -0.5),
        'ffn_w1': nrm(22, (n_dense, D, FFN_DIM), D ** -0.5),
        'ffn_w3': nrm(23, (n_dense, D, FFN_DIM), D ** -0.5),
        'ffn_w2': nrm(24, (n_dense, FFN_DIM, D), FFN_DIM ** -0.5),
        'moe_router_w': nrm(25, (n_moe, D, N_EXPERTS), D ** -0.5),
        'moe_router_b': nrm(26, (n_moe, N_EXPERTS), 0.01),
        'moe_w1': nrm(27, (n_moe, N_EXPERTS, D, EXPERT_DIM), D ** -0.5),
        'moe_w3': nrm(28, (n_moe, N_EXPERTS, D, EXPERT_DIM), D ** -0.5),
        'moe_w2': nrm(29, (n_moe, N_EXPERTS, EXPERT_DIM, D), EXPERT_DIM ** -0.5),
        'final_norm_g': 1.0 + nrm(30, (D,), 0.02),
    }


def reference(x, c, ctx, c_ctx, ada_w, ada_b, norm_mix_g, norm_ffn_g, w_in, na_rpb, da_lambda, da_subln_g,
              ssm_conv_w, ssm_conv_b, ssm_dt_bias, ssm_a_log, ssm_d, ssm_norm_g, w_branch_a, w_branch_b,
              w_branch_c, w_out, ffn_w1, ffn_w3, ffn_w2, moe_router_w, moe_router_b, moe_w1, moe_w3, moe_w2,
              final_norm_g):
    Bn, L, D = x.shape
    rope_cos, rope_sin = axial_rope_tables(L, DA_HEAD_DIM)
    silu_c = jax.nn.silu(c)
    silu_cc = jax.nn.silu(c_ctx)
    h_lat, h_ctx = x, ctx
    for i in range(DEPTH):
        last = i == DEPTH - 1
        mod_lat = jnp.split((silu_c @ ada_w[i] + ada_b[i])[:, None, :], 6, axis=-1)
        mod_ctx = jnp.split(silu_cc @ ada_w[i] + ada_b[i], 6, axis=-1)
        u_lat = modulate(rms_norm(h_lat, norm_mix_g[i]), mod_lat[0], mod_lat[1])
        u_ctx = modulate(rms_norm(h_ctx, norm_mix_g[i]), mod_ctx[0], mod_ctx[1])
        mix_lat, mix_ctx = hybrid_mixer(u_lat, u_ctx, w_in[i], na_rpb[i], da_lambda[i], da_subln_g[i],
                                        ssm_conv_w[i], ssm_conv_b[i], ssm_dt_bias[i], ssm_a_log[i], ssm_d[i],
                                        ssm_norm_g[i], w_branch_a[i], w_branch_b[i], w_branch_c[i], w_out[i],
                                        0.8 - 0.6 * math.exp(-0.3 * i), rope_cos, rope_sin, not last)
        h_lat = h_lat + mod_lat[2] * mix_lat
        tokens = modulate(rms_norm(h_lat, norm_ffn_g[i]), mod_lat[3], mod_lat[4]).reshape(-1, D)
        if not last:
            h_ctx = h_ctx + mod_ctx[2] * mix_ctx
            v_ctx = modulate(rms_norm(h_ctx, norm_ffn_g[i]), mod_ctx[3], mod_ctx[4])
            tokens = jnp.concatenate([tokens, v_ctx.reshape(-1, D)], axis=0)
        j = i // 2
        if i % 2 == 0:
            f = swiglu(tokens, ffn_w1[j], ffn_w3[j], ffn_w2[j])
        else:
            f = moe_swiglu(tokens, moe_router_w[j], moe_router_b[j], moe_w1[j], moe_w3[j], moe_w2[j])
        h_lat = h_lat + mod_lat[5] * f[:Bn * L].reshape(Bn, L, D)
        if not last:
            h_ctx = h_ctx + mod_ctx[5] * f[Bn * L:].reshape(Bn, -1, D)
    return rms_norm(h_lat, final_norm_g)
```

```python
import functools
import math

import numpy as np
import jax
import jax.numpy as jnp
from jax import lax
from jax.experimental import pallas as pl
from jax.experimental.pallas import tpu as pltpu

F32 = jnp.float32
BF16 = jnp.bfloat16

GRID_W = 64
NA_HEADS, NA_HEAD_DIM, NA_WIN_H, NA_WIN_W = 8, 64, 8, 16
DA_HEADS, DA_HEAD_DIM = 4, 64
ROPE_BASE = 10000.0
SSM_D_INNER, SSM_HEAD_DIM, SSM_GROUPS, SSM_STATE, SSM_CONV_W = 1024, 64, 2, 128, 5
SSM_HEADS = SSM_D_INNER // SSM_HEAD_DIM
N_EXPERTS, TOP_K = 8, 2
EPS = 1e-6
NEG_INF = -1e30

LANES = 128
SUBLANES = 8
VMEM_LIMIT_BYTES = 56 * 1024 * 1024
SSD_CHUNK = 128
MOE_ROWS = 512


def _cparams(sem):
    return pltpu.CompilerParams(dimension_semantics=sem, vmem_limit_bytes=VMEM_LIMIT_BYTES)


def _dot(a, b):
    return jnp.dot(a, b, preferred_element_type=F32)


def _dot_nt(a, b):
    return lax.dot_general(a, b, (((1,), (1,)), ((), ())), preferred_element_type=F32)


def _silu(x):
    return x * jax.nn.sigmoid(x)


def _gmm_body(bexp_ref, neww_ref, mod_ref, x_ref, *rest, nk, n_w, epi):
    del bexp_ref, mod_ref
    w_refs, rest = rest[:n_w], rest[n_w:]
    if epi == "bias":
        bias_ref, rest = rest[0], rest[1:]
    elif epi == "resid":
        res_ref, gate_ref, rest = rest[0], rest[1], rest[2:]
    o_ref, rest = rest[0], rest[1:]
    wbf_refs, acc_refs = rest[:n_w], rest[n_w:]
    i = pl.program_id(1)
    k = pl.program_id(2)

    @pl.when(neww_ref[i] == 1)
    def _():
        for w_ref, wbf in zip(w_refs, wbf_refs):
            wbf[k] = w_ref[...].astype(BF16)

    x = x_ref[...]
    parts = [_dot(x, wbf[k]) for wbf in wbf_refs]

    def finalize(vals):
        if epi == "swiglu":
            o_ref[...] = (_silu(vals[0]) * vals[1]).astype(o_ref.dtype)
        elif epi == "bias":
            o_ref[...] = (vals[0] + bias_ref[...]).astype(o_ref.dtype)
        elif epi == "resid":
            o_ref[...] = (res_ref[...] + gate_ref[...] * vals[0]).astype(o_ref.dtype)
        else:
            o_ref[...] = vals[0].astype(o_ref.dtype)

    if nk == 1:
        finalize(parts)
    else:
        @pl.when(k == 0)
        def _():
            for acc, p in zip(acc_refs, parts):
                acc[...] = p

        @pl.when(k > 0)
        def _():
            for acc, p in zip(acc_refs, parts):
                acc[...] += p

        @pl.when(k == nk - 1)
        def _():
            finalize([acc[...] for acc in acc_refs])


def _gmm(x, ws, bexp, *, tm, tn, nk=1, n_cols=None, epi="plain", out_dtype=F32, bias=None, res=None,
         mod=None, mod_idx=None, mod_part=0, n_rows=None):
    m_rows = x.shape[0] if n_rows is None else n_rows
    k_dim = x.shape[1]
    n_dim = ws[0].shape[2] if n_cols is None else n_cols
    n_w = len(ws)
    tk = k_dim // nk
    nb = m_rows // tm
    assert m_rows % tm == 0 and n_dim % tn == 0 and k_dim % nk == 0
    bexp = bexp.astype(jnp.int32)
    neww = jnp.concatenate([jnp.ones((1,), jnp.int32), (bexp[1:] != bexp[:-1]).astype(jnp.int32)])
    if mod_idx is None:
        mod_idx = jnp.zeros((nb,), jnp.int32)

    x_spec = pl.BlockSpec((tm, tk), lambda j, i, k, be, nw, md: (i, k))
    w_spec = pl.BlockSpec((None, tk, tn),
                          lambda j, i, k, be, nw, md: (be[i], jnp.where(nw[i] == 1, k, nk - 1), j))
    o_spec = pl.BlockSpec((tm, tn), lambda j, i, k, be, nw, md: (i, j))
    in_specs = [x_spec] + [w_spec] * n_w
    args = [x] + list(ws)
    if epi == "bias":
        in_specs.append(pl.BlockSpec((1, tn), lambda j, i, k, be, nw, md: (0, j)))
        args.append(bias)
    elif epi == "resid":
        part_off = mod_part * (n_dim // tn)
        in_specs.append(o_spec)
        in_specs.append(pl.BlockSpec((None, 1, tn), lambda j, i, k, be, nw, md: (md[i], 0, part_off + j)))
        args += [res, mod]
    scratch = [pltpu.VMEM((nk, tk, tn), BF16) for _ in range(n_w)]
    if nk > 1:
        scratch += [pltpu.VMEM((tm, tn), F32) for _ in range(n_w)]
    grid_spec = pltpu.PrefetchScalarGridSpec(
        num_scalar_prefetch=3, grid=(n_dim // tn, nb, nk), in_specs=in_specs, out_specs=o_spec,
        scratch_shapes=scratch)
    return pl.pallas_call(
        functools.partial(_gmm_body, nk=nk, n_w=n_w, epi=epi),
        grid_spec=grid_spec,
        out_shape=jax.ShapeDtypeStruct((m_rows, n_dim), out_dtype),
        compiler_params=_cparams(("arbitrary", "arbitrary", "arbitrary")),
        name="gmm_" + epi,
    )(bexp, neww, mod_idx.astype(jnp.int32), *args)


def _split2(v):
    hi = v.astype(BF16)
    lo = (v - hi.astype(F32)).astype(BF16)
    return hi, lo


def _norm_body(x_ref, g_ref, *rest, modulate, router):
    if modulate:
        shift_ref, scale_ref, rest = rest[0], rest[1], rest[2:]
    if router:
        rw_ref, rb_ref, rest = rest[0], rest[1], rest[2:]
    o_ref = rest[0]
    x = x_ref[...]
    y = x * lax.rsqrt(jnp.mean(x * x, axis=-1, keepdims=True) + EPS) * g_ref[...]
    if modulate:
        y = y * (1.0 + scale_ref[...]) + shift_ref[...]
    o_ref[...] = y.astype(o_ref.dtype)
    if router:
        lg_ref = rest[1]
        y_hi, y_lo = _split2(y)
        w_hi, w_lo = _split2(rw_ref[...])
        lg_ref[...] = _dot(y_hi, w_hi) + _dot(y_lo, w_hi) + _dot(y_hi, w_lo) + rb_ref[...]


def _norm(x, g, *, tm, n_rows, mod=None, mod_idx_fn=None, parts=(0, 1), out_dtype=BF16, router=None):
    d = x.shape[1]
    nb = n_rows // tm
    row = pl.BlockSpec((tm, d), lambda i: (i, 0))
    in_specs = [row, pl.BlockSpec((1, d), lambda i: (0, 0))]
    args = [x, g.reshape(1, d)]
    modulate = mod is not None
    if modulate:
        for p in parts:
            in_specs.append(pl.BlockSpec((None, 1, d), lambda i, p=p: (mod_idx_fn(i), 0, p)))
            args.append(mod)
    out_shape = [jax.ShapeDtypeStruct((n_rows, d), out_dtype)]
    out_specs = [row]
    if router is not None:
        rw, rb = router
        in_specs += [pl.BlockSpec((d, LANES), lambda i: (0, 0)), pl.BlockSpec((1, LANES), lambda i: (0, 0))]
        args += [rw, rb]
        out_shape.append(jax.ShapeDtypeStruct((n_rows, LANES), F32))
        out_specs.append(pl.BlockSpec((tm, LANES), lambda i: (i, 0)))
    res = pl.pallas_call(
        functools.partial(_norm_body, modulate=modulate, router=router is not None),
        grid=(nb,), in_specs=in_specs, out_specs=out_specs, out_shape=out_shape,
        compiler_params=_cparams(("arbitrary",)), name="rmsnorm",
    )(*args)
    return res if router is not None else res[0]


def _resid_norm_body(h_ref, f_ref, gate_ref, g_ref, o_ref, *, norm):
    x = h_ref[...] + gate_ref[...] * f_ref[...]
    if norm:
        x = x * lax.rsqrt(jnp.mean(x * x, axis=-1, keepdims=True) + EPS) * g_ref[...]
    o_ref[...] = x


def _resid_norm(h, f, mod, mod_idx_fn, part, g, *, tm, n_rows, norm):
    d = h.shape[1]
    row = pl.BlockSpec((tm, d), lambda i: (i, 0))
    return pl.pallas_call(
        functools.partial(_resid_norm_body, norm=norm), grid=(n_rows // tm,),
        in_specs=[row, row, pl.BlockSpec((None, 1, d), lambda i: (mod_idx_fn(i), 0, part)),
                  pl.BlockSpec((1, d), lambda i: (0, 0))],
        out_specs=row, out_shape=jax.ShapeDtypeStruct((n_rows, d), F32),
        compiler_params=_cparams(("arbitrary",)), name="resid_norm",
    )(h, f, mod, g.reshape(1, d))


def _rope_body(q_ref, k_ref, c_ref, sm_ref, sp_ref, qo_ref, ko_ref, *, q_scale):
    c, sm, sp = c_ref[...], sm_ref[...], sp_ref[...]
    width = q_ref.shape[1]
    for src, dst, scale in ((q_ref, qo_ref, q_scale), (k_ref, ko_ref, 1.0)):
        for g in range(width // LANES):
            sl = slice(g * LANES, (g + 1) * LANES)
            x = src[:, sl]
            y = x * c + pltpu.roll(x, LANES - 16, 1) * sm + pltpu.roll(x, 16, 1) * sp
            dst[:, sl] = (y * scale).astype(dst.dtype)


def _rope_tables(seq, n_id_rows):
    t = jnp.arange(seq)
    pos = jnp.stack([t // GRID_W, t % GRID_W], axis=-1).astype(F32)
    n_freq = DA_HEAD_DIM // 4
    inv_freq = ROPE_BASE ** (-jnp.arange(n_freq, dtype=F32) / n_freq)
    ang = pos[:, :, None] * inv_freq
    d = np.arange(LANES) % DA_HEAD_DIM
    kind, which, f = d // 32, (d % 32) // 16, d % 16
    a = ang[:, kind, f]
    cos, sin = jnp.cos(a), jnp.sin(a)
    sm = jnp.where(which == 0, -sin, 0.0)
    sp = jnp.where(which == 1, sin, 0.0)
    pad = lambda v, fill: jnp.concatenate([v, jnp.full((n_id_rows, LANES), fill, F32)], axis=0)
    return pad(cos, 1.0), pad(sm, 0.0), pad(sp, 0.0)


def _rope(proj, tabs, *, tm, n_rows, n_lat_rows, seq, q_col, k_col, width):
    lat_blocks = n_lat_rows // tm
    per_seq = seq // tm
    tab_idx = lambda i: (jnp.where(i < lat_blocks, i % per_seq, per_seq), 0)
    out = jax.ShapeDtypeStruct((n_rows, width), BF16)
    return pl.pallas_call(
        functools.partial(_rope_body, q_scale=DA_HEAD_DIM ** -0.5),
        grid=(n_rows // tm,),
        in_specs=[pl.BlockSpec((tm, width), lambda i: (i, q_col // width)),
                  pl.BlockSpec((tm, width), lambda i: (i, k_col // width)),
                  pl.BlockSpec((tm, LANES), tab_idx), pl.BlockSpec((tm, LANES), tab_idx),
                  pl.BlockSpec((tm, LANES), tab_idx)],
        out_specs=[pl.BlockSpec((tm, width), lambda i: (i, 0))] * 2,
        out_shape=[out, out], compiler_params=_cparams(("arbitrary",)), name="rope",
    )(proj, proj, *tabs)


def _na_body(q_ref, k_ref, v_ref, kc_ref, vc_ref, tb_ref, o_ref, kb, vb, kcb, vcb, *, rows, wh):
    kb[...] = k_ref[...].astype(BF16)
    vb[...] = v_ref[...].astype(BF16)
    kcb[...] = kc_ref[...].astype(BF16)
    vcb[...] = vc_ref[...].astype(BF16)
    lane = lax.broadcasted_iota(jnp.int32, (GRID_W, LANES), 1)
    first = lane < NA_HEAD_DIM
    scale = NA_HEAD_DIM ** -0.5

    def body(r, carry):
        rs = jnp.clip(r - wh // 2, 0, rows - wh)
        d0 = rs - r + (NA_WIN_H - 1)
        q = q_ref[pl.ds(pl.multiple_of(r * GRID_W, GRID_W), GRID_W), :] * scale
        kw = kb[pl.ds(pl.multiple_of(rs * GRID_W, GRID_W), wh * GRID_W), :]
        vw = vb[pl.ds(pl.multiple_of(rs * GRID_W, GRID_W), wh * GRID_W), :]
        outs = []
        for hh in range(2):
            qm = jnp.where(first if hh == 0 else jnp.logical_not(first), q, 0.0).astype(BF16)
            s_l = _dot_nt(qm, kw) + tb_ref[hh, d0]
            s_c = _dot_nt(qm, kcb[...])
            m = jnp.maximum(jnp.max(s_l, axis=-1, keepdims=True), jnp.max(s_c, axis=-1, keepdims=True))
            p_l = jnp.exp(s_l - m)
            p_c = jnp.exp(s_c - m)
            den = jnp.sum(p_l, axis=-1, keepdims=True) + jnp.sum(p_c, axis=-1, keepdims=True)
            o = _dot(p_l.astype(BF16), vw) + _dot(p_c.astype(BF16), vcb[...])
            outs.append(o / den)
        o_ref[pl.ds(pl.multiple_of(r * GRID_W, GRID_W), GRID_W), :] = (
            jnp.where(first, outs[0], outs[1]).astype(o_ref.dtype))
        return carry

    lax.fori_loop(0, rows, body, 0)


def _na_bias_table(rpb, rows):
    wh = min(NA_WIN_H, rows)
    qc = np.arange(GRID_W)[:, None]
    kc = np.arange(GRID_W)[None, :]
    ws = np.clip(qc - NA_WIN_W // 2, 0, GRID_W - NA_WIN_W)
    ok = (kc >= ws) & (kc < ws + NA_WIN_W)
    dcol = np.clip(kc - qc + NA_WIN_W - 1, 0, 2 * NA_WIN_W - 2)
    n_d0 = NA_WIN_H
    drow = np.clip(np.arange(n_d0)[:, None] + np.arange(wh)[None, :], 0, 2 * NA_WIN_H - 2)
    t = rpb.astype(F32)[:, drow]
    t = t[:, :, :, dcol]
    t = jnp.where(ok, t, NEG_INF)
    t = t.transpose(0, 1, 3, 2, 4).reshape(NA_HEADS, n_d0, GRID_W, wh * GRID_W)
    return t.reshape(NA_HEADS // 2, 2, n_d0, GRID_W, wh * GRID_W)


def _na(proj, rpb, *, batch, seq, ctx_len, q_col, k_col, v_col):
    rows = seq // GRID_W
    wh = min(NA_WIN_H, rows)
    pairs = NA_HEADS // 2
    tb = _na_bias_table(rpb, rows)
    ctx0 = batch * seq // ctx_len
    lat = lambda col: pl.BlockSpec((seq, LANES), lambda b, p: (b, col // LANES + p))
    ctx = lambda col: pl.BlockSpec((ctx_len, LANES), lambda b, p: (ctx0 + b, col // LANES + p))
    return pl.pallas_call(
        functools.partial(_na_body, rows=rows, wh=wh),
        grid=(batch, pairs),
        in_specs=[lat(q_col), lat(k_col), lat(v_col), ctx(k_col), ctx(v_col),
                  pl.BlockSpec((None, 2, NA_WIN_H, GRID_W, wh * GRID_W), lambda b, p: (p, 0, 0, 0, 0))],
        out_specs=pl.BlockSpec((seq, LANES), lambda b, p: (b, p)),
        out_shape=jax.ShapeDtypeStruct((batch * seq, pairs * LANES), BF16),
        scratch_shapes=[pltpu.VMEM((seq, LANES), BF16), pltpu.VMEM((seq, LANES), BF16),
                        pltpu.VMEM((ctx_len, LANES), BF16), pltpu.VMEM((ctx_len, LANES), BF16)],
        compiler_params=_cparams(("arbitrary", "arbitrary")), name="na_attn",
    )(proj, proj, proj, proj, proj, tb)


def _ctx_attn_body(q_ref, k_ref, v_ref, o_ref):
    kb = k_ref[...].astype(BF16)
    vb = v_ref[...].astype(BF16)
    q = q_ref[...] * (NA_HEAD_DIM ** -0.5)
    lane = lax.broadcasted_iota(jnp.int32, q.shape, 1)
    first = lane < NA_HEAD_DIM
    outs = []
    for hh in range(2):
        qm = jnp.where(first if hh == 0 else jnp.logical_not(first), q, 0.0).astype(BF16)
        s = _dot_nt(qm, kb)
        p = jnp.exp(s - jnp.max(s, axis=-1, keepdims=True))
        outs.append(_dot(p.astype(BF16), vb) / jnp.sum(p, axis=-1, keepdims=True))
    o_ref[...] = jnp.where(first, outs[0], outs[1]).astype(o_ref.dtype)


def _ctx_attn(proj, *, batch, seq, ctx_len, q_col, k_col, v_col):
    pairs = NA_HEADS // 2
    ctx0 = batch * seq // ctx_len
    ctx = lambda col: pl.BlockSpec((ctx_len, LANES), lambda b, p: (ctx0 + b, col // LANES + p))
    return pl.pallas_call(
        _ctx_attn_body, grid=(batch, pairs),
        in_specs=[ctx(q_col), ctx(k_col), ctx(v_col)],
        out_specs=pl.BlockSpec((ctx_len, LANES), lambda b, p: (b, p)),
        out_shape=jax.ShapeDtypeStruct((batch * ctx_len, pairs * LANES), BF16),
        compiler_params=_cparams(("arbitrary", "arbitrary")), name="ctx_attn",
    )(proj, proj, proj)


def _da_body(lv_ref, q_ref, *rest, lam_init, has_lat):
    if has_lat:
        kl_ref, vl_ref, rest = rest[0], rest[1], rest[2:]
    kc_ref, vc_ref, g_ref, o_ref = rest[:4]
    scr = rest[4:]
    t = pl.program_id(2)
    if has_lat:
        vlb, vcb = scr

        @pl.when(t == 0)
        def _():
            vlb[...] = vl_ref[...].astype(BF16)
            vcb[...] = vc_ref[...].astype(BF16)
    else:
        (vcb,) = scr
        vcb[...] = vc_ref[...].astype(BF16)

    lv = lv_ref[...]
    lam = (jnp.exp(jnp.sum(lv[0:1] * lv[1:2], axis=-1, keepdims=True))
           - jnp.exp(jnp.sum(lv[2:3] * lv[3:4], axis=-1, keepdims=True)) + lam_init)
    q = q_ref[...]
    tq = q.shape[0]
    lane = lax.broadcasted_iota(jnp.int32, q.shape, 1)
    zero = jnp.zeros_like(q)
    qq = jnp.concatenate([jnp.where(lane < DA_HEAD_DIM, q, zero), jnp.where(lane >= DA_HEAD_DIM, q, zero)], axis=0)
    s_c = _dot_nt(qq, kc_ref[...])
    m = jnp.max(s_c, axis=-1, keepdims=True)
    if has_lat:
        s_l = _dot_nt(qq, kl_ref[...])
        m = jnp.maximum(m, jnp.max(s_l, axis=-1, keepdims=True))
        p_l = jnp.exp(s_l - m)
    p_c = jnp.exp(s_c - m)
    den = jnp.sum(p_c, axis=-1, keepdims=True)
    if has_lat:
        den = den + jnp.sum(p_l, axis=-1, keepdims=True)
    inv = 1.0 / den
    p_c = p_c * inv
    a_c = (p_c[:tq] - lam * p_c[tq:]).astype(BF16)
    o = _dot(a_c, vcb[...])
    if has_lat:
        p_l = p_l * inv
        a_l = (p_l[:tq] - lam * p_l[tq:]).astype(BF16)
        o = o + _dot(a_l, vlb[...])
    o = o * lax.rsqrt(jnp.mean(o * o, axis=-1, keepdims=True) + EPS) * g_ref[...]
    o_ref[...] = (o * (1.0 - lam_init)).astype(o_ref.dtype)


def _da(q_r, k_r, proj, lam_vec, subln_g, lam_init, *, batch, seq, ctx_len, v_col, tq, latent):
    ctx0 = batch * seq // ctx_len
    vcb0 = v_col // LANES
    g2 = subln_g.reshape(1, 2 * DA_HEAD_DIM).astype(F32)
    small = [pl.BlockSpec((4, DA_HEAD_DIM), lambda b, h, t: (0, 0))]
    gspec = pl.BlockSpec((1, LANES), lambda b, h, t: (0, 0))
    kc = pl.BlockSpec((ctx_len, LANES), lambda b, h, t: (ctx0 + b, h))
    vc = pl.BlockSpec((ctx_len, LANES), lambda b, h, t: (ctx0 + b, vcb0 + h))
    if latent:
        nq = seq // tq
        qs = pl.BlockSpec((tq, LANES), lambda b, h, t: (b * nq + t, h))
        kl = pl.BlockSpec((seq, LANES), lambda b, h, t: (b, h))
        vl = pl.BlockSpec((seq, LANES), lambda b, h, t: (b, vcb0 + h))
        in_specs = small + [qs, kl, vl, kc, vc, gspec]
        args = (lam_vec, q_r, k_r, proj, k_r, proj, g2)
        scratch = [pltpu.VMEM((seq, LANES), BF16), pltpu.VMEM((ctx_len, LANES), BF16)]
        n_out = batch * seq
    else:
        nq = ctx_len // tq
        cq0 = batch * seq // tq
        qs = pl.BlockSpec((tq, LANES), lambda b, h, t: (cq0 + b * nq + t, h))
        in_specs = small + [qs, kc, vc, gspec]
        args = (lam_vec, q_r, k_r, proj, g2)
        scratch = [pltpu.VMEM((ctx_len, LANES), BF16)]
        n_out = batch * ctx_len
    return pl.pallas_call(
        functools.partial(_da_body, lam_init=lam_init, has_lat=latent),
        grid=(batch, DA_HEADS, nq), in_specs=in_specs,
        out_specs=pl.BlockSpec((tq, LANES), lambda b, h, t: (b * nq + t, h)),
        out_shape=jax.ShapeDtypeStruct((n_out, DA_HEADS * LANES), BF16),
        scratch_shapes=scratch,
        compiler_params=_cparams(("arbitrary", "arbitrary", "arbitrary")),
        name="diff_attn_lat" if latent else "diff_attn_ctx",
    )(*args)


def _ssd_body(x0_ref, x1_ref, bc_ref, x0p_ref, x1p_ref, bcp_ref, x0n_ref, x1n_ref, bcn_ref, dt_ref,
              cw_ref, cb_ref, prm_ref, dsk_ref, ex_ref, init_ref, y_ref, sout_ref, st_ref, *, nc):
    q_len = SSD_CHUNK
    dirn = pl.program_id(1)
    z = pl.program_id(2)
    zz = jnp.where(dirn == 0, z, nc - 1 - z)
    fwd = dirn == 0
    half = SSM_D_INNER // SSM_GROUPS
    heads_per_group = SSM_HEADS // SSM_GROUPS

    @pl.when(z == 0)
    def _():
        st_ref[...] = init_ref[...]

    def conv_piece(main_ref, prev_ref, next_ref, c0):
        width = main_ref.shape[1]
        prev = jnp.where(zz == 0, 0.0, prev_ref[...])
        nxt = jnp.where(zz == nc - 1, 0.0, next_ref[...])
        ext = jnp.concatenate([prev, main_ref[...], nxt], axis=0)
        n_ext = ext.shape[0]
        acc = jnp.zeros((q_len, width), F32) + cb_ref[:, c0:c0 + width]
        for tap in range(SSM_CONV_W):
            sh = (SSM_CONV_W // 2 - tap) % n_ext
            e = ext if sh == 0 else pltpu.roll(ext, sh, 0)
            acc = acc + e[SUBLANES:SUBLANES + q_len] * cw_ref[tap:tap + 1, c0:c0 + width]
        return _silu(acc)

    xs = (conv_piece(x0_ref, x0p_ref, x0n_ref, 0), conv_piece(x1_ref, x1p_ref, x1n_ref, half))
    bcm = conv_piece(bc_ref, bcp_ref, bcn_ref, SSM_D_INNER)

    raw = dt_ref[...]
    raw = jnp.where(fwd, raw, pltpu.roll(raw, LANES - SSM_HEADS, 1))
    xb = raw + prm_ref[0:1, :]
    dtv = jnp.maximum(xb, 0.0) + jnp.log1p(jnp.exp(-jnp.abs(xb)))
    a = dtv * (-jnp.exp(prm_ref[1:2, :]))
    rowi = lax.broadcasted_iota(jnp.int32, (q_len, LANES), 0)
    coli = lax.broadcasted_iota(jnp.int32, (q_len, LANES), 1)
    cum = a
    sft = 1
    while sft < q_len:
        cum = cum + jnp.where(rowi >= sft, pltpu.roll(cum, sft, 0), 0.0)
        sft *= 2
    tot = cum[q_len - 1:q_len, :]
    g = jnp.where(fwd, cum, cum - a)
    g_t = g.T
    e_g = jnp.exp(g)
    e_tg = jnp.exp(tot - g)
    w_state = jnp.where(fwd, e_tg, e_g)
    w_yoff = jnp.where(fwd, e_g, e_tg)
    e_tot = jnp.exp(tot)
    sgn = jnp.where(fwd, 1.0, -1.0)
    tri = jnp.where(fwd, rowi - coli, coli - rowi) >= 0

    stack = jnp.concatenate([dtv, dtv * w_state, w_yoff, jnp.broadcast_to(e_tot, (SUBLANES, LANES))], axis=0)
    s_hi, s_lo = _split2(stack)
    ex = ex_ref[...]
    expd = _dot(s_hi, ex) + _dot(s_lo, ex)
    dt_e, dts_e, wy_e = expd[0:q_len], expd[q_len:2 * q_len], expd[2 * q_len:3 * q_len]
    tot_e = expd[3 * q_len:3 * q_len + 1]

    lane = lax.broadcasted_iota(jnp.int32, (q_len, LANES), 1)
    first = lane < SSM_HEAD_DIM
    for grp in range(SSM_GROUPS):
        x_g = xs[grp]
        csl = slice(grp * half, (grp + 1) * half)
        b_g = bcm[:, grp * SSM_STATE:(grp + 1) * SSM_STATE]
        c_g = bcm[:, (SSM_GROUPS + grp) * SSM_STATE:(SSM_GROUPS + grp + 1) * SSM_STATE].astype(BF16)
        xdt = (x_g * dt_e[:, csl]).astype(BF16)
        xdts = (x_g * dts_e[:, csl]).astype(BF16)
        cb = _dot_nt(c_g, b_g.astype(BF16))
        st_g = st_ref[:, csl]
        y_g = _dot(c_g, st_g.astype(BF16)) * wy_e[:, csl] + dsk_ref[:, csl] * x_g
        for pr in range(heads_per_group // 2):
            outs = []
            for hh in range(2):
                col = grp * heads_per_group + 2 * pr + hh
                seg = sgn * (g[:, col:col + 1] - g_t[col:col + 1, :])
                mat = (cb * jnp.where(tri, jnp.exp(seg), 0.0)).astype(BF16)
                outs.append(_dot(mat, xdt[:, pr * LANES:(pr + 1) * LANES]))
            lo = grp * half + pr * LANES
            y_ref[:, lo:lo + LANES] = y_g[:, pr * LANES:(pr + 1) * LANES] + jnp.where(first, outs[0], outs[1])
        st_ref[:, csl] = tot_e[:, csl] * st_g + _dot(b_g.T.astype(BF16), xdts)

    @pl.when(z == nc - 1)
    def _():
        sout_ref[...] = st_ref[...]


def _ssd(proj, conv_w, conv_b, dt_bias, a_log, d_skip, init, *, batch, n_tok, row0, x_col, dt_col):
    q_len = SSD_CHUNK
    nc = n_tok // q_len
    half = SSM_D_INNER // SSM_GROUPS
    conv_dim = SSM_D_INNER + 2 * SSM_GROUPS * SSM_STATE
    rb0 = row0 // q_len
    hb = q_len // SUBLANES
    n_halo = proj.shape[0] // SUBLANES
    zz = lambda d, z: jnp.where(d == 0, z, nc - 1 - z)
    rb = lambda b, d, z: rb0 + b * nc + zz(d, z)
    main = lambda c: pl.BlockSpec((q_len, half), lambda b, d, z: (rb(b, d, z), c))
    prev = lambda c: pl.BlockSpec((SUBLANES, half), lambda b, d, z: (jnp.maximum(rb(b, d, z) * hb - 1, 0), c))
    nxt = lambda c: pl.BlockSpec((SUBLANES, half),
                                 lambda b, d, z: (jnp.minimum(rb(b, d, z) * hb + hb, n_halo - 1), c))
    cols = [x_col // half + j for j in range(conv_dim // half)]
    const2 = lambda shape: pl.BlockSpec(shape, lambda b, d, z: (0, 0))
    prm = jnp.zeros((2, SUBLANES, LANES), F32)
    prm = prm.at[:, 0, :SSM_HEADS].set(dt_bias.astype(F32)).at[:, 1, :SSM_HEADS].set(a_log.astype(F32))
    dsk = jnp.repeat(d_skip.astype(F32), SSM_HEAD_DIM, axis=-1).reshape(2, 1, SSM_D_INNER)
    ex = (np.arange(LANES)[:, None] == (np.arange(SSM_D_INNER) // SSM_HEAD_DIM)[None, :])
    ex = jnp.asarray(ex, BF16)
    cw = jnp.zeros((SUBLANES, conv_dim), F32).at[:SSM_CONV_W].set(conv_w.astype(F32))
    state_spec = pl.BlockSpec((None, None, SSM_STATE, SSM_D_INNER), lambda b, d, z: (b, d, 0, 0))
    y, s_out = pl.pallas_call(
        functools.partial(_ssd_body, nc=nc),
        grid=(batch, 2, nc),
        in_specs=[main(cols[0]), main(cols[1]), main(cols[2]), prev(cols[0]), prev(cols[1]), prev(cols[2]),
                  nxt(cols[0]), nxt(cols[1]), nxt(cols[2]),
                  pl.BlockSpec((q_len, LANES), lambda b, d, z: (rb(b, d, z), dt_col // LANES)),
                  const2((SUBLANES, conv_dim)), const2((1, conv_dim)),
                  pl.BlockSpec((None, SUBLANES, LANES), lambda b, d, z: (d, 0, 0)),
                  pl.BlockSpec((None, 1, SSM_D_INNER), lambda b, d, z: (d, 0, 0)),
                  const2((LANES, SSM_D_INNER)), state_spec],
        out_specs=[pl.BlockSpec((None, q_len, SSM_D_INNER), lambda b, d, z: (d, b * nc + zz(d, z), 0)),
                   state_spec],
        out_shape=[jax.ShapeDtypeStruct((2, batch * n_tok, SSM_D_INNER), F32),
                   jax.ShapeDtypeStruct((batch, 2, SSM_STATE, SSM_D_INNER), F32)],
        scratch_shapes=[pltpu.VMEM((SSM_STATE, SSM_D_INNER), F32)],
        compiler_params=_cparams(("arbitrary", "arbitrary", "arbitrary")), name="ssd_scan",
    )(proj, proj, proj, proj, proj, proj, proj, proj, proj, proj, cw, conv_b.reshape(1, conv_dim).astype(F32),
      prm, dsk, ex, init)
    return y, s_out


def _ssm_norm_body(y_ref, z_ref, g_ref, o_ref):
    yz = (y_ref[0] + y_ref[1]) * _silu(z_ref[...])
    gw = SSM_D_INNER // SSM_GROUPS
    for grp in range(SSM_GROUPS):
        v = yz[:, grp * gw:(grp + 1) * gw]
        v = v * lax.rsqrt(jnp.mean(v * v, axis=-1, keepdims=True) + EPS)
        o_ref[:, grp * gw:(grp + 1) * gw] = (v * g_ref[:, grp * gw:(grp + 1) * gw]).astype(o_ref.dtype)


def _ssm_norm(y, proj, g, *, tm, row0, z_col):
    n = y.shape[1]
    zb0 = row0 // tm
    return pl.pallas_call(
        _ssm_norm_body, grid=(n // tm,),
        in_specs=[pl.BlockSpec((2, tm, SSM_D_INNER), lambda i: (0, i, 0)),
                  pl.BlockSpec((tm, SSM_D_INNER), lambda i: (zb0 + i, z_col // SSM_D_INNER)),
                  pl.BlockSpec((1, SSM_D_INNER), lambda i: (0, 0))],
        out_specs=pl.BlockSpec((tm, SSM_D_INNER), lambda i: (i, 0)),
        out_shape=jax.ShapeDtypeStruct((n, SSM_D_INNER), BF16),
        compiler_params=_cparams(("arbitrary",)), name="ssm_gated_norm",
    )(y, proj, g.reshape(1, SSM_D_INNER).astype(F32))


def _merge_body(oa_ref, ob_ref, oc_ref, ga_ref, gb_ref, gc_ref, wa_ref, wb_ref, wc_ref, o_ref, wab, wbb, wcb):
    @pl.when(pl.program_id(1) == 0)
    def _():
        wab[...] = wa_ref[...].astype(BF16)
        wbb[...] = wb_ref[...].astype(BF16)
        wcb[...] = wc_ref[...].astype(BF16)

    acc = jax.nn.sigmoid(ga_ref[...]) * _dot(oa_ref[...], wab[...])
    acc = acc + jax.nn.sigmoid(gb_ref[...]) * _dot(ob_ref[...], wbb[...])
    acc = acc + jax.nn.sigmoid(gc_ref[...]) * _dot(oc_ref[...], wcb[...])
    o_ref[...] = acc.astype(o_ref.dtype)


def _merge(o_a, o_b, o_c, gates, w_a, w_b, w_c, layer, *, tm, tn, n_rows):
    d = w_a.shape[2]
    nj = d // tn
    row = lambda w: pl.BlockSpec((tm, w), lambda j, i: (i, 0))
    gate = lambda x: pl.BlockSpec((tm, tn), lambda j, i: (i, x * nj + j))
    wsp = lambda kk: pl.BlockSpec((None, kk, tn), lambda j, i: (layer, 0, j))
    ka, kb_, kc = w_a.shape[1], w_b.shape[1], w_c.shape[1]
    return pl.pallas_call(
        _merge_body, grid=(nj, n_rows // tm),
        in_specs=[row(ka), row(kb_), row(kc), gate(0), gate(1), gate(2), wsp(ka), wsp(kb_), wsp(kc)],
        out_specs=pl.BlockSpec((tm, tn), lambda j, i: (i, j)),
        out_shape=jax.ShapeDtypeStruct((n_rows, d), BF16),
        scratch_shapes=[pltpu.VMEM((ka, tn), BF16), pltpu.VMEM((kb_, tn), BF16), pltpu.VMEM((kc, tn), BF16)],
        compiler_params=_cparams(("arbitrary", "arbitrary")), name="branch_merge",
    )(o_a, o_b, o_c, gates, gates, gates, w_a, w_b, w_c)


def _route(logits, blk):
    n = logits.shape[0]
    top_logit, top_idx = lax.top_k(logits, TOP_K)
    gate = jax.nn.softmax(top_logit, axis=-1)
    e_flat = top_idx.reshape(-1).astype(jnp.int32)
    onehot = (e_flat[:, None] == jnp.arange(N_EXPERTS, dtype=jnp.int32)[None, :]).astype(jnp.int32)
    counts = jnp.sum(onehot, axis=0)
    rank = jnp.sum((jnp.cumsum(onehot, axis=0) - onehot) * onehot, axis=1)
    padded = (counts + blk - 1) // blk * blk
    pad_end = jnp.cumsum(padded)
    pad_start = pad_end - padded
    dest = pad_start[e_flat] + rank
    n_blocks = -(-(n * TOP_K) // blk) + N_EXPERTS
    cap = n_blocks * blk
    tok_flat = jnp.repeat(jnp.arange(n, dtype=jnp.int32), TOP_K)
    slot_tok = jnp.zeros((cap,), jnp.int32).at[dest].set(tok_flat)
    block_expert = jnp.clip(jnp.searchsorted(pad_end, jnp.arange(n_blocks) * blk, side="right"), 0, N_EXPERTS - 1)
    return gate, dest, slot_tok, block_expert.astype(jnp.int32)


def kernel(x, c, ctx, c_ctx, ada_w, ada_b, norm_mix_g, norm_ffn_g, w_in, na_rpb, da_lambda, da_subln_g,
           ssm_conv_w, ssm_conv_b, ssm_dt_bias, ssm_a_log, ssm_d, ssm_norm_g, w_branch_a, w_branch_b,
           w_branch_c, w_out, ffn_w1, ffn_w3, ffn_w2, moe_router_w, moe_router_b, moe_w1, moe_w3, moe_w2,
           final_norm_g):
    batch, seq, d = x.shape
    ctx_len = ctx.shape[1]
    depth = ada_w.shape[0]
    n_lat, n_ctx = batch * seq, batch * ctx_len
    n_all = n_lat + n_ctx
    tm = next(t for t in (1024, 512, 256) if n_ctx % t == 0 and seq % t == 0)
    n_mod = 16
    assert batch + 1 <= n_mod
    lat_blocks = n_lat // tm
    per_seq = seq // tm
    te = 256
    assert n_ctx % te == 0 and seq % te == 0
    mod_idx_fn = lambda i: jnp.where(i < n_lat // te, i // (seq // te), batch)
    mod_idx_all = jnp.where(jnp.arange(n_all // tm) < lat_blocks, jnp.arange(n_all // tm) // per_seq, batch)
    layer_blocks = lambda nb, idx: jnp.full((nb,), idx, jnp.int32)

    na_w, da_w = NA_HEADS * NA_HEAD_DIM, DA_HEADS * 2 * DA_HEAD_DIM
    conv_dim = SSM_D_INNER + 2 * SSM_GROUPS * SSM_STATE
    col_qa, col_ka, col_va = 0, na_w, 2 * na_w
    col_qb, col_kb, col_vb = 3 * na_w, 3 * na_w + da_w, 3 * na_w + 2 * da_w
    col_z = 3 * na_w + 3 * da_w
    col_x = col_z + SSM_D_INNER
    col_dt = col_x + conv_dim
    col_gate = col_dt + 2 * SSM_HEADS
    n_main = 6144
    assert col_dt + LANES <= n_main and n_main % 512 == 0 and col_dt % LANES == 0

    h = jnp.concatenate([x.reshape(n_lat, d), ctx.reshape(n_ctx, d)], axis=0)
    cond = jnp.zeros((n_mod, d), F32).at[:batch].set(jax.nn.silu(c)).at[batch].set(jax.nn.silu(c_ctx))
    cond = cond.astype(BF16)
    rope_tabs = _rope_tables(seq, tm)
    zero_state = jnp.zeros((batch, 2, SSM_STATE, SSM_D_INNER), F32)
    w_gate = w_in[:, :, col_gate:]

    out = None
    for li in range(depth):
        last = li == depth - 1
        lam_init = 0.8 - 0.6 * math.exp(-0.3 * li)
        n_out = n_lat if last else n_all
        mod = _gmm(cond, [ada_w], layer_blocks(1, li), tm=n_mod, tn=512, epi="bias",
                   bias=ada_b[li].reshape(1, 6 * d)).reshape(n_mod, 1, 6 * d)
        u = _norm(h, norm_mix_g[li], tm=te, n_rows=n_all, mod=mod, mod_idx_fn=mod_idx_fn, parts=(0, 1))
        proj = _gmm(u, [w_in], layer_blocks(n_all // tm, li), tm=tm, tn=512, n_cols=n_main)
        gates = _gmm(u, [w_gate], layer_blocks(n_out // tm, li), tm=tm, tn=512, n_rows=n_out)

        o_a = _na(proj, na_rpb[li], batch=batch, seq=seq, ctx_len=ctx_len, q_col=col_qa, k_col=col_ka, v_col=col_va)
        q_r, k_r = _rope(proj, rope_tabs, tm=tm, n_rows=n_all, n_lat_rows=n_lat, seq=seq,
                         q_col=col_qb, k_col=col_kb, width=da_w)
        o_b = _da(q_r, k_r, proj, da_lambda[li].astype(F32), da_subln_g[li], lam_init, batch=batch, seq=seq,
                  ctx_len=ctx_len, v_col=col_vb, tq=256, latent=True)
        ssd_args = (ssm_conv_w[li], ssm_conv_b[li], ssm_dt_bias[li], ssm_a_log[li], ssm_d[li])
        y_ctx, s_ctx = _ssd(proj, *ssd_args, zero_state, batch=batch, n_tok=ctx_len, row0=n_lat,
                            x_col=col_x, dt_col=col_dt)
        y_lat, _ = _ssd(proj, *ssd_args, s_ctx, batch=batch, n_tok=seq, row0=0, x_col=col_x, dt_col=col_dt)
        o_c = _ssm_norm(y_lat, proj, ssm_norm_g[li], tm=te, row0=0, z_col=col_z)
        if not last:
            o_a_c = _ctx_attn(proj, batch=batch, seq=seq, ctx_len=ctx_len, q_col=col_qa, k_col=col_ka, v_col=col_va)
            o_b_c = _da(q_r, k_r, proj, da_lambda[li].astype(F32), da_subln_g[li], lam_init, batch=batch, seq=seq,
                        ctx_len=ctx_len, v_col=col_vb, tq=ctx_len, latent=False)
            o_c_c = _ssm_norm(y_ctx, proj, ssm_norm_g[li], tm=te, row0=n_lat, z_col=col_z)
            o_a = jnp.concatenate([o_a, o_a_c], axis=0)
            o_b = jnp.concatenate([o_b, o_b_c], axis=0)
            o_c = jnp.concatenate([o_c, o_c_c], axis=0)
        merged = _merge(o_a, o_b, o_c, gates, w_branch_a, w_branch_b, w_branch_c, li, tm=tm, tn=512, n_rows=n_out)
        h = _gmm(merged, [w_out], layer_blocks(n_out // tm, li), tm=tm, tn=512, epi="resid", res=h,
                 mod=mod, mod_idx=mod_idx_all[:n_out // tm], mod_part=2, n_rows=n_out)

        j = li // 2
        if li % 2 == 0:
            tokens = _norm(h, norm_ffn_g[li], tm=te, n_rows=n_out, mod=mod, mod_idx_fn=mod_idx_fn, parts=(3, 4))
            hid = _gmm(tokens, [ffn_w1, ffn_w3], layer_blocks(n_out // tm, j), tm=tm, tn=512,
                       epi="swiglu", out_dtype=BF16)
            h = _gmm(hid, [ffn_w2], layer_blocks(n_out // tm, j), tm=tm, tn=512, nk=2, epi="resid", res=h,
                     mod=mod, mod_idx=mod_idx_all[:n_out // tm], mod_part=5, n_rows=n_out)
            if last:
                out = _norm(h, final_norm_g, tm=te, n_rows=n_lat, out_dtype=F32)
        else:
            rw = jnp.zeros((d, LANES), F32).at[:, :N_EXPERTS].set(moe_router_w[j].astype(F32))
            rb = jnp.zeros((1, LANES), F32).at[0, :N_EXPERTS].set(moe_router_b[j].astype(F32))
            tokens, logits = _norm(h, norm_ffn_g[li], tm=te, n_rows=n_out, mod=mod, mod_idx_fn=mod_idx_fn,
                                   parts=(3, 4), router=(rw, rb))
            gate, dest, slot_tok, block_expert = _route(logits[:, :N_EXPERTS], MOE_ROWS)
            block_expert = block_expert + j * N_EXPERTS
            stack = lambda w: w.reshape((-1,) + w.shape[2:])
            x_sorted = tokens[slot_tok]
            hid = _gmm(x_sorted, [stack(moe_w1), stack(moe_w3)], block_expert, tm=MOE_ROWS, tn=512, epi="swiglu",
                       out_dtype=BF16)
            y_sorted = _gmm(hid, [stack(moe_w2)], block_expert, tm=MOE_ROWS, tn=512, nk=2)
            f = jnp.sum((y_sorted[dest] * gate.reshape(-1)[:, None]).reshape(n_out, TOP_K, d), axis=1)
            h = _resid_norm(h, f, mod, mod_idx_fn, 5, final_norm_g, tm=te, n_rows=n_out, norm=last)
            if last:
                out = h
    return out.reshape(batch, seq, d)
```

```python
import functools
import math

import numpy as np
import jax
import jax.numpy as jnp
from jax import lax
from jax.experimental import pallas as pl
from jax.experimental.pallas import tpu as pltpu

F32 = jnp.float32
BF16 = jnp.bfloat16

GRID_W = 64
NA_HEADS, NA_HEAD_DIM, NA_WIN_H, NA_WIN_W = 8, 64, 8, 16
DA_HEADS, DA_HEAD_DIM = 4, 64
ROPE_BASE = 10000.0
SSM_D_INNER, SSM_HEAD_DIM, SSM_GROUPS, SSM_STATE, SSM_CONV_W = 1024, 64, 2, 128, 5
SSM_HEADS = SSM_D_INNER // SSM_HEAD_DIM
N_EXPERTS, TOP_K = 8, 2
EPS = 1e-6
NEG_INF = -1e30

LANES = 128
SUBLANES = 8
VMEM_LIMIT_BYTES = 56 * 1024 * 1024
SSD_CHUNK = 128
MOE_ROWS = 512


def _cparams(sem):
    return pltpu.CompilerParams(dimension_semantics=sem, vmem_limit_bytes=VMEM_LIMIT_BYTES)


def _dot(a, b):
    return jnp.dot(a, b, preferred_element_type=F32)


def _dot_nt(a, b):
    return lax.dot_general(a, b, (((1,), (1,)), ((), ())), preferred_element_type=F32)


def _silu(x):
    return x * jax.nn.sigmoid(x)


def _gmm_body(bexp_ref, neww_ref, mod_ref, nv_ref, x_ref, *rest, nk, n_w, epi):
    del bexp_ref, mod_ref
    w_refs, rest = rest[:n_w], rest[n_w:]
    if epi == "bias":
        bias_ref, rest = rest[0], rest[1:]
    elif epi == "resid":
        res_ref, gate_ref, rest = rest[0], rest[1], rest[2:]
    o_ref, rest = rest[0], rest[1:]
    wbf_refs, acc_refs = rest[:n_w], rest[n_w:]
    i = pl.program_id(1)
    k = pl.program_id(2)

    def finalize(vals):
        if epi == "swiglu":
            o_ref[...] = (_silu(vals[0]) * vals[1]).astype(o_ref.dtype)
        elif epi == "bias":
            o_ref[...] = (vals[0] + bias_ref[...]).astype(o_ref.dtype)
        elif epi == "resid":
            o_ref[...] = (res_ref[...] + gate_ref[...] * vals[0]).astype(o_ref.dtype)
        else:
            o_ref[...] = vals[0].astype(o_ref.dtype)

    @pl.when(jnp.logical_and(i >= nv_ref[0], k == nk - 1))
    def _():
        o_ref[...] = jnp.zeros_like(o_ref)

    @pl.when(i < nv_ref[0])
    def _():
        @pl.when(neww_ref[i] == 1)
        def _():
            for w_ref, wbf in zip(w_refs, wbf_refs):
                wbf[k] = w_ref[...].astype(BF16)

        x = x_ref[...]
        parts = [_dot(x, wbf[k]) for wbf in wbf_refs]
        if nk == 1:
            finalize(parts)
        else:
            @pl.when(k == 0)
            def _():
                for acc, p in zip(acc_refs, parts):
                    acc[...] = p

            @pl.when(k > 0)
            def _():
                for acc, p in zip(acc_refs, parts):
                    acc[...] += p

            @pl.when(k == nk - 1)
            def _():
                finalize([acc[...] for acc in acc_refs])


def _gmm(x, ws, bexp, *, tm, tn, nk=1, n_cols=None, epi="plain", out_dtype=F32, bias=None, res=None,
         mod=None, mod_idx=None, mod_part=0, n_rows=None, n_used=None, w_buffers=2):
    m_rows = x.shape[0] if n_rows is None else n_rows
    k_dim = x.shape[1]
    n_dim = ws[0].shape[2] if n_cols is None else n_cols
    n_w = len(ws)
    tk = k_dim // nk
    nb = m_rows // tm
    assert m_rows % tm == 0 and n_dim % tn == 0 and k_dim % nk == 0
    bexp = bexp.astype(jnp.int32)
    neww = jnp.concatenate([jnp.ones((1,), jnp.int32), (bexp[1:] != bexp[:-1]).astype(jnp.int32)])
    if mod_idx is None:
        mod_idx = jnp.zeros((nb,), jnp.int32)
    if n_used is None:
        n_used = jnp.full((1,), nb, jnp.int32)

    rb = lambda i, nv: jnp.minimum(i, nv[0] - 1)
    x_spec = pl.BlockSpec((tm, tk), lambda j, i, k, be, nw, md, nv: (rb(i, nv), k))
    w_map = lambda j, i, k, be, nw, md, nv: (be[rb(i, nv)], jnp.where(nw[rb(i, nv)] == 1, k, nk - 1), j)
    if w_buffers == 2:
        w_spec = pl.BlockSpec((None, tk, tn), w_map)
    else:
        w_spec = pl.BlockSpec((None, tk, tn), w_map, pipeline_mode=pl.Buffered(w_buffers))
    o_spec = pl.BlockSpec((tm, tn), lambda j, i, k, be, nw, md, nv: (i, j))
    in_specs = [x_spec] + [w_spec] * n_w
    args = [x] + list(ws)
    if epi == "bias":
        in_specs.append(pl.BlockSpec((1, tn), lambda j, i, k, be, nw, md, nv: (0, j)))
        args.append(bias)
    elif epi == "resid":
        part_off = mod_part * (n_dim // tn)
        in_specs.append(o_spec)
        in_specs.append(pl.BlockSpec((None, 1, tn),
                                     lambda j, i, k, be, nw, md, nv: (md[rb(i, nv)], 0, part_off + j)))
        args += [res, mod]
    scratch = [pltpu.VMEM((nk, tk, tn), BF16) for _ in range(n_w)]
    if nk > 1:
        scratch += [pltpu.VMEM((tm, tn), F32) for _ in range(n_w)]
    grid_spec = pltpu.PrefetchScalarGridSpec(
        num_scalar_prefetch=4, grid=(n_dim // tn, nb, nk), in_specs=in_specs, out_specs=o_spec,
        scratch_shapes=scratch)
    return pl.pallas_call(
        functools.partial(_gmm_body, nk=nk, n_w=n_w, epi=epi),
        grid_spec=grid_spec,
        out_shape=jax.ShapeDtypeStruct((m_rows, n_dim), out_dtype),
        compiler_params=_cparams(("arbitrary", "arbitrary", "arbitrary")),
        name="gmm_" + epi,
    )(bexp, neww, mod_idx.astype(jnp.int32), n_used.astype(jnp.int32), *args)


def _split2(v):
    hi = v.astype(BF16)
    lo = (v - hi.astype(F32)).astype(BF16)
    return hi, lo


def _norm_body(x_ref, g_ref, *rest, modulate, router):
    if modulate:
        shift_ref, scale_ref, rest = rest[0], rest[1], rest[2:]
    if router:
        rw_ref, rb_ref, rest = rest[0], rest[1], rest[2:]
    o_ref = rest[0]
    x = x_ref[...]
    y = x * lax.rsqrt(jnp.mean(x * x, axis=-1, keepdims=True) + EPS) * g_ref[...]
    if modulate:
        y = y * (1.0 + scale_ref[...]) + shift_ref[...]
    o_ref[...] = y.astype(o_ref.dtype)
    if router:
        rt_ref = rest[1]
        y_hi, y_lo = _split2(y)
        w_hi, w_lo = _split2(rw_ref[...])
        logits = _dot(y_hi, w_hi) + _dot(y_lo, w_hi) + _dot(y_hi, w_lo) + rb_ref[...]
        lane = lax.broadcasted_iota(jnp.int32, logits.shape, 1).astype(F32)
        lg = jnp.where(lane < N_EXPERTS, logits, -jnp.inf)
        m1 = jnp.max(lg, axis=-1, keepdims=True)
        i1 = jnp.min(jnp.where(lg == m1, lane, float(LANES)), axis=-1, keepdims=True)
        lg2 = jnp.where(lane == i1, -jnp.inf, lg)
        m2 = jnp.max(lg2, axis=-1, keepdims=True)
        i2 = jnp.min(jnp.where(lg2 == m2, lane, float(LANES)), axis=-1, keepdims=True)
        e2 = jnp.exp(m2 - m1)
        g1 = 1.0 / (1.0 + e2)
        rt = jnp.where(lane == 0, i1, jnp.where(lane == 1, i2, jnp.where(lane == 2, g1, jnp.where(lane == 3, e2 * g1, 0.0))))
        rt_ref[...] = rt


def _norm(x, g, *, tm, n_rows, mod=None, mod_idx_fn=None, parts=(0, 1), out_dtype=BF16, router=None):
    d = x.shape[1]
    nb = n_rows // tm
    row = pl.BlockSpec((tm, d), lambda i: (i, 0))
    in_specs = [row, pl.BlockSpec((1, d), lambda i: (0, 0))]
    args = [x, g.reshape(1, d)]
    modulate = mod is not None
    if modulate:
        for p in parts:
            in_specs.append(pl.BlockSpec((None, 1, d), lambda i, p=p: (mod_idx_fn(i), 0, p)))
            args.append(mod)
    out_shape = [jax.ShapeDtypeStruct((n_rows, d), out_dtype)]
    out_specs = [row]
    if router is not None:
        rw, rb = router
        in_specs += [pl.BlockSpec((d, LANES), lambda i: (0, 0)), pl.BlockSpec((1, LANES), lambda i: (0, 0))]
        args += [rw, rb]
        out_shape.append(jax.ShapeDtypeStruct((n_rows, LANES), F32))
        out_specs.append(pl.BlockSpec((tm, LANES), lambda i: (i, 0)))
    res = pl.pallas_call(
        functools.partial(_norm_body, modulate=modulate, router=router is not None),
        grid=(nb,), in_specs=in_specs, out_specs=out_specs, out_shape=out_shape,
        compiler_params=_cparams(("arbitrary",)), name="rmsnorm",
    )(*args)
    return res if router is not None else res[0]


def _rope_body(q_ref, k_ref, c_ref, sm_ref, sp_ref, qo_ref, ko_ref, *, q_scale):
    c, sm, sp = c_ref[...], sm_ref[...], sp_ref[...]
    width = q_ref.shape[1]
    for src, dst, scale in ((q_ref, qo_ref, q_scale), (k_ref, ko_ref, 1.0)):
        for g in range(width // LANES):
            sl = slice(g * LANES, (g + 1) * LANES)
            x = src[:, sl]
            y = x * c + pltpu.roll(x, LANES - 16, 1) * sm + pltpu.roll(x, 16, 1) * sp
            dst[:, sl] = (y * scale).astype(dst.dtype)


def _rope_tables(seq, n_id_rows):
    t = jnp.arange(seq)
    pos = jnp.stack([t // GRID_W, t % GRID_W], axis=-1).astype(F32)
    n_freq = DA_HEAD_DIM // 4
    inv_freq = ROPE_BASE ** (-jnp.arange(n_freq, dtype=F32) / n_freq)
    ang = pos[:, :, None] * inv_freq
    d = np.arange(LANES) % DA_HEAD_DIM
    kind, which, f = d // 32, (d % 32) // 16, d % 16
    a = ang[:, kind, f]
    cos, sin = jnp.cos(a), jnp.sin(a)
    sm = jnp.where(which == 0, -sin, 0.0)
    sp = jnp.where(which == 1, sin, 0.0)
    pad = lambda v, fill: jnp.concatenate([v, jnp.full((n_id_rows, LANES), fill, F32)], axis=0)
    return pad(cos, 1.0), pad(sm, 0.0), pad(sp, 0.0)


def _rope(proj, tabs, *, tm, n_rows, n_lat_rows, seq, q_col, k_col, width):
    lat_blocks = n_lat_rows // tm
    per_seq = seq // tm
    tab_idx = lambda i: (jnp.where(i < lat_blocks, i % per_seq, per_seq), 0)
    out = jax.ShapeDtypeStruct((n_rows, width), BF16)
    return pl.pallas_call(
        functools.partial(_rope_body, q_scale=DA_HEAD_DIM ** -0.5),
        grid=(n_rows // tm,),
        in_specs=[pl.BlockSpec((tm, width), lambda i: (i, q_col // width)),
                  pl.BlockSpec((tm, width), lambda i: (i, k_col // width)),
                  pl.BlockSpec((tm, LANES), tab_idx), pl.BlockSpec((tm, LANES), tab_idx),
                  pl.BlockSpec((tm, LANES), tab_idx)],
        out_specs=[pl.BlockSpec((tm, width), lambda i: (i, 0))] * 2,
        out_shape=[out, out], compiler_params=_cparams(("arbitrary",)), name="rope",
    )(proj, proj, *tabs)


def _na_body(q_ref, k_ref, v_ref, kc_ref, vc_ref, tb_ref, o_ref, kb, vb, kcb, vcb, *, rows, wh):
    kb[...] = k_ref[...].astype(BF16)
    vb[...] = v_ref[...].astype(BF16)
    kcb[...] = kc_ref[...].astype(BF16)
    vcb[...] = vc_ref[...].astype(BF16)
    lane = lax.broadcasted_iota(jnp.int32, (GRID_W, LANES), 1)
    first = lane < NA_HEAD_DIM
    scale = NA_HEAD_DIM ** -0.5

    def body(r, carry):
        rs = jnp.clip(r - wh // 2, 0, rows - wh)
        d0 = rs - r + (NA_WIN_H - 1)
        q = q_ref[pl.ds(pl.multiple_of(r * GRID_W, GRID_W), GRID_W), :] * scale
        kw = kb[pl.ds(pl.multiple_of(rs * GRID_W, GRID_W), wh * GRID_W), :]
        vw = vb[pl.ds(pl.multiple_of(rs * GRID_W, GRID_W), wh * GRID_W), :]
        outs = []
        for hh in range(2):
            qm = jnp.where(first if hh == 0 else jnp.logical_not(first), q, 0.0).astype(BF16)
            s_l = _dot_nt(qm, kw) + tb_ref[hh, d0]
            s_c = _dot_nt(qm, kcb[...])
            m = jnp.maximum(jnp.max(s_l, axis=-1, keepdims=True), jnp.max(s_c, axis=-1, keepdims=True))
            p_l = jnp.exp(s_l - m)
            p_c = jnp.exp(s_c - m)
            den = jnp.sum(p_l, axis=-1, keepdims=True) + jnp.sum(p_c, axis=-1, keepdims=True)
            o = _dot(p_l.astype(BF16), vw) + _dot(p_c.astype(BF16), vcb[...])
            outs.append(o / den)
        o_ref[pl.ds(pl.multiple_of(r * GRID_W, GRID_W), GRID_W), :] = (
            jnp.where(first, outs[0], outs[1]).astype(o_ref.dtype))
        return carry

    lax.fori_loop(0, rows, body, 0)


def _na_bias_table(rpb, rows):
    wh = min(NA_WIN_H, rows)
    qc = np.arange(GRID_W)[:, None]
    kc = np.arange(GRID_W)[None, :]
    ws = np.clip(qc - NA_WIN_W // 2, 0, GRID_W - NA_WIN_W)
    ok = (kc >= ws) & (kc < ws + NA_WIN_W)
    dcol = np.clip(kc - qc + NA_WIN_W - 1, 0, 2 * NA_WIN_W - 2)
    n_d0 = NA_WIN_H
    drow = np.clip(np.arange(n_d0)[:, None] + np.arange(wh)[None, :], 0, 2 * NA_WIN_H - 2)
    t = rpb.astype(F32)[:, drow]
    t = t[:, :, :, dcol]
    t = jnp.where(ok, t, NEG_INF)
    t = t.transpose(0, 1, 3, 2, 4).reshape(NA_HEADS, n_d0, GRID_W, wh * GRID_W)
    return t.reshape(NA_HEADS // 2, 2, n_d0, GRID_W, wh * GRID_W)


def _na(proj, rpb, *, batch, seq, ctx_len, q_col, k_col, v_col):
    rows = seq // GRID_W
    wh = min(NA_WIN_H, rows)
    pairs = NA_HEADS // 2
    tb = _na_bias_table(rpb, rows)
    ctx0 = batch * seq // ctx_len
    lat = lambda col: pl.BlockSpec((seq, LANES), lambda b, p: (b, col // LANES + p))
    ctx = lambda col: pl.BlockSpec((ctx_len, LANES), lambda b, p: (ctx0 + b, col // LANES + p))
    return pl.pallas_call(
        functools.partial(_na_body, rows=rows, wh=wh),
        grid=(batch, pairs),
        in_specs=[lat(q_col), lat(k_col), lat(v_col), ctx(k_col), ctx(v_col),
                  pl.BlockSpec((None, 2, NA_WIN_H, GRID_W, wh * GRID_W), lambda b, p: (p, 0, 0, 0, 0))],
        out_specs=pl.BlockSpec((seq, LANES), lambda b, p: (b, p)),
        out_shape=jax.ShapeDtypeStruct((batch * seq, pairs * LANES), BF16),
        scratch_shapes=[pltpu.VMEM((seq, LANES), BF16), pltpu.VMEM((seq, LANES), BF16),
                        pltpu.VMEM((ctx_len, LANES), BF16), pltpu.VMEM((ctx_len, LANES), BF16)],
        compiler_params=_cparams(("arbitrary", "arbitrary")), name="na_attn",
    )(proj, proj, proj, proj, proj, tb)


def _ctx_attn_body(q_ref, k_ref, v_ref, o_ref):
    kb = k_ref[...].astype(BF16)
    vb = v_ref[...].astype(BF16)
    q = q_ref[...] * (NA_HEAD_DIM ** -0.5)
    lane = lax.broadcasted_iota(jnp.int32, q.shape, 1)
    first = lane < NA_HEAD_DIM
    outs = []
    for hh in range(2):
        qm = jnp.where(first if hh == 0 else jnp.logical_not(first), q, 0.0).astype(BF16)
        s = _dot_nt(qm, kb)
        p = jnp.exp(s - jnp.max(s, axis=-1, keepdims=True))
        outs.append(_dot(p.astype(BF16), vb) / jnp.sum(p, axis=-1, keepdims=True))
    o_ref[...] = jnp.where(first, outs[0], outs[1]).astype(o_ref.dtype)


def _ctx_attn(proj, *, batch, seq, ctx_len, q_col, k_col, v_col):
    pairs = NA_HEADS // 2
    ctx0 = batch * seq // ctx_len
    ctx = lambda col: pl.BlockSpec((ctx_len, LANES), lambda b, p: (ctx0 + b, col // LANES + p))
    return pl.pallas_call(
        _ctx_attn_body, grid=(batch, pairs),
        in_specs=[ctx(q_col), ctx(k_col), ctx(v_col)],
        out_specs=pl.BlockSpec((ctx_len, LANES), lambda b, p: (b, p)),
        out_shape=jax.ShapeDtypeStruct((batch * ctx_len, pairs * LANES), BF16),
        compiler_params=_cparams(("arbitrary", "arbitrary")), name="ctx_attn",
    )(proj, proj, proj)


def _da_body(lv_ref, q_ref, *rest, lam_init, has_lat):
    if has_lat:
        kl_ref, vl_ref, rest = rest[0], rest[1], rest[2:]
    kc_ref, vc_ref, g_ref, o_ref = rest[:4]
    scr = rest[4:]
    t = pl.program_id(2)
    if has_lat:
        vlb, vcb = scr

        @pl.when(t == 0)
        def _():
            vlb[...] = vl_ref[...].astype(BF16)
            vcb[...] = vc_ref[...].astype(BF16)
    else:
        (vcb,) = scr
        vcb[...] = vc_ref[...].astype(BF16)

    lv = lv_ref[...]
    lam = (jnp.exp(jnp.sum(lv[0:1] * lv[1:2], axis=-1, keepdims=True))
           - jnp.exp(jnp.sum(lv[2:3] * lv[3:4], axis=-1, keepdims=True)) + lam_init)
    q = q_ref[...]
    tq = q.shape[0]
    lane = lax.broadcasted_iota(jnp.int32, q.shape, 1)
    zero = jnp.zeros_like(q)
    qq = jnp.concatenate([jnp.where(lane < DA_HEAD_DIM, q, zero), jnp.where(lane >= DA_HEAD_DIM, q, zero)], axis=0)
    s_c = _dot_nt(qq, kc_ref[...])
    m = jnp.max(s_c, axis=-1, keepdims=True)
    if has_lat:
        s_l = _dot_nt(qq, kl_ref[...])
        m = jnp.maximum(m, jnp.max(s_l, axis=-1, keepdims=True))
        p_l = jnp.exp(s_l - m)
    p_c = jnp.exp(s_c - m)
    den = jnp.sum(p_c, axis=-1, keepdims=True)
    if has_lat:
        den = den + jnp.sum(p_l, axis=-1, keepdims=True)
    inv = 1.0 / den
    p_c = p_c * inv
    a_c = (p_c[:tq] - lam * p_c[tq:]).astype(BF16)
    o = _dot(a_c, vcb[...])
    if has_lat:
        p_l = p_l * inv
        a_l = (p_l[:tq] - lam * p_l[tq:]).astype(BF16)
        o = o + _dot(a_l, vlb[...])
    o = o * lax.rsqrt(jnp.mean(o * o, axis=-1, keepdims=True) + EPS) * g_ref[...]
    o_ref[...] = (o * (1.0 - lam_init)).astype(o_ref.dtype)


def _da(q_r, k_r, proj, lam_vec, subln_g, lam_init, *, batch, seq, ctx_len, v_col, tq, latent):
    ctx0 = batch * seq // ctx_len
    vcb0 = v_col // LANES
    g2 = subln_g.reshape(1, 2 * DA_HEAD_DIM).astype(F32)
    small = [pl.BlockSpec((4, DA_HEAD_DIM), lambda b, h, t: (0, 0))]
    gspec = pl.BlockSpec((1, LANES), lambda b, h, t: (0, 0))
    kc = pl.BlockSpec((ctx_len, LANES), lambda b, h, t: (ctx0 + b, h))
    vc = pl.BlockSpec((ctx_len, LANES), lambda b, h, t: (ctx0 + b, vcb0 + h))
    if latent:
        nq = seq // tq
        qs = pl.BlockSpec((tq, LANES), lambda b, h, t: (b * nq + t, h))
        kl = pl.BlockSpec((seq, LANES), lambda b, h, t: (b, h))
        vl = pl.BlockSpec((seq, LANES), lambda b, h, t: (b, vcb0 + h))
        in_specs = small + [qs, kl, vl, kc, vc, gspec]
        args = (lam_vec, q_r, k_r, proj, k_r, proj, g2)
        scratch = [pltpu.VMEM((seq, LANES), BF16), pltpu.VMEM((ctx_len, LANES), BF16)]
        n_out = batch * seq
    else:
        nq = ctx_len // tq
        cq0 = batch * seq // tq
        qs = pl.BlockSpec((tq, LANES), lambda b, h, t: (cq0 + b * nq + t, h))
        in_specs = small + [qs, kc, vc, gspec]
        args = (lam_vec, q_r, k_r, proj, g2)
        scratch = [pltpu.VMEM((ctx_len, LANES), BF16)]
        n_out = batch * ctx_len
    return pl.pallas_call(
        functools.partial(_da_body, lam_init=lam_init, has_lat=latent),
        grid=(batch, DA_HEADS, nq), in_specs=in_specs,
        out_specs=pl.BlockSpec((tq, LANES), lambda b, h, t: (b * nq + t, h)),
        out_shape=jax.ShapeDtypeStruct((n_out, DA_HEADS * LANES), BF16),
        scratch_shapes=scratch,
        compiler_params=_cparams(("arbitrary", "arbitrary", "arbitrary")),
        name="diff_attn_lat" if latent else "diff_attn_ctx",
    )(*args)


def _ssd_body(x0_ref, x1_ref, bc_ref, x0p_ref, x1p_ref, bcp_ref, x0n_ref, x1n_ref, bcn_ref, dt_ref,
              cw_ref, cb_ref, prm_ref, dsk_ref, ex_ref, init_ref, y_ref, sout_ref, st_ref, *, nc):
    q_len = SSD_CHUNK
    dirn = pl.program_id(1)
    z = pl.program_id(2)
    zz = jnp.where(dirn == 0, z, nc - 1 - z)
    fwd = dirn == 0
    half = SSM_D_INNER // SSM_GROUPS
    heads_per_group = SSM_HEADS // SSM_GROUPS

    @pl.when(z == 0)
    def _():
        st_ref[...] = init_ref[...]

    def conv_piece(main_ref, prev_ref, next_ref, c0):
        width = main_ref.shape[1]
        prev = jnp.where(zz == 0, 0.0, prev_ref[...])
        nxt = jnp.where(zz == nc - 1, 0.0, next_ref[...])
        ext = jnp.concatenate([prev, main_ref[...], nxt], axis=0)
        n_ext = ext.shape[0]
        acc = jnp.zeros((q_len, width), F32) + cb_ref[:, c0:c0 + width]
        for tap in range(SSM_CONV_W):
            sh = (SSM_CONV_W // 2 - tap) % n_ext
            e = ext if sh == 0 else pltpu.roll(ext, sh, 0)
            acc = acc + e[SUBLANES:SUBLANES + q_len] * cw_ref[tap:tap + 1, c0:c0 + width]
        return _silu(acc)

    xs = (conv_piece(x0_ref, x0p_ref, x0n_ref, 0), conv_piece(x1_ref, x1p_ref, x1n_ref, half))
    bcm = conv_piece(bc_ref, bcp_ref, bcn_ref, SSM_D_INNER)

    raw = dt_ref[...]
    raw = jnp.where(fwd, raw, pltpu.roll(raw, LANES - SSM_HEADS, 1))
    xb = raw + prm_ref[0:1, :]
    dtv = jnp.maximum(xb, 0.0) + jnp.log1p(jnp.exp(-jnp.abs(xb)))
    a = dtv * (-jnp.exp(prm_ref[1:2, :]))
    rowi = lax.broadcasted_iota(jnp.int32, (q_len, LANES), 0)
    coli = lax.broadcasted_iota(jnp.int32, (q_len, LANES), 1)
    cum = a
    sft = 1
    while sft < q_len:
        cum = cum + jnp.where(rowi >= sft, pltpu.roll(cum, sft, 0), 0.0)
        sft *= 2
    tot = cum[q_len - 1:q_len, :]
    g = jnp.where(fwd, cum, cum - a)
    g_t = g.T
    e_g = jnp.exp(g)
    e_tg = jnp.exp(tot - g)
    w_state = jnp.where(fwd, e_tg, e_g)
    w_yoff = jnp.where(fwd, e_g, e_tg)
    e_tot = jnp.exp(tot)
    sgn = jnp.where(fwd, 1.0, -1.0)
    tri = jnp.where(fwd, rowi - coli, coli - rowi) >= 0

    stack = jnp.concatenate([dtv, dtv * w_state, w_yoff, jnp.broadcast_to(e_tot, (SUBLANES, LANES))], axis=0)
    s_hi, s_lo = _split2(stack)
    ex = ex_ref[...]
    expd = _dot(s_hi, ex) + _dot(s_lo, ex)
    dt_e, dts_e, wy_e = expd[0:q_len], expd[q_len:2 * q_len], expd[2 * q_len:3 * q_len]
    tot_e = expd[3 * q_len:3 * q_len + 1]

    lane = lax.broadcasted_iota(jnp.int32, (q_len, LANES), 1)
    first = lane < SSM_HEAD_DIM
    for grp in range(SSM_GROUPS):
        x_g = xs[grp]
        csl = slice(grp * half, (grp + 1) * half)
        b_g = bcm[:, grp * SSM_STATE:(grp + 1) * SSM_STATE]
        c_g = bcm[:, (SSM_GROUPS + grp) * SSM_STATE:(SSM_GROUPS + grp + 1) * SSM_STATE].astype(BF16)
        xdt = (x_g * dt_e[:, csl]).astype(BF16)
        xdts = (x_g * dts_e[:, csl]).astype(BF16)
        cb = _dot_nt(c_g, b_g.astype(BF16))
        st_g = st_ref[:, csl]
        y_g = _dot(c_g, st_g.astype(BF16)) * wy_e[:, csl] + dsk_ref[:, csl] * x_g
        for pr in range(heads_per_group // 2):
            outs = []
            for hh in range(2):
                col = grp * heads_per_group + 2 * pr + hh
                seg = sgn * (g[:, col:col + 1] - g_t[col:col + 1, :])
                mat = (cb * jnp.where(tri, jnp.exp(seg), 0.0)).astype(BF16)
                outs.append(_dot(mat, xdt[:, pr * LANES:(pr + 1) * LANES]))
            lo = grp * half + pr * LANES
            y_ref[:, lo:lo + LANES] = y_g[:, pr * LANES:(pr + 1) * LANES] + jnp.where(first, outs[0], outs[1])
        st_ref[:, csl] = tot_e[:, csl] * st_g + _dot(b_g.T.astype(BF16), xdts)

    @pl.when(z == nc - 1)
    def _():
        sout_ref[...] = st_ref[...]


def _ssd(proj, conv_w, conv_b, dt_bias, a_log, d_skip, init, *, batch, n_tok, row0, x_col, dt_col):
    q_len = SSD_CHUNK
    nc = n_tok // q_len
    half = SSM_D_INNER // SSM_GROUPS
    conv_dim = SSM_D_INNER + 2 * SSM_GROUPS * SSM_STATE
    rb0 = row0 // q_len
    hb = q_len // SUBLANES
    n_halo = proj.shape[0] // SUBLANES
    zz = lambda d, z: jnp.where(d == 0, z, nc - 1 - z)
    rb = lambda b, d, z: rb0 + b * nc + zz(d, z)
    main = lambda c: pl.BlockSpec((q_len, half), lambda b, d, z: (rb(b, d, z), c))
    prev = lambda c: pl.BlockSpec((SUBLANES, half), lambda b, d, z: (jnp.maximum(rb(b, d, z) * hb - 1, 0), c))
    nxt = lambda c: pl.BlockSpec((SUBLANES, half),
                                 lambda b, d, z: (jnp.minimum(rb(b, d, z) * hb + hb, n_halo - 1), c))
    cols = [x_col // half + j for j in range(conv_dim // half)]
    const2 = lambda shape: pl.BlockSpec(shape, lambda b, d, z: (0, 0))
    prm = jnp.zeros((2, SUBLANES, LANES), F32)
    prm = prm.at[:, 0, :SSM_HEADS].set(dt_bias.astype(F32)).at[:, 1, :SSM_HEADS].set(a_log.astype(F32))
    dsk = jnp.repeat(d_skip.astype(F32), SSM_HEAD_DIM, axis=-1).reshape(2, 1, SSM_D_INNER)
    ex = (np.arange(LANES)[:, None] == (np.arange(SSM_D_INNER) // SSM_HEAD_DIM)[None, :])
    ex = jnp.asarray(ex, BF16)
    cw = jnp.zeros((SUBLANES, conv_dim), F32).at[:SSM_CONV_W].set(conv_w.astype(F32))
    state_spec = pl.BlockSpec((None, None, SSM_STATE, SSM_D_INNER), lambda b, d, z: (b, d, 0, 0))
    y, s_out = pl.pallas_call(
        functools.partial(_ssd_body, nc=nc),
        grid=(batch, 2, nc),
        in_specs=[main(cols[0]), main(cols[1]), main(cols[2]), prev(cols[0]), prev(cols[1]), prev(cols[2]),
                  nxt(cols[0]), nxt(cols[1]), nxt(cols[2]),
                  pl.BlockSpec((q_len, LANES), lambda b, d, z: (rb(b, d, z), dt_col // LANES)),
                  const2((SUBLANES, conv_dim)), const2((1, conv_dim)),
                  pl.BlockSpec((None, SUBLANES, LANES), lambda b, d, z: (d, 0, 0)),
                  pl.BlockSpec((None, 1, SSM_D_INNER), lambda b, d, z: (d, 0, 0)),
                  const2((LANES, SSM_D_INNER)), state_spec],
        out_specs=[pl.BlockSpec((None, q_len, SSM_D_INNER), lambda b, d, z: (d, b * nc + zz(d, z), 0)),
                   state_spec],
        out_shape=[jax.ShapeDtypeStruct((2, batch * n_tok, SSM_D_INNER), F32),
                   jax.ShapeDtypeStruct((batch, 2, SSM_STATE, SSM_D_INNER), F32)],
        scratch_shapes=[pltpu.VMEM((SSM_STATE, SSM_D_INNER), F32)],
        compiler_params=_cparams(("arbitrary", "arbitrary", "arbitrary")), name="ssd_scan",
    )(proj, proj, proj, proj, proj, proj, proj, proj, proj, proj, cw, conv_b.reshape(1, conv_dim).astype(F32),
      prm, dsk, ex, init)
    return y, s_out


def _ssm_norm_body(y_ref, z_ref, g_ref, o_ref):
    yz = (y_ref[0] + y_ref[1]) * _silu(z_ref[...])
    gw = SSM_D_INNER // SSM_GROUPS
    for grp in range(SSM_GROUPS):
        v = yz[:, grp * gw:(grp + 1) * gw]
        v = v * lax.rsqrt(jnp.mean(v * v, axis=-1, keepdims=True) + EPS)
        o_ref[:, grp * gw:(grp + 1) * gw] = (v * g_ref[:, grp * gw:(grp + 1) * gw]).astype(o_ref.dtype)


def _ssm_norm(y, proj, g, *, tm, row0, z_col):
    n = y.shape[1]
    zb0 = row0 // tm
    return pl.pallas_call(
        _ssm_norm_body, grid=(n // tm,),
        in_specs=[pl.BlockSpec((2, tm, SSM_D_INNER), lambda i: (0, i, 0)),
                  pl.BlockSpec((tm, SSM_D_INNER), lambda i: (zb0 + i, z_col // SSM_D_INNER)),
                  pl.BlockSpec((1, SSM_D_INNER), lambda i: (0, 0))],
        out_specs=pl.BlockSpec((tm, SSM_D_INNER), lambda i: (i, 0)),
        out_shape=jax.ShapeDtypeStruct((n, SSM_D_INNER), BF16),
        compiler_params=_cparams(("arbitrary",)), name="ssm_gated_norm",
    )(y, proj, g.reshape(1, SSM_D_INNER).astype(F32))


def _merge_body(oa_ref, ob_ref, oc_ref, ga_ref, gb_ref, gc_ref, wa_ref, wb_ref, wc_ref, o_ref, wab, wbb, wcb):
    @pl.when(pl.program_id(1) == 0)
    def _():
        wab[...] = wa_ref[...].astype(BF16)
        wbb[...] = wb_ref[...].astype(BF16)
        wcb[...] = wc_ref[...].astype(BF16)

    acc = jax.nn.sigmoid(ga_ref[...]) * _dot(oa_ref[...], wab[...])
    acc = acc + jax.nn.sigmoid(gb_ref[...]) * _dot(ob_ref[...], wbb[...])
    acc = acc + jax.nn.sigmoid(gc_ref[...]) * _dot(oc_ref[...], wcb[...])
    o_ref[...] = acc.astype(o_ref.dtype)


def _merge(o_a, o_b, o_c, gates, w_a, w_b, w_c, layer, *, tm, tn, n_rows):
    d = w_a.shape[2]
    nj = d // tn
    row = lambda w: pl.BlockSpec((tm, w), lambda j, i: (i, 0))
    gate = lambda x: pl.BlockSpec((tm, tn), lambda j, i: (i, x * nj + j))
    wsp = lambda kk: pl.BlockSpec((None, kk, tn), lambda j, i: (layer, 0, j))
    ka, kb_, kc = w_a.shape[1], w_b.shape[1], w_c.shape[1]
    return pl.pallas_call(
        _merge_body, grid=(nj, n_rows // tm),
        in_specs=[row(ka), row(kb_), row(kc), gate(0), gate(1), gate(2), wsp(ka), wsp(kb_), wsp(kc)],
        out_specs=pl.BlockSpec((tm, tn), lambda j, i: (i, j)),
        out_shape=jax.ShapeDtypeStruct((n_rows, d), BF16),
        scratch_shapes=[pltpu.VMEM((ka, tn), BF16), pltpu.VMEM((kb_, tn), BF16), pltpu.VMEM((kc, tn), BF16)],
        compiler_params=_cparams(("arbitrary", "arbitrary")), name="branch_merge",
    )(o_a, o_b, o_c, gates, gates, gates, w_a, w_b, w_c)


def _route(top_idx, blk):
    n = top_idx.shape[0]
    e_flat = top_idx.reshape(-1)
    onehot = (e_flat[:, None] == jnp.arange(N_EXPERTS, dtype=jnp.int32)[None, :]).astype(jnp.int32)
    counts = jnp.sum(onehot, axis=0)
    rank = jnp.sum((jnp.cumsum(onehot, axis=0) - onehot) * onehot, axis=1)
    padded = (counts + blk - 1) // blk * blk
    pad_end = jnp.cumsum(padded)
    pad_start = pad_end - padded
    dest = jnp.sum(onehot * pad_start[None, :], axis=1) + rank
    n_blocks = -(-(n * TOP_K) // blk) + N_EXPERTS
    cap = n_blocks * blk
    tok_flat = jnp.repeat(jnp.arange(n, dtype=jnp.int32), TOP_K)
    slot_tok = jnp.zeros((cap,), jnp.int32).at[dest].set(tok_flat)
    blk_start = jnp.arange(n_blocks, dtype=jnp.int32) * blk
    block_expert = jnp.clip(jnp.sum((pad_end[None, :] <= blk_start[:, None]).astype(jnp.int32), axis=1),
                            0, N_EXPERTS - 1)
    n_used = (pad_end[-1] // blk).astype(jnp.int32).reshape(1)
    return dest.astype(jnp.int32), slot_tok, block_expert, n_used


GATHER_UNROLL = 8


def _row_copy(src_hbm, row, buf, slot, sem):
    return pltpu.make_async_copy(src_hbm.at[pl.ds(row, 1), :], buf.at[pl.ds(slot, 1), :], sem)


def _gather_body(nv_ref, idx_ref, src_hbm, o_ref, buf, sem, *, rows):
    @pl.when(pl.program_id(0) >= nv_ref[0])
    def _():
        o_ref[...] = jnp.zeros_like(o_ref)

    @pl.when(pl.program_id(0) < nv_ref[0])
    def _():
        def issue(g, c):
            for u in range(GATHER_UNROLL):
                r = g * GATHER_UNROLL + u
                _row_copy(src_hbm, idx_ref[0, r], buf, r, sem).start(priority=u % 2)
            return c

        def drain(g, c):
            for u in range(GATHER_UNROLL):
                r = g * GATHER_UNROLL + u
                _row_copy(src_hbm, idx_ref[0, r], buf, r, sem).wait()
            return c

        lax.fori_loop(0, rows // GATHER_UNROLL, issue, 0)
        lax.fori_loop(0, rows // GATHER_UNROLL, drain, 0)
        o_ref[...] = buf[...].astype(o_ref.dtype)


def _gather_rows(src, idx, n_used, *, rows, out_dtype):
    n, d = src.shape
    nb = idx.shape[0] // rows
    assert rows % GATHER_UNROLL == 0
    blk = lambda i, nv: jnp.minimum(i, nv[0] - 1)
    grid_spec = pltpu.PrefetchScalarGridSpec(
        num_scalar_prefetch=1, grid=(nb,),
        in_specs=[pl.BlockSpec((None, 1, rows), lambda i, nv: (blk(i, nv), 0, 0), memory_space=pltpu.SMEM),
                  pl.BlockSpec(memory_space=pl.ANY)],
        out_specs=pl.BlockSpec((rows, d), lambda i, nv: (i, 0)),
        scratch_shapes=[pltpu.VMEM((rows, d), src.dtype), pltpu.SemaphoreType.DMA])
    return pl.pallas_call(
        functools.partial(_gather_body, rows=rows), grid_spec=grid_spec,
        out_shape=jax.ShapeDtypeStruct((nb * rows, d), out_dtype),
        compiler_params=_cparams(("arbitrary",)), name="moe_dispatch",
    )(n_used, idx.reshape(nb, 1, rows), src)


def _combine_body(dest_ref, h_ref, rt_ref, gate_ref, g_ref, y_hbm, o_ref, buf, sem, *, rows, norm):
    def issue(g, c):
        for u in range(GATHER_UNROLL):
            r = g * GATHER_UNROLL + u
            for k in range(TOP_K):
                _row_copy(y_hbm, dest_ref[0, r * TOP_K + k], buf.at[k], r, sem).start(priority=k)
        return c

    def drain(g, c):
        for u in range(GATHER_UNROLL):
            r = g * GATHER_UNROLL + u
            for k in range(TOP_K):
                _row_copy(y_hbm, dest_ref[0, r * TOP_K + k], buf.at[k], r, sem).wait()
        return c

    lax.fori_loop(0, rows // GATHER_UNROLL, issue, 0)
    lax.fori_loop(0, rows // GATHER_UNROLL, drain, 0)
    rt = rt_ref[...]
    f = buf[0] * rt[:, TOP_K:TOP_K + 1] + buf[1] * rt[:, TOP_K + 1:TOP_K + 2]
    x = h_ref[...] + gate_ref[...] * f
    if norm:
        x = x * lax.rsqrt(jnp.mean(x * x, axis=-1, keepdims=True) + EPS) * g_ref[...]
    o_ref[...] = x


def _combine(h, y_sorted, dest, rt, mod, mod_idx_fn, part, g, *, rows, n_rows, norm):
    d = h.shape[1]
    nb = n_rows // rows
    row = pl.BlockSpec((rows, d), lambda i: (i, 0))
    return pl.pallas_call(
        functools.partial(_combine_body, rows=rows, norm=norm), grid=(nb,),
        in_specs=[pl.BlockSpec((None, 1, rows * TOP_K), lambda i: (i, 0, 0), memory_space=pltpu.SMEM),
                  row, pl.BlockSpec((rows, LANES), lambda i: (i, 0)),
                  pl.BlockSpec((None, 1, d), lambda i: (mod_idx_fn(i), 0, part)),
                  pl.BlockSpec((1, d), lambda i: (0, 0)), pl.BlockSpec(memory_space=pl.ANY)],
        out_specs=row, out_shape=jax.ShapeDtypeStruct((n_rows, d), F32),
        scratch_shapes=[pltpu.VMEM((TOP_K, rows, d), y_sorted.dtype), pltpu.SemaphoreType.DMA],
        compiler_params=_cparams(("arbitrary",)), name="moe_combine",
    )(dest.reshape(nb, 1, rows * TOP_K), h, rt, mod, g.reshape(1, d), y_sorted)


def kernel(x, c, ctx, c_ctx, ada_w, ada_b, norm_mix_g, norm_ffn_g, w_in, na_rpb, da_lambda, da_subln_g,
           ssm_conv_w, ssm_conv_b, ssm_dt_bias, ssm_a_log, ssm_d, ssm_norm_g, w_branch_a, w_branch_b,
           w_branch_c, w_out, ffn_w1, ffn_w3, ffn_w2, moe_router_w, moe_router_b, moe_w1, moe_w3, moe_w2,
           final_norm_g):
    batch, seq, d = x.shape
    ctx_len = ctx.shape[1]
    depth = ada_w.shape[0]
    n_lat, n_ctx = batch * seq, batch * ctx_len
    n_all = n_lat + n_ctx
    tm = next(t for t in (1024, 512, 256) if n_ctx % t == 0 and seq % t == 0)
    n_mod = 16
    assert batch + 1 <= n_mod
    lat_blocks = n_lat // tm
    per_seq = seq // tm
    te = 256
    assert n_ctx % te == 0 and seq % te == 0
    mod_idx_fn = lambda i: jnp.where(i < n_lat // te, i // (seq // te), batch)
    mod_idx_all = jnp.where(jnp.arange(n_all // tm) < lat_blocks, jnp.arange(n_all // tm) // per_seq, batch)
    layer_blocks = lambda nb, idx: jnp.full((nb,), idx, jnp.int32)

    na_w, da_w = NA_HEADS * NA_HEAD_DIM, DA_HEADS * 2 * DA_HEAD_DIM
    conv_dim = SSM_D_INNER + 2 * SSM_GROUPS * SSM_STATE
    col_qa, col_ka, col_va = 0, na_w, 2 * na_w
    col_qb, col_kb, col_vb = 3 * na_w, 3 * na_w + da_w, 3 * na_w + 2 * da_w
    col_z = 3 * na_w + 3 * da_w
    col_x = col_z + SSM_D_INNER
    col_dt = col_x + conv_dim
    col_gate = col_dt + 2 * SSM_HEADS
    n_main = 6144
    assert col_dt + LANES <= n_main and n_main % 512 == 0 and col_dt % LANES == 0

    h = jnp.concatenate([x.reshape(n_lat, d), ctx.reshape(n_ctx, d)], axis=0)
    cond = jnp.zeros((n_mod, d), F32).at[:batch].set(jax.nn.silu(c)).at[batch].set(jax.nn.silu(c_ctx))
    cond = cond.astype(BF16)
    rope_tabs = _rope_tables(seq, tm)
    zero_state = jnp.zeros((batch, 2, SSM_STATE, SSM_D_INNER), F32)
    w_gate = w_in[:, :, col_gate:]

    out = None
    for li in range(depth):
        last = li == depth - 1
        lam_init = 0.8 - 0.6 * math.exp(-0.3 * li)
        n_out = n_lat if last else n_all
        mod = _gmm(cond, [ada_w], layer_blocks(1, li), tm=n_mod, tn=512, epi="bias",
                   bias=ada_b[li].reshape(1, 6 * d)).reshape(n_mod, 1, 6 * d)
        u = _norm(h, norm_mix_g[li], tm=te, n_rows=n_all, mod=mod, mod_idx_fn=mod_idx_fn, parts=(0, 1))
        proj = _gmm(u, [w_in], layer_blocks(n_all // tm, li), tm=tm, tn=1024, n_cols=n_main)
        gates = _gmm(u, [w_gate], layer_blocks(n_out // tm, li), tm=tm, tn=1024, n_rows=n_out)

        o_a = _na(proj, na_rpb[li], batch=batch, seq=seq, ctx_len=ctx_len, q_col=col_qa, k_col=col_ka, v_col=col_va)
        q_r, k_r = _rope(proj, rope_tabs, tm=tm, n_rows=n_all, n_lat_rows=n_lat, seq=seq,
                         q_col=col_qb, k_col=col_kb, width=da_w)
        o_b = _da(q_r, k_r, proj, da_lambda[li].astype(F32), da_subln_g[li], lam_init, batch=batch, seq=seq,
                  ctx_len=ctx_len, v_col=col_vb, tq=256, latent=True)
        ssd_args = (ssm_conv_w[li], ssm_conv_b[li], ssm_dt_bias[li], ssm_a_log[li], ssm_d[li])
        y_ctx, s_ctx = _ssd(proj, *ssd_args, zero_state, batch=batch, n_tok=ctx_len, row0=n_lat,
                            x_col=col_x, dt_col=col_dt)
        y_lat, _ = _ssd(proj, *ssd_args, s_ctx, batch=batch, n_tok=seq, row0=0, x_col=col_x, dt_col=col_dt)
        o_c = _ssm_norm(y_lat, proj, ssm_norm_g[li], tm=te, row0=0, z_col=col_z)
        if not last:
            o_a_c = _ctx_attn(proj, batch=batch, seq=seq, ctx_len=ctx_len, q_col=col_qa, k_col=col_ka, v_col=col_va)
            o_b_c = _da(q_r, k_r, proj, da_lambda[li].astype(F32), da_subln_g[li], lam_init, batch=batch, seq=seq,
                        ctx_len=ctx_len, v_col=col_vb, tq=ctx_len, latent=False)
            o_c_c = _ssm_norm(y_ctx, proj, ssm_norm_g[li], tm=te, row0=n_lat, z_col=col_z)
            o_a = jnp.concatenate([o_a, o_a_c], axis=0)
            o_b = jnp.concatenate([o_b, o_b_c], axis=0)
            o_c = jnp.concatenate([o_c, o_c_c], axis=0)
        merged = _merge(o_a, o_b, o_c, gates, w_branch_a, w_branch_b, w_branch_c, li, tm=tm, tn=512, n_rows=n_out)
        h = _gmm(merged, [w_out], layer_blocks(n_out // tm, li), tm=tm, tn=1024, epi="resid", res=h,
                 mod=mod, mod_idx=mod_idx_all[:n_out // tm], mod_part=2, n_rows=n_out)

        j = li // 2
        if li % 2 == 0:
            tokens = _norm(h, norm_ffn_g[li], tm=te, n_rows=n_out, mod=mod, mod_idx_fn=mod_idx_fn, parts=(3, 4))
            hid = _gmm(tokens, [ffn_w1, ffn_w3], layer_blocks(n_out // tm, j), tm=tm, tn=512,
                       epi="swiglu", out_dtype=BF16)
            h = _gmm(hid, [ffn_w2], layer_blocks(n_out // tm, j), tm=tm, tn=512, epi="resid", res=h,
                     mod=mod, mod_idx=mod_idx_all[:n_out // tm], mod_part=5, n_rows=n_out, w_buffers=1)
            if last:
                out = _norm(h, final_norm_g, tm=te, n_rows=n_lat, out_dtype=F32)
        else:
            rw = jnp.zeros((d, LANES), F32).at[:, :N_EXPERTS].set(moe_router_w[j].astype(F32))
            rb = jnp.zeros((1, LANES), F32).at[0, :N_EXPERTS].set(moe_router_b[j].astype(F32))
            tokens, rt = _norm(h, norm_ffn_g[li], tm=te, n_rows=n_out, mod=mod, mod_idx_fn=mod_idx_fn,
                               parts=(3, 4), router=(rw, rb), out_dtype=F32)
            dest, slot_tok, block_expert, n_used = _route(rt[:, :TOP_K].astype(jnp.int32), MOE_ROWS)
            block_expert = block_expert + j * N_EXPERTS
            stack = lambda w: w.reshape((-1,) + w.shape[2:])
            x_sorted = _gather_rows(tokens, slot_tok, n_used, rows=MOE_ROWS, out_dtype=BF16)
            hid = _gmm(x_sorted, [stack(moe_w1), stack(moe_w3)], block_expert, tm=MOE_ROWS, tn=512, epi="swiglu",
                       out_dtype=BF16, n_used=n_used)
            y_sorted = _gmm(hid, [stack(moe_w2)], block_expert, tm=MOE_ROWS, tn=512, nk=2, n_used=n_used)
            h = _combine(h, y_sorted, dest, rt, mod, mod_idx_fn, 5, final_norm_g, rows=te, n_rows=n_out, norm=last)
            if last:
                out = h
    return out.reshape(batch, seq, d)
```

```python
import functools
import math

import numpy as np
import jax
import jax.numpy as jnp
from jax import lax
from jax.experimental import pallas as pl
from jax.experimental.pallas import tpu as pltpu

F32 = jnp.float32
BF16 = jnp.bfloat16

GRID_W = 64
NA_HEADS, NA_HEAD_DIM, NA_WIN_H, NA_WIN_W = 8, 64, 8, 16
DA_HEADS, DA_HEAD_DIM = 4, 64
ROPE_BASE = 10000.0
SSM_D_INNER, SSM_HEAD_DIM, SSM_GROUPS, SSM_STATE, SSM_CONV_W = 1024, 64, 2, 128, 5
SSM_HEADS = SSM_D_INNER // SSM_HEAD_DIM
N_EXPERTS, TOP_K = 8, 2
EPS = 1e-6
NEG_INF = -1e30

LANES = 128
SUBLANES = 8
VMEM_LIMIT_BYTES = 56 * 1024 * 1024
SSD_CHUNK = 128
MOE_ROWS = 1024
NA_ROW_UNROLL = 4
DA_Q_SPLIT = 1
LOG2E = math.log2(math.e)


def _cparams(sem):
    return pltpu.CompilerParams(dimension_semantics=sem, vmem_limit_bytes=VMEM_LIMIT_BYTES)


def _dot(a, b):
    return jnp.dot(a, b, preferred_element_type=F32)


def _dot_nt(a, b):
    return lax.dot_general(a, b, (((1,), (1,)), ((), ())), preferred_element_type=F32)


def _silu(x):
    return x * jax.nn.sigmoid(x)


def _gmm_body(bexp_ref, neww_ref, mod_ref, nv_ref, x_ref, *rest, nk, n_w, epi):
    del bexp_ref, mod_ref
    w_refs, rest = rest[:n_w], rest[n_w:]
    if epi == "bias":
        bias_ref, rest = rest[0], rest[1:]
    elif epi == "resid":
        res_ref, gate_ref, rest = rest[0], rest[1], rest[2:]
    o_ref, rest = rest[0], rest[1:]
    wbf_refs, acc_refs = rest[:n_w], rest[n_w:]
    i = pl.program_id(1)
    k = pl.program_id(2)

    def finalize(vals):
        if epi == "swiglu":
            o_ref[...] = (_silu(vals[0]) * vals[1]).astype(o_ref.dtype)
        elif epi == "bias":
            o_ref[...] = (vals[0] + bias_ref[...]).astype(o_ref.dtype)
        elif epi == "resid":
            o_ref[...] = (res_ref[...] + gate_ref[...] * vals[0]).astype(o_ref.dtype)
        elif epi == "sigmoid":
            o_ref[...] = jax.nn.sigmoid(vals[0]).astype(o_ref.dtype)
        else:
            o_ref[...] = vals[0].astype(o_ref.dtype)

    @pl.when(jnp.logical_and(i >= nv_ref[0], k == nk - 1))
    def _():
        o_ref[...] = jnp.zeros_like(o_ref)

    @pl.when(i < nv_ref[0])
    def _():
        @pl.when(neww_ref[i] == 1)
        def _():
            for w_ref, wbf in zip(w_refs, wbf_refs):
                wbf[k] = w_ref[...].astype(BF16)

        x = x_ref[...]
        parts = [_dot(x, wbf[k]) for wbf in wbf_refs]
        if nk == 1:
            finalize(parts)
        else:
            @pl.when(k == 0)
            def _():
                for acc, p in zip(acc_refs, parts):
                    acc[...] = p

            @pl.when(k > 0)
            def _():
                for acc, p in zip(acc_refs, parts):
                    acc[...] += p

            @pl.when(k == nk - 1)
            def _():
                finalize([acc[...] for acc in acc_refs])


def _gmm(x, ws, bexp, *, tm, tn, nk=1, n_cols=None, epi="plain", out_dtype=F32, bias=None, res=None,
         mod=None, mod_idx=None, mod_part=0, n_rows=None, n_used=None, w_buffers=2):
    m_rows = x.shape[0] if n_rows is None else n_rows
    k_dim = x.shape[1]
    n_dim = ws[0].shape[2] if n_cols is None else n_cols
    n_w = len(ws)
    tk = k_dim // nk
    nb = m_rows // tm
    assert m_rows % tm == 0 and n_dim % tn == 0 and k_dim % nk == 0
    bexp = bexp.astype(jnp.int32)
    neww = jnp.concatenate([jnp.ones((1,), jnp.int32), (bexp[1:] != bexp[:-1]).astype(jnp.int32)])
    if mod_idx is None:
        mod_idx = jnp.zeros((nb,), jnp.int32)
    if n_used is None:
        n_used = jnp.full((1,), nb, jnp.int32)

    rb = lambda i, nv: jnp.minimum(i, nv[0] - 1)
    x_spec = pl.BlockSpec((tm, tk), lambda j, i, k, be, nw, md, nv: (rb(i, nv), k))
    w_map = lambda j, i, k, be, nw, md, nv: (be[rb(i, nv)], jnp.where(nw[rb(i, nv)] == 1, k, nk - 1), j)
    if w_buffers == 2:
        w_spec = pl.BlockSpec((None, tk, tn), w_map)
    else:
        w_spec = pl.BlockSpec((None, tk, tn), w_map, pipeline_mode=pl.Buffered(w_buffers))
    o_spec = pl.BlockSpec((tm, tn), lambda j, i, k, be, nw, md, nv: (i, j))
    in_specs = [x_spec] + [w_spec] * n_w
    args = [x] + list(ws)
    if epi == "bias":
        in_specs.append(pl.BlockSpec((1, tn), lambda j, i, k, be, nw, md, nv: (0, j)))
        args.append(bias)
    elif epi == "resid":
        part_off = mod_part * (n_dim // tn)
        in_specs.append(o_spec)
        in_specs.append(pl.BlockSpec((None, 1, tn),
                                     lambda j, i, k, be, nw, md, nv: (md[rb(i, nv)], 0, part_off + j)))
        args += [res, mod]
    scratch = [pltpu.VMEM((nk, tk, tn), BF16) for _ in range(n_w)]
    if nk > 1:
        scratch += [pltpu.VMEM((tm, tn), F32) for _ in range(n_w)]
    grid_spec = pltpu.PrefetchScalarGridSpec(
        num_scalar_prefetch=4, grid=(n_dim // tn, nb, nk), in_specs=in_specs, out_specs=o_spec,
        scratch_shapes=scratch)
    return pl.pallas_call(
        functools.partial(_gmm_body, nk=nk, n_w=n_w, epi=epi),
        grid_spec=grid_spec,
        out_shape=jax.ShapeDtypeStruct((m_rows, n_dim), out_dtype),
        compiler_params=_cparams(("arbitrary", "arbitrary", "arbitrary")),
        name="gmm_" + epi,
    )(bexp, neww, mod_idx.astype(jnp.int32), n_used.astype(jnp.int32), *args)


def _split2(v):
    hi = v.astype(BF16)
    lo = (v - hi.astype(F32)).astype(BF16)
    return hi, lo


def _norm_body(x_ref, g_ref, *rest, modulate, router):
    if modulate:
        shift_ref, scale_ref, rest = rest[0], rest[1], rest[2:]
    if router:
        rw_ref, rb_ref, rest = rest[0], rest[1], rest[2:]
    o_ref = rest[0]
    x = x_ref[...]
    y = x * lax.rsqrt(jnp.mean(x * x, axis=-1, keepdims=True) + EPS) * g_ref[...]
    if modulate:
        y = y * (1.0 + scale_ref[...]) + shift_ref[...]
    o_ref[...] = y.astype(o_ref.dtype)
    if router:
        rt_ref = rest[1]
        y_hi, y_lo = _split2(y)
        w_hi, w_lo = _split2(rw_ref[...])
        logits = _dot(y_hi, w_hi) + _dot(y_lo, w_hi) + _dot(y_hi, w_lo) + rb_ref[...]
        lane = lax.broadcasted_iota(jnp.int32, logits.shape, 1).astype(F32)
        lg = jnp.where(lane < N_EXPERTS, logits, -jnp.inf)
        m1 = jnp.max(lg, axis=-1, keepdims=True)
        i1 = jnp.min(jnp.where(lg == m1, lane, float(LANES)), axis=-1, keepdims=True)
        lg2 = jnp.where(lane == i1, -jnp.inf, lg)
        m2 = jnp.max(lg2, axis=-1, keepdims=True)
        i2 = jnp.min(jnp.where(lg2 == m2, lane, float(LANES)), axis=-1, keepdims=True)
        e2 = jnp.exp(m2 - m1)
        g1 = 1.0 / (1.0 + e2)
        rt = jnp.where(lane == 0, i1, jnp.where(lane == 1, i2, jnp.where(lane == 2, g1, jnp.where(lane == 3, e2 * g1, 0.0))))
        rt_ref[...] = rt


def _norm(x, g, *, tm, n_rows, mod=None, mod_idx_fn=None, parts=(0, 1), out_dtype=BF16, router=None):
    d = x.shape[1]
    nb = n_rows // tm
    row = pl.BlockSpec((tm, d), lambda i: (i, 0))
    in_specs = [row, pl.BlockSpec((1, d), lambda i: (0, 0))]
    args = [x, g.reshape(1, d)]
    modulate = mod is not None
    if modulate:
        for p in parts:
            in_specs.append(pl.BlockSpec((None, 1, d), lambda i, p=p: (mod_idx_fn(i), 0, p)))
            args.append(mod)
    out_shape = [jax.ShapeDtypeStruct((n_rows, d), out_dtype)]
    out_specs = [row]
    if router is not None:
        rw, rb = router
        in_specs += [pl.BlockSpec((d, LANES), lambda i: (0, 0)), pl.BlockSpec((1, LANES), lambda i: (0, 0))]
        args += [rw, rb]
        out_shape.append(jax.ShapeDtypeStruct((n_rows, LANES), F32))
        out_specs.append(pl.BlockSpec((tm, LANES), lambda i: (i, 0)))
    res = pl.pallas_call(
        functools.partial(_norm_body, modulate=modulate, router=router is not None),
        grid=(nb,), in_specs=in_specs, out_specs=out_specs, out_shape=out_shape,
        compiler_params=_cparams(("arbitrary",)), name="rmsnorm",
    )(*args)
    return res if router is not None else res[0]


def _rope_body(q_ref, k_ref, c_ref, sm_ref, sp_ref, qo_ref, ko_ref, *, q_scale):
    c, sm, sp = c_ref[...], sm_ref[...], sp_ref[...]
    width = q_ref.shape[1]
    for src, dst, scale in ((q_ref, qo_ref, q_scale), (k_ref, ko_ref, 1.0)):
        for g in range(width // LANES):
            sl = slice(g * LANES, (g + 1) * LANES)
            x = src[:, sl]
            y = x * c + pltpu.roll(x, LANES - 16, 1) * sm + pltpu.roll(x, 16, 1) * sp
            dst[:, sl] = (y * scale).astype(dst.dtype)


def _rope_tables(seq, n_id_rows):
    t = jnp.arange(seq)
    pos = jnp.stack([t // GRID_W, t % GRID_W], axis=-1).astype(F32)
    n_freq = DA_HEAD_DIM // 4
    inv_freq = ROPE_BASE ** (-jnp.arange(n_freq, dtype=F32) / n_freq)
    ang = pos[:, :, None] * inv_freq
    d = np.arange(LANES) % DA_HEAD_DIM
    kind, which, f = d // 32, (d % 32) // 16, d % 16
    a = ang[:, kind, f]
    cos, sin = jnp.cos(a), jnp.sin(a)
    sm = jnp.where(which == 0, -sin, 0.0)
    sp = jnp.where(which == 1, sin, 0.0)
    pad = lambda v, fill: jnp.concatenate([v, jnp.full((n_id_rows, LANES), fill, F32)], axis=0)
    return pad(cos, 1.0), pad(sm, 0.0), pad(sp, 0.0)


def _rope(proj, tabs, *, tm, n_rows, n_lat_rows, seq, q_col, k_col, width):
    lat_blocks = n_lat_rows // tm
    per_seq = seq // tm
    tab_idx = lambda i: (jnp.where(i < lat_blocks, i % per_seq, per_seq), 0)
    out = jax.ShapeDtypeStruct((n_rows, width), BF16)
    return pl.pallas_call(
        functools.partial(_rope_body, q_scale=DA_HEAD_DIM ** -0.5 * LOG2E),
        grid=(n_rows // tm,),
        in_specs=[pl.BlockSpec((tm, width), lambda i: (i, q_col // width)),
                  pl.BlockSpec((tm, width), lambda i: (i, k_col // width)),
                  pl.BlockSpec((tm, LANES), tab_idx), pl.BlockSpec((tm, LANES), tab_idx),
                  pl.BlockSpec((tm, LANES), tab_idx)],
        out_specs=[pl.BlockSpec((tm, width), lambda i: (i, 0))] * 2,
        out_shape=[out, out], compiler_params=_cparams(("arbitrary",)), name="rope",
    )(proj, proj, *tabs)


def _na_body(q_ref, k_ref, v_ref, kc_ref, vc_ref, tb_ref, o_ref, kb, vb, kcb, vcb, *, rows, wh):
    kb[...] = k_ref[...].astype(BF16)
    vb[...] = v_ref[...].astype(BF16)
    kcb[...] = kc_ref[...].astype(BF16)
    vcb[...] = vc_ref[...].astype(BF16)
    lane = lax.broadcasted_iota(jnp.int32, (GRID_W, LANES), 1)
    first = lane < NA_HEAD_DIM
    scale = NA_HEAD_DIM ** -0.5

    def body(grp, carry):
        rr = [grp * NA_ROW_UNROLL + j for j in range(NA_ROW_UNROLL)]
        rs = [jnp.clip(r - wh // 2, 0, rows - wh) for r in rr]
        row_of = lambda r: pl.ds(pl.multiple_of(r * GRID_W, GRID_W), GRID_W)
        win_of = lambda r0: pl.ds(pl.multiple_of(r0 * GRID_W, GRID_W), wh * GRID_W)
        qq = []
        for r in rr:
            q = q_ref[row_of(r), :] * scale
            qq.append(jnp.concatenate([jnp.where(first, q, 0.0), jnp.where(first, 0.0, q)], axis=0).astype(BF16))
        s_l = [_dot_nt(x, kb[win_of(r0), :]) + tb_ref[r0 - r + (NA_WIN_H - 1)] for x, r, r0 in zip(qq, rr, rs)]
        s_c = [_dot_nt(x, kcb[...]) for x in qq]
        m = [jnp.maximum(jnp.max(a, axis=-1, keepdims=True), jnp.max(b, axis=-1, keepdims=True))
             for a, b in zip(s_l, s_c)]
        p_l = [jnp.exp(a - mm) for a, mm in zip(s_l, m)]
        p_c = [jnp.exp(b - mm) for b, mm in zip(s_c, m)]
        den = [jnp.sum(a, axis=-1, keepdims=True) + jnp.sum(b, axis=-1, keepdims=True) for a, b in zip(p_l, p_c)]
        o = [_dot(a.astype(BF16), vb[win_of(r0), :]) + _dot(b.astype(BF16), vcb[...])
             for a, b, r0 in zip(p_l, p_c, rs)]
        for r, oo, dd in zip(rr, o, den):
            oo = oo / dd
            o_ref[row_of(r), :] = jnp.where(first, oo[:GRID_W], oo[GRID_W:]).astype(o_ref.dtype)
        return carry

    assert rows % NA_ROW_UNROLL == 0
    lax.fori_loop(0, rows // NA_ROW_UNROLL, body, 0)


def _na_bias_table(rpb, rows):
    wh = min(NA_WIN_H, rows)
    qc = np.arange(GRID_W)[:, None]
    kc = np.arange(GRID_W)[None, :]
    ws = np.clip(qc - NA_WIN_W // 2, 0, GRID_W - NA_WIN_W)
    ok = (kc >= ws) & (kc < ws + NA_WIN_W)
    n_drow, n_dcol = 2 * NA_WIN_H - 1, 2 * NA_WIN_W - 1
    off = GRID_W - NA_WIN_W
    line = jnp.full((NA_HEADS, n_drow, 2 * GRID_W), NEG_INF, F32).at[:, :, off:off + n_dcol].set(rpb.astype(F32))
    span = 2 * GRID_W - 1
    t = jnp.tile(line, (1, 1, GRID_W))[:, :, :GRID_W * span].reshape(NA_HEADS, n_drow, GRID_W, span)
    t = jnp.where(ok, t[:, :, :, GRID_W - 1:], NEG_INF)
    n_d0 = NA_WIN_H
    t = jnp.stack([t[:, d0:d0 + wh] for d0 in range(n_d0)], axis=1)
    t = t.transpose(0, 1, 3, 2, 4).reshape(NA_HEADS // 2, 2, n_d0, GRID_W, wh * GRID_W)
    return t.transpose(0, 2, 1, 3, 4).reshape(NA_HEADS // 2, n_d0, 2 * GRID_W, wh * GRID_W)


def _na(proj, rpb, *, batch, seq, ctx_len, q_col, k_col, v_col):
    rows = seq // GRID_W
    wh = min(NA_WIN_H, rows)
    pairs = NA_HEADS // 2
    tb = _na_bias_table(rpb, rows)
    ctx0 = batch * seq // ctx_len
    lat = lambda col: pl.BlockSpec((seq, LANES), lambda b, p: (b, col // LANES + p))
    ctx = lambda col: pl.BlockSpec((ctx_len, LANES), lambda b, p: (ctx0 + b, col // LANES + p))
    return pl.pallas_call(
        functools.partial(_na_body, rows=rows, wh=wh),
        grid=(batch, pairs),
        in_specs=[lat(q_col), lat(k_col), lat(v_col), ctx(k_col), ctx(v_col),
                  pl.BlockSpec((None, NA_WIN_H, 2 * GRID_W, wh * GRID_W), lambda b, p: (p, 0, 0, 0))],
        out_specs=pl.BlockSpec((seq, LANES), lambda b, p: (b, p)),
        out_shape=jax.ShapeDtypeStruct((batch * seq, pairs * LANES), BF16),
        scratch_shapes=[pltpu.VMEM((seq, LANES), BF16), pltpu.VMEM((seq, LANES), BF16),
                        pltpu.VMEM((ctx_len, LANES), BF16), pltpu.VMEM((ctx_len, LANES), BF16)],
        compiler_params=_cparams(("arbitrary", "arbitrary")), name="na_attn",
    )(proj, proj, proj, proj, proj, tb)


def _ctx_attn_body(q_ref, k_ref, v_ref, o_ref):
    kb = k_ref[...].astype(BF16)
    vb = v_ref[...].astype(BF16)
    q = q_ref[...] * (NA_HEAD_DIM ** -0.5)
    lane = lax.broadcasted_iota(jnp.int32, q.shape, 1)
    first = lane < NA_HEAD_DIM
    outs = []
    for hh in range(2):
        qm = jnp.where(first if hh == 0 else jnp.logical_not(first), q, 0.0).astype(BF16)
        s = _dot_nt(qm, kb)
        p = jnp.exp(s - jnp.max(s, axis=-1, keepdims=True))
        outs.append(_dot(p.astype(BF16), vb) / jnp.sum(p, axis=-1, keepdims=True))
    o_ref[...] = jnp.where(first, outs[0], outs[1]).astype(o_ref.dtype)


def _ctx_attn(proj, *, batch, seq, ctx_len, q_col, k_col, v_col):
    pairs = NA_HEADS // 2
    ctx0 = batch * seq // ctx_len
    ctx = lambda col: pl.BlockSpec((ctx_len, LANES), lambda b, p: (ctx0 + b, col // LANES + p))
    return pl.pallas_call(
        _ctx_attn_body, grid=(batch, pairs),
        in_specs=[ctx(q_col), ctx(k_col), ctx(v_col)],
        out_specs=pl.BlockSpec((ctx_len, LANES), lambda b, p: (b, p)),
        out_shape=jax.ShapeDtypeStruct((batch * ctx_len, pairs * LANES), BF16),
        compiler_params=_cparams(("arbitrary", "arbitrary")), name="ctx_attn",
    )(proj, proj, proj)


def _da_body(lv_ref, q_ref, *rest, lam_init, has_lat):
    if has_lat:
        kl_ref, vl_ref, rest = rest[0], rest[1], rest[2:]
    kc_ref, vc_ref, g_ref, o_ref = rest[:4]
    scr = rest[4:]
    t = pl.program_id(2)
    if has_lat:
        vlb, vcb = scr

        @pl.when(t == 0)
        def _():
            vlb[...] = vl_ref[...].astype(BF16)
            vcb[...] = vc_ref[...].astype(BF16)
    else:
        (vcb,) = scr
        vcb[...] = vc_ref[...].astype(BF16)

    lv = lv_ref[...]
    lam = (jnp.exp(jnp.sum(lv[0:1] * lv[1:2], axis=-1, keepdims=True))
           - jnp.exp(jnp.sum(lv[2:3] * lv[3:4], axis=-1, keepdims=True)) + lam_init)
    sq = q_ref.shape[0] // DA_Q_SPLIT
    lane = lax.broadcasted_iota(jnp.int32, (sq, LANES), 1)
    for sb in range(DA_Q_SPLIT):
        q = q_ref[sb * sq:(sb + 1) * sq, :]
        zero = jnp.zeros_like(q)
        qq = jnp.concatenate([jnp.where(lane < DA_HEAD_DIM, q, zero), jnp.where(lane >= DA_HEAD_DIM, q, zero)],
                             axis=0)
        s_c = _dot_nt(qq, kc_ref[...])
        m = jnp.max(s_c, axis=-1, keepdims=True)
        if has_lat:
            s_l = _dot_nt(qq, kl_ref[...])
            m = jnp.maximum(m, jnp.max(s_l, axis=-1, keepdims=True))
            p_l = jnp.exp2(s_l - m)
        p_c = jnp.exp2(s_c - m)
        den = jnp.sum(p_c, axis=-1, keepdims=True)
        if has_lat:
            den = den + jnp.sum(p_l, axis=-1, keepdims=True)
        ratio = lam * den[:sq] / den[sq:]
        o = _dot((p_c[:sq] - ratio * p_c[sq:]).astype(BF16), vcb[...])
        if has_lat:
            o = o + _dot((p_l[:sq] - ratio * p_l[sq:]).astype(BF16), vlb[...])
        o = o / den[:sq]
        o = o * lax.rsqrt(jnp.mean(o * o, axis=-1, keepdims=True) + EPS) * g_ref[...]
        o_ref[sb * sq:(sb + 1) * sq, :] = (o * (1.0 - lam_init)).astype(o_ref.dtype)


def _da(q_r, k_r, proj, lam_vec, subln_g, lam_init, *, batch, seq, ctx_len, v_col, tq, latent):
    ctx0 = batch * seq // ctx_len
    vcb0 = v_col // LANES
    g2 = subln_g.reshape(1, 2 * DA_HEAD_DIM).astype(F32)
    small = [pl.BlockSpec((4, DA_HEAD_DIM), lambda b, h, t: (0, 0))]
    gspec = pl.BlockSpec((1, LANES), lambda b, h, t: (0, 0))
    kc = pl.BlockSpec((ctx_len, LANES), lambda b, h, t: (ctx0 + b, h))
    vc = pl.BlockSpec((ctx_len, LANES), lambda b, h, t: (ctx0 + b, vcb0 + h))
    if latent:
        nq = seq // tq
        qs = pl.BlockSpec((tq, LANES), lambda b, h, t: (b * nq + t, h))
        kl = pl.BlockSpec((seq, LANES), lambda b, h, t: (b, h))
        vl = pl.BlockSpec((seq, LANES), lambda b, h, t: (b, vcb0 + h))
        in_specs = small + [qs, kl, vl, kc, vc, gspec]
        args = (lam_vec, q_r, k_r, proj, k_r, proj, g2)
        scratch = [pltpu.VMEM((seq, LANES), BF16), pltpu.VMEM((ctx_len, LANES), BF16)]
        n_out = batch * seq
    else:
        nq = ctx_len // tq
        cq0 = batch * seq // tq
        qs = pl.BlockSpec((tq, LANES), lambda b, h, t: (cq0 + b * nq + t, h))
        in_specs = small + [qs, kc, vc, gspec]
        args = (lam_vec, q_r, k_r, proj, g2)
        scratch = [pltpu.VMEM((ctx_len, LANES), BF16)]
        n_out = batch * ctx_len
    return pl.pallas_call(
        functools.partial(_da_body, lam_init=lam_init, has_lat=latent),
        grid=(batch, DA_HEADS, nq), in_specs=in_specs,
        out_specs=pl.BlockSpec((tq, LANES), lambda b, h, t: (b * nq + t, h)),
        out_shape=jax.ShapeDtypeStruct((n_out, DA_HEADS * LANES), BF16),
        scratch_shapes=scratch,
        compiler_params=_cparams(("arbitrary", "arbitrary", "arbitrary")),
        name="diff_attn_lat" if latent else "diff_attn_ctx",
    )(*args)


def _ssd_body(x0_ref, x1_ref, bc_ref, x0p_ref, x1p_ref, bcp_ref, x0n_ref, x1n_ref, bcn_ref, dt_ref,
              cw_ref, cb_ref, prm_ref, dsk_ref, ex_ref, init_ref, y_ref, sout_ref, st_ref, u_ref, *, nc):
    q_len = SSD_CHUNK
    dirn = pl.program_id(1)
    z = pl.program_id(2)
    zz = jnp.where(dirn == 0, z, nc - 1 - z)
    fwd = dirn == 0
    half = SSM_D_INNER // SSM_GROUPS
    heads_per_group = SSM_HEADS // SSM_GROUPS

    @pl.when(z == 0)
    def _():
        st_ref[...] = init_ref[...]

    def conv_piece(main_ref, prev_ref, next_ref, c0):
        width = main_ref.shape[1]
        prev = jnp.where(zz == 0, 0.0, prev_ref[...])
        nxt = jnp.where(zz == nc - 1, 0.0, next_ref[...])
        ext = jnp.concatenate([prev, main_ref[...], nxt], axis=0)
        n_ext = ext.shape[0]
        acc = jnp.zeros((q_len, width), F32) + cb_ref[:, c0:c0 + width]
        for tap in range(SSM_CONV_W):
            sh = (SSM_CONV_W // 2 - tap) % n_ext
            e = ext if sh == 0 else pltpu.roll(ext, sh, 0)
            acc = acc + e[SUBLANES:SUBLANES + q_len] * cw_ref[tap:tap + 1, c0:c0 + width]
        return _silu(acc)

    @pl.when(fwd)
    def _():
        u_ref[zz, :, 0:half] = conv_piece(x0_ref, x0p_ref, x0n_ref, 0)
        u_ref[zz, :, half:2 * half] = conv_piece(x1_ref, x1p_ref, x1n_ref, half)
        u_ref[zz, :, SSM_D_INNER:] = conv_piece(bc_ref, bcp_ref, bcn_ref, SSM_D_INNER)

    xs = (u_ref[zz, :, 0:half], u_ref[zz, :, half:2 * half])
    bcm = u_ref[zz, :, SSM_D_INNER:]

    raw = dt_ref[...]
    raw = jnp.where(fwd, raw, pltpu.roll(raw, LANES - SSM_HEADS, 1))
    xb = raw + prm_ref[0:1, :]
    dtv = jnp.maximum(xb, 0.0) + jnp.log1p(jnp.exp(-jnp.abs(xb)))
    a = dtv * (-jnp.exp(prm_ref[1:2, :]))
    rowi = lax.broadcasted_iota(jnp.int32, (q_len, LANES), 0)
    coli = lax.broadcasted_iota(jnp.int32, (q_len, LANES), 1)
    cum = a
    sft = 1
    while sft < q_len:
        cum = cum + jnp.where(rowi >= sft, pltpu.roll(cum, sft, 0), 0.0)
        sft *= 2
    tot = cum[q_len - 1:q_len, :]
    g = jnp.where(fwd, cum, cum - a)
    e_g = jnp.exp(g)
    e_tg = jnp.exp(tot - g)
    w_state = jnp.where(fwd, e_tg, e_g)
    w_yoff = jnp.where(fwd, e_g, e_tg)
    e_tot = jnp.exp(tot)
    gs = g * jnp.where(fwd, LOG2E, -LOG2E)
    gs_t = gs.T
    tri = jnp.where(fwd, rowi - coli, coli - rowi) >= 0

    stack = jnp.concatenate([dtv, dtv * w_state, w_yoff, jnp.broadcast_to(e_tot, (SUBLANES, LANES))], axis=0)
    s_hi, s_lo = _split2(stack)
    ex = ex_ref[...]
    expd = _dot(s_hi, ex) + _dot(s_lo, ex)
    dt_e, dts_e, wy_e = expd[0:q_len], expd[q_len:2 * q_len], expd[2 * q_len:3 * q_len]
    tot_e = expd[3 * q_len:3 * q_len + 1]

    lane = lax.broadcasted_iota(jnp.int32, (q_len, LANES), 1)
    first = lane < SSM_HEAD_DIM
    for grp in range(SSM_GROUPS):
        x_g = xs[grp]
        csl = slice(grp * half, (grp + 1) * half)
        b_g = bcm[:, grp * SSM_STATE:(grp + 1) * SSM_STATE]
        c_g = bcm[:, (SSM_GROUPS + grp) * SSM_STATE:(SSM_GROUPS + grp + 1) * SSM_STATE].astype(BF16)
        xdt = (x_g * dt_e[:, csl]).astype(BF16)
        xdts = (x_g * dts_e[:, csl]).astype(BF16)
        cb = _dot_nt(c_g, b_g.astype(BF16))
        st_g = st_ref[:, csl]
        y_g = _dot(c_g, st_g.astype(BF16)) * wy_e[:, csl] + dsk_ref[:, csl] * x_g
        for pr in range(heads_per_group // 2):
            outs = []
            for hh in range(2):
                col = grp * heads_per_group + 2 * pr + hh
                seg = gs[:, col:col + 1] - gs_t[col:col + 1, :]
                mat = (cb * jnp.where(tri, jnp.exp2(seg), 0.0)).astype(BF16)
                outs.append(_dot(mat, xdt[:, pr * LANES:(pr + 1) * LANES]))
            lo = grp * half + pr * LANES
            y_ref[:, lo:lo + LANES] = y_g[:, pr * LANES:(pr + 1) * LANES] + jnp.where(first, outs[0], outs[1])
        st_ref[:, csl] = tot_e[:, csl] * st_g + _dot(b_g.T.astype(BF16), xdts)

    @pl.when(z == nc - 1)
    def _():
        sout_ref[...] = st_ref[...]


def _ssd(proj, conv_w, conv_b, dt_bias, a_log, d_skip, init, *, batch, n_tok, row0, x_col, dt_col):
    q_len = SSD_CHUNK
    nc = n_tok // q_len
    half = SSM_D_INNER // SSM_GROUPS
    conv_dim = SSM_D_INNER + 2 * SSM_GROUPS * SSM_STATE
    rb0 = row0 // q_len
    hb = q_len // SUBLANES
    n_halo = proj.shape[0] // SUBLANES
    zz = lambda d, z: jnp.where(d == 0, z, nc - 1 - z)
    rb = lambda b, d, z: rb0 + b * nc + zz(d, z)
    rbx = lambda b, d, z: rb0 + b * nc + jnp.where(d == 0, z, nc - 1)
    main = lambda c: pl.BlockSpec((q_len, half), lambda b, d, z: (rbx(b, d, z), c))
    prev = lambda c: pl.BlockSpec((SUBLANES, half), lambda b, d, z: (jnp.maximum(rbx(b, d, z) * hb - 1, 0), c))
    nxt = lambda c: pl.BlockSpec((SUBLANES, half),
                                 lambda b, d, z: (jnp.minimum(rbx(b, d, z) * hb + hb, n_halo - 1), c))
    cols = [x_col // half + j for j in range(conv_dim // half)]
    const2 = lambda shape: pl.BlockSpec(shape, lambda b, d, z: (0, 0))
    prm = jnp.zeros((2, SUBLANES, LANES), F32)
    prm = prm.at[:, 0, :SSM_HEADS].set(dt_bias.astype(F32)).at[:, 1, :SSM_HEADS].set(a_log.astype(F32))
    dsk = jnp.repeat(d_skip.astype(F32), SSM_HEAD_DIM, axis=-1).reshape(2, 1, SSM_D_INNER)
    ex = (np.arange(LANES)[:, None] == (np.arange(SSM_D_INNER) // SSM_HEAD_DIM)[None, :])
    ex = jnp.asarray(ex, BF16)
    cw = jnp.zeros((SUBLANES, conv_dim), F32).at[:SSM_CONV_W].set(conv_w.astype(F32))
    state_spec = pl.BlockSpec((None, None, SSM_STATE, SSM_D_INNER), lambda b, d, z: (b, d, 0, 0))
    y, s_out = pl.pallas_call(
        functools.partial(_ssd_body, nc=nc),
        grid=(batch, 2, nc),
        in_specs=[main(cols[0]), main(cols[1]), main(cols[2]), prev(cols[0]), prev(cols[1]), prev(cols[2]),
                  nxt(cols[0]), nxt(cols[1]), nxt(cols[2]),
                  pl.BlockSpec((q_len, LANES), lambda b, d, z: (rb(b, d, z), dt_col // LANES)),
                  const2((SUBLANES, conv_dim)), const2((1, conv_dim)),
                  pl.BlockSpec((None, SUBLANES, LANES), lambda b, d, z: (d, 0, 0)),
                  pl.BlockSpec((None, 1, SSM_D_INNER), lambda b, d, z: (d, 0, 0)),
                  const2((LANES, SSM_D_INNER)), state_spec],
        out_specs=[pl.BlockSpec((None, q_len, SSM_D_INNER), lambda b, d, z: (d, b * nc + zz(d, z), 0)),
                   state_spec],
        out_shape=[jax.ShapeDtypeStruct((2, batch * n_tok, SSM_D_INNER), F32),
                   jax.ShapeDtypeStruct((batch, 2, SSM_STATE, SSM_D_INNER), F32)],
        scratch_shapes=[pltpu.VMEM((SSM_STATE, SSM_D_INNER), F32), pltpu.VMEM((nc, q_len, conv_dim), F32)],
        compiler_params=_cparams(("arbitrary", "arbitrary", "arbitrary")), name="ssd_scan",
    )(proj, proj, proj, proj, proj, proj, proj, proj, proj, proj, cw, conv_b.reshape(1, conv_dim).astype(F32),
      prm, dsk, ex, init)
    return y, s_out


def _ssm_norm_body(y_ref, z_ref, g_ref, o_ref):
    yz = (y_ref[0] + y_ref[1]) * _silu(z_ref[...])
    gw = SSM_D_INNER // SSM_GROUPS
    for grp in range(SSM_GROUPS):
        v = yz[:, grp * gw:(grp + 1) * gw]
        v = v * lax.rsqrt(jnp.mean(v * v, axis=-1, keepdims=True) + EPS)
        o_ref[:, grp * gw:(grp + 1) * gw] = (v * g_ref[:, grp * gw:(grp + 1) * gw]).astype(o_ref.dtype)


def _ssm_norm(y, proj, g, *, tm, row0, z_col):
    n = y.shape[1]
    zb0 = row0 // tm
    return pl.pallas_call(
        _ssm_norm_body, grid=(n // tm,),
        in_specs=[pl.BlockSpec((2, tm, SSM_D_INNER), lambda i: (0, i, 0)),
                  pl.BlockSpec((tm, SSM_D_INNER), lambda i: (zb0 + i, z_col // SSM_D_INNER)),
                  pl.BlockSpec((1, SSM_D_INNER), lambda i: (0, 0))],
        out_specs=pl.BlockSpec((tm, SSM_D_INNER), lambda i: (i, 0)),
        out_shape=jax.ShapeDtypeStruct((n, SSM_D_INNER), BF16),
        compiler_params=_cparams(("arbitrary",)), name="ssm_gated_norm",
    )(y, proj, g.reshape(1, SSM_D_INNER).astype(F32))


def _merge_body(oa_ref, ob_ref, oc_ref, ga_ref, gb_ref, gc_ref, wa_ref, wb_ref, wc_ref, o_ref, wab, wbb, wcb):
    @pl.when(pl.program_id(1) == 0)
    def _():
        wab[...] = wa_ref[...].astype(BF16)
        wbb[...] = wb_ref[...].astype(BF16)
        wcb[...] = wc_ref[...].astype(BF16)

    acc = ga_ref[...].astype(F32) * _dot(oa_ref[...], wab[...])
    acc = acc + gb_ref[...].astype(F32) * _dot(ob_ref[...], wbb[...])
    acc = acc + gc_ref[...].astype(F32) * _dot(oc_ref[...], wcb[...])
    o_ref[...] = acc.astype(o_ref.dtype)


def _merge(o_a, o_b, o_c, gates, w_a, w_b, w_c, layer, *, tm, tn, n_rows):
    d = w_a.shape[2]
    nj = d // tn
    row = lambda w: pl.BlockSpec((tm, w), lambda j, i: (i, 0))
    gate = lambda x: pl.BlockSpec((tm, tn), lambda j, i: (i, x * nj + j))
    wsp = lambda kk: pl.BlockSpec((None, kk, tn), lambda j, i: (layer, 0, j))
    ka, kb_, kc = w_a.shape[1], w_b.shape[1], w_c.shape[1]
    return pl.pallas_call(
        _merge_body, grid=(nj, n_rows // tm),
        in_specs=[row(ka), row(kb_), row(kc), gate(0), gate(1), gate(2), wsp(ka), wsp(kb_), wsp(kc)],
        out_specs=pl.BlockSpec((tm, tn), lambda j, i: (i, j)),
        out_shape=jax.ShapeDtypeStruct((n_rows, d), BF16),
        scratch_shapes=[pltpu.VMEM((ka, tn), BF16), pltpu.VMEM((kb_, tn), BF16), pltpu.VMEM((kc, tn), BF16)],
        compiler_params=_cparams(("arbitrary", "arbitrary")), name="branch_merge",
    )(o_a, o_b, o_c, gates, gates, gates, w_a, w_b, w_c)


def _route(top_idx, blk):
    n = top_idx.shape[0]
    e_flat = top_idx.reshape(-1)
    onehot = (e_flat[:, None] == jnp.arange(N_EXPERTS, dtype=jnp.int32)[None, :]).astype(jnp.int32)
    counts = jnp.sum(onehot, axis=0)
    rank = jnp.sum((jnp.cumsum(onehot, axis=0) - onehot) * onehot, axis=1)
    padded = (counts + blk - 1) // blk * blk
    pad_end = jnp.cumsum(padded)
    pad_start = pad_end - padded
    dest = jnp.sum(onehot * pad_start[None, :], axis=1) + rank
    n_blocks = -(-(n * TOP_K) // blk) + N_EXPERTS
    cap = n_blocks * blk
    tok_flat = jnp.repeat(jnp.arange(n, dtype=jnp.int32), TOP_K)
    slot_tok = jnp.zeros((cap,), jnp.int32).at[dest].set(tok_flat)
    blk_start = jnp.arange(n_blocks, dtype=jnp.int32) * blk
    block_expert = jnp.clip(jnp.sum((pad_end[None, :] <= blk_start[:, None]).astype(jnp.int32), axis=1),
                            0, N_EXPERTS - 1)
    n_used = (pad_end[-1] // blk).astype(jnp.int32).reshape(1)
    return dest.astype(jnp.int32), slot_tok, block_expert, n_used


GATHER_UNROLL = 8


def _row_copy(src_hbm, row, buf, slot, sem):
    return pltpu.make_async_copy(src_hbm.at[pl.ds(row, 1), :], buf.at[pl.ds(slot, 1), :], sem)


def _gather_body(nv_ref, idx_ref, src_hbm, o_ref, buf, sem, *, rows):
    @pl.when(pl.program_id(0) >= nv_ref[0])
    def _():
        o_ref[...] = jnp.zeros_like(o_ref)

    @pl.when(pl.program_id(0) < nv_ref[0])
    def _():
        def issue(g, c):
            for u in range(GATHER_UNROLL):
                r = g * GATHER_UNROLL + u
                _row_copy(src_hbm, idx_ref[0, r], buf, r, sem).start(priority=u % 2)
            return c

        def drain(g, c):
            for u in range(GATHER_UNROLL):
                r = g * GATHER_UNROLL + u
                _row_copy(src_hbm, idx_ref[0, r], buf, r, sem).wait()
            return c

        lax.fori_loop(0, rows // GATHER_UNROLL, issue, 0)
        lax.fori_loop(0, rows // GATHER_UNROLL, drain, 0)
        o_ref[...] = buf[...].astype(o_ref.dtype)


def _gather_rows(src, idx, n_used, *, rows, out_dtype):
    n, d = src.shape
    nb = idx.shape[0] // rows
    assert rows % GATHER_UNROLL == 0
    blk = lambda i, nv: jnp.minimum(i, nv[0] - 1)
    grid_spec = pltpu.PrefetchScalarGridSpec(
        num_scalar_prefetch=1, grid=(nb,),
        in_specs=[pl.BlockSpec((None, 1, rows), lambda i, nv: (blk(i, nv), 0, 0), memory_space=pltpu.SMEM),
                  pl.BlockSpec(memory_space=pl.ANY)],
        out_specs=pl.BlockSpec((rows, d), lambda i, nv: (i, 0)),
        scratch_shapes=[pltpu.VMEM((rows, d), src.dtype), pltpu.SemaphoreType.DMA])
    return pl.pallas_call(
        functools.partial(_gather_body, rows=rows), grid_spec=grid_spec,
        out_shape=jax.ShapeDtypeStruct((nb * rows, d), out_dtype),
        compiler_params=_cparams(("arbitrary",)), name="moe_dispatch",
    )(n_used, idx.reshape(nb, 1, rows), src)


def _combine_body(dest_ref, h_ref, rt_ref, gate_ref, g_ref, y_hbm, o_ref, buf, sem, *, rows, norm):
    def issue(g, c):
        for u in range(GATHER_UNROLL):
            r = g * GATHER_UNROLL + u
            for k in range(TOP_K):
                _row_copy(y_hbm, dest_ref[0, r * TOP_K + k], buf.at[k], r, sem).start(priority=k)
        return c

    def drain(g, c):
        for u in range(GATHER_UNROLL):
            r = g * GATHER_UNROLL + u
            for k in range(TOP_K):
                _row_copy(y_hbm, dest_ref[0, r * TOP_K + k], buf.at[k], r, sem).wait()
        return c

    lax.fori_loop(0, rows // GATHER_UNROLL, issue, 0)
    lax.fori_loop(0, rows // GATHER_UNROLL, drain, 0)
    rt = rt_ref[...]
    f = buf[0] * rt[:, TOP_K:TOP_K + 1] + buf[1] * rt[:, TOP_K + 1:TOP_K + 2]
    x = h_ref[...] + gate_ref[...] * f
    if norm:
        x = x * lax.rsqrt(jnp.mean(x * x, axis=-1, keepdims=True) + EPS) * g_ref[...]
    o_ref[...] = x


def _combine(h, y_sorted, dest, rt, mod, mod_idx_fn, part, g, *, rows, n_rows, norm):
    d = h.shape[1]
    nb = n_rows // rows
    row = pl.BlockSpec((rows, d), lambda i: (i, 0))
    return pl.pallas_call(
        functools.partial(_combine_body, rows=rows, norm=norm), grid=(nb,),
        in_specs=[pl.BlockSpec((None, 1, rows * TOP_K), lambda i: (i, 0, 0), memory_space=pltpu.SMEM),
                  row, pl.BlockSpec((rows, LANES), lambda i: (i, 0)),
                  pl.BlockSpec((None, 1, d), lambda i: (mod_idx_fn(i), 0, part)),
                  pl.BlockSpec((1, d), lambda i: (0, 0)), pl.BlockSpec(memory_space=pl.ANY)],
        out_specs=row, out_shape=jax.ShapeDtypeStruct((n_rows, d), F32),
        scratch_shapes=[pltpu.VMEM((TOP_K, rows, d), y_sorted.dtype), pltpu.SemaphoreType.DMA],
        compiler_params=_cparams(("arbitrary",)), name="moe_combine",
    )(dest.reshape(nb, 1, rows * TOP_K), h, rt, mod, g.reshape(1, d), y_sorted)


def kernel(x, c, ctx, c_ctx, ada_w, ada_b, norm_mix_g, norm_ffn_g, w_in, na_rpb, da_lambda, da_subln_g,
           ssm_conv_w, ssm_conv_b, ssm_dt_bias, ssm_a_log, ssm_d, ssm_norm_g, w_branch_a, w_branch_b,
           w_branch_c, w_out, ffn_w1, ffn_w3, ffn_w2, moe_router_w, moe_router_b, moe_w1, moe_w3, moe_w2,
           final_norm_g):
    batch, seq, d = x.shape
    ctx_len = ctx.shape[1]
    depth = ada_w.shape[0]
    n_lat, n_ctx = batch * seq, batch * ctx_len
    n_all = n_lat + n_ctx
    tm = next(t for t in (1024, 512, 256) if n_ctx % t == 0 and seq % t == 0)
    n_mod = 16
    assert batch + 1 <= n_mod
    lat_blocks = n_lat // tm
    per_seq = seq // tm
    te = 256
    assert n_ctx % te == 0 and seq % te == 0
    mod_idx_fn = lambda i: jnp.where(i < n_lat // te, i // (seq // te), batch)
    mod_idx_all = jnp.where(jnp.arange(n_all // tm) < lat_blocks, jnp.arange(n_all // tm) // per_seq, batch)
    layer_blocks = lambda nb, idx: jnp.full((nb,), idx, jnp.int32)

    na_w, da_w = NA_HEADS * NA_HEAD_DIM, DA_HEADS * 2 * DA_HEAD_DIM
    conv_dim = SSM_D_INNER + 2 * SSM_GROUPS * SSM_STATE
    col_qa, col_ka, col_va = 0, na_w, 2 * na_w
    col_qb, col_kb, col_vb = 3 * na_w, 3 * na_w + da_w, 3 * na_w + 2 * da_w
    col_z = 3 * na_w + 3 * da_w
    col_x = col_z + SSM_D_INNER
    col_dt = col_x + conv_dim
    col_gate = col_dt + 2 * SSM_HEADS
    n_main = 6144
    assert col_dt + LANES <= n_main and n_main % 512 == 0 and col_dt % LANES == 0

    h = jnp.concatenate([x.reshape(n_lat, d), ctx.reshape(n_ctx, d)], axis=0)
    cond = jnp.zeros((n_mod, d), F32).at[:batch].set(jax.nn.silu(c)).at[batch].set(jax.nn.silu(c_ctx))
    cond = cond.astype(BF16)
    rope_tabs = _rope_tables(seq, tm)
    zero_state = jnp.zeros((batch, 2, SSM_STATE, SSM_D_INNER), F32)
    w_gate = w_in[:, :, col_gate:]

    out = None
    for li in range(depth):
        last = li == depth - 1
        lam_init = 0.8 - 0.6 * math.exp(-0.3 * li)
        n_out = n_lat if last else n_all
        mod = _gmm(cond, [ada_w], layer_blocks(1, li), tm=n_mod, tn=512, epi="bias",
                   bias=ada_b[li].reshape(1, 6 * d)).reshape(n_mod, 1, 6 * d)
        u = _norm(h, norm_mix_g[li], tm=te, n_rows=n_all, mod=mod, mod_idx_fn=mod_idx_fn, parts=(0, 1))
        proj = _gmm(u, [w_in], layer_blocks(n_all // tm, li), tm=tm, tn=1024, n_cols=n_main)
        gates = _gmm(u, [w_gate], layer_blocks(n_out // tm, li), tm=tm, tn=1024, n_rows=n_out, epi="sigmoid",
                     out_dtype=BF16)

        o_a = _na(proj, na_rpb[li], batch=batch, seq=seq, ctx_len=ctx_len, q_col=col_qa, k_col=col_ka, v_col=col_va)
        q_r, k_r = _rope(proj, rope_tabs, tm=tm, n_rows=n_all, n_lat_rows=n_lat, seq=seq,
                         q_col=col_qb, k_col=col_kb, width=da_w)
        o_b = _da(q_r, k_r, proj, da_lambda[li].astype(F32), da_subln_g[li], lam_init, batch=batch, seq=seq,
                  ctx_len=ctx_len, v_col=col_vb, tq=256, latent=True)
        ssd_args = (ssm_conv_w[li], ssm_conv_b[li], ssm_dt_bias[li], ssm_a_log[li], ssm_d[li])
        y_ctx, s_ctx = _ssd(proj, *ssd_args, zero_state, batch=batch, n_tok=ctx_len, row0=n_lat,
                            x_col=col_x, dt_col=col_dt)
        y_lat, _ = _ssd(proj, *ssd_args, s_ctx, batch=batch, n_tok=seq, row0=0, x_col=col_x, dt_col=col_dt)
        o_c = _ssm_norm(y_lat, proj, ssm_norm_g[li], tm=te, row0=0, z_col=col_z)
        if not last:
            o_a_c = _ctx_attn(proj, batch=batch, seq=seq, ctx_len=ctx_len, q_col=col_qa, k_col=col_ka, v_col=col_va)
            o_b_c = _da(q_r, k_r, proj, da_lambda[li].astype(F32), da_subln_g[li], lam_init, batch=batch, seq=seq,
                        ctx_len=ctx_len, v_col=col_vb, tq=ctx_len, latent=False)
            o_c_c = _ssm_norm(y_ctx, proj, ssm_norm_g[li], tm=te, row0=n_lat, z_col=col_z)
            o_a = jnp.concatenate([o_a, o_a_c], axis=0)
            o_b = jnp.concatenate([o_b, o_b_c], axis=0)
            o_c = jnp.concatenate([o_c, o_c_c], axis=0)
        merged = _merge(o_a, o_b, o_c, gates, w_branch_a, w_branch_b, w_branch_c, li, tm=tm, tn=512, n_rows=n_out)
        h = _gmm(merged, [w_out], layer_blocks(n_out // tm, li), tm=tm, tn=1024, epi="resid", res=h,
                 mod=mod, mod_idx=mod_idx_all[:n_out // tm], mod_part=2, n_rows=n_out)

        j = li // 2
        if li % 2 == 0:
            tokens = _norm(h, norm_ffn_g[li], tm=te, n_rows=n_out, mod=mod, mod_idx_fn=mod_idx_fn, parts=(3, 4))
            hid = _gmm(tokens, [ffn_w1, ffn_w3], layer_blocks(n_out // tm, j), tm=tm, tn=512,
                       epi="swiglu", out_dtype=BF16)
            h = _gmm(hid, [ffn_w2], layer_blocks(n_out // tm, j), tm=tm, tn=512, epi="resid", res=h,
                     mod=mod, mod_idx=mod_idx_all[:n_out // tm], mod_part=5, n_rows=n_out, w_buffers=1)
            if last:
                out = _norm(h, final_norm_g, tm=te, n_rows=n_lat, out_dtype=F32)
        else:
            rw = jnp.zeros((d, LANES), F32).at[:, :N_EXPERTS].set(moe_router_w[j].astype(F32))
            rb = jnp.zeros((1, LANES), F32).at[0, :N_EXPERTS].set(moe_router_b[j].astype(F32))
            tokens, rt = _norm(h, norm_ffn_g[li], tm=te, n_rows=n_out, mod=mod, mod_idx_fn=mod_idx_fn,
                               parts=(3, 4), router=(rw, rb), out_dtype=F32)
            dest, slot_tok, block_expert, n_used = _route(rt[:, :TOP_K].astype(jnp.int32), MOE_ROWS)
            block_expert = block_expert + j * N_EXPERTS
            stack = lambda w: w.reshape((-1,) + w.shape[2:])
            x_sorted = _gather_rows(tokens, slot_tok, n_used, rows=MOE_ROWS, out_dtype=BF16)
            hid = _gmm(x_sorted, [stack(moe_w1), stack(moe_w3)], block_expert, tm=MOE_ROWS, tn=512, epi="swiglu",
                       out_dtype=BF16, n_used=n_used)
            y_sorted = _gmm(hid, [stack(moe_w2)], block_expert, tm=MOE_ROWS, tn=512, nk=2, n_used=n_used)
            h = _combine(h, y_sorted, dest, rt, mod, mod_idx_fn, 5, final_norm_g, rows=te, n_rows=n_out, norm=last)
            if last:
                out = h
    return out.reshape(batch, seq, d)
```

```python
import functools
import math

import numpy as np
import jax
import jax.numpy as jnp
from jax import lax
from jax.experimental import pallas as pl
from jax.experimental.pallas import tpu as pltpu

F32 = jnp.float32
BF16 = jnp.bfloat16

GRID_W = 64
NA_HEADS, NA_HEAD_DIM, NA_WIN_H, NA_WIN_W = 8, 64, 8, 16
DA_HEADS, DA_HEAD_DIM = 4, 64
ROPE_BASE = 10000.0
SSM_D_INNER, SSM_HEAD_DIM, SSM_GROUPS, SSM_STATE, SSM_CONV_W = 1024, 64, 2, 128, 5
SSM_HEADS = SSM_D_INNER // SSM_HEAD_DIM
N_EXPERTS, TOP_K = 8, 2
EPS = 1e-6
NEG_INF = -1e30

LANES = 128
SUBLANES = 8
VMEM_LIMIT_BYTES = 56 * 1024 * 1024
SSD_CHUNK = 128
MOE_ROWS = 1024
NA_ROW_UNROLL = 4
LOG2E = math.log2(math.e)


def _cparams(sem):
    return pltpu.CompilerParams(dimension_semantics=sem, vmem_limit_bytes=VMEM_LIMIT_BYTES)


def _dot(a, b):
    return jnp.dot(a, b, preferred_element_type=F32)


def _dot_nt(a, b):
    return lax.dot_general(a, b, (((1,), (1,)), ((), ())), preferred_element_type=F32)


def _silu(x):
    return x * jax.nn.sigmoid(x)


def _gmm_body(bexp_ref, neww_ref, mod_ref, nv_ref, x_ref, *rest, nk, n_w, epi, res_first_blocks):
    del bexp_ref, mod_ref
    w_refs, rest = rest[:n_w], rest[n_w:]
    if epi == "bias":
        bias_ref, rest = rest[0], rest[1:]
    elif epi == "resid":
        res_ref, rest = rest[0], rest[1:]
        if res_first_blocks is not None:
            res2_ref, rest = rest[0], rest[1:]
        gate_ref, rest = rest[0], rest[1:]
    o_ref, rest = rest[0], rest[1:]
    wbf_refs, acc_refs = rest[:n_w], rest[n_w:]
    i = pl.program_id(1)
    k = pl.program_id(2)

    def finalize(vals):
        if epi == "swiglu":
            o_ref[...] = (_silu(vals[0]) * vals[1]).astype(o_ref.dtype)
        elif epi == "bias":
            o_ref[...] = (vals[0] + bias_ref[...]).astype(o_ref.dtype)
        elif epi == "resid":
            res = res_ref[...]
            if res_first_blocks is not None:
                res = jnp.where(i < res_first_blocks, res, res2_ref[...])
            o_ref[...] = (res + gate_ref[...] * vals[0]).astype(o_ref.dtype)
        elif epi == "sigmoid":
            o_ref[...] = jax.nn.sigmoid(vals[0]).astype(o_ref.dtype)
        else:
            o_ref[...] = vals[0].astype(o_ref.dtype)

    @pl.when(jnp.logical_and(i >= nv_ref[0], k == nk - 1))
    def _():
        o_ref[...] = jnp.zeros_like(o_ref)

    @pl.when(i < nv_ref[0])
    def _():
        @pl.when(neww_ref[i] == 1)
        def _():
            for w_ref, wbf in zip(w_refs, wbf_refs):
                wbf[k] = w_ref[...].astype(BF16)

        x = x_ref[...]
        parts = [_dot(x, wbf[k]) for wbf in wbf_refs]
        if nk == 1:
            finalize(parts)
        else:
            @pl.when(k == 0)
            def _():
                for acc, p in zip(acc_refs, parts):
                    acc[...] = p

            @pl.when(k > 0)
            def _():
                for acc, p in zip(acc_refs, parts):
                    acc[...] += p

            @pl.when(k == nk - 1)
            def _():
                finalize([acc[...] for acc in acc_refs])


def _gmm(x, ws, bexp, *, tm, tn, nk=1, n_cols=None, epi="plain", out_dtype=F32, bias=None, res=None,
         res2=None, mod=None, mod_idx=None, mod_part=0, n_rows=None, n_used=None, w_buffers=2):
    m_rows = x.shape[0] if n_rows is None else n_rows
    k_dim = x.shape[1]
    n_dim = ws[0].shape[2] if n_cols is None else n_cols
    n_w = len(ws)
    tk = k_dim // nk
    nb = m_rows // tm
    res_first_blocks = None
    assert m_rows % tm == 0 and n_dim % tn == 0 and k_dim % nk == 0
    bexp = bexp.astype(jnp.int32)
    neww = jnp.concatenate([jnp.ones((1,), jnp.int32), (bexp[1:] != bexp[:-1]).astype(jnp.int32)])
    if mod_idx is None:
        mod_idx = jnp.zeros((nb,), jnp.int32)
    if n_used is None:
        n_used = jnp.full((1,), nb, jnp.int32)

    rb = lambda i, nv: jnp.minimum(i, nv[0] - 1)
    x_spec = pl.BlockSpec((tm, tk), lambda j, i, k, be, nw, md, nv: (rb(i, nv), k))
    w_map = lambda j, i, k, be, nw, md, nv: (be[rb(i, nv)], jnp.where(nw[rb(i, nv)] == 1, k, nk - 1), j)
    if w_buffers == 2:
        w_spec = pl.BlockSpec((None, tk, tn), w_map)
    else:
        w_spec = pl.BlockSpec((None, tk, tn), w_map, pipeline_mode=pl.Buffered(w_buffers))
    o_spec = pl.BlockSpec((tm, tn), lambda j, i, k, be, nw, md, nv: (i, j))
    in_specs = [x_spec] + [w_spec] * n_w
    args = [x] + list(ws)
    if epi == "bias":
        in_specs.append(pl.BlockSpec((1, tn), lambda j, i, k, be, nw, md, nv: (0, j)))
        args.append(bias)
    elif epi == "resid":
        part_off = mod_part * (n_dim // tn)
        if res2 is None:
            in_specs.append(o_spec)
            args.append(res)
        else:
            res_first_blocks = res.shape[0] // tm
            in_specs.append(pl.BlockSpec(
                (tm, tn), lambda j, i, k, be, nw, md, nv: (jnp.minimum(i, res_first_blocks - 1), j)))
            in_specs.append(pl.BlockSpec(
                (tm, tn), lambda j, i, k, be, nw, md, nv: (jnp.maximum(i - res_first_blocks, 0), j)))
            args += [res, res2]
        in_specs.append(pl.BlockSpec((None, 1, tn),
                                     lambda j, i, k, be, nw, md, nv: (md[rb(i, nv)], 0, part_off + j)))
        args.append(mod)
    scratch = [pltpu.VMEM((nk, tk, tn), BF16) for _ in range(n_w)]
    if nk > 1:
        scratch += [pltpu.VMEM((tm, tn), F32) for _ in range(n_w)]
    grid_spec = pltpu.PrefetchScalarGridSpec(
        num_scalar_prefetch=4, grid=(n_dim // tn, nb, nk), in_specs=in_specs, out_specs=o_spec,
        scratch_shapes=scratch)
    return pl.pallas_call(
        functools.partial(_gmm_body, nk=nk, n_w=n_w, epi=epi, res_first_blocks=res_first_blocks),
        grid_spec=grid_spec,
        out_shape=jax.ShapeDtypeStruct((m_rows, n_dim), out_dtype),
        compiler_params=_cparams(("arbitrary", "arbitrary", "arbitrary")),
        name="gmm_" + epi,
    )(bexp, neww, mod_idx.astype(jnp.int32), n_used.astype(jnp.int32), *args)


def _split2(v):
    hi = v.astype(BF16)
    lo = (v - hi.astype(F32)).astype(BF16)
    return hi, lo


def _norm_body(x_ref, *rest, modulate, router, first_blocks):
    if first_blocks is not None:
        x2_ref, rest = rest[0], rest[1:]
    g_ref, rest = rest[0], rest[1:]
    if modulate:
        shift_ref, scale_ref, rest = rest[0], rest[1], rest[2:]
    if router:
        rw_ref, rb_ref, rest = rest[0], rest[1], rest[2:]
    o_ref = rest[0]
    x = x_ref[...]
    if first_blocks is not None:
        x = jnp.where(pl.program_id(0) < first_blocks, x, x2_ref[...])
    y = x * lax.rsqrt(jnp.mean(x * x, axis=-1, keepdims=True) + EPS) * g_ref[...]
    if modulate:
        y = y * (1.0 + scale_ref[...]) + shift_ref[...]
    o_ref[...] = y.astype(o_ref.dtype)
    if router:
        rt_ref = rest[1]
        y_hi, y_lo = _split2(y)
        w_hi, w_lo = _split2(rw_ref[...])
        logits = _dot(y_hi, w_hi) + _dot(y_lo, w_hi) + _dot(y_hi, w_lo) + rb_ref[...]
        lane = lax.broadcasted_iota(jnp.int32, logits.shape, 1).astype(F32)
        lg = jnp.where(lane < N_EXPERTS, logits, -jnp.inf)
        m1 = jnp.max(lg, axis=-1, keepdims=True)
        i1 = jnp.min(jnp.where(lg == m1, lane, float(LANES)), axis=-1, keepdims=True)
        lg2 = jnp.where(lane == i1, -jnp.inf, lg)
        m2 = jnp.max(lg2, axis=-1, keepdims=True)
        i2 = jnp.min(jnp.where(lg2 == m2, lane, float(LANES)), axis=-1, keepdims=True)
        e2 = jnp.exp(m2 - m1)
        g1 = 1.0 / (1.0 + e2)
        rt = jnp.where(lane == 0, i1, jnp.where(lane == 1, i2, jnp.where(lane == 2, g1, jnp.where(lane == 3, e2 * g1, 0.0))))
        rt_ref[...] = rt


def _norm(x, g, *, tm, n_rows, mod=None, mod_idx_fn=None, parts=(0, 1), out_dtype=BF16, router=None, x2=None):
    d = x.shape[1]
    nb = n_rows // tm
    row = pl.BlockSpec((tm, d), lambda i: (i, 0))
    first_blocks = None if x2 is None else x.shape[0] // tm
    if x2 is None:
        in_specs, args = [row], [x]
    else:
        in_specs, args = list(_two_source_specs((tm, d), first_blocks, lambda i: i)), [x, x2]
    in_specs.append(pl.BlockSpec((1, d), lambda i: (0, 0)))
    args.append(g.reshape(1, d))
    modulate = mod is not None
    if modulate:
        for p in parts:
            in_specs.append(pl.BlockSpec((None, 1, d), lambda i, p=p: (mod_idx_fn(i), 0, p)))
            args.append(mod)
    out_shape = [jax.ShapeDtypeStruct((n_rows, d), out_dtype)]
    out_specs = [row]
    if router is not None:
        rw, rb = router
        in_specs += [pl.BlockSpec((d, LANES), lambda i: (0, 0)), pl.BlockSpec((1, LANES), lambda i: (0, 0))]
        args += [rw, rb]
        out_shape.append(jax.ShapeDtypeStruct((n_rows, LANES), F32))
        out_specs.append(pl.BlockSpec((tm, LANES), lambda i: (i, 0)))
    res = pl.pallas_call(
        functools.partial(_norm_body, modulate=modulate, router=router is not None, first_blocks=first_blocks),
        grid=(nb,), in_specs=in_specs, out_specs=out_specs, out_shape=out_shape,
        compiler_params=_cparams(("arbitrary",)), name="rmsnorm",
    )(*args)
    return res if router is not None else res[0]


def _rope_body(q_ref, k_ref, c_ref, sm_ref, sp_ref, qo_ref, ko_ref, *, q_scale):
    c, sm, sp = c_ref[...], sm_ref[...], sp_ref[...]
    width = q_ref.shape[1]
    for src, dst, scale in ((q_ref, qo_ref, q_scale), (k_ref, ko_ref, 1.0)):
        for g in range(width // LANES):
            sl = slice(g * LANES, (g + 1) * LANES)
            x = src[:, sl]
            y = x * c + pltpu.roll(x, LANES - 16, 1) * sm + pltpu.roll(x, 16, 1) * sp
            dst[:, sl] = (y * scale).astype(dst.dtype)


def _rope_tables(seq, n_id_rows):
    t = jnp.arange(seq)
    pos = jnp.stack([t // GRID_W, t % GRID_W], axis=-1).astype(F32)
    n_freq = DA_HEAD_DIM // 4
    inv_freq = ROPE_BASE ** (-jnp.arange(n_freq, dtype=F32) / n_freq)
    ang = pos[:, :, None] * inv_freq
    d = np.arange(LANES) % DA_HEAD_DIM
    kind, which, f = d // 32, (d % 32) // 16, d % 16
    a = ang[:, kind, f]
    cos, sin = jnp.cos(a), jnp.sin(a)
    sm = jnp.where(which == 0, -sin, 0.0)
    sp = jnp.where(which == 1, sin, 0.0)
    pad = lambda v, fill: jnp.concatenate([v, jnp.full((n_id_rows, LANES), fill, F32)], axis=0)
    return pad(cos, 1.0), pad(sm, 0.0), pad(sp, 0.0)


def _rope(proj, tabs, *, tm, n_rows, n_lat_rows, seq, q_col, k_col, width):
    lat_blocks = n_lat_rows // tm
    per_seq = seq // tm
    tab_idx = lambda i: (jnp.where(i < lat_blocks, i % per_seq, per_seq), 0)
    out = jax.ShapeDtypeStruct((n_rows, width), BF16)
    return pl.pallas_call(
        functools.partial(_rope_body, q_scale=DA_HEAD_DIM ** -0.5 * LOG2E),
        grid=(n_rows // tm,),
        in_specs=[pl.BlockSpec((tm, width), lambda i: (i, q_col // width)),
                  pl.BlockSpec((tm, width), lambda i: (i, k_col // width)),
                  pl.BlockSpec((tm, LANES), tab_idx), pl.BlockSpec((tm, LANES), tab_idx),
                  pl.BlockSpec((tm, LANES), tab_idx)],
        out_specs=[pl.BlockSpec((tm, width), lambda i: (i, 0))] * 2,
        out_shape=[out, out], compiler_params=_cparams(("arbitrary",)), name="rope",
    )(proj, proj, *tabs)


def _na_body(q_ref, k_ref, v_ref, kc_ref, vc_ref, tb_ref, o_ref, kb, vb, kcb, vcb, *, rows, wh):
    kb[...] = k_ref[...].astype(BF16)
    vb[...] = v_ref[...].astype(BF16)
    kcb[...] = kc_ref[...].astype(BF16)
    vcb[...] = vc_ref[...].astype(BF16)
    lane = lax.broadcasted_iota(jnp.int32, (GRID_W, LANES), 1)
    first = lane < NA_HEAD_DIM
    scale = NA_HEAD_DIM ** -0.5

    def body(grp, carry):
        rr = [grp * NA_ROW_UNROLL + j for j in range(NA_ROW_UNROLL)]
        rs = [jnp.clip(r - wh // 2, 0, rows - wh) for r in rr]
        row_of = lambda r: pl.ds(pl.multiple_of(r * GRID_W, GRID_W), GRID_W)
        win_of = lambda r0: pl.ds(pl.multiple_of(r0 * GRID_W, GRID_W), wh * GRID_W)
        qq = []
        for r in rr:
            q = q_ref[row_of(r), :] * scale
            qq.append(jnp.concatenate([jnp.where(first, q, 0.0), jnp.where(first, 0.0, q)], axis=0).astype(BF16))
        s_l = [_dot_nt(x, kb[win_of(r0), :]) + tb_ref[r0 - r + (NA_WIN_H - 1)] for x, r, r0 in zip(qq, rr, rs)]
        s_c = [_dot_nt(x, kcb[...]) for x in qq]
        m = [jnp.maximum(jnp.max(a, axis=-1, keepdims=True), jnp.max(b, axis=-1, keepdims=True))
             for a, b in zip(s_l, s_c)]
        p_l = [jnp.exp(a - mm) for a, mm in zip(s_l, m)]
        p_c = [jnp.exp(b - mm) for b, mm in zip(s_c, m)]
        den = [jnp.sum(a, axis=-1, keepdims=True) + jnp.sum(b, axis=-1, keepdims=True) for a, b in zip(p_l, p_c)]
        o = [_dot(a.astype(BF16), vb[win_of(r0), :]) + _dot(b.astype(BF16), vcb[...])
             for a, b, r0 in zip(p_l, p_c, rs)]
        for r, oo, dd in zip(rr, o, den):
            oo = oo / dd
            o_ref[row_of(r), :] = jnp.where(first, oo[:GRID_W], oo[GRID_W:]).astype(o_ref.dtype)
        return carry

    assert rows % NA_ROW_UNROLL == 0
    lax.fori_loop(0, rows // NA_ROW_UNROLL, body, 0)


def _na_bias_table(rpb, rows):
    wh = min(NA_WIN_H, rows)
    qc = np.arange(GRID_W)[:, None]
    kc = np.arange(GRID_W)[None, :]
    ws = np.clip(qc - NA_WIN_W // 2, 0, GRID_W - NA_WIN_W)
    ok = (kc >= ws) & (kc < ws + NA_WIN_W)
    n_drow, n_dcol = 2 * NA_WIN_H - 1, 2 * NA_WIN_W - 1
    off = GRID_W - NA_WIN_W
    line = jnp.full((NA_HEADS, n_drow, 2 * GRID_W), NEG_INF, F32).at[:, :, off:off + n_dcol].set(rpb.astype(F32))
    span = 2 * GRID_W - 1
    t = jnp.tile(line, (1, 1, GRID_W))[:, :, :GRID_W * span].reshape(NA_HEADS, n_drow, GRID_W, span)
    t = jnp.where(ok, t[:, :, :, GRID_W - 1:], NEG_INF)
    n_d0 = NA_WIN_H
    t = jnp.stack([t[:, d0:d0 + wh] for d0 in range(n_d0)], axis=1)
    t = t.transpose(0, 1, 3, 2, 4).reshape(NA_HEADS // 2, 2, n_d0, GRID_W, wh * GRID_W)
    return t.transpose(0, 2, 1, 3, 4).reshape(NA_HEADS // 2, n_d0, 2 * GRID_W, wh * GRID_W)


def _na(proj, rpb, *, batch, seq, ctx_len, q_col, k_col, v_col):
    rows = seq // GRID_W
    wh = min(NA_WIN_H, rows)
    pairs = NA_HEADS // 2
    tb = _na_bias_table(rpb, rows)
    ctx0 = batch * seq // ctx_len
    lat = lambda col: pl.BlockSpec((seq, LANES), lambda b, p: (b, col // LANES + p))
    ctx = lambda col: pl.BlockSpec((ctx_len, LANES), lambda b, p: (ctx0 + b, col // LANES + p))
    return pl.pallas_call(
        functools.partial(_na_body, rows=rows, wh=wh),
        grid=(batch, pairs),
        in_specs=[lat(q_col), lat(k_col), lat(v_col), ctx(k_col), ctx(v_col),
                  pl.BlockSpec((None, NA_WIN_H, 2 * GRID_W, wh * GRID_W), lambda b, p: (p, 0, 0, 0))],
        out_specs=pl.BlockSpec((seq, LANES), lambda b, p: (b, p)),
        out_shape=jax.ShapeDtypeStruct((batch * seq, pairs * LANES), BF16),
        scratch_shapes=[pltpu.VMEM((seq, LANES), BF16), pltpu.VMEM((seq, LANES), BF16),
                        pltpu.VMEM((ctx_len, LANES), BF16), pltpu.VMEM((ctx_len, LANES), BF16)],
        compiler_params=_cparams(("arbitrary", "arbitrary")), name="na_attn",
    )(proj, proj, proj, proj, proj, tb)


def _ctx_attn_body(q_ref, k_ref, v_ref, o_ref):
    kb = k_ref[...].astype(BF16)
    vb = v_ref[...].astype(BF16)
    q = q_ref[...] * (NA_HEAD_DIM ** -0.5)
    lane = lax.broadcasted_iota(jnp.int32, q.shape, 1)
    first = lane < NA_HEAD_DIM
    outs = []
    for hh in range(2):
        qm = jnp.where(first if hh == 0 else jnp.logical_not(first), q, 0.0).astype(BF16)
        s = _dot_nt(qm, kb)
        p = jnp.exp(s - jnp.max(s, axis=-1, keepdims=True))
        outs.append(_dot(p.astype(BF16), vb) / jnp.sum(p, axis=-1, keepdims=True))
    o_ref[...] = jnp.where(first, outs[0], outs[1]).astype(o_ref.dtype)


def _ctx_attn(proj, *, batch, seq, ctx_len, q_col, k_col, v_col):
    pairs = NA_HEADS // 2
    ctx0 = batch * seq // ctx_len
    ctx = lambda col: pl.BlockSpec((ctx_len, LANES), lambda b, p: (ctx0 + b, col // LANES + p))
    return pl.pallas_call(
        _ctx_attn_body, grid=(batch, pairs),
        in_specs=[ctx(q_col), ctx(k_col), ctx(v_col)],
        out_specs=pl.BlockSpec((ctx_len, LANES), lambda b, p: (b, p)),
        out_shape=jax.ShapeDtypeStruct((batch * ctx_len, pairs * LANES), BF16),
        compiler_params=_cparams(("arbitrary", "arbitrary")), name="ctx_attn",
    )(proj, proj, proj)


def _da_body(lv_ref, q_ref, *rest, lam_init, has_lat):
    if has_lat:
        kl_ref, vl_ref, rest = rest[0], rest[1], rest[2:]
    kc_ref, vc_ref, g_ref, o_ref = rest[:4]
    scr = rest[4:]
    t = pl.program_id(2)
    if has_lat:
        vlb, vcb = scr

        @pl.when(t == 0)
        def _():
            vlb[...] = vl_ref[...].astype(BF16)
            vcb[...] = vc_ref[...].astype(BF16)
    else:
        (vcb,) = scr
        vcb[...] = vc_ref[...].astype(BF16)

    lv = lv_ref[...]
    lam = (jnp.exp(jnp.sum(lv[0:1] * lv[1:2], axis=-1, keepdims=True))
           - jnp.exp(jnp.sum(lv[2:3] * lv[3:4], axis=-1, keepdims=True)) + lam_init)
    q = q_ref[...]
    tq = q.shape[0]
    lane = lax.broadcasted_iota(jnp.int32, q.shape, 1)
    zero = jnp.zeros_like(q)
    qq = jnp.concatenate([jnp.where(lane < DA_HEAD_DIM, q, zero), jnp.where(lane >= DA_HEAD_DIM, q, zero)], axis=0)
    s_c = _dot_nt(qq, kc_ref[...])
    m = jnp.max(s_c, axis=-1, keepdims=True)
    if has_lat:
        s_l = _dot_nt(qq, kl_ref[...])
        m = jnp.maximum(m, jnp.max(s_l, axis=-1, keepdims=True))
        p_l = jnp.exp2(s_l - m)
    p_c = jnp.exp2(s_c - m)
    den = jnp.sum(p_c, axis=-1, keepdims=True)
    if has_lat:
        den = den + jnp.sum(p_l, axis=-1, keepdims=True)
    ratio = lam * den[:tq] / den[tq:]
    o = _dot((p_c[:tq] - ratio * p_c[tq:]).astype(BF16), vcb[...])
    if has_lat:
        o = o + _dot((p_l[:tq] - ratio * p_l[tq:]).astype(BF16), vlb[...])
    o = o / den[:tq]
    o = o * lax.rsqrt(jnp.mean(o * o, axis=-1, keepdims=True) + EPS) * g_ref[...]
    o_ref[...] = (o * (1.0 - lam_init)).astype(o_ref.dtype)


def _da(q_r, k_r, proj, lam_vec, subln_g, lam_init, *, batch, seq, ctx_len, v_col, tq, latent):
    ctx0 = batch * seq // ctx_len
    vcb0 = v_col // LANES
    g2 = subln_g.reshape(1, 2 * DA_HEAD_DIM).astype(F32)
    small = [pl.BlockSpec((4, DA_HEAD_DIM), lambda b, h, t: (0, 0))]
    gspec = pl.BlockSpec((1, LANES), lambda b, h, t: (0, 0))
    kc = pl.BlockSpec((ctx_len, LANES), lambda b, h, t: (ctx0 + b, h))
    vc = pl.BlockSpec((ctx_len, LANES), lambda b, h, t: (ctx0 + b, vcb0 + h))
    if latent:
        nq = seq // tq
        qs = pl.BlockSpec((tq, LANES), lambda b, h, t: (b * nq + t, h))
        kl = pl.BlockSpec((seq, LANES), lambda b, h, t: (b, h))
        vl = pl.BlockSpec((seq, LANES), lambda b, h, t: (b, vcb0 + h))
        in_specs = small + [qs, kl, vl, kc, vc, gspec]
        args = (lam_vec, q_r, k_r, proj, k_r, proj, g2)
        scratch = [pltpu.VMEM((seq, LANES), BF16), pltpu.VMEM((ctx_len, LANES), BF16)]
        n_out = batch * seq
    else:
        nq = ctx_len // tq
        cq0 = batch * seq // tq
        qs = pl.BlockSpec((tq, LANES), lambda b, h, t: (cq0 + b * nq + t, h))
        in_specs = small + [qs, kc, vc, gspec]
        args = (lam_vec, q_r, k_r, proj, g2)
        scratch = [pltpu.VMEM((ctx_len, LANES), BF16)]
        n_out = batch * ctx_len
    return pl.pallas_call(
        functools.partial(_da_body, lam_init=lam_init, has_lat=latent),
        grid=(batch, DA_HEADS, nq), in_specs=in_specs,
        out_specs=pl.BlockSpec((tq, LANES), lambda b, h, t: (b * nq + t, h)),
        out_shape=jax.ShapeDtypeStruct((n_out, DA_HEADS * LANES), BF16),
        scratch_shapes=scratch,
        compiler_params=_cparams(("arbitrary", "arbitrary", "arbitrary")),
        name="diff_attn_lat" if latent else "diff_attn_ctx",
    )(*args)


def _ssd_body(x0_ref, x1_ref, bc_ref, x0p_ref, x1p_ref, bcp_ref, x0n_ref, x1n_ref, bcn_ref, dt_ref,
              cw_ref, cb_ref, prm_ref, dsk_ref, ex_ref, init_ref, y_ref, sout_ref, st_ref, u_ref, *, nc):
    q_len = SSD_CHUNK
    dirn = pl.program_id(1)
    z = pl.program_id(2)
    zz = jnp.where(dirn == 0, z, nc - 1 - z)
    fwd = dirn == 0
    half = SSM_D_INNER // SSM_GROUPS
    heads_per_group = SSM_HEADS // SSM_GROUPS

    @pl.when(z == 0)
    def _():
        st_ref[...] = init_ref[...]

    def conv_piece(main_ref, prev_ref, next_ref, c0):
        width = main_ref.shape[1]
        prev = jnp.where(zz == 0, 0.0, prev_ref[...])
        nxt = jnp.where(zz == nc - 1, 0.0, next_ref[...])
        ext = jnp.concatenate([prev, main_ref[...], nxt], axis=0)
        n_ext = ext.shape[0]
        acc = jnp.zeros((q_len, width), F32) + cb_ref[:, c0:c0 + width]
        for tap in range(SSM_CONV_W):
            sh = (SSM_CONV_W // 2 - tap) % n_ext
            e = ext if sh == 0 else pltpu.roll(ext, sh, 0)
            acc = acc + e[SUBLANES:SUBLANES + q_len] * cw_ref[tap:tap + 1, c0:c0 + width]
        return _silu(acc)

    @pl.when(fwd)
    def _():
        u_ref[zz, :, 0:half] = conv_piece(x0_ref, x0p_ref, x0n_ref, 0)
        u_ref[zz, :, half:2 * half] = conv_piece(x1_ref, x1p_ref, x1n_ref, half)
        u_ref[zz, :, SSM_D_INNER:] = conv_piece(bc_ref, bcp_ref, bcn_ref, SSM_D_INNER)

    xs = (u_ref[zz, :, 0:half], u_ref[zz, :, half:2 * half])
    bcm = u_ref[zz, :, SSM_D_INNER:]

    raw = dt_ref[...]
    raw = jnp.where(fwd, raw, pltpu.roll(raw, LANES - SSM_HEADS, 1))
    xb = raw + prm_ref[0:1, :]
    dtv = jnp.maximum(xb, 0.0) + jnp.log1p(jnp.exp(-jnp.abs(xb)))
    a = dtv * (-jnp.exp(prm_ref[1:2, :]))
    rowi = lax.broadcasted_iota(jnp.int32, (q_len, LANES), 0)
    coli = lax.broadcasted_iota(jnp.int32, (q_len, LANES), 1)
    cum = a
    sft = 1
    while sft < q_len:
        cum = cum + jnp.where(rowi >= sft, pltpu.roll(cum, sft, 0), 0.0)
        sft *= 2
    tot = cum[q_len - 1:q_len, :]
    g = jnp.where(fwd, cum, cum - a)
    e_g = jnp.exp(g)
    e_tg = jnp.exp(tot - g)
    w_state = jnp.where(fwd, e_tg, e_g)
    w_yoff = jnp.where(fwd, e_g, e_tg)
    e_tot = jnp.exp(tot)
    gs = g * jnp.where(fwd, LOG2E, -LOG2E)
    gs_t = gs.T
    tri = jnp.where(fwd, rowi - coli, coli - rowi) >= 0

    stack = jnp.concatenate([dtv, dtv * w_state, w_yoff, jnp.broadcast_to(e_tot, (SUBLANES, LANES))], axis=0)
    s_hi, s_lo = _split2(stack)
    ex = ex_ref[...]
    expd = _dot(s_hi, ex) + _dot(s_lo, ex)
    dt_e, dts_e, wy_e = expd[0:q_len], expd[q_len:2 * q_len], expd[2 * q_len:3 * q_len]
    tot_e = expd[3 * q_len:3 * q_len + 1]

    lane = lax.broadcasted_iota(jnp.int32, (q_len, LANES), 1)
    first = lane < SSM_HEAD_DIM
    for grp in range(SSM_GROUPS):
        x_g = xs[grp]
        csl = slice(grp * half, (grp + 1) * half)
        b_g = bcm[:, grp * SSM_STATE:(grp + 1) * SSM_STATE]
        c_g = bcm[:, (SSM_GROUPS + grp) * SSM_STATE:(SSM_GROUPS + grp + 1) * SSM_STATE].astype(BF16)
        xdt = (x_g * dt_e[:, csl]).astype(BF16)
        xdts = (x_g * dts_e[:, csl]).astype(BF16)
        cb = _dot_nt(c_g, b_g.astype(BF16))
        st_g = st_ref[:, csl]
        y_g = _dot(c_g, st_g.astype(BF16)) * wy_e[:, csl] + dsk_ref[:, csl] * x_g
        for pr in range(heads_per_group // 2):
            outs = []
            for hh in range(2):
                col = grp * heads_per_group + 2 * pr + hh
                seg = gs[:, col:col + 1] - gs_t[col:col + 1, :]
                mat = (cb * jnp.where(tri, jnp.exp2(seg), 0.0)).astype(BF16)
                outs.append(_dot(mat, xdt[:, pr * LANES:(pr + 1) * LANES]))
            lo = grp * half + pr * LANES
            y_ref[:, lo:lo + LANES] = y_g[:, pr * LANES:(pr + 1) * LANES] + jnp.where(first, outs[0], outs[1])
        st_ref[:, csl] = tot_e[:, csl] * st_g + _dot(b_g.T.astype(BF16), xdts)

    @pl.when(z == nc - 1)
    def _():
        sout_ref[...] = st_ref[...]


def _ssd(proj, conv_w, conv_b, dt_bias, a_log, d_skip, init, *, batch, n_tok, row0, x_col, dt_col):
    q_len = SSD_CHUNK
    nc = n_tok // q_len
    half = SSM_D_INNER // SSM_GROUPS
    conv_dim = SSM_D_INNER + 2 * SSM_GROUPS * SSM_STATE
    rb0 = row0 // q_len
    hb = q_len // SUBLANES
    n_halo = proj.shape[0] // SUBLANES
    zz = lambda d, z: jnp.where(d == 0, z, nc - 1 - z)
    rb = lambda b, d, z: rb0 + b * nc + zz(d, z)
    rbx = lambda b, d, z: rb0 + b * nc + jnp.where(d == 0, z, nc - 1)
    main = lambda c: pl.BlockSpec((q_len, half), lambda b, d, z: (rbx(b, d, z), c))
    prev = lambda c: pl.BlockSpec((SUBLANES, half), lambda b, d, z: (jnp.maximum(rbx(b, d, z) * hb - 1, 0), c))
    nxt = lambda c: pl.BlockSpec((SUBLANES, half),
                                 lambda b, d, z: (jnp.minimum(rbx(b, d, z) * hb + hb, n_halo - 1), c))
    cols = [x_col // half + j for j in range(conv_dim // half)]
    const2 = lambda shape: pl.BlockSpec(shape, lambda b, d, z: (0, 0))
    prm = jnp.zeros((2, SUBLANES, LANES), F32)
    prm = prm.at[:, 0, :SSM_HEADS].set(dt_bias.astype(F32)).at[:, 1, :SSM_HEADS].set(a_log.astype(F32))
    dsk = jnp.repeat(d_skip.astype(F32), SSM_HEAD_DIM, axis=-1).reshape(2, 1, SSM_D_INNER)
    ex = (np.arange(LANES)[:, None] == (np.arange(SSM_D_INNER) // SSM_HEAD_DIM)[None, :])
    ex = jnp.asarray(ex, BF16)
    cw = jnp.zeros((SUBLANES, conv_dim), F32).at[:SSM_CONV_W].set(conv_w.astype(F32))
    state_spec = pl.BlockSpec((None, None, SSM_STATE, SSM_D_INNER), lambda b, d, z: (b, d, 0, 0))
    y, s_out = pl.pallas_call(
        functools.partial(_ssd_body, nc=nc),
        grid=(batch, 2, nc),
        in_specs=[main(cols[0]), main(cols[1]), main(cols[2]), prev(cols[0]), prev(cols[1]), prev(cols[2]),
                  nxt(cols[0]), nxt(cols[1]), nxt(cols[2]),
                  pl.BlockSpec((q_len, LANES), lambda b, d, z: (rb(b, d, z), dt_col // LANES)),
                  const2((SUBLANES, conv_dim)), const2((1, conv_dim)),
                  pl.BlockSpec((None, SUBLANES, LANES), lambda b, d, z: (d, 0, 0)),
                  pl.BlockSpec((None, 1, SSM_D_INNER), lambda b, d, z: (d, 0, 0)),
                  const2((LANES, SSM_D_INNER)), state_spec],
        out_specs=[pl.BlockSpec((None, q_len, SSM_D_INNER), lambda b, d, z: (d, b * nc + zz(d, z), 0)),
                   state_spec],
        out_shape=[jax.ShapeDtypeStruct((2, batch * n_tok, SSM_D_INNER), F32),
                   jax.ShapeDtypeStruct((batch, 2, SSM_STATE, SSM_D_INNER), F32)],
        scratch_shapes=[pltpu.VMEM((SSM_STATE, SSM_D_INNER), F32), pltpu.VMEM((nc, q_len, conv_dim), F32)],
        compiler_params=_cparams(("arbitrary", "arbitrary", "arbitrary")), name="ssd_scan",
    )(proj, proj, proj, proj, proj, proj, proj, proj, proj, proj, cw, conv_b.reshape(1, conv_dim).astype(F32),
      prm, dsk, ex, init)
    return y, s_out


def _ssm_norm_body(y_ref, z_ref, g_ref, o_ref):
    yz = (y_ref[0] + y_ref[1]) * _silu(z_ref[...])
    gw = SSM_D_INNER // SSM_GROUPS
    for grp in range(SSM_GROUPS):
        v = yz[:, grp * gw:(grp + 1) * gw]
        v = v * lax.rsqrt(jnp.mean(v * v, axis=-1, keepdims=True) + EPS)
        o_ref[:, grp * gw:(grp + 1) * gw] = (v * g_ref[:, grp * gw:(grp + 1) * gw]).astype(o_ref.dtype)


def _ssm_norm(y, proj, g, *, tm, row0, z_col):
    n = y.shape[1]
    zb0 = row0 // tm
    return pl.pallas_call(
        _ssm_norm_body, grid=(n // tm,),
        in_specs=[pl.BlockSpec((2, tm, SSM_D_INNER), lambda i: (0, i, 0)),
                  pl.BlockSpec((tm, SSM_D_INNER), lambda i: (zb0 + i, z_col // SSM_D_INNER)),
                  pl.BlockSpec((1, SSM_D_INNER), lambda i: (0, 0))],
        out_specs=pl.BlockSpec((tm, SSM_D_INNER), lambda i: (i, 0)),
        out_shape=jax.ShapeDtypeStruct((n, SSM_D_INNER), BF16),
        compiler_params=_cparams(("arbitrary",)), name="ssm_gated_norm",
    )(y, proj, g.reshape(1, SSM_D_INNER).astype(F32))


def _two_source_specs(block, first_blocks, row_block_of):
    first = pl.BlockSpec(block, lambda *g: (jnp.minimum(row_block_of(*g), first_blocks - 1), 0))
    second = pl.BlockSpec(block, lambda *g: (jnp.maximum(row_block_of(*g) - first_blocks, 0), 0))
    return first, second


def _merge_body(*refs, lat_blocks, has_ctx):
    n_src = 6 if has_ctx else 3
    src, (ga_ref, gb_ref, gc_ref, wa_ref, wb_ref, wc_ref, o_ref, wab, wbb, wcb) = refs[:n_src], refs[n_src:]
    i = pl.program_id(1)

    @pl.when(i == 0)
    def _():
        wab[...] = wa_ref[...].astype(BF16)
        wbb[...] = wb_ref[...].astype(BF16)
        wcb[...] = wc_ref[...].astype(BF16)

    if has_ctx:
        oa, ob, oc = (jnp.where(i < lat_blocks, src[2 * n][...], src[2 * n + 1][...]) for n in range(3))
    else:
        oa, ob, oc = (r[...] for r in src)
    acc = ga_ref[...].astype(F32) * _dot(oa, wab[...])
    acc = acc + gb_ref[...].astype(F32) * _dot(ob, wbb[...])
    acc = acc + gc_ref[...].astype(F32) * _dot(oc, wcb[...])
    o_ref[...] = acc.astype(o_ref.dtype)


def _merge(o_lat, o_ctx, gates, w_a, w_b, w_c, layer, *, tm, tn, n_rows):
    d = w_a.shape[2]
    nj = d // tn
    lat_blocks = o_lat[0].shape[0] // tm
    gate = lambda x: pl.BlockSpec((tm, tn), lambda j, i: (i, x * nj + j))
    wsp = lambda kk: pl.BlockSpec((None, kk, tn), lambda j, i: (layer, 0, j))
    widths = (w_a.shape[1], w_b.shape[1], w_c.shape[1])
    src_specs, src = [], []
    for n, kk in enumerate(widths):
        if o_ctx is None:
            src_specs.append(pl.BlockSpec((tm, kk), lambda j, i: (i, 0)))
            src.append(o_lat[n])
        else:
            src_specs += _two_source_specs((tm, kk), lat_blocks, lambda j, i: i)
            src += [o_lat[n], o_ctx[n]]
    return pl.pallas_call(
        functools.partial(_merge_body, lat_blocks=lat_blocks, has_ctx=o_ctx is not None), grid=(nj, n_rows // tm),
        in_specs=src_specs + [gate(0), gate(1), gate(2)] + [wsp(kk) for kk in widths],
        out_specs=pl.BlockSpec((tm, tn), lambda j, i: (i, j)),
        out_shape=jax.ShapeDtypeStruct((n_rows, d), BF16),
        scratch_shapes=[pltpu.VMEM((kk, tn), BF16) for kk in widths],
        compiler_params=_cparams(("arbitrary", "arbitrary")), name="branch_merge",
    )(*src, gates, gates, gates, w_a, w_b, w_c)


def _route(top_idx, blk):
    n = top_idx.shape[0]
    e_flat = top_idx.reshape(-1)
    onehot = (e_flat[:, None] == jnp.arange(N_EXPERTS, dtype=jnp.int32)[None, :]).astype(jnp.int32)
    counts = jnp.sum(onehot, axis=0)
    rank = jnp.sum((jnp.cumsum(onehot, axis=0) - onehot) * onehot, axis=1)
    padded = (counts + blk - 1) // blk * blk
    pad_end = jnp.cumsum(padded)
    pad_start = pad_end - padded
    dest = jnp.sum(onehot * pad_start[None, :], axis=1) + rank
    n_blocks = -(-(n * TOP_K) // blk) + N_EXPERTS
    cap = n_blocks * blk
    tok_flat = jnp.repeat(jnp.arange(n, dtype=jnp.int32), TOP_K)
    slot_tok = jnp.zeros((cap,), jnp.int32).at[dest].set(tok_flat)
    blk_start = jnp.arange(n_blocks, dtype=jnp.int32) * blk
    block_expert = jnp.clip(jnp.sum((pad_end[None, :] <= blk_start[:, None]).astype(jnp.int32), axis=1),
                            0, N_EXPERTS - 1)
    n_used = (pad_end[-1] // blk).astype(jnp.int32).reshape(1)
    return dest.astype(jnp.int32), slot_tok, block_expert, n_used


GATHER_UNROLL = 8


def _row_copy(src_hbm, row, buf, slot, sem):
    return pltpu.make_async_copy(src_hbm.at[pl.ds(row, 1), :], buf.at[pl.ds(slot, 1), :], sem)


def _gather_body(nv_ref, idx_ref, src_hbm, o_ref, buf, sem, *, rows):
    @pl.when(pl.program_id(0) >= nv_ref[0])
    def _():
        o_ref[...] = jnp.zeros_like(o_ref)

    @pl.when(pl.program_id(0) < nv_ref[0])
    def _():
        def issue(g, c):
            for u in range(GATHER_UNROLL):
                r = g * GATHER_UNROLL + u
                _row_copy(src_hbm, idx_ref[0, r], buf, r, sem).start(priority=u % 2)
            return c

        def drain(g, c):
            for u in range(GATHER_UNROLL):
                r = g * GATHER_UNROLL + u
                _row_copy(src_hbm, idx_ref[0, r], buf, r, sem).wait()
            return c

        lax.fori_loop(0, rows // GATHER_UNROLL, issue, 0)
        lax.fori_loop(0, rows // GATHER_UNROLL, drain, 0)
        o_ref[...] = buf[...].astype(o_ref.dtype)


def _gather_rows(src, idx, n_used, *, rows, out_dtype):
    n, d = src.shape
    nb = idx.shape[0] // rows
    assert rows % GATHER_UNROLL == 0
    blk = lambda i, nv: jnp.minimum(i, nv[0] - 1)
    grid_spec = pltpu.PrefetchScalarGridSpec(
        num_scalar_prefetch=1, grid=(nb,),
        in_specs=[pl.BlockSpec((None, 1, rows), lambda i, nv: (blk(i, nv), 0, 0), memory_space=pltpu.SMEM),
                  pl.BlockSpec(memory_space=pl.ANY)],
        out_specs=pl.BlockSpec((rows, d), lambda i, nv: (i, 0)),
        scratch_shapes=[pltpu.VMEM((rows, d), src.dtype), pltpu.SemaphoreType.DMA])
    return pl.pallas_call(
        functools.partial(_gather_body, rows=rows), grid_spec=grid_spec,
        out_shape=jax.ShapeDtypeStruct((nb * rows, d), out_dtype),
        compiler_params=_cparams(("arbitrary",)), name="moe_dispatch",
    )(n_used, idx.reshape(nb, 1, rows), src)


def _combine_body(dest_ref, h_ref, rt_ref, gate_ref, g_ref, y_hbm, o_ref, buf, sem, *, rows, norm):
    def issue(g, c):
        for u in range(GATHER_UNROLL):
            r = g * GATHER_UNROLL + u
            for k in range(TOP_K):
                _row_copy(y_hbm, dest_ref[0, r * TOP_K + k], buf.at[k], r, sem).start(priority=k)
        return c

    def drain(g, c):
        for u in range(GATHER_UNROLL):
            r = g * GATHER_UNROLL + u
            for k in range(TOP_K):
                _row_copy(y_hbm, dest_ref[0, r * TOP_K + k], buf.at[k], r, sem).wait()
        return c

    lax.fori_loop(0, rows // GATHER_UNROLL, issue, 0)
    lax.fori_loop(0, rows // GATHER_UNROLL, drain, 0)
    rt = rt_ref[...]
    f = buf[0] * rt[:, TOP_K:TOP_K + 1] + buf[1] * rt[:, TOP_K + 1:TOP_K + 2]
    x = h_ref[...] + gate_ref[...] * f
    if norm:
        x = x * lax.rsqrt(jnp.mean(x * x, axis=-1, keepdims=True) + EPS) * g_ref[...]
    o_ref[...] = x


def _combine(h, y_sorted, dest, rt, mod, mod_idx_fn, part, g, *, rows, n_rows, norm):
    d = h.shape[1]
    nb = n_rows // rows
    row = pl.BlockSpec((rows, d), lambda i: (i, 0))
    return pl.pallas_call(
        functools.partial(_combine_body, rows=rows, norm=norm), grid=(nb,),
        in_specs=[pl.BlockSpec((None, 1, rows * TOP_K), lambda i: (i, 0, 0), memory_space=pltpu.SMEM),
                  row, pl.BlockSpec((rows, LANES), lambda i: (i, 0)),
                  pl.BlockSpec((None, 1, d), lambda i: (mod_idx_fn(i), 0, part)),
                  pl.BlockSpec((1, d), lambda i: (0, 0)), pl.BlockSpec(memory_space=pl.ANY)],
        out_specs=row, out_shape=jax.ShapeDtypeStruct((n_rows, d), F32),
        scratch_shapes=[pltpu.VMEM((TOP_K, rows, d), y_sorted.dtype), pltpu.SemaphoreType.DMA],
        compiler_params=_cparams(("arbitrary",)), name="moe_combine",
    )(dest.reshape(nb, 1, rows * TOP_K), h, rt, mod, g.reshape(1, d), y_sorted)


def kernel(x, c, ctx, c_ctx, ada_w, ada_b, norm_mix_g, norm_ffn_g, w_in, na_rpb, da_lambda, da_subln_g,
           ssm_conv_w, ssm_conv_b, ssm_dt_bias, ssm_a_log, ssm_d, ssm_norm_g, w_branch_a, w_branch_b,
           w_branch_c, w_out, ffn_w1, ffn_w3, ffn_w2, moe_router_w, moe_router_b, moe_w1, moe_w3, moe_w2,
           final_norm_g):
    batch, seq, d = x.shape
    ctx_len = ctx.shape[1]
    depth = ada_w.shape[0]
    n_lat, n_ctx = batch * seq, batch * ctx_len
    n_all = n_lat + n_ctx
    tm = next(t for t in (1024, 512, 256) if n_ctx % t == 0 and seq % t == 0)
    n_mod = 16
    assert batch + 1 <= n_mod
    lat_blocks = n_lat // tm
    per_seq = seq // tm
    te = 256
    assert n_ctx % te == 0 and seq % te == 0
    mod_idx_fn = lambda i: jnp.where(i < n_lat // te, i // (seq // te), batch)
    mod_idx_all = jnp.where(jnp.arange(n_all // tm) < lat_blocks, jnp.arange(n_all // tm) // per_seq, batch)
    layer_blocks = lambda nb, idx: jnp.full((nb,), idx, jnp.int32)

    na_w, da_w = NA_HEADS * NA_HEAD_DIM, DA_HEADS * 2 * DA_HEAD_DIM
    conv_dim = SSM_D_INNER + 2 * SSM_GROUPS * SSM_STATE
    col_qa, col_ka, col_va = 0, na_w, 2 * na_w
    col_qb, col_kb, col_vb = 3 * na_w, 3 * na_w + da_w, 3 * na_w + 2 * da_w
    col_z = 3 * na_w + 3 * da_w
    col_x = col_z + SSM_D_INNER
    col_dt = col_x + conv_dim
    col_gate = col_dt + 2 * SSM_HEADS
    n_main = 6144
    assert col_dt + LANES <= n_main and n_main % 512 == 0 and col_dt % LANES == 0

    h, h_ctx = x.reshape(n_lat, d), ctx.reshape(n_ctx, d)
    cond =jnp.zeros((n_mod, d), F32).at[:batch].set(jax.nn.silu(c)).at[batch].set(jax.nn.silu(c_ctx))
    cond = cond.astype(BF16)
    rope_tabs = _rope_tables(seq, tm)
    zero_state = jnp.zeros((batch, 2, SSM_STATE, SSM_D_INNER), F32)
    w_main = w_in[:, :, :n_main]
    w_gate = w_in[:, :, col_gate:]

    out = None
    for li in range(depth):
        last = li == depth - 1
        lam_init = 0.8 - 0.6 * math.exp(-0.3 * li)
        n_out = n_lat if last else n_all
        mod = _gmm(cond, [ada_w], layer_blocks(1, li), tm=n_mod, tn=512, epi="bias",
                   bias=ada_b[li].reshape(1, 6 * d)).reshape(n_mod, 1, 6 * d)
        u = _norm(h, norm_mix_g[li], tm=te, n_rows=n_all, mod=mod, mod_idx_fn=mod_idx_fn, parts=(0, 1), x2=h_ctx)
        proj = _gmm(u, [w_main], layer_blocks(n_all // tm, li), tm=tm, tn=1024)
        gates = _gmm(u, [w_gate], layer_blocks(n_out // tm, li), tm=tm, tn=1024, n_rows=n_out, epi="sigmoid",
                     out_dtype=BF16)

        o_a = _na(proj, na_rpb[li], batch=batch, seq=seq, ctx_len=ctx_len, q_col=col_qa, k_col=col_ka, v_col=col_va)
        q_r, k_r = _rope(proj, rope_tabs, tm=tm, n_rows=n_all, n_lat_rows=n_lat, seq=seq,
                         q_col=col_qb, k_col=col_kb, width=da_w)
        o_b = _da(q_r, k_r, proj, da_lambda[li].astype(F32), da_subln_g[li], lam_init, batch=batch, seq=seq,
                  ctx_len=ctx_len, v_col=col_vb, tq=256, latent=True)
        ssd_args = (ssm_conv_w[li], ssm_conv_b[li], ssm_dt_bias[li], ssm_a_log[li], ssm_d[li])
        y_ctx, s_ctx = _ssd(proj, *ssd_args, zero_state, batch=batch, n_tok=ctx_len, row0=n_lat,
                            x_col=col_x, dt_col=col_dt)
        y_lat, _ = _ssd(proj, *ssd_args, s_ctx, batch=batch, n_tok=seq, row0=0, x_col=col_x, dt_col=col_dt)
        o_c = _ssm_norm(y_lat, proj, ssm_norm_g[li], tm=te, row0=0, z_col=col_z)
        if not last:
            o_a_c = _ctx_attn(proj, batch=batch, seq=seq, ctx_len=ctx_len, q_col=col_qa, k_col=col_ka, v_col=col_va)
            o_b_c = _da(q_r, k_r, proj, da_lambda[li].astype(F32), da_subln_g[li], lam_init, batch=batch, seq=seq,
                        ctx_len=ctx_len, v_col=col_vb, tq=ctx_len, latent=False)
            o_c_c = _ssm_norm(y_ctx, proj, ssm_norm_g[li], tm=te, row0=n_lat, z_col=col_z)
        merged = _merge((o_a, o_b, o_c), None if last else (o_a_c, o_b_c, o_c_c), gates,
                        w_branch_a, w_branch_b, w_branch_c, li, tm=tm, tn=512, n_rows=n_out)
        h = _gmm(merged, [w_out], layer_blocks(n_out // tm, li), tm=tm, tn=1024, epi="resid", res=h, res2=h_ctx,
                 mod=mod, mod_idx=mod_idx_all[:n_out // tm], mod_part=2, n_rows=n_out)
        h_ctx = None

        j = li // 2
        if li % 2 == 0:
            tokens = _norm(h, norm_ffn_g[li], tm=te, n_rows=n_out, mod=mod, mod_idx_fn=mod_idx_fn, parts=(3, 4))
            hid = _gmm(tokens, [ffn_w1, ffn_w3], layer_blocks(n_out // tm, j), tm=tm, tn=512,
                       epi="swiglu", out_dtype=BF16)
            h = _gmm(hid, [ffn_w2], layer_blocks(n_out // tm, j), tm=tm, tn=512, epi="resid", res=h,
                     mod=mod, mod_idx=mod_idx_all[:n_out // tm], mod_part=5, n_rows=n_out, w_buffers=1)
            if last:
                out = _norm(h, final_norm_g, tm=te, n_rows=n_lat, out_dtype=F32)
        else:
            rw = jnp.zeros((d, LANES), F32).at[:, :N_EXPERTS].set(moe_router_w[j].astype(F32))
            rb = jnp.zeros((1, LANES), F32).at[0, :N_EXPERTS].set(moe_router_b[j].astype(F32))
            tokens, rt = _norm(h, norm_ffn_g[li], tm=te, n_rows=n_out, mod=mod, mod_idx_fn=mod_idx_fn,
                               parts=(3, 4), router=(rw, rb), out_dtype=F32)
            dest, slot_tok, block_expert, n_used = _route(rt[:, :TOP_K].astype(jnp.int32), MOE_ROWS)
            block_expert = block_expert + j * N_EXPERTS
            stack = lambda w: w.reshape((-1,) + w.shape[2:])
            x_sorted = _gather_rows(tokens, slot_tok, n_used, rows=MOE_ROWS, out_dtype=BF16)
            hid = _gmm(x_sorted, [stack(moe_w1), stack(moe_w3)], block_expert, tm=MOE_ROWS, tn=512, epi="swiglu",
                       out_dtype=BF16, n_used=n_used)
            y_sorted = _gmm(hid, [stack(moe_w2)], block_expert, tm=MOE_ROWS, tn=512, nk=2, n_used=n_used)
            h = _combine(h, y_sorted, dest, rt, mod, mod_idx_fn, 5, final_norm_g, rows=te, n_rows=n_out, norm=last)
            if last:
                out = h
    return out.reshape(batch, seq, d)
```

```python
import functools
import math

import numpy as np
import jax
import jax.numpy as jnp
from jax import lax
from jax.experimental import pallas as pl
from jax.experimental.pallas import tpu as pltpu

F32 = jnp.float32
BF16 = jnp.bfloat16

GRID_W = 64
NA_HEADS, NA_HEAD_DIM, NA_WIN_H, NA_WIN_W = 8, 64, 8, 16
DA_HEADS, DA_HEAD_DIM = 4, 64
ROPE_BASE = 10000.0
SSM_D_INNER, SSM_HEAD_DIM, SSM_GROUPS, SSM_STATE, SSM_CONV_W = 1024, 64, 2, 128, 5
SSM_HEADS = SSM_D_INNER // SSM_HEAD_DIM
N_EXPERTS, TOP_K = 8, 2
EPS = 1e-6
NEG_INF = -1e30

LANES = 128
SUBLANES = 8
VMEM_LIMIT_BYTES = 56 * 1024 * 1024
SSD_CHUNK = 128
MOE_ROWS = 1024
NA_ROW_UNROLL = 4
LOG2E = math.log2(math.e)


def _cparams(sem):
    return pltpu.CompilerParams(dimension_semantics=sem, vmem_limit_bytes=VMEM_LIMIT_BYTES)


def _dot(a, b):
    return jnp.dot(a, b, preferred_element_type=F32)


def _dot_nt(a, b):
    return lax.dot_general(a, b, (((1,), (1,)), ((), ())), preferred_element_type=F32)


def _silu(x):
    return x * jax.nn.sigmoid(x)


def _gmm_body(bexp_ref, neww_ref, mod_ref, nv_ref, x_ref, *rest, nk, n_w, epi, res_first_blocks, w_transposed):
    del bexp_ref, mod_ref
    w_refs, rest = rest[:n_w], rest[n_w:]
    if epi == "bias":
        bias_ref, rest = rest[0], rest[1:]
    elif epi == "resid":
        res_ref, rest = rest[0], rest[1:]
        if res_first_blocks is not None:
            res2_ref, rest = rest[0], rest[1:]
        gate_ref, rest = rest[0], rest[1:]
    o_ref, rest = rest[0], rest[1:]
    wbf_refs, acc_refs = rest[:n_w], rest[n_w:]
    i = pl.program_id(1)
    k = pl.program_id(2)

    def finalize(vals):
        if epi == "swiglu":
            o_ref[...] = (_silu(vals[0]) * vals[1]).astype(o_ref.dtype)
        elif epi == "bias":
            o_ref[...] = (vals[0] + bias_ref[...]).astype(o_ref.dtype)
        elif epi == "resid":
            res = res_ref[...]
            if res_first_blocks is not None:
                res = jnp.where(i < res_first_blocks, res, res2_ref[...])
            o_ref[...] = (res + gate_ref[...] * vals[0]).astype(o_ref.dtype)
        elif epi == "sigmoid":
            o_ref[...] = jax.nn.sigmoid(vals[0]).astype(o_ref.dtype)
        else:
            o_ref[...] = vals[0].astype(o_ref.dtype)

    @pl.when(jnp.logical_and(i >= nv_ref[0], k == nk - 1))
    def _():
        o_ref[...] = jnp.zeros_like(o_ref)

    @pl.when(i < nv_ref[0])
    def _():
        @pl.when(neww_ref[i] == 1)
        def _():
            for w_ref, wbf in zip(w_refs, wbf_refs):
                w = w_ref[...]
                wbf[k] = (w.T if w_transposed else w).astype(BF16)

        x = x_ref[...]
        parts = [_dot(x, wbf[k]) for wbf in wbf_refs]
        if nk == 1:
            finalize(parts)
        else:
            @pl.when(k == 0)
            def _():
                for acc, p in zip(acc_refs, parts):
                    acc[...] = p

            @pl.when(k > 0)
            def _():
                for acc, p in zip(acc_refs, parts):
                    acc[...] += p

            @pl.when(k == nk - 1)
            def _():
                finalize([acc[...] for acc in acc_refs])


def _gmm(x, ws, bexp, *, tm, tn, nk=1, n_cols=None, epi="plain", out_dtype=F32, bias=None, res=None,
         res2=None, mod=None, mod_idx=None, mod_part=0, n_rows=None, n_used=None, w_buffers=2, w_rows=None):
    m_rows = x.shape[0] if n_rows is None else n_rows
    k_dim = x.shape[1]
    n_dim = ws[0].shape[2] if n_cols is None else n_cols
    n_w = len(ws)
    tk = k_dim // nk
    nb = m_rows // tm
    res_first_blocks = None
    assert m_rows % tm == 0 and n_dim % tn == 0 and k_dim % nk == 0
    bexp = bexp.astype(jnp.int32)
    neww = jnp.concatenate([jnp.ones((1,), jnp.int32), (bexp[1:] != bexp[:-1]).astype(jnp.int32)])
    if mod_idx is None:
        mod_idx = jnp.zeros((nb,), jnp.int32)
    if n_used is None:
        n_used = jnp.full((1,), nb, jnp.int32)

    rb = lambda i, nv: jnp.minimum(i, nv[0] - 1)
    x_spec = pl.BlockSpec((tm, tk), lambda j, i, k, be, nw, md, nv: (rb(i, nv), k))
    kb_of = lambda i, k, nw, nv: jnp.where(nw[rb(i, nv)] == 1, k, nk - 1)
    if w_rows is None:
        w_block = (None, tk, tn)
        w_map = lambda j, i, k, be, nw, md, nv: (be[rb(i, nv)], kb_of(i, k, nw, nv), j)
    else:
        n_total = ws[0].shape[1]
        assert n_total % SUBLANES == 0 and w_rows % SUBLANES == 0 and tn % SUBLANES == 0
        ws = [w.reshape(-1, k_dim) for w in ws]
        w_block = (pl.Element(tn), pl.Element(tk))
        w_map = lambda j, i, k, be, nw, md, nv: (
            pl.multiple_of(be[rb(i, nv)] * n_total + w_rows + j * tn, SUBLANES),
            pl.multiple_of(kb_of(i, k, nw, nv) * tk, LANES))
    if w_buffers == 2:
        w_spec = pl.BlockSpec(w_block, w_map)
    else:
        w_spec = pl.BlockSpec(w_block, w_map, pipeline_mode=pl.Buffered(w_buffers))
    o_spec = pl.BlockSpec((tm, tn), lambda j, i, k, be, nw, md, nv: (i, j))
    in_specs = [x_spec] + [w_spec] * n_w
    args = [x] + list(ws)
    if epi == "bias":
        in_specs.append(pl.BlockSpec((1, tn), lambda j, i, k, be, nw, md, nv: (0, j)))
        args.append(bias)
    elif epi == "resid":
        part_off = mod_part * (n_dim // tn)
        if res2 is None:
            in_specs.append(o_spec)
            args.append(res)
        else:
            res_first_blocks = res.shape[0] // tm
            in_specs.append(pl.BlockSpec(
                (tm, tn), lambda j, i, k, be, nw, md, nv: (jnp.minimum(i, res_first_blocks - 1), j)))
            in_specs.append(pl.BlockSpec(
                (tm, tn), lambda j, i, k, be, nw, md, nv: (jnp.maximum(i - res_first_blocks, 0), j)))
            args += [res, res2]
        in_specs.append(pl.BlockSpec((None, 1, tn),
                                     lambda j, i, k, be, nw, md, nv: (md[rb(i, nv)], 0, part_off + j)))
        args.append(mod)
    scratch = [pltpu.VMEM((nk, tk, tn), BF16) for _ in range(n_w)]
    if nk > 1:
        scratch += [pltpu.VMEM((tm, tn), F32) for _ in range(n_w)]
    grid_spec = pltpu.PrefetchScalarGridSpec(
        num_scalar_prefetch=4, grid=(n_dim // tn, nb, nk), in_specs=in_specs, out_specs=o_spec,
        scratch_shapes=scratch)
    return pl.pallas_call(
        functools.partial(_gmm_body, nk=nk, n_w=n_w, epi=epi, res_first_blocks=res_first_blocks,
                          w_transposed=w_rows is not None),
        grid_spec=grid_spec,
        out_shape=jax.ShapeDtypeStruct((m_rows, n_dim), out_dtype),
        compiler_params=_cparams(("arbitrary", "arbitrary", "arbitrary")),
        name="gmm_" + epi,
    )(bexp, neww, mod_idx.astype(jnp.int32), n_used.astype(jnp.int32), *args)


def _split2(v):
    hi = v.astype(BF16)
    lo = (v - hi.astype(F32)).astype(BF16)
    return hi, lo


def _norm_body(x_ref, *rest, modulate, router, first_blocks):
    if first_blocks is not None:
        x2_ref, rest = rest[0], rest[1:]
    g_ref, rest = rest[0], rest[1:]
    if modulate:
        shift_ref, scale_ref, rest = rest[0], rest[1], rest[2:]
    if router:
        rw_ref, rb_ref, rest = rest[0], rest[1], rest[2:]
    o_ref = rest[0]
    x = x_ref[...]
    if first_blocks is not None:
        x = jnp.where(pl.program_id(0) < first_blocks, x, x2_ref[...])
    y = x * lax.rsqrt(jnp.mean(x * x, axis=-1, keepdims=True) + EPS) * g_ref[...]
    if modulate:
        y = y * (1.0 + scale_ref[...]) + shift_ref[...]
    o_ref[...] = y.astype(o_ref.dtype)
    if router:
        rt_ref = rest[1]
        y_hi, y_lo = _split2(y)
        w_hi, w_lo = _split2(rw_ref[...])
        logits = _dot(y_hi, w_hi) + _dot(y_lo, w_hi) + _dot(y_hi, w_lo) + rb_ref[...]
        lane = lax.broadcasted_iota(jnp.int32, logits.shape, 1).astype(F32)
        lg = jnp.where(lane < N_EXPERTS, logits, -jnp.inf)
        m1 = jnp.max(lg, axis=-1, keepdims=True)
        i1 = jnp.min(jnp.where(lg == m1, lane, float(LANES)), axis=-1, keepdims=True)
        lg2 = jnp.where(lane == i1, -jnp.inf, lg)
        m2 = jnp.max(lg2, axis=-1, keepdims=True)
        i2 = jnp.min(jnp.where(lg2 == m2, lane, float(LANES)), axis=-1, keepdims=True)
        e2 = jnp.exp(m2 - m1)
        g1 = 1.0 / (1.0 + e2)
        rt = jnp.where(lane == 0, i1, jnp.where(lane == 1, i2, jnp.where(lane == 2, g1, jnp.where(lane == 3, e2 * g1, 0.0))))
        rt_ref[...] = rt


def _norm(x, g, *, tm, n_rows, mod=None, mod_idx_fn=None, parts=(0, 1), out_dtype=BF16, router=None, x2=None):
    d = x.shape[1]
    nb = n_rows // tm
    row = pl.BlockSpec((tm, d), lambda i: (i, 0))
    first_blocks = None if x2 is None else x.shape[0] // tm
    if x2 is None:
        in_specs, args = [row], [x]
    else:
        in_specs, args = list(_two_source_specs((tm, d), first_blocks, lambda i: i)), [x, x2]
    in_specs.append(pl.BlockSpec((1, d), lambda i: (0, 0)))
    args.append(g.reshape(1, d))
    modulate = mod is not None
    if modulate:
        for p in parts:
            in_specs.append(pl.BlockSpec((None, 1, d), lambda i, p=p: (mod_idx_fn(i), 0, p)))
            args.append(mod)
    out_shape = [jax.ShapeDtypeStruct((n_rows, d), out_dtype)]
    out_specs = [row]
    if router is not None:
        rw, rb = router
        in_specs += [pl.BlockSpec((d, LANES), lambda i: (0, 0)), pl.BlockSpec((1, LANES), lambda i: (0, 0))]
        args += [rw, rb]
        out_shape.append(jax.ShapeDtypeStruct((n_rows, LANES), F32))
        out_specs.append(pl.BlockSpec((tm, LANES), lambda i: (i, 0)))
    res = pl.pallas_call(
        functools.partial(_norm_body, modulate=modulate, router=router is not None, first_blocks=first_blocks),
        grid=(nb,), in_specs=in_specs, out_specs=out_specs, out_shape=out_shape,
        compiler_params=_cparams(("arbitrary",)), name="rmsnorm",
    )(*args)
    return res if router is not None else res[0]


def _rope_body(q_ref, k_ref, c_ref, sm_ref, sp_ref, qo_ref, ko_ref, *, q_scale):
    c, sm, sp = c_ref[...], sm_ref[...], sp_ref[...]
    width = q_ref.shape[1]
    for src, dst, scale in ((q_ref, qo_ref, q_scale), (k_ref, ko_ref, 1.0)):
        for g in range(width // LANES):
            sl = slice(g * LANES, (g + 1) * LANES)
            x = src[:, sl]
            y = x * c + pltpu.roll(x, LANES - 16, 1) * sm + pltpu.roll(x, 16, 1) * sp
            dst[:, sl] = (y * scale).astype(dst.dtype)


def _rope_tables(seq, n_id_rows):
    t = jnp.arange(seq)
    pos = jnp.stack([t // GRID_W, t % GRID_W], axis=-1).astype(F32)
    n_freq = DA_HEAD_DIM // 4
    inv_freq = ROPE_BASE ** (-jnp.arange(n_freq, dtype=F32) / n_freq)
    ang = pos[:, :, None] * inv_freq
    d = np.arange(LANES) % DA_HEAD_DIM
    kind, which, f = d // 32, (d % 32) // 16, d % 16
    a = ang[:, kind, f]
    cos, sin = jnp.cos(a), jnp.sin(a)
    sm = jnp.where(which == 0, -sin, 0.0)
    sp = jnp.where(which == 1, sin, 0.0)
    pad = lambda v, fill: jnp.concatenate([v, jnp.full((n_id_rows, LANES), fill, F32)], axis=0)
    return pad(cos, 1.0), pad(sm, 0.0), pad(sp, 0.0)


def _rope(proj, tabs, *, tm, n_rows, n_lat_rows, seq, q_col, k_col, width):
    lat_blocks = n_lat_rows // tm
    per_seq = seq // tm
    tab_idx = lambda i: (jnp.where(i < lat_blocks, i % per_seq, per_seq), 0)
    out = jax.ShapeDtypeStruct((n_rows, width), BF16)
    return pl.pallas_call(
        functools.partial(_rope_body, q_scale=DA_HEAD_DIM ** -0.5 * LOG2E),
        grid=(n_rows // tm,),
        in_specs=[pl.BlockSpec((tm, width), lambda i: (i, q_col // width)),
                  pl.BlockSpec((tm, width), lambda i: (i, k_col // width)),
                  pl.BlockSpec((tm, LANES), tab_idx), pl.BlockSpec((tm, LANES), tab_idx),
                  pl.BlockSpec((tm, LANES), tab_idx)],
        out_specs=[pl.BlockSpec((tm, width), lambda i: (i, 0))] * 2,
        out_shape=[out, out], compiler_params=_cparams(("arbitrary",)), name="rope",
    )(proj, proj, *tabs)


def _na_body(q_ref, k_ref, v_ref, kc_ref, vc_ref, tb_ref, o_ref, kb, vb, kcb, vcb, *, rows, wh):
    kb[...] = k_ref[...].astype(BF16)
    vb[...] = v_ref[...].astype(BF16)
    kcb[...] = kc_ref[...].astype(BF16)
    vcb[...] = vc_ref[...].astype(BF16)
    lane = lax.broadcasted_iota(jnp.int32, (GRID_W, LANES), 1)
    first = lane < NA_HEAD_DIM
    scale = NA_HEAD_DIM ** -0.5

    def body(grp, carry):
        rr = [grp * NA_ROW_UNROLL + j for j in range(NA_ROW_UNROLL)]
        rs = [jnp.clip(r - wh // 2, 0, rows - wh) for r in rr]
        row_of = lambda r: pl.ds(pl.multiple_of(r * GRID_W, GRID_W), GRID_W)
        win_of = lambda r0: pl.ds(pl.multiple_of(r0 * GRID_W, GRID_W), wh * GRID_W)
        qq = []
        for r in rr:
            q = q_ref[row_of(r), :] * scale
            qq.append(jnp.concatenate([jnp.where(first, q, 0.0), jnp.where(first, 0.0, q)], axis=0).astype(BF16))
        s_l = [_dot_nt(x, kb[win_of(r0), :]) + tb_ref[r0 - r + (NA_WIN_H - 1)] for x, r, r0 in zip(qq, rr, rs)]
        s_c = [_dot_nt(x, kcb[...]) for x in qq]
        m = [jnp.maximum(jnp.max(a, axis=-1, keepdims=True), jnp.max(b, axis=-1, keepdims=True))
             for a, b in zip(s_l, s_c)]
        p_l = [jnp.exp(a - mm) for a, mm in zip(s_l, m)]
        p_c = [jnp.exp(b - mm) for b, mm in zip(s_c, m)]
        den = [jnp.sum(a, axis=-1, keepdims=True) + jnp.sum(b, axis=-1, keepdims=True) for a, b in zip(p_l, p_c)]
        o = [_dot(a.astype(BF16), vb[win_of(r0), :]) + _dot(b.astype(BF16), vcb[...])
             for a, b, r0 in zip(p_l, p_c, rs)]
        for r, oo, dd in zip(rr, o, den):
            oo = oo / dd
            o_ref[row_of(r), :] = jnp.where(first, oo[:GRID_W], oo[GRID_W:]).astype(o_ref.dtype)
        return carry

    assert rows % NA_ROW_UNROLL == 0
    lax.fori_loop(0, rows // NA_ROW_UNROLL, body, 0)


def _na_bias_table(rpb, rows):
    wh = min(NA_WIN_H, rows)
    qc = np.arange(GRID_W)[:, None]
    kc = np.arange(GRID_W)[None, :]
    ws = np.clip(qc - NA_WIN_W // 2, 0, GRID_W - NA_WIN_W)
    ok = (kc >= ws) & (kc < ws + NA_WIN_W)
    n_drow, n_dcol = 2 * NA_WIN_H - 1, 2 * NA_WIN_W - 1
    off = GRID_W - NA_WIN_W
    line = jnp.full((NA_HEADS, n_drow, 2 * GRID_W), NEG_INF, F32).at[:, :, off:off + n_dcol].set(rpb.astype(F32))
    span = 2 * GRID_W - 1
    t = jnp.tile(line, (1, 1, GRID_W))[:, :, :GRID_W * span].reshape(NA_HEADS, n_drow, GRID_W, span)
    t = jnp.where(ok, t[:, :, :, GRID_W - 1:], NEG_INF)
    n_d0 = NA_WIN_H
    t = jnp.stack([t[:, d0:d0 + wh] for d0 in range(n_d0)], axis=1)
    t = t.transpose(0, 1, 3, 2, 4).reshape(NA_HEADS // 2, 2, n_d0, GRID_W, wh * GRID_W)
    return t.transpose(0, 2, 1, 3, 4).reshape(NA_HEADS // 2, n_d0, 2 * GRID_W, wh * GRID_W)


def _na(proj, rpb, *, batch, seq, ctx_len, q_col, k_col, v_col):
    rows = seq // GRID_W
    wh = min(NA_WIN_H, rows)
    pairs = NA_HEADS // 2
    tb = _na_bias_table(rpb, rows)
    ctx0 = batch * seq // ctx_len
    lat = lambda col: pl.BlockSpec((seq, LANES), lambda b, p: (b, col // LANES + p))
    ctx = lambda col: pl.BlockSpec((ctx_len, LANES), lambda b, p: (ctx0 + b, col // LANES + p))
    return pl.pallas_call(
        functools.partial(_na_body, rows=rows, wh=wh),
        grid=(batch, pairs),
        in_specs=[lat(q_col), lat(k_col), lat(v_col), ctx(k_col), ctx(v_col),
                  pl.BlockSpec((None, NA_WIN_H, 2 * GRID_W, wh * GRID_W), lambda b, p: (p, 0, 0, 0))],
        out_specs=pl.BlockSpec((seq, LANES), lambda b, p: (b, p)),
        out_shape=jax.ShapeDtypeStruct((batch * seq, pairs * LANES), BF16),
        scratch_shapes=[pltpu.VMEM((seq, LANES), BF16), pltpu.VMEM((seq, LANES), BF16),
                        pltpu.VMEM((ctx_len, LANES), BF16), pltpu.VMEM((ctx_len, LANES), BF16)],
        compiler_params=_cparams(("arbitrary", "arbitrary")), name="na_attn",
    )(proj, proj, proj, proj, proj, tb)


def _ctx_attn_body(q_ref, k_ref, v_ref, o_ref):
    kb = k_ref[...].astype(BF16)
    vb = v_ref[...].astype(BF16)
    q = q_ref[...] * (NA_HEAD_DIM ** -0.5)
    lane = lax.broadcasted_iota(jnp.int32, q.shape, 1)
    first = lane < NA_HEAD_DIM
    outs = []
    for hh in range(2):
        qm = jnp.where(first if hh == 0 else jnp.logical_not(first), q, 0.0).astype(BF16)
        s = _dot_nt(qm, kb)
        p = jnp.exp(s - jnp.max(s, axis=-1, keepdims=True))
        outs.append(_dot(p.astype(BF16), vb) / jnp.sum(p, axis=-1, keepdims=True))
    o_ref[...] = jnp.where(first, outs[0], outs[1]).astype(o_ref.dtype)


def _ctx_attn(proj, *, batch, seq, ctx_len, q_col, k_col, v_col):
    pairs = NA_HEADS // 2
    ctx0 = batch * seq // ctx_len
    ctx = lambda col: pl.BlockSpec((ctx_len, LANES), lambda b, p: (ctx0 + b, col // LANES + p))
    return pl.pallas_call(
        _ctx_attn_body, grid=(batch, pairs),
        in_specs=[ctx(q_col), ctx(k_col), ctx(v_col)],
        out_specs=pl.BlockSpec((ctx_len, LANES), lambda b, p: (b, p)),
        out_shape=jax.ShapeDtypeStruct((batch * ctx_len, pairs * LANES), BF16),
        compiler_params=_cparams(("arbitrary", "arbitrary")), name="ctx_attn",
    )(proj, proj, proj)


def _da_body(lv_ref, q_ref, *rest, lam_init, has_lat):
    if has_lat:
        kl_ref, vl_ref, rest = rest[0], rest[1], rest[2:]
    kc_ref, vc_ref, g_ref, o_ref = rest[:4]
    scr = rest[4:]
    t = pl.program_id(2)
    if has_lat:
        vlb, vcb = scr

        @pl.when(t == 0)
        def _():
            vlb[...] = vl_ref[...].astype(BF16)
            vcb[...] = vc_ref[...].astype(BF16)
    else:
        (vcb,) = scr
        vcb[...] = vc_ref[...].astype(BF16)

    lv = lv_ref[...]
    lam = (jnp.exp(jnp.sum(lv[0:1] * lv[1:2], axis=-1, keepdims=True))
           - jnp.exp(jnp.sum(lv[2:3] * lv[3:4], axis=-1, keepdims=True)) + lam_init)
    q = q_ref[...]
    tq = q.shape[0]
    lane = lax.broadcasted_iota(jnp.int32, q.shape, 1)
    zero = jnp.zeros_like(q)
    qq = jnp.concatenate([jnp.where(lane < DA_HEAD_DIM, q, zero), jnp.where(lane >= DA_HEAD_DIM, q, zero)], axis=0)
    s_c = _dot_nt(qq, kc_ref[...])
    m = jnp.max(s_c, axis=-1, keepdims=True)
    if has_lat:
        s_l = _dot_nt(qq, kl_ref[...])
        m = jnp.maximum(m, jnp.max(s_l, axis=-1, keepdims=True))
        p_l = jnp.exp2(s_l - m)
    p_c = jnp.exp2(s_c - m)
    den = jnp.sum(p_c, axis=-1, keepdims=True)
    if has_lat:
        den = den + jnp.sum(p_l, axis=-1, keepdims=True)
    ratio = lam * den[:tq] / den[tq:]
    o = _dot((p_c[:tq] - ratio * p_c[tq:]).astype(BF16), vcb[...])
    if has_lat:
        o = o + _dot((p_l[:tq] - ratio * p_l[tq:]).astype(BF16), vlb[...])
    o = o / den[:tq]
    o = o * lax.rsqrt(jnp.mean(o * o, axis=-1, keepdims=True) + EPS) * g_ref[...]
    o_ref[...] = (o * (1.0 - lam_init)).astype(o_ref.dtype)


def _da(q_r, k_r, proj, lam_vec, subln_g, lam_init, *, batch, seq, ctx_len, v_col, tq, latent):
    ctx0 = batch * seq // ctx_len
    vcb0 = v_col // LANES
    g2 = subln_g.reshape(1, 2 * DA_HEAD_DIM).astype(F32)
    small = [pl.BlockSpec((4, DA_HEAD_DIM), lambda b, h, t: (0, 0))]
    gspec = pl.BlockSpec((1, LANES), lambda b, h, t: (0, 0))
    kc = pl.BlockSpec((ctx_len, LANES), lambda b, h, t: (ctx0 + b, h))
    vc = pl.BlockSpec((ctx_len, LANES), lambda b, h, t: (ctx0 + b, vcb0 + h))
    if latent:
        nq = seq // tq
        qs = pl.BlockSpec((tq, LANES), lambda b, h, t: (b * nq + t, h))
        kl = pl.BlockSpec((seq, LANES), lambda b, h, t: (b, h))
        vl = pl.BlockSpec((seq, LANES), lambda b, h, t: (b, vcb0 + h))
        in_specs = small + [qs, kl, vl, kc, vc, gspec]
        args = (lam_vec, q_r, k_r, proj, k_r, proj, g2)
        scratch = [pltpu.VMEM((seq, LANES), BF16), pltpu.VMEM((ctx_len, LANES), BF16)]
        n_out = batch * seq
    else:
        nq = ctx_len // tq
        cq0 = batch * seq // tq
        qs = pl.BlockSpec((tq, LANES), lambda b, h, t: (cq0 + b * nq + t, h))
        in_specs = small + [qs, kc, vc, gspec]
        args = (lam_vec, q_r, k_r, proj, g2)
        scratch = [pltpu.VMEM((ctx_len, LANES), BF16)]
        n_out = batch * ctx_len
    return pl.pallas_call(
        functools.partial(_da_body, lam_init=lam_init, has_lat=latent),
        grid=(batch, DA_HEADS, nq), in_specs=in_specs,
        out_specs=pl.BlockSpec((tq, LANES), lambda b, h, t: (b * nq + t, h)),
        out_shape=jax.ShapeDtypeStruct((n_out, DA_HEADS * LANES), BF16),
        scratch_shapes=scratch,
        compiler_params=_cparams(("arbitrary", "arbitrary", "arbitrary")),
        name="diff_attn_lat" if latent else "diff_attn_ctx",
    )(*args)


def _ssd_body(x0_ref, x1_ref, bc_ref, x0p_ref, x1p_ref, bcp_ref, x0n_ref, x1n_ref, bcn_ref, dt_ref,
              cw_ref, cb_ref, prm_ref, dsk_ref, ex_ref, init_ref, y_ref, sout_ref, st_ref, u_ref, *, nc):
    q_len = SSD_CHUNK
    dirn = pl.program_id(1)
    z = pl.program_id(2)
    zz = jnp.where(dirn == 0, z, nc - 1 - z)
    fwd = dirn == 0
    half = SSM_D_INNER // SSM_GROUPS
    heads_per_group = SSM_HEADS // SSM_GROUPS

    @pl.when(z == 0)
    def _():
        st_ref[...] = init_ref[...]

    def conv_piece(main_ref, prev_ref, next_ref, c0):
        width = main_ref.shape[1]
        prev = jnp.where(zz == 0, 0.0, prev_ref[...])
        nxt = jnp.where(zz == nc - 1, 0.0, next_ref[...])
        ext = jnp.concatenate([prev, main_ref[...], nxt], axis=0)
        n_ext = ext.shape[0]
        acc = jnp.zeros((q_len, width), F32) + cb_ref[:, c0:c0 + width]
        for tap in range(SSM_CONV_W):
            sh = (SSM_CONV_W // 2 - tap) % n_ext
            e = ext if sh == 0 else pltpu.roll(ext, sh, 0)
            acc = acc + e[SUBLANES:SUBLANES + q_len] * cw_ref[tap:tap + 1, c0:c0 + width]
        return _silu(acc)

    @pl.when(fwd)
    def _():
        u_ref[zz, :, 0:half] = conv_piece(x0_ref, x0p_ref, x0n_ref, 0)
        u_ref[zz, :, half:2 * half] = conv_piece(x1_ref, x1p_ref, x1n_ref, half)
        u_ref[zz, :, SSM_D_INNER:] = conv_piece(bc_ref, bcp_ref, bcn_ref, SSM_D_INNER)

    xs = (u_ref[zz, :, 0:half], u_ref[zz, :, half:2 * half])
    bcm = u_ref[zz, :, SSM_D_INNER:]

    raw = dt_ref[...]
    raw = jnp.where(fwd, raw, pltpu.roll(raw, LANES - SSM_HEADS, 1))
    xb = raw + prm_ref[0:1, :]
    dtv = jnp.maximum(xb, 0.0) + jnp.log1p(jnp.exp(-jnp.abs(xb)))
    a = dtv * (-jnp.exp(prm_ref[1:2, :]))
    rowi = lax.broadcasted_iota(jnp.int32, (q_len, LANES), 0)
    coli = lax.broadcasted_iota(jnp.int32, (q_len, LANES), 1)
    cum = a
    sft = 1
    while sft < q_len:
        cum = cum + jnp.where(rowi >= sft, pltpu.roll(cum, sft, 0), 0.0)
        sft *= 2
    tot = cum[q_len - 1:q_len, :]
    g = jnp.where(fwd, cum, cum - a)
    e_g = jnp.exp(g)
    e_tg = jnp.exp(tot - g)
    w_state = jnp.where(fwd, e_tg, e_g)
    w_yoff = jnp.where(fwd, e_g, e_tg)
    e_tot = jnp.exp(tot)
    gs = g * jnp.where(fwd, LOG2E, -LOG2E)
    gs_t = gs.T
    tri = jnp.where(fwd, rowi - coli, coli - rowi) >= 0

    stack = jnp.concatenate([dtv, dtv * w_state, w_yoff, jnp.broadcast_to(e_tot, (SUBLANES, LANES))], axis=0)
    s_hi, s_lo = _split2(stack)
    ex = ex_ref[...]
    expd = _dot(s_hi, ex) + _dot(s_lo, ex)
    dt_e, dts_e, wy_e = expd[0:q_len], expd[q_len:2 * q_len], expd[2 * q_len:3 * q_len]
    tot_e = expd[3 * q_len:3 * q_len + 1]

    lane = lax.broadcasted_iota(jnp.int32, (q_len, LANES), 1)
    first = lane < SSM_HEAD_DIM
    for grp in range(SSM_GROUPS):
        x_g = xs[grp]
        csl = slice(grp * half, (grp + 1) * half)
        b_g = bcm[:, grp * SSM_STATE:(grp + 1) * SSM_STATE]
        c_g = bcm[:, (SSM_GROUPS + grp) * SSM_STATE:(SSM_GROUPS + grp + 1) * SSM_STATE].astype(BF16)
        xdt = (x_g * dt_e[:, csl]).astype(BF16)
        xdts = (x_g * dts_e[:, csl]).astype(BF16)
        cb = _dot_nt(c_g, b_g.astype(BF16))
        st_g = st_ref[:, csl]
        y_g = _dot(c_g, st_g.astype(BF16)) * wy_e[:, csl] + dsk_ref[:, csl] * x_g
        for pr in range(heads_per_group // 2):
            outs = []
            for hh in range(2):
                col = grp * heads_per_group + 2 * pr + hh
                seg = gs[:, col:col + 1] - gs_t[col:col + 1, :]
                mat = (cb * jnp.where(tri, jnp.exp2(seg), 0.0)).astype(BF16)
                outs.append(_dot(mat, xdt[:, pr * LANES:(pr + 1) * LANES]))
            lo = grp * half + pr * LANES
            y_ref[:, lo:lo + LANES] = y_g[:, pr * LANES:(pr + 1) * LANES] + jnp.where(first, outs[0], outs[1])
        st_ref[:, csl] = tot_e[:, csl] * st_g + _dot(b_g.T.astype(BF16), xdts)

    @pl.when(z == nc - 1)
    def _():
        sout_ref[...] = st_ref[...]


def _ssd(proj, conv_w, conv_b, dt_bias, a_log, d_skip, init, *, batch, n_tok, row0, x_col, dt_col):
    q_len = SSD_CHUNK
    nc = n_tok // q_len
    half = SSM_D_INNER // SSM_GROUPS
    conv_dim = SSM_D_INNER + 2 * SSM_GROUPS * SSM_STATE
    rb0 = row0 // q_len
    hb = q_len // SUBLANES
    n_halo = proj.shape[0] // SUBLANES
    zz = lambda d, z: jnp.where(d == 0, z, nc - 1 - z)
    rb = lambda b, d, z: rb0 + b * nc + zz(d, z)
    rbx = lambda b, d, z: rb0 + b * nc + jnp.where(d == 0, z, nc - 1)
    main = lambda c: pl.BlockSpec((q_len, half), lambda b, d, z: (rbx(b, d, z), c))
    prev = lambda c: pl.BlockSpec((SUBLANES, half), lambda b, d, z: (jnp.maximum(rbx(b, d, z) * hb - 1, 0), c))
    nxt = lambda c: pl.BlockSpec((SUBLANES, half),
                                 lambda b, d, z: (jnp.minimum(rbx(b, d, z) * hb + hb, n_halo - 1), c))
    cols = [x_col // half + j for j in range(conv_dim // half)]
    const2 = lambda shape: pl.BlockSpec(shape, lambda b, d, z: (0, 0))
    prm = jnp.zeros((2, SUBLANES, LANES), F32)
    prm = prm.at[:, 0, :SSM_HEADS].set(dt_bias.astype(F32)).at[:, 1, :SSM_HEADS].set(a_log.astype(F32))
    dsk = jnp.repeat(d_skip.astype(F32), SSM_HEAD_DIM, axis=-1).reshape(2, 1, SSM_D_INNER)
    ex = (np.arange(LANES)[:, None] == (np.arange(SSM_D_INNER) // SSM_HEAD_DIM)[None, :])
    ex = jnp.asarray(ex, BF16)
    cw = jnp.zeros((SUBLANES, conv_dim), F32).at[:SSM_CONV_W].set(conv_w.astype(F32))
    state_spec = pl.BlockSpec((None, None, SSM_STATE, SSM_D_INNER), lambda b, d, z: (b, d, 0, 0))
    y, s_out = pl.pallas_call(
        functools.partial(_ssd_body, nc=nc),
        grid=(batch, 2, nc),
        in_specs=[main(cols[0]), main(cols[1]), main(cols[2]), prev(cols[0]), prev(cols[1]), prev(cols[2]),
                  nxt(cols[0]), nxt(cols[1]), nxt(cols[2]),
                  pl.BlockSpec((q_len, LANES), lambda b, d, z: (rb(b, d, z), dt_col // LANES)),
                  const2((SUBLANES, conv_dim)), const2((1, conv_dim)),
                  pl.BlockSpec((None, SUBLANES, LANES), lambda b, d, z: (d, 0, 0)),
                  pl.BlockSpec((None, 1, SSM_D_INNER), lambda b, d, z: (d, 0, 0)),
                  const2((LANES, SSM_D_INNER)), state_spec],
        out_specs=[pl.BlockSpec((None, q_len, SSM_D_INNER), lambda b, d, z: (d, b * nc + zz(d, z), 0)),
                   state_spec],
        out_shape=[jax.ShapeDtypeStruct((2, batch * n_tok, SSM_D_INNER), F32),
                   jax.ShapeDtypeStruct((batch, 2, SSM_STATE, SSM_D_INNER), F32)],
        scratch_shapes=[pltpu.VMEM((SSM_STATE, SSM_D_INNER), F32), pltpu.VMEM((nc, q_len, conv_dim), F32)],
        compiler_params=_cparams(("arbitrary", "arbitrary", "arbitrary")), name="ssd_scan",
    )(proj, proj, proj, proj, proj, proj, proj, proj, proj, proj, cw, conv_b.reshape(1, conv_dim).astype(F32),
      prm, dsk, ex, init)
    return y, s_out


def _ssm_norm_body(y_ref, z_ref, g_ref, o_ref):
    yz = (y_ref[0] + y_ref[1]) * _silu(z_ref[...])
    gw = SSM_D_INNER // SSM_GROUPS
    for grp in range(SSM_GROUPS):
        v = yz[:, grp * gw:(grp + 1) * gw]
        v = v * lax.rsqrt(jnp.mean(v * v, axis=-1, keepdims=True) + EPS)
        o_ref[:, grp * gw:(grp + 1) * gw] = (v * g_ref[:, grp * gw:(grp + 1) * gw]).astype(o_ref.dtype)


def _ssm_norm(y, proj, g, *, tm, row0, z_col):
    n = y.shape[1]
    zb0 = row0 // tm
    return pl.pallas_call(
        _ssm_norm_body, grid=(n // tm,),
        in_specs=[pl.BlockSpec((2, tm, SSM_D_INNER), lambda i: (0, i, 0)),
                  pl.BlockSpec((tm, SSM_D_INNER), lambda i: (zb0 + i, z_col // SSM_D_INNER)),
                  pl.BlockSpec((1, SSM_D_INNER), lambda i: (0, 0))],
        out_specs=pl.BlockSpec((tm, SSM_D_INNER), lambda i: (i, 0)),
        out_shape=jax.ShapeDtypeStruct((n, SSM_D_INNER), BF16),
        compiler_params=_cparams(("arbitrary",)), name="ssm_gated_norm",
    )(y, proj, g.reshape(1, SSM_D_INNER).astype(F32))


def _two_source_specs(block, first_blocks, row_block_of):
    first = pl.BlockSpec(block, lambda *g: (jnp.minimum(row_block_of(*g), first_blocks - 1), 0))
    second = pl.BlockSpec(block, lambda *g: (jnp.maximum(row_block_of(*g) - first_blocks, 0), 0))
    return first, second


def _merge_body(*refs, lat_blocks, has_ctx):
    n_src = 6 if has_ctx else 3
    src, (ga_ref, gb_ref, gc_ref, wa_ref, wb_ref, wc_ref, o_ref, wab, wbb, wcb) = refs[:n_src], refs[n_src:]
    i = pl.program_id(1)

    @pl.when(i == 0)
    def _():
        wab[...] = wa_ref[...].astype(BF16)
        wbb[...] = wb_ref[...].astype(BF16)
        wcb[...] = wc_ref[...].astype(BF16)

    if has_ctx:
        oa, ob, oc = (jnp.where(i < lat_blocks, src[2 * n][...], src[2 * n + 1][...]) for n in range(3))
    else:
        oa, ob, oc = (r[...] for r in src)
    acc = ga_ref[...].astype(F32) * _dot(oa, wab[...])
    acc = acc + gb_ref[...].astype(F32) * _dot(ob, wbb[...])
    acc = acc + gc_ref[...].astype(F32) * _dot(oc, wcb[...])
    o_ref[...] = acc.astype(o_ref.dtype)


def _merge(o_lat, o_ctx, gates, w_a, w_b, w_c, layer, *, tm, tn, n_rows):
    d = w_a.shape[2]
    nj = d // tn
    lat_blocks = o_lat[0].shape[0] // tm
    gate = lambda x: pl.BlockSpec((tm, tn), lambda j, i: (i, x * nj + j))
    wsp = lambda kk: pl.BlockSpec((None, kk, tn), lambda j, i: (layer, 0, j))
    widths = (w_a.shape[1], w_b.shape[1], w_c.shape[1])
    src_specs, src = [], []
    for n, kk in enumerate(widths):
        if o_ctx is None:
            src_specs.append(pl.BlockSpec((tm, kk), lambda j, i: (i, 0)))
            src.append(o_lat[n])
        else:
            src_specs += _two_source_specs((tm, kk), lat_blocks, lambda j, i: i)
            src += [o_lat[n], o_ctx[n]]
    return pl.pallas_call(
        functools.partial(_merge_body, lat_blocks=lat_blocks, has_ctx=o_ctx is not None), grid=(nj, n_rows // tm),
        in_specs=src_specs + [gate(0), gate(1), gate(2)] + [wsp(kk) for kk in widths],
        out_specs=pl.BlockSpec((tm, tn), lambda j, i: (i, j)),
        out_shape=jax.ShapeDtypeStruct((n_rows, d), BF16),
        scratch_shapes=[pltpu.VMEM((kk, tn), BF16) for kk in widths],
        compiler_params=_cparams(("arbitrary", "arbitrary")), name="branch_merge",
    )(*src, gates, gates, gates, w_a, w_b, w_c)


def _route(top_idx, blk):
    n = top_idx.shape[0]
    e_flat = top_idx.reshape(-1)
    onehot = (e_flat[:, None] == jnp.arange(N_EXPERTS, dtype=jnp.int32)[None, :]).astype(jnp.int32)
    counts = jnp.sum(onehot, axis=0)
    rank = jnp.sum((jnp.cumsum(onehot, axis=0) - onehot) * onehot, axis=1)
    padded = (counts + blk - 1) // blk * blk
    pad_end = jnp.cumsum(padded)
    pad_start = pad_end - padded
    dest = jnp.sum(onehot * pad_start[None, :], axis=1) + rank
    n_blocks = -(-(n * TOP_K) // blk) + N_EXPERTS
    cap = n_blocks * blk
    tok_flat = jnp.repeat(jnp.arange(n, dtype=jnp.int32), TOP_K)
    slot_tok = jnp.zeros((cap,), jnp.int32).at[dest].set(tok_flat)
    blk_start = jnp.arange(n_blocks, dtype=jnp.int32) * blk
    block_expert = jnp.clip(jnp.sum((pad_end[None, :] <= blk_start[:, None]).astype(jnp.int32), axis=1),
                            0, N_EXPERTS - 1)
    n_used = (pad_end[-1] // blk).astype(jnp.int32).reshape(1)
    return dest.astype(jnp.int32), slot_tok, block_expert, n_used


GATHER_UNROLL = 8


def _row_copy(src_hbm, row, buf, slot, sem):
    return pltpu.make_async_copy(src_hbm.at[pl.ds(row, 1), :], buf.at[pl.ds(slot, 1), :], sem)


def _gather_body(nv_ref, idx_ref, src_hbm, o_ref, buf, sem, *, rows):
    @pl.when(pl.program_id(0) >= nv_ref[0])
    def _():
        o_ref[...] = jnp.zeros_like(o_ref)

    @pl.when(pl.program_id(0) < nv_ref[0])
    def _():
        def issue(g, c):
            for u in range(GATHER_UNROLL):
                r = g * GATHER_UNROLL + u
                _row_copy(src_hbm, idx_ref[0, r], buf, r, sem).start(priority=u % 2)
            return c

        def drain(g, c):
            for u in range(GATHER_UNROLL):
                r = g * GATHER_UNROLL + u
                _row_copy(src_hbm, idx_ref[0, r], buf, r, sem).wait()
            return c

        lax.fori_loop(0, rows // GATHER_UNROLL, issue, 0)
        lax.fori_loop(0, rows // GATHER_UNROLL, drain, 0)
        o_ref[...] = buf[...].astype(o_ref.dtype)


def _gather_rows(src, idx, n_used, *, rows, out_dtype):
    n, d = src.shape
    nb = idx.shape[0] // rows
    assert rows % GATHER_UNROLL == 0
    blk = lambda i, nv: jnp.minimum(i, nv[0] - 1)
    grid_spec = pltpu.PrefetchScalarGridSpec(
        num_scalar_prefetch=1, grid=(nb,),
        in_specs=[pl.BlockSpec((None, 1, rows), lambda i, nv: (blk(i, nv), 0, 0), memory_space=pltpu.SMEM),
                  pl.BlockSpec(memory_space=pl.ANY)],
        out_specs=pl.BlockSpec((rows, d), lambda i, nv: (i, 0)),
        scratch_shapes=[pltpu.VMEM((rows, d), src.dtype), pltpu.SemaphoreType.DMA])
    return pl.pallas_call(
        functools.partial(_gather_body, rows=rows), grid_spec=grid_spec,
        out_shape=jax.ShapeDtypeStruct((nb * rows, d), out_dtype),
        compiler_params=_cparams(("arbitrary",)), name="moe_dispatch",
    )(n_used, idx.reshape(nb, 1, rows), src)


def _combine_body(dest_ref, h_ref, rt_ref, gate_ref, g_ref, y_hbm, o_ref, buf, sem, *, rows, norm):
    def issue(g, c):
        for u in range(GATHER_UNROLL):
            r = g * GATHER_UNROLL + u
            for k in range(TOP_K):
                _row_copy(y_hbm, dest_ref[0, r * TOP_K + k], buf.at[k], r, sem).start(priority=k)
        return c

    def drain(g, c):
        for u in range(GATHER_UNROLL):
            r = g * GATHER_UNROLL + u
            for k in range(TOP_K):
                _row_copy(y_hbm, dest_ref[0, r * TOP_K + k], buf.at[k], r, sem).wait()
        return c

    lax.fori_loop(0, rows // GATHER_UNROLL, issue, 0)
    lax.fori_loop(0, rows // GATHER_UNROLL, drain, 0)
    rt = rt_ref[...]
    f = buf[0] * rt[:, TOP_K:TOP_K + 1] + buf[1] * rt[:, TOP_K + 1:TOP_K + 2]
    x = h_ref[...] + gate_ref[...] * f
    if norm:
        x = x * lax.rsqrt(jnp.mean(x * x, axis=-1, keepdims=True) + EPS) * g_ref[...]
    o_ref[...] = x


def _combine(h, y_sorted, dest, rt, mod, mod_idx_fn, part, g, *, rows, n_rows, norm):
    d = h.shape[1]
    nb = n_rows // rows
    row = pl.BlockSpec((rows, d), lambda i: (i, 0))
    return pl.pallas_call(
        functools.partial(_combine_body, rows=rows, norm=norm), grid=(nb,),
        in_specs=[pl.BlockSpec((None, 1, rows * TOP_K), lambda i: (i, 0, 0), memory_space=pltpu.SMEM),
                  row, pl.BlockSpec((rows, LANES), lambda i: (i, 0)),
                  pl.BlockSpec((None, 1, d), lambda i: (mod_idx_fn(i), 0, part)),
                  pl.BlockSpec((1, d), lambda i: (0, 0)), pl.BlockSpec(memory_space=pl.ANY)],
        out_specs=row, out_shape=jax.ShapeDtypeStruct((n_rows, d), F32),
        scratch_shapes=[pltpu.VMEM((TOP_K, rows, d), y_sorted.dtype), pltpu.SemaphoreType.DMA],
        compiler_params=_cparams(("arbitrary",)), name="moe_combine",
    )(dest.reshape(nb, 1, rows * TOP_K), h, rt, mod, g.reshape(1, d), y_sorted)


def kernel(x, c, ctx, c_ctx, ada_w, ada_b, norm_mix_g, norm_ffn_g, w_in, na_rpb, da_lambda, da_subln_g,
           ssm_conv_w, ssm_conv_b, ssm_dt_bias, ssm_a_log, ssm_d, ssm_norm_g, w_branch_a, w_branch_b,
           w_branch_c, w_out, ffn_w1, ffn_w3, ffn_w2, moe_router_w, moe_router_b, moe_w1, moe_w3, moe_w2,
           final_norm_g):
    batch, seq, d = x.shape
    ctx_len = ctx.shape[1]
    depth = ada_w.shape[0]
    n_lat, n_ctx = batch * seq, batch * ctx_len
    n_all = n_lat + n_ctx
    tm = next(t for t in (1024, 512, 256) if n_ctx % t == 0 and seq % t == 0)
    n_mod = 16
    assert batch + 1 <= n_mod
    lat_blocks = n_lat // tm
    per_seq = seq // tm
    te = 256
    assert n_ctx % te == 0 and seq % te == 0
    mod_idx_fn = lambda i: jnp.where(i < n_lat // te, i // (seq // te), batch)
    mod_idx_all = jnp.where(jnp.arange(n_all // tm) < lat_blocks, jnp.arange(n_all // tm) // per_seq, batch)
    layer_blocks = lambda nb, idx: jnp.full((nb,), idx, jnp.int32)

    na_w, da_w = NA_HEADS * NA_HEAD_DIM, DA_HEADS * 2 * DA_HEAD_DIM
    conv_dim = SSM_D_INNER + 2 * SSM_GROUPS * SSM_STATE
    col_qa, col_ka, col_va = 0, na_w, 2 * na_w
    col_qb, col_kb, col_vb = 3 * na_w, 3 * na_w + da_w, 3 * na_w + 2 * da_w
    col_z = 3 * na_w + 3 * da_w
    col_x = col_z + SSM_D_INNER
    col_dt = col_x + conv_dim
    col_gate = col_dt + 2 * SSM_HEADS
    n_main = 6144
    assert col_dt + LANES <= n_main and n_main % 512 == 0 and col_dt % LANES == 0

    h, h_ctx = x.reshape(n_lat, d), ctx.reshape(n_ctx, d)
    cond =jnp.zeros((n_mod, d), F32).at[:batch].set(jax.nn.silu(c)).at[batch].set(jax.nn.silu(c_ctx))
    cond = cond.astype(BF16)
    rope_tabs = _rope_tables(seq, tm)
    zero_state = jnp.zeros((batch, 2, SSM_STATE, SSM_D_INNER), F32)
    w_in_t = jnp.swapaxes(w_in, 1, 2)

    out = None
    for li in range(depth):
        last = li == depth - 1
        lam_init = 0.8 - 0.6 * math.exp(-0.3 * li)
        n_out = n_lat if last else n_all
        mod = _gmm(cond, [ada_w], layer_blocks(1, li), tm=n_mod, tn=512, epi="bias",
                   bias=ada_b[li].reshape(1, 6 * d)).reshape(n_mod, 1, 6 * d)
        u = _norm(h, norm_mix_g[li], tm=te, n_rows=n_all, mod=mod, mod_idx_fn=mod_idx_fn, parts=(0, 1), x2=h_ctx)
        proj = _gmm(u, [w_in_t], layer_blocks(n_all // tm, li), tm=tm, tn=1024, n_cols=n_main, w_rows=0)
        gates = _gmm(u, [w_in_t], layer_blocks(n_out // tm, li), tm=tm, tn=1024, n_rows=n_out, epi="sigmoid",
                     out_dtype=BF16, n_cols=3 * d, w_rows=col_gate)

        o_a = _na(proj, na_rpb[li], batch=batch, seq=seq, ctx_len=ctx_len, q_col=col_qa, k_col=col_ka, v_col=col_va)
        q_r, k_r = _rope(proj, rope_tabs, tm=tm, n_rows=n_all, n_lat_rows=n_lat, seq=seq,
                         q_col=col_qb, k_col=col_kb, width=da_w)
        o_b = _da(q_r, k_r, proj, da_lambda[li].astype(F32), da_subln_g[li], lam_init, batch=batch, seq=seq,
                  ctx_len=ctx_len, v_col=col_vb, tq=256, latent=True)
        ssd_args = (ssm_conv_w[li], ssm_conv_b[li], ssm_dt_bias[li], ssm_a_log[li], ssm_d[li])
        y_ctx, s_ctx = _ssd(proj, *ssd_args, zero_state, batch=batch, n_tok=ctx_len, row0=n_lat,
                            x_col=col_x, dt_col=col_dt)
        y_lat, _ = _ssd(proj, *ssd_args, s_ctx, batch=batch, n_tok=seq, row0=0, x_col=col_x, dt_col=col_dt)
        o_c = _ssm_norm(y_lat, proj, ssm_norm_g[li], tm=te, row0=0, z_col=col_z)
        if not last:
            o_a_c = _ctx_attn(proj, batch=batch, seq=seq, ctx_len=ctx_len, q_col=col_qa, k_col=col_ka, v_col=col_va)
            o_b_c = _da(q_r, k_r, proj, da_lambda[li].astype(F32), da_subln_g[li], lam_init, batch=batch, seq=seq,
                        ctx_len=ctx_len, v_col=col_vb, tq=ctx_len, latent=False)
            o_c_c = _ssm_norm(y_ctx, proj, ssm_norm_g[li], tm=te, row0=n_lat, z_col=col_z)
        merged = _merge((o_a, o_b, o_c), None if last else (o_a_c, o_b_c, o_c_c), gates,
                        w_branch_a, w_branch_b, w_branch_c, li, tm=tm, tn=512, n_rows=n_out)
        h = _gmm(merged, [w_out], layer_blocks(n_out // tm, li), tm=tm, tn=1024, epi="resid", res=h, res2=h_ctx,
                 mod=mod, mod_idx=mod_idx_all[:n_out // tm], mod_part=2, n_rows=n_out)
        h_ctx = None

        j = li // 2
        if li % 2 == 0:
            tokens = _norm(h, norm_ffn_g[li], tm=te, n_rows=n_out, mod=mod, mod_idx_fn=mod_idx_fn, parts=(3, 4))
            hid = _gmm(tokens, [ffn_w1, ffn_w3], layer_blocks(n_out // tm, j), tm=tm, tn=512,
                       epi="swiglu", out_dtype=BF16)
            h = _gmm(hid, [ffn_w2], layer_blocks(n_out // tm, j), tm=tm, tn=512, epi="resid", res=h,
                     mod=mod, mod_idx=mod_idx_all[:n_out // tm], mod_part=5, n_rows=n_out, w_buffers=1)
            if last:
                out = _norm(h, final_norm_g, tm=te, n_rows=n_lat, out_dtype=F32)
        else:
            rw = jnp.zeros((d, LANES), F32).at[:, :N_EXPERTS].set(moe_router_w[j].astype(F32))
            rb = jnp.zeros((1, LANES), F32).at[0, :N_EXPERTS].set(moe_router_b[j].astype(F32))
            tokens, rt = _norm(h, norm_ffn_g[li], tm=te, n_rows=n_out, mod=mod, mod_idx_fn=mod_idx_fn,
                               parts=(3, 4), router=(rw, rb), out_dtype=F32)
            dest, slot_tok, block_expert, n_used = _route(rt[:, :TOP_K].astype(jnp.int32), MOE_ROWS)
            block_expert = block_expert + j * N_EXPERTS
            stack = lambda w: w.reshape((-1,) + w.shape[2:])
            x_sorted = _gather_rows(tokens, slot_tok, n_used, rows=MOE_ROWS, out_dtype=BF16)
            hid = _gmm(x_sorted, [stack(moe_w1), stack(moe_w3)], block_expert, tm=MOE_ROWS, tn=512, epi="swiglu",
                       out_dtype=BF16, n_used=n_used)
            y_sorted = _gmm(hid, [stack(moe_w2)], block_expert, tm=MOE_ROWS, tn=512, nk=2, n_used=n_used)
            h = _combine(h, y_sorted, dest, rt, mod, mod_idx_fn, 5, final_norm_g, rows=te, n_rows=n_out, norm=last)
            if last:
                out = h
    return out.reshape(batch, seq, d)
```

```python
import functools
import math

import numpy as np
import jax
import jax.numpy as jnp
from jax import lax
from jax.experimental import pallas as pl
from jax.experimental.pallas import tpu as pltpu

F32 = jnp.float32
BF16 = jnp.bfloat16

GRID_W = 64
NA_HEADS, NA_HEAD_DIM, NA_WIN_H, NA_WIN_W = 8, 64, 8, 16
DA_HEADS, DA_HEAD_DIM = 4, 64
ROPE_BASE = 10000.0
SSM_D_INNER, SSM_HEAD_DIM, SSM_GROUPS, SSM_STATE, SSM_CONV_W = 1024, 64, 2, 128, 5
SSM_HEADS = SSM_D_INNER // SSM_HEAD_DIM
N_EXPERTS, TOP_K = 8, 2
EPS = 1e-6
NEG_INF = -1e30

LANES = 128
SUBLANES = 8
VMEM_LIMIT_BYTES = 56 * 1024 * 1024
SSD_CHUNK = 128
MOE_ROWS = 1024
NA_ROW_UNROLL = 4
LOG2E = math.log2(math.e)


def _cparams(sem):
    return pltpu.CompilerParams(dimension_semantics=sem, vmem_limit_bytes=VMEM_LIMIT_BYTES)


def _dot(a, b):
    return jnp.dot(a, b, preferred_element_type=F32)


def _dot_nt(a, b):
    return lax.dot_general(a, b, (((1,), (1,)), ((), ())), preferred_element_type=F32)


def _silu(x):
    return x * jax.nn.sigmoid(x)


def _gmm_body(bexp_ref, neww_ref, mod_ref, nv_ref, x_ref, *rest, nk, n_w, epi, res_first_blocks, w_transposed):
    del bexp_ref, mod_ref
    w_refs, rest = rest[:n_w], rest[n_w:]
    if epi == "bias":
        bias_ref, rest = rest[0], rest[1:]
    elif epi == "resid":
        res_ref, rest = rest[0], rest[1:]
        if res_first_blocks is not None:
            res2_ref, rest = rest[0], rest[1:]
        gate_ref, rest = rest[0], rest[1:]
    o_ref, rest = rest[0], rest[1:]
    wbf_refs, acc_refs = rest[:n_w], rest[n_w:]
    i = pl.program_id(1)
    k = pl.program_id(2)

    def finalize(vals):
        if epi == "swiglu":
            o_ref[...] = (_silu(vals[0]) * vals[1]).astype(o_ref.dtype)
        elif epi == "bias":
            o_ref[...] = (vals[0] + bias_ref[...]).astype(o_ref.dtype)
        elif epi == "resid":
            res = res_ref[...]
            if res_first_blocks is not None:
                res = jnp.where(i < res_first_blocks, res, res2_ref[...])
            o_ref[...] = (res + gate_ref[...] * vals[0]).astype(o_ref.dtype)
        elif epi == "sigmoid":
            o_ref[...] = jax.nn.sigmoid(vals[0]).astype(o_ref.dtype)
        else:
            o_ref[...] = vals[0].astype(o_ref.dtype)

    @pl.when(jnp.logical_and(i >= nv_ref[0], k == nk - 1))
    def _():
        o_ref[...] = jnp.zeros_like(o_ref)

    @pl.when(i < nv_ref[0])
    def _():
        @pl.when(neww_ref[i] == 1)
        def _():
            for w_ref, wbf in zip(w_refs, wbf_refs):
                w = w_ref[...]
                wbf[k] = (w.T if w_transposed else w).astype(BF16)

        x = x_ref[...]
        parts = [_dot(x, wbf[k]) for wbf in wbf_refs]
        if nk == 1:
            finalize(parts)
        else:
            @pl.when(k == 0)
            def _():
                for acc, p in zip(acc_refs, parts):
                    acc[...] = p

            @pl.when(k > 0)
            def _():
                for acc, p in zip(acc_refs, parts):
                    acc[...] += p

            @pl.when(k == nk - 1)
            def _():
                finalize([acc[...] for acc in acc_refs])


def _gmm(x, ws, bexp, *, tm, tn, nk=1, n_cols=None, epi="plain", out_dtype=F32, bias=None, res=None,
         res2=None, mod=None, mod_idx=None, mod_part=0, n_rows=None, n_used=None, w_buffers=2, w_rows=None):
    m_rows = x.shape[0] if n_rows is None else n_rows
    k_dim = x.shape[1]
    n_dim = ws[0].shape[2] if n_cols is None else n_cols
    n_w = len(ws)
    tk = k_dim // nk
    nb = m_rows // tm
    res_first_blocks = None
    assert m_rows % tm == 0 and n_dim % tn == 0 and k_dim % nk == 0
    bexp = bexp.astype(jnp.int32)
    neww = jnp.concatenate([jnp.ones((1,), jnp.int32), (bexp[1:] != bexp[:-1]).astype(jnp.int32)])
    if mod_idx is None:
        mod_idx = jnp.zeros((nb,), jnp.int32)
    if n_used is None:
        n_used = jnp.full((1,), nb, jnp.int32)

    rb = lambda i, nv: jnp.minimum(i, nv[0] - 1)
    x_spec = pl.BlockSpec((tm, tk), lambda j, i, k, be, nw, md, nv: (rb(i, nv), k))
    kb_of = lambda i, k, nw, nv: jnp.where(nw[rb(i, nv)] == 1, k, nk - 1)
    if w_rows is None:
        w_block = (None, tk, tn)
        w_map = lambda j, i, k, be, nw, md, nv: (be[rb(i, nv)], kb_of(i, k, nw, nv), j)
    else:
        n_total = ws[0].shape[1]
        assert n_total % SUBLANES == 0 and w_rows % SUBLANES == 0 and tn % SUBLANES == 0
        ws = [w.reshape(-1, k_dim) for w in ws]
        w_block = (pl.Element(tn), pl.Element(tk))
        w_map = lambda j, i, k, be, nw, md, nv: (
            pl.multiple_of(be[rb(i, nv)] * n_total + w_rows + j * tn, SUBLANES),
            pl.multiple_of(kb_of(i, k, nw, nv) * tk, LANES))
    if w_buffers == 2:
        w_spec = pl.BlockSpec(w_block, w_map)
    else:
        w_spec = pl.BlockSpec(w_block, w_map, pipeline_mode=pl.Buffered(w_buffers))
    o_spec = pl.BlockSpec((tm, tn), lambda j, i, k, be, nw, md, nv: (i, j))
    in_specs = [x_spec] + [w_spec] * n_w
    args = [x] + list(ws)
    if epi == "bias":
        in_specs.append(pl.BlockSpec((1, tn), lambda j, i, k, be, nw, md, nv: (0, j)))
        args.append(bias)
    elif epi == "resid":
        part_off = mod_part * (n_dim // tn)
        if res2 is None:
            in_specs.append(o_spec)
            args.append(res)
        else:
            res_first_blocks = res.shape[0] // tm
            in_specs.append(pl.BlockSpec(
                (tm, tn), lambda j, i, k, be, nw, md, nv: (jnp.minimum(i, res_first_blocks - 1), j)))
            in_specs.append(pl.BlockSpec(
                (tm, tn), lambda j, i, k, be, nw, md, nv: (jnp.maximum(i - res_first_blocks, 0), j)))
            args += [res, res2]
        in_specs.append(pl.BlockSpec((None, 1, tn),
                                     lambda j, i, k, be, nw, md, nv: (md[rb(i, nv)], 0, part_off + j)))
        args.append(mod)
    scratch = [pltpu.VMEM((nk, tk, tn), BF16) for _ in range(n_w)]
    if nk > 1:
        scratch += [pltpu.VMEM((tm, tn), F32) for _ in range(n_w)]
    grid_spec = pltpu.PrefetchScalarGridSpec(
        num_scalar_prefetch=4, grid=(n_dim // tn, nb, nk), in_specs=in_specs, out_specs=o_spec,
        scratch_shapes=scratch)
    return pl.pallas_call(
        functools.partial(_gmm_body, nk=nk, n_w=n_w, epi=epi, res_first_blocks=res_first_blocks,
                          w_transposed=w_rows is not None),
        grid_spec=grid_spec,
        out_shape=jax.ShapeDtypeStruct((m_rows, n_dim), out_dtype),
        compiler_params=_cparams(("arbitrary", "arbitrary", "arbitrary")),
        name="gmm_" + epi,
    )(bexp, neww, mod_idx.astype(jnp.int32), n_used.astype(jnp.int32), *args)


def _split2(v):
    hi = v.astype(BF16)
    lo = (v - hi.astype(F32)).astype(BF16)
    return hi, lo


def _norm_body(x_ref, *rest, modulate, router, first_blocks):
    if first_blocks is not None:
        x2_ref, rest = rest[0], rest[1:]
    g_ref, rest = rest[0], rest[1:]
    if modulate:
        shift_ref, scale_ref, rest = rest[0], rest[1], rest[2:]
    if router:
        rw_ref, rb_ref, rest = rest[0], rest[1], rest[2:]
    o_ref = rest[0]
    x = x_ref[...]
    if first_blocks is not None:
        x = jnp.where(pl.program_id(0) < first_blocks, x, x2_ref[...])
    y = x * lax.rsqrt(jnp.mean(x * x, axis=-1, keepdims=True) + EPS) * g_ref[...]
    if modulate:
        y = y * (1.0 + scale_ref[...]) + shift_ref[...]
    o_ref[...] = y.astype(o_ref.dtype)
    if router:
        rt_ref = rest[1]
        y_hi, y_lo = _split2(y)
        w_hi, w_lo = _split2(rw_ref[...])
        logits = _dot(y_hi, w_hi) + _dot(y_lo, w_hi) + _dot(y_hi, w_lo) + rb_ref[...]
        lane = lax.broadcasted_iota(jnp.int32, logits.shape, 1).astype(F32)
        lg = jnp.where(lane < N_EXPERTS, logits, -jnp.inf)
        m1 = jnp.max(lg, axis=-1, keepdims=True)
        i1 = jnp.min(jnp.where(lg == m1, lane, float(LANES)), axis=-1, keepdims=True)
        lg2 = jnp.where(lane == i1, -jnp.inf, lg)
        m2 = jnp.max(lg2, axis=-1, keepdims=True)
        i2 = jnp.min(jnp.where(lg2 == m2, lane, float(LANES)), axis=-1, keepdims=True)
        e2 = jnp.exp(m2 - m1)
        g1 = 1.0 / (1.0 + e2)
        rt = jnp.where(lane == 0, i1, jnp.where(lane == 1, i2, jnp.where(lane == 2, g1, jnp.where(lane == 3, e2 * g1, 0.0))))
        rt_ref[...] = rt


def _norm(x, g, *, tm, n_rows, mod=None, mod_idx_fn=None, parts=(0, 1), out_dtype=BF16, router=None, x2=None):
    d = x.shape[1]
    nb = n_rows // tm
    row = pl.BlockSpec((tm, d), lambda i: (i, 0))
    first_blocks = None if x2 is None else x.shape[0] // tm
    if x2 is None:
        in_specs, args = [row], [x]
    else:
        in_specs, args = list(_two_source_specs((tm, d), first_blocks, lambda i: i)), [x, x2]
    in_specs.append(pl.BlockSpec((1, d), lambda i: (0, 0)))
    args.append(g.reshape(1, d))
    modulate = mod is not None
    if modulate:
        for p in parts:
            in_specs.append(pl.BlockSpec((None, 1, d), lambda i, p=p: (mod_idx_fn(i), 0, p)))
            args.append(mod)
    out_shape = [jax.ShapeDtypeStruct((n_rows, d), out_dtype)]
    out_specs = [row]
    if router is not None:
        rw, rb = router
        in_specs += [pl.BlockSpec((d, LANES), lambda i: (0, 0)), pl.BlockSpec((1, LANES), lambda i: (0, 0))]
        args += [rw, rb]
        out_shape.append(jax.ShapeDtypeStruct((n_rows, LANES), F32))
        out_specs.append(pl.BlockSpec((tm, LANES), lambda i: (i, 0)))
    res = pl.pallas_call(
        functools.partial(_norm_body, modulate=modulate, router=router is not None, first_blocks=first_blocks),
        grid=(nb,), in_specs=in_specs, out_specs=out_specs, out_shape=out_shape,
        compiler_params=_cparams(("arbitrary",)), name="rmsnorm",
    )(*args)
    return res if router is not None else res[0]


def _rope_body(q_ref, k_ref, c_ref, sm_ref, sp_ref, qo_ref, ko_ref, *, q_scale):
    c, sm, sp = c_ref[...], sm_ref[...], sp_ref[...]
    width = q_ref.shape[1]
    for src, dst, scale in ((q_ref, qo_ref, q_scale), (k_ref, ko_ref, 1.0)):
        for g in range(width // LANES):
            sl = slice(g * LANES, (g + 1) * LANES)
            x = src[:, sl]
            y = x * c + pltpu.roll(x, LANES - 16, 1) * sm + pltpu.roll(x, 16, 1) * sp
            dst[:, sl] = (y * scale).astype(dst.dtype)


def _rope_tables(seq, n_id_rows):
    t = jnp.arange(seq)
    pos = jnp.stack([t // GRID_W, t % GRID_W], axis=-1).astype(F32)
    n_freq = DA_HEAD_DIM // 4
    inv_freq = ROPE_BASE ** (-jnp.arange(n_freq, dtype=F32) / n_freq)
    ang = pos[:, :, None] * inv_freq
    d = np.arange(LANES) % DA_HEAD_DIM
    kind, which, f = d // 32, (d % 32) // 16, d % 16
    a = ang[:, kind, f]
    cos, sin = jnp.cos(a), jnp.sin(a)
    sm = jnp.where(which == 0, -sin, 0.0)
    sp = jnp.where(which == 1, sin, 0.0)
    pad = lambda v, fill: jnp.concatenate([v, jnp.full((n_id_rows, LANES), fill, F32)], axis=0)
    return pad(cos, 1.0), pad(sm, 0.0), pad(sp, 0.0)


def _rope(proj, tabs, *, tm, n_rows, n_lat_rows, seq, q_col, k_col, width):
    lat_blocks = n_lat_rows // tm
    per_seq = seq // tm
    tab_idx = lambda i: (jnp.where(i < lat_blocks, i % per_seq, per_seq), 0)
    out = jax.ShapeDtypeStruct((n_rows, width), BF16)
    return pl.pallas_call(
        functools.partial(_rope_body, q_scale=DA_HEAD_DIM ** -0.5 * LOG2E),
        grid=(n_rows // tm,),
        in_specs=[pl.BlockSpec((tm, width), lambda i: (i, q_col // width)),
                  pl.BlockSpec((tm, width), lambda i: (i, k_col // width)),
                  pl.BlockSpec((tm, LANES), tab_idx), pl.BlockSpec((tm, LANES), tab_idx),
                  pl.BlockSpec((tm, LANES), tab_idx)],
        out_specs=[pl.BlockSpec((tm, width), lambda i: (i, 0))] * 2,
        out_shape=[out, out], compiler_params=_cparams(("arbitrary",)), name="rope",
    )(proj, proj, *tabs)


def _na_body(q_ref, k_ref, v_ref, kc_ref, vc_ref, tb_ref, o_ref, kb, vb, kcb, vcb, *, rows, wh):
    kb[...] = k_ref[...].astype(BF16)
    vb[...] = v_ref[...].astype(BF16)
    kcb[...] = kc_ref[...].astype(BF16)
    vcb[...] = vc_ref[...].astype(BF16)
    lane = lax.broadcasted_iota(jnp.int32, (GRID_W, LANES), 1)
    first = lane < NA_HEAD_DIM
    scale = NA_HEAD_DIM ** -0.5

    def body(grp, carry):
        rr = [grp * NA_ROW_UNROLL + j for j in range(NA_ROW_UNROLL)]
        rs = [jnp.clip(r - wh // 2, 0, rows - wh) for r in rr]
        row_of = lambda r: pl.ds(pl.multiple_of(r * GRID_W, GRID_W), GRID_W)
        win_of = lambda r0: pl.ds(pl.multiple_of(r0 * GRID_W, GRID_W), wh * GRID_W)
        qq = []
        for r in rr:
            q = q_ref[row_of(r), :] * scale
            qq.append(jnp.concatenate([jnp.where(first, q, 0.0), jnp.where(first, 0.0, q)], axis=0).astype(BF16))
        s_l = [_dot_nt(x, kb[win_of(r0), :]) + tb_ref[r0 - r + (NA_WIN_H - 1)] for x, r, r0 in zip(qq, rr, rs)]
        s_c = [_dot_nt(x, kcb[...]) for x in qq]
        m = [jnp.maximum(jnp.max(a, axis=-1, keepdims=True), jnp.max(b, axis=-1, keepdims=True))
             for a, b in zip(s_l, s_c)]
        p_l = [jnp.exp(a - mm) for a, mm in zip(s_l, m)]
        p_c = [jnp.exp(b - mm) for b, mm in zip(s_c, m)]
        den = [jnp.sum(a, axis=-1, keepdims=True) + jnp.sum(b, axis=-1, keepdims=True) for a, b in zip(p_l, p_c)]
        o = [_dot(a.astype(BF16), vb[win_of(r0), :]) + _dot(b.astype(BF16), vcb[...])
             for a, b, r0 in zip(p_l, p_c, rs)]
        for r, oo, dd in zip(rr, o, den):
            oo = oo / dd
            o_ref[row_of(r), :] = jnp.where(first, oo[:GRID_W], oo[GRID_W:]).astype(o_ref.dtype)
        return carry

    assert rows % NA_ROW_UNROLL == 0
    lax.fori_loop(0, rows // NA_ROW_UNROLL, body, 0, unroll=2)


def _na_bias_table(rpb, rows):
    wh = min(NA_WIN_H, rows)
    qc = np.arange(GRID_W)[:, None]
    kc = np.arange(GRID_W)[None, :]
    ws = np.clip(qc - NA_WIN_W // 2, 0, GRID_W - NA_WIN_W)
    ok = (kc >= ws) & (kc < ws + NA_WIN_W)
    n_drow, n_dcol = 2 * NA_WIN_H - 1, 2 * NA_WIN_W - 1
    off = GRID_W - NA_WIN_W
    line = jnp.full((NA_HEADS, n_drow, 2 * GRID_W), NEG_INF, F32).at[:, :, off:off + n_dcol].set(rpb.astype(F32))
    span = 2 * GRID_W - 1
    t = jnp.tile(line, (1, 1, GRID_W))[:, :, :GRID_W * span].reshape(NA_HEADS, n_drow, GRID_W, span)
    t = jnp.where(ok, t[:, :, :, GRID_W - 1:], NEG_INF)
    n_d0 = NA_WIN_H
    t = jnp.stack([t[:, d0:d0 + wh] for d0 in range(n_d0)], axis=1)
    t = t.transpose(0, 1, 3, 2, 4).reshape(NA_HEADS // 2, 2, n_d0, GRID_W, wh * GRID_W)
    return t.transpose(0, 2, 1, 3, 4).reshape(NA_HEADS // 2, n_d0, 2 * GRID_W, wh * GRID_W)


def _na(proj, rpb, *, batch, seq, ctx_len, q_col, k_col, v_col):
    rows = seq // GRID_W
    wh = min(NA_WIN_H, rows)
    pairs = NA_HEADS // 2
    tb = _na_bias_table(rpb, rows)
    ctx0 = batch * seq // ctx_len
    lat = lambda col: pl.BlockSpec((seq, LANES), lambda b, p: (b, col // LANES + p))
    ctx = lambda col: pl.BlockSpec((ctx_len, LANES), lambda b, p: (ctx0 + b, col // LANES + p))
    return pl.pallas_call(
        functools.partial(_na_body, rows=rows, wh=wh),
        grid=(batch, pairs),
        in_specs=[lat(q_col), lat(k_col), lat(v_col), ctx(k_col), ctx(v_col),
                  pl.BlockSpec((None, NA_WIN_H, 2 * GRID_W, wh * GRID_W), lambda b, p: (p, 0, 0, 0))],
        out_specs=pl.BlockSpec((seq, LANES), lambda b, p: (b, p)),
        out_shape=jax.ShapeDtypeStruct((batch * seq, pairs * LANES), BF16),
        scratch_shapes=[pltpu.VMEM((seq, LANES), BF16), pltpu.VMEM((seq, LANES), BF16),
                        pltpu.VMEM((ctx_len, LANES), BF16), pltpu.VMEM((ctx_len, LANES), BF16)],
        compiler_params=_cparams(("arbitrary", "arbitrary")), name="na_attn",
    )(proj, proj, proj, proj, proj, tb)


def _ctx_attn_body(q_ref, k_ref, v_ref, o_ref):
    kb = k_ref[...].astype(BF16)
    vb = v_ref[...].astype(BF16)
    q = q_ref[...] * (NA_HEAD_DIM ** -0.5)
    lane = lax.broadcasted_iota(jnp.int32, q.shape, 1)
    first = lane < NA_HEAD_DIM
    outs = []
    for hh in range(2):
        qm = jnp.where(first if hh == 0 else jnp.logical_not(first), q, 0.0).astype(BF16)
        s = _dot_nt(qm, kb)
        p = jnp.exp(s - jnp.max(s, axis=-1, keepdims=True))
        outs.append(_dot(p.astype(BF16), vb) / jnp.sum(p, axis=-1, keepdims=True))
    o_ref[...] = jnp.where(first, outs[0], outs[1]).astype(o_ref.dtype)


def _ctx_attn(proj, *, batch, seq, ctx_len, q_col, k_col, v_col):
    pairs = NA_HEADS // 2
    ctx0 = batch * seq // ctx_len
    ctx = lambda col: pl.BlockSpec((ctx_len, LANES), lambda b, p: (ctx0 + b, col // LANES + p))
    return pl.pallas_call(
        _ctx_attn_body, grid=(batch, pairs),
        in_specs=[ctx(q_col), ctx(k_col), ctx(v_col)],
        out_specs=pl.BlockSpec((ctx_len, LANES), lambda b, p: (b, p)),
        out_shape=jax.ShapeDtypeStruct((batch * ctx_len, pairs * LANES), BF16),
        compiler_params=_cparams(("arbitrary", "arbitrary")), name="ctx_attn",
    )(proj, proj, proj)


def _da_body(lv_ref, q_ref, *rest, lam_init, has_lat):
    if has_lat:
        kl_ref, vl_ref, rest = rest[0], rest[1], rest[2:]
    kc_ref, vc_ref, g_ref, o_ref = rest[:4]
    scr = rest[4:]
    t = pl.program_id(2)
    if has_lat:
        vlb, vcb = scr

        @pl.when(t == 0)
        def _():
            vlb[...] = vl_ref[...].astype(BF16)
            vcb[...] = vc_ref[...].astype(BF16)
    else:
        (vcb,) = scr
        vcb[...] = vc_ref[...].astype(BF16)

    lv = lv_ref[...]
    lam = (jnp.exp(jnp.sum(lv[0:1] * lv[1:2], axis=-1, keepdims=True))
           - jnp.exp(jnp.sum(lv[2:3] * lv[3:4], axis=-1, keepdims=True)) + lam_init)
    q = q_ref[...]
    tq = q.shape[0]
    lane = lax.broadcasted_iota(jnp.int32, q.shape, 1)
    zero = jnp.zeros_like(q)
    qq = jnp.concatenate([jnp.where(lane < DA_HEAD_DIM, q, zero), jnp.where(lane >= DA_HEAD_DIM, q, zero)], axis=0)
    s_c = _dot_nt(qq, kc_ref[...])
    m = jnp.max(s_c, axis=-1, keepdims=True)
    if has_lat:
        s_l = _dot_nt(qq, kl_ref[...])
        m = jnp.maximum(m, jnp.max(s_l, axis=-1, keepdims=True))
        p_l = jnp.exp2(s_l - m)
    p_c = jnp.exp2(s_c - m)
    den = jnp.sum(p_c, axis=-1, keepdims=True)
    if has_lat:
        den = den + jnp.sum(p_l, axis=-1, keepdims=True)
    ratio = lam * den[:tq] / den[tq:]
    o = _dot((p_c[:tq] - ratio * p_c[tq:]).astype(BF16), vcb[...])
    if has_lat:
        o = o + _dot((p_l[:tq] - ratio * p_l[tq:]).astype(BF16), vlb[...])
    o = o / den[:tq]
    o = o * lax.rsqrt(jnp.mean(o * o, axis=-1, keepdims=True) + EPS) * g_ref[...]
    o_ref[...] = (o * (1.0 - lam_init)).astype(o_ref.dtype)


def _da(q_r, k_r, proj, lam_vec, subln_g, lam_init, *, batch, seq, ctx_len, v_col, tq, latent):
    ctx0 = batch * seq // ctx_len
    vcb0 = v_col // LANES
    g2 = subln_g.reshape(1, 2 * DA_HEAD_DIM).astype(F32)
    small = [pl.BlockSpec((4, DA_HEAD_DIM), lambda b, h, t: (0, 0))]
    gspec = pl.BlockSpec((1, LANES), lambda b, h, t: (0, 0))
    kc = pl.BlockSpec((ctx_len, LANES), lambda b, h, t: (ctx0 + b, h))
    vc = pl.BlockSpec((ctx_len, LANES), lambda b, h, t: (ctx0 + b, vcb0 + h))
    if latent:
        nq = seq // tq
        qs = pl.BlockSpec((tq, LANES), lambda b, h, t: (b * nq + t, h))
        kl = pl.BlockSpec((seq, LANES), lambda b, h, t: (b, h))
        vl = pl.BlockSpec((seq, LANES), lambda b, h, t: (b, vcb0 + h))
        in_specs = small + [qs, kl, vl, kc, vc, gspec]
        args = (lam_vec, q_r, k_r, proj, k_r, proj, g2)
        scratch = [pltpu.VMEM((seq, LANES), BF16), pltpu.VMEM((ctx_len, LANES), BF16)]
        n_out = batch * seq
    else:
        nq = ctx_len // tq
        cq0 = batch * seq // tq
        qs = pl.BlockSpec((tq, LANES), lambda b, h, t: (cq0 + b * nq + t, h))
        in_specs = small + [qs, kc, vc, gspec]
        args = (lam_vec, q_r, k_r, proj, g2)
        scratch = [pltpu.VMEM((ctx_len, LANES), BF16)]
        n_out = batch * ctx_len
    return pl.pallas_call(
        functools.partial(_da_body, lam_init=lam_init, has_lat=latent),
        grid=(batch, DA_HEADS, nq), in_specs=in_specs,
        out_specs=pl.BlockSpec((tq, LANES), lambda b, h, t: (b * nq + t, h)),
        out_shape=jax.ShapeDtypeStruct((n_out, DA_HEADS * LANES), BF16),
        scratch_shapes=scratch,
        compiler_params=_cparams(("arbitrary", "arbitrary", "arbitrary")),
        name="diff_attn_lat" if latent else "diff_attn_ctx",
    )(*args)


def _ssd_body(x0_ref, x1_ref, bc_ref, x0p_ref, x1p_ref, bcp_ref, x0n_ref, x1n_ref, bcn_ref, dt_ref,
              cw_ref, cb_ref, prm_ref, dsk_ref, ex_ref, init_ref, y_ref, sout_ref, st_ref, u_ref, *, nc):
    q_len = SSD_CHUNK
    dirn = pl.program_id(1)
    z = pl.program_id(2)
    zz = jnp.where(dirn == 0, z, nc - 1 - z)
    fwd = dirn == 0
    half = SSM_D_INNER // SSM_GROUPS
    heads_per_group = SSM_HEADS // SSM_GROUPS

    @pl.when(z == 0)
    def _():
        st_ref[...] = init_ref[...]

    def conv_piece(main_ref, prev_ref, next_ref, c0):
        width = main_ref.shape[1]
        prev = jnp.where(zz == 0, 0.0, prev_ref[...])
        nxt = jnp.where(zz == nc - 1, 0.0, next_ref[...])
        ext = jnp.concatenate([prev, main_ref[...], nxt], axis=0)
        n_ext = ext.shape[0]
        acc = jnp.zeros((q_len, width), F32) + cb_ref[:, c0:c0 + width]
        for tap in range(SSM_CONV_W):
            sh = (SSM_CONV_W // 2 - tap) % n_ext
            e = ext if sh == 0 else pltpu.roll(ext, sh, 0)
            acc = acc + e[SUBLANES:SUBLANES + q_len] * cw_ref[tap:tap + 1, c0:c0 + width]
        return _silu(acc)

    @pl.when(fwd)
    def _():
        u_ref[zz, :, 0:half] = conv_piece(x0_ref, x0p_ref, x0n_ref, 0)
        u_ref[zz, :, half:2 * half] = conv_piece(x1_ref, x1p_ref, x1n_ref, half)
        u_ref[zz, :, SSM_D_INNER:] = conv_piece(bc_ref, bcp_ref, bcn_ref, SSM_D_INNER)

    xs = (u_ref[zz, :, 0:half], u_ref[zz, :, half:2 * half])
    bcm = u_ref[zz, :, SSM_D_INNER:]

    raw = dt_ref[...]
    raw = jnp.where(fwd, raw, pltpu.roll(raw, LANES - SSM_HEADS, 1))
    xb = raw + prm_ref[0:1, :]
    dtv = jnp.maximum(xb, 0.0) + jnp.log1p(jnp.exp(-jnp.abs(xb)))
    a = dtv * (-jnp.exp(prm_ref[1:2, :]))
    rowi = lax.broadcasted_iota(jnp.int32, (q_len, LANES), 0)
    coli = lax.broadcasted_iota(jnp.int32, (q_len, LANES), 1)
    cum = a
    sft = 1
    while sft < q_len:
        cum = cum + jnp.where(rowi >= sft, pltpu.roll(cum, sft, 0), 0.0)
        sft *= 2
    tot = cum[q_len - 1:q_len, :]
    g = jnp.where(fwd, cum, cum - a)
    e_g = jnp.exp(g)
    e_tg = jnp.exp(tot - g)
    w_state = jnp.where(fwd, e_tg, e_g)
    w_yoff = jnp.where(fwd, e_g, e_tg)
    e_tot = jnp.exp(tot)
    gs = g * jnp.where(fwd, LOG2E, -LOG2E)
    gs_t = gs.T
    tri = jnp.where(fwd, rowi - coli, coli - rowi) >= 0

    stack = jnp.concatenate([dtv, dtv * w_state, w_yoff, jnp.broadcast_to(e_tot, (SUBLANES, LANES))], axis=0)
    s_hi, s_lo = _split2(stack)
    ex = ex_ref[...]
    expd = _dot(s_hi, ex) + _dot(s_lo, ex)
    dt_e, dts_e, wy_e = expd[0:q_len], expd[q_len:2 * q_len], expd[2 * q_len:3 * q_len]
    tot_e = expd[3 * q_len:3 * q_len + 1]

    lane = lax.broadcasted_iota(jnp.int32, (q_len, LANES), 1)
    first = lane < SSM_HEAD_DIM
    for grp in range(SSM_GROUPS):
        x_g = xs[grp]
        csl = slice(grp * half, (grp + 1) * half)
        b_g = bcm[:, grp * SSM_STATE:(grp + 1) * SSM_STATE]
        c_g = bcm[:, (SSM_GROUPS + grp) * SSM_STATE:(SSM_GROUPS + grp + 1) * SSM_STATE].astype(BF16)
        xdt = (x_g * dt_e[:, csl]).astype(BF16)
        xdts = (x_g * dts_e[:, csl]).astype(BF16)
        cb = _dot_nt(c_g, b_g.astype(BF16))
        st_g = st_ref[:, csl]
        y_g = _dot(c_g, st_g.astype(BF16)) * wy_e[:, csl] + dsk_ref[:, csl] * x_g
        for pr in range(heads_per_group // 2):
            outs = []
            for hh in range(2):
                col = grp * heads_per_group + 2 * pr + hh
                seg = gs[:, col:col + 1] - gs_t[col:col + 1, :]
                mat = (cb * jnp.where(tri, jnp.exp2(seg), 0.0)).astype(BF16)
                outs.append(_dot(mat, xdt[:, pr * LANES:(pr + 1) * LANES]))
            lo = grp * half + pr * LANES
            y_ref[:, lo:lo + LANES] = y_g[:, pr * LANES:(pr + 1) * LANES] + jnp.where(first, outs[0], outs[1])
        st_ref[:, csl] = tot_e[:, csl] * st_g + _dot(b_g.T.astype(BF16), xdts)

    @pl.when(z == nc - 1)
    def _():
        sout_ref[...] = st_ref[...]


def _ssd(proj, conv_w, conv_b, dt_bias, a_log, d_skip, init, *, batch, n_tok, row0, x_col, dt_col):
    q_len = SSD_CHUNK
    nc = n_tok // q_len
    half = SSM_D_INNER // SSM_GROUPS
    conv_dim = SSM_D_INNER + 2 * SSM_GROUPS * SSM_STATE
    rb0 = row0 // q_len
    hb = q_len // SUBLANES
    n_halo = proj.shape[0] // SUBLANES
    zz = lambda d, z: jnp.where(d == 0, z, nc - 1 - z)
    rb = lambda b, d, z: rb0 + b * nc + zz(d, z)
    rbx = lambda b, d, z: rb0 + b * nc + jnp.where(d == 0, z, nc - 1)
    main = lambda c: pl.BlockSpec((q_len, half), lambda b, d, z: (rbx(b, d, z), c))
    prev = lambda c: pl.BlockSpec((SUBLANES, half), lambda b, d, z: (jnp.maximum(rbx(b, d, z) * hb - 1, 0), c))
    nxt = lambda c: pl.BlockSpec((SUBLANES, half),
                                 lambda b, d, z: (jnp.minimum(rbx(b, d, z) * hb + hb, n_halo - 1), c))
    cols = [x_col // half + j for j in range(conv_dim // half)]
    const2 = lambda shape: pl.BlockSpec(shape, lambda b, d, z: (0, 0))
    prm = jnp.zeros((2, SUBLANES, LANES), F32)
    prm = prm.at[:, 0, :SSM_HEADS].set(dt_bias.astype(F32)).at[:, 1, :SSM_HEADS].set(a_log.astype(F32))
    dsk = jnp.repeat(d_skip.astype(F32), SSM_HEAD_DIM, axis=-1).reshape(2, 1, SSM_D_INNER)
    ex = (np.arange(LANES)[:, None] == (np.arange(SSM_D_INNER) // SSM_HEAD_DIM)[None, :])
    ex = jnp.asarray(ex, BF16)
    cw = jnp.zeros((SUBLANES, conv_dim), F32).at[:SSM_CONV_W].set(conv_w.astype(F32))
    state_spec = pl.BlockSpec((None, None, SSM_STATE, SSM_D_INNER), lambda b, d, z: (b, d, 0, 0))
    y, s_out = pl.pallas_call(
        functools.partial(_ssd_body, nc=nc),
        grid=(batch, 2, nc),
        in_specs=[main(cols[0]), main(cols[1]), main(cols[2]), prev(cols[0]), prev(cols[1]), prev(cols[2]),
                  nxt(cols[0]), nxt(cols[1]), nxt(cols[2]),
                  pl.BlockSpec((q_len, LANES), lambda b, d, z: (rb(b, d, z), dt_col // LANES)),
                  const2((SUBLANES, conv_dim)), const2((1, conv_dim)),
                  pl.BlockSpec((None, SUBLANES, LANES), lambda b, d, z: (d, 0, 0)),
                  pl.BlockSpec((None, 1, SSM_D_INNER), lambda b, d, z: (d, 0, 0)),
                  const2((LANES, SSM_D_INNER)), state_spec],
        out_specs=[pl.BlockSpec((None, q_len, SSM_D_INNER), lambda b, d, z: (d, b * nc + zz(d, z), 0)),
                   state_spec],
        out_shape=[jax.ShapeDtypeStruct((2, batch * n_tok, SSM_D_INNER), F32),
                   jax.ShapeDtypeStruct((batch, 2, SSM_STATE, SSM_D_INNER), F32)],
        scratch_shapes=[pltpu.VMEM((SSM_STATE, SSM_D_INNER), F32), pltpu.VMEM((nc, q_len, conv_dim), F32)],
        compiler_params=_cparams(("arbitrary", "arbitrary", "arbitrary")), name="ssd_scan",
    )(proj, proj, proj, proj, proj, proj, proj, proj, proj, proj, cw, conv_b.reshape(1, conv_dim).astype(F32),
      prm, dsk, ex, init)
    return y, s_out


def _ssm_norm_body(y_ref, z_ref, g_ref, o_ref):
    yz = (y_ref[0] + y_ref[1]) * _silu(z_ref[...])
    gw = SSM_D_INNER // SSM_GROUPS
    for grp in range(SSM_GROUPS):
        v = yz[:, grp * gw:(grp + 1) * gw]
        v = v * lax.rsqrt(jnp.mean(v * v, axis=-1, keepdims=True) + EPS)
        o_ref[:, grp * gw:(grp + 1) * gw] = (v * g_ref[:, grp * gw:(grp + 1) * gw]).astype(o_ref.dtype)


def _ssm_norm(y, proj, g, *, tm, row0, z_col):
    n = y.shape[1]
    zb0 = row0 // tm
    return pl.pallas_call(
        _ssm_norm_body, grid=(n // tm,),
        in_specs=[pl.BlockSpec((2, tm, SSM_D_INNER), lambda i: (0, i, 0)),
                  pl.BlockSpec((tm, SSM_D_INNER), lambda i: (zb0 + i, z_col // SSM_D_INNER)),
                  pl.BlockSpec((1, SSM_D_INNER), lambda i: (0, 0))],
        out_specs=pl.BlockSpec((tm, SSM_D_INNER), lambda i: (i, 0)),
        out_shape=jax.ShapeDtypeStruct((n, SSM_D_INNER), BF16),
        compiler_params=_cparams(("arbitrary",)), name="ssm_gated_norm",
    )(y, proj, g.reshape(1, SSM_D_INNER).astype(F32))


def _two_source_specs(block, first_blocks, row_block_of):
    first = pl.BlockSpec(block, lambda *g: (jnp.minimum(row_block_of(*g), first_blocks - 1), 0))
    second = pl.BlockSpec(block, lambda *g: (jnp.maximum(row_block_of(*g) - first_blocks, 0), 0))
    return first, second


def _merge_body(*refs, lat_blocks, has_ctx):
    n_src = 6 if has_ctx else 3
    src, (ga_ref, gb_ref, gc_ref, wa_ref, wb_ref, wc_ref, o_ref, wab, wbb, wcb) = refs[:n_src], refs[n_src:]
    i = pl.program_id(1)

    @pl.when(i == 0)
    def _():
        wab[...] = wa_ref[...].astype(BF16)
        wbb[...] = wb_ref[...].astype(BF16)
        wcb[...] = wc_ref[...].astype(BF16)

    if has_ctx:
        oa, ob, oc = (jnp.where(i < lat_blocks, src[2 * n][...], src[2 * n + 1][...]) for n in range(3))
    else:
        oa, ob, oc = (r[...] for r in src)
    acc = ga_ref[...].astype(F32) * _dot(oa, wab[...])
    acc = acc + gb_ref[...].astype(F32) * _dot(ob, wbb[...])
    acc = acc + gc_ref[...].astype(F32) * _dot(oc, wcb[...])
    o_ref[...] = acc.astype(o_ref.dtype)


def _merge(o_lat, o_ctx, gates, w_a, w_b, w_c, layer, *, tm, tn, n_rows):
    d = w_a.shape[2]
    nj = d // tn
    lat_blocks = o_lat[0].shape[0] // tm
    gate = lambda x: pl.BlockSpec((tm, tn), lambda j, i: (i, x * nj + j))
    wsp = lambda kk: pl.BlockSpec((None, kk, tn), lambda j, i: (layer, 0, j))
    widths = (w_a.shape[1], w_b.shape[1], w_c.shape[1])
    src_specs, src = [], []
    for n, kk in enumerate(widths):
        if o_ctx is None:
            src_specs.append(pl.BlockSpec((tm, kk), lambda j, i: (i, 0)))
            src.append(o_lat[n])
        else:
            src_specs += _two_source_specs((tm, kk), lat_blocks, lambda j, i: i)
            src += [o_lat[n], o_ctx[n]]
    return pl.pallas_call(
        functools.partial(_merge_body, lat_blocks=lat_blocks, has_ctx=o_ctx is not None), grid=(nj, n_rows // tm),
        in_specs=src_specs + [gate(0), gate(1), gate(2)] + [wsp(kk) for kk in widths],
        out_specs=pl.BlockSpec((tm, tn), lambda j, i: (i, j)),
        out_shape=jax.ShapeDtypeStruct((n_rows, d), BF16),
        scratch_shapes=[pltpu.VMEM((kk, tn), BF16) for kk in widths],
        compiler_params=_cparams(("arbitrary", "arbitrary")), name="branch_merge",
    )(*src, gates, gates, gates, w_a, w_b, w_c)


def _route(top_idx, blk):
    n = top_idx.shape[0]
    e_flat = top_idx.reshape(-1)
    onehot = (e_flat[:, None] == jnp.arange(N_EXPERTS, dtype=jnp.int32)[None, :]).astype(jnp.int32)
    counts = jnp.sum(onehot, axis=0)
    rank = jnp.sum((jnp.cumsum(onehot, axis=0) - onehot) * onehot, axis=1)
    padded = (counts + blk - 1) // blk * blk
    pad_end = jnp.cumsum(padded)
    pad_start = pad_end - padded
    dest = jnp.sum(onehot * pad_start[None, :], axis=1) + rank
    n_blocks = -(-(n * TOP_K) // blk) + N_EXPERTS
    cap = n_blocks * blk
    tok_flat = jnp.repeat(jnp.arange(n, dtype=jnp.int32), TOP_K)
    slot_tok = jnp.zeros((cap,), jnp.int32).at[dest].set(tok_flat)
    blk_start = jnp.arange(n_blocks, dtype=jnp.int32) * blk
    block_expert = jnp.clip(jnp.sum((pad_end[None, :] <= blk_start[:, None]).astype(jnp.int32), axis=1),
                            0, N_EXPERTS - 1)
    n_used = (pad_end[-1] // blk).astype(jnp.int32).reshape(1)
    return dest.astype(jnp.int32), slot_tok, block_expert, n_used


GATHER_UNROLL = 8


def _row_copy(src_hbm, row, buf, slot, sem):
    return pltpu.make_async_copy(src_hbm.at[pl.ds(row, 1), :], buf.at[pl.ds(slot, 1), :], sem)


def _gather_body(nv_ref, idx_ref, src_hbm, o_ref, buf, sem, *, rows):
    @pl.when(pl.program_id(0) >= nv_ref[0])
    def _():
        o_ref[...] = jnp.zeros_like(o_ref)

    @pl.when(pl.program_id(0) < nv_ref[0])
    def _():
        def issue(g, c):
            for u in range(GATHER_UNROLL):
                r = g * GATHER_UNROLL + u
                _row_copy(src_hbm, idx_ref[0, r], buf, r, sem).start(priority=u % 2)
            return c

        def drain(g, c):
            for u in range(GATHER_UNROLL):
                r = g * GATHER_UNROLL + u
                _row_copy(src_hbm, idx_ref[0, r], buf, r, sem).wait()
            return c

        lax.fori_loop(0, rows // GATHER_UNROLL, issue, 0)
        lax.fori_loop(0, rows // GATHER_UNROLL, drain, 0)
        o_ref[...] = buf[...].astype(o_ref.dtype)


def _gather_rows(src, idx, n_used, *, rows, out_dtype):
    n, d = src.shape
    nb = idx.shape[0] // rows
    assert rows % GATHER_UNROLL == 0
    blk = lambda i, nv: jnp.minimum(i, nv[0] - 1)
    grid_spec = pltpu.PrefetchScalarGridSpec(
        num_scalar_prefetch=1, grid=(nb,),
        in_specs=[pl.BlockSpec((None, 1, rows), lambda i, nv: (blk(i, nv), 0, 0), memory_space=pltpu.SMEM),
                  pl.BlockSpec(memory_space=pl.ANY)],
        out_specs=pl.BlockSpec((rows, d), lambda i, nv: (i, 0)),
        scratch_shapes=[pltpu.VMEM((rows, d), src.dtype), pltpu.SemaphoreType.DMA])
    return pl.pallas_call(
        functools.partial(_gather_body, rows=rows), grid_spec=grid_spec,
        out_shape=jax.ShapeDtypeStruct((nb * rows, d), out_dtype),
        compiler_params=_cparams(("arbitrary",)), name="moe_dispatch",
    )(n_used, idx.reshape(nb, 1, rows), src)


def _combine_body(dest_ref, h_ref, rt_ref, gate_ref, g_ref, y_hbm, o_ref, buf, sem, *, rows, norm):
    def issue(g, c):
        for u in range(GATHER_UNROLL):
            r = g * GATHER_UNROLL + u
            for k in range(TOP_K):
                _row_copy(y_hbm, dest_ref[0, r * TOP_K + k], buf.at[k], r, sem).start(priority=k)
        return c

    def drain(g, c):
        for u in range(GATHER_UNROLL):
            r = g * GATHER_UNROLL + u
            for k in range(TOP_K):
                _row_copy(y_hbm, dest_ref[0, r * TOP_K + k], buf.at[k], r, sem).wait()
        return c

    lax.fori_loop(0, rows // GATHER_UNROLL, issue, 0)
    lax.fori_loop(0, rows // GATHER_UNROLL, drain, 0)
    rt = rt_ref[...]
    f = buf[0] * rt[:, TOP_K:TOP_K + 1] + buf[1] * rt[:, TOP_K + 1:TOP_K + 2]
    x = h_ref[...] + gate_ref[...] * f
    if norm:
        x = x * lax.rsqrt(jnp.mean(x * x, axis=-1, keepdims=True) + EPS) * g_ref[...]
    o_ref[...] = x


def _combine(h, y_sorted, dest, rt, mod, mod_idx_fn, part, g, *, rows, n_rows, norm):
    d = h.shape[1]
    nb = n_rows // rows
    row = pl.BlockSpec((rows, d), lambda i: (i, 0))
    return pl.pallas_call(
        functools.partial(_combine_body, rows=rows, norm=norm), grid=(nb,),
        in_specs=[pl.BlockSpec((None, 1, rows * TOP_K), lambda i: (i, 0, 0), memory_space=pltpu.SMEM),
                  row, pl.BlockSpec((rows, LANES), lambda i: (i, 0)),
                  pl.BlockSpec((None, 1, d), lambda i: (mod_idx_fn(i), 0, part)),
                  pl.BlockSpec((1, d), lambda i: (0, 0)), pl.BlockSpec(memory_space=pl.ANY)],
        out_specs=row, out_shape=jax.ShapeDtypeStruct((n_rows, d), F32),
        scratch_shapes=[pltpu.VMEM((TOP_K, rows, d), y_sorted.dtype), pltpu.SemaphoreType.DMA],
        compiler_params=_cparams(("arbitrary",)), name="moe_combine",
    )(dest.reshape(nb, 1, rows * TOP_K), h, rt, mod, g.reshape(1, d), y_sorted)


def kernel(x, c, ctx, c_ctx, ada_w, ada_b, norm_mix_g, norm_ffn_g, w_in, na_rpb, da_lambda, da_subln_g,
           ssm_conv_w, ssm_conv_b, ssm_dt_bias, ssm_a_log, ssm_d, ssm_norm_g, w_branch_a, w_branch_b,
           w_branch_c, w_out, ffn_w1, ffn_w3, ffn_w2, moe_router_w, moe_router_b, moe_w1, moe_w3, moe_w2,
           final_norm_g):
    batch, seq, d = x.shape
    ctx_len = ctx.shape[1]
    depth = ada_w.shape[0]
    n_lat, n_ctx = batch * seq, batch * ctx_len
    n_all = n_lat + n_ctx
    tm = next(t for t in (1024, 512, 256) if n_ctx % t == 0 and seq % t == 0)
    n_mod = 16
    assert batch + 1 <= n_mod
    lat_blocks = n_lat // tm
    per_seq = seq // tm
    te = next(t for t in (512, 256) if n_ctx % t == 0 and seq % t == 0)
    mod_idx_fn = lambda i: jnp.where(i < n_lat // te, i // (seq // te), batch)
    mod_idx_all = jnp.where(jnp.arange(n_all // tm) < lat_blocks, jnp.arange(n_all // tm) // per_seq, batch)
    layer_blocks = lambda nb, idx: jnp.full((nb,), idx, jnp.int32)

    na_w, da_w = NA_HEADS * NA_HEAD_DIM, DA_HEADS * 2 * DA_HEAD_DIM
    conv_dim = SSM_D_INNER + 2 * SSM_GROUPS * SSM_STATE
    col_qa, col_ka, col_va = 0, na_w, 2 * na_w
    col_qb, col_kb, col_vb = 3 * na_w, 3 * na_w + da_w, 3 * na_w + 2 * da_w
    col_z = 3 * na_w + 3 * da_w
    col_x = col_z + SSM_D_INNER
    col_dt = col_x + conv_dim
    col_gate = col_dt + 2 * SSM_HEADS
    n_main = 6144
    assert col_dt + LANES <= n_main and n_main % 512 == 0 and col_dt % LANES == 0

    h, h_ctx = x.reshape(n_lat, d), ctx.reshape(n_ctx, d)
    cond =jnp.zeros((n_mod, d), F32).at[:batch].set(jax.nn.silu(c)).at[batch].set(jax.nn.silu(c_ctx))
    cond = cond.astype(BF16)
    rope_tabs = _rope_tables(seq, tm)
    zero_state = jnp.zeros((batch, 2, SSM_STATE, SSM_D_INNER), F32)
    w_in_t = jnp.swapaxes(w_in, 1, 2)

    out = None
    for li in range(depth):
        last = li == depth - 1
        lam_init = 0.8 - 0.6 * math.exp(-0.3 * li)
        n_out = n_lat if last else n_all
        mod = _gmm(cond, [ada_w], layer_blocks(1, li), tm=n_mod, tn=512, epi="bias",
                   bias=ada_b[li].reshape(1, 6 * d)).reshape(n_mod, 1, 6 * d)
        u = _norm(h, norm_mix_g[li], tm=te, n_rows=n_all, mod=mod, mod_idx_fn=mod_idx_fn, parts=(0, 1), x2=h_ctx)
        proj = _gmm(u, [w_in_t], layer_blocks(n_all // tm, li), tm=tm, tn=1024, n_cols=n_main, w_rows=0)
        gates = _gmm(u, [w_in_t], layer_blocks(n_out // tm, li), tm=tm, tn=1024, n_rows=n_out, epi="sigmoid",
                     out_dtype=BF16, n_cols=3 * d, w_rows=col_gate)

        o_a = _na(proj, na_rpb[li], batch=batch, seq=seq, ctx_len=ctx_len, q_col=col_qa, k_col=col_ka, v_col=col_va)
        q_r, k_r = _rope(proj, rope_tabs, tm=tm, n_rows=n_all, n_lat_rows=n_lat, seq=seq,
                         q_col=col_qb, k_col=col_kb, width=da_w)
        o_b = _da(q_r, k_r, proj, da_lambda[li].astype(F32), da_subln_g[li], lam_init, batch=batch, seq=seq,
                  ctx_len=ctx_len, v_col=col_vb, tq=256, latent=True)
        ssd_args = (ssm_conv_w[li], ssm_conv_b[li], ssm_dt_bias[li], ssm_a_log[li], ssm_d[li])
        y_ctx, s_ctx = _ssd(proj, *ssd_args, zero_state, batch=batch, n_tok=ctx_len, row0=n_lat,
                            x_col=col_x, dt_col=col_dt)
        y_lat, _ = _ssd(proj, *ssd_args, s_ctx, batch=batch, n_tok=seq, row0=0, x_col=col_x, dt_col=col_dt)
        o_c = _ssm_norm(y_lat, proj, ssm_norm_g[li], tm=te, row0=0, z_col=col_z)
        if not last:
            o_a_c = _ctx_attn(proj, batch=batch, seq=seq, ctx_len=ctx_len, q_col=col_qa, k_col=col_ka, v_col=col_va)
            o_b_c = _da(q_r, k_r, proj, da_lambda[li].astype(F32), da_subln_g[li], lam_init, batch=batch, seq=seq,
                        ctx_len=ctx_len, v_col=col_vb, tq=ctx_len, latent=False)
            o_c_c = _ssm_norm(y_ctx, proj, ssm_norm_g[li], tm=te, row0=n_lat, z_col=col_z)
        merged = _merge((o_a, o_b, o_c), None if last else (o_a_c, o_b_c, o_c_c), gates,
                        w_branch_a, w_branch_b, w_branch_c, li, tm=tm, tn=512, n_rows=n_out)
        h = _gmm(merged, [w_out], layer_blocks(n_out // tm, li), tm=tm, tn=1024, epi="resid", res=h, res2=h_ctx,
                 mod=mod, mod_idx=mod_idx_all[:n_out // tm], mod_part=2, n_rows=n_out)
        h_ctx = None

        j = li // 2
        if li % 2 == 0:
            tokens = _norm(h, norm_ffn_g[li], tm=te, n_rows=n_out, mod=mod, mod_idx_fn=mod_idx_fn, parts=(3, 4))
            hid = _gmm(tokens, [ffn_w1, ffn_w3], layer_blocks(n_out // tm, j), tm=tm, tn=512,
                       epi="swiglu", out_dtype=BF16)
            h = _gmm(hid, [ffn_w2], layer_blocks(n_out // tm, j), tm=tm, tn=512, epi="resid", res=h,
                     mod=mod, mod_idx=mod_idx_all[:n_out // tm], mod_part=5, n_rows=n_out, w_buffers=1)
            if last:
                out = _norm(h, final_norm_g, tm=te, n_rows=n_lat, out_dtype=F32)
        else:
            rw = jnp.zeros((d, LANES), F32).at[:, :N_EXPERTS].set(moe_router_w[j].astype(F32))
            rb = jnp.zeros((1, LANES), F32).at[0, :N_EXPERTS].set(moe_router_b[j].astype(F32))
            tokens, rt = _norm(h, norm_ffn_g[li], tm=te, n_rows=n_out, mod=mod, mod_idx_fn=mod_idx_fn,
                               parts=(3, 4), router=(rw, rb), out_dtype=F32)
            dest, slot_tok, block_expert, n_used = _route(rt[:, :TOP_K].astype(jnp.int32), MOE_ROWS)
            block_expert = block_expert + j * N_EXPERTS
            stack = lambda w: w.reshape((-1,) + w.shape[2:])
            x_sorted = _gather_rows(tokens, slot_tok, n_used, rows=MOE_ROWS, out_dtype=BF16)
            hid = _gmm(x_sorted, [stack(moe_w1), stack(moe_w3)], block_expert, tm=MOE_ROWS, tn=512, epi="swiglu",
                       out_dtype=BF16, n_used=n_used)
            y_sorted = _gmm(hid, [stack(moe_w2)], block_expert, tm=MOE_ROWS, tn=512, n_used=n_used, w_buffers=1)
            h = _combine(h, y_sorted, dest, rt, mod, mod_idx_fn, 5, final_norm_g, rows=te, n_rows=n_out, norm=last)
            if last:
                out = h
    return out.reshape(batch, seq, d)
```

```python
import functools
import math

import numpy as np
import jax
import jax.numpy as jnp
from jax import lax
from jax.experimental import pallas as pl
from jax.experimental.pallas import tpu as pltpu

F32 = jnp.float32
BF16 = jnp.bfloat16

GRID_W = 64
NA_HEADS, NA_HEAD_DIM, NA_WIN_H, NA_WIN_W = 8, 64, 8, 16
DA_HEADS, DA_HEAD_DIM = 4, 64
ROPE_BASE = 10000.0
SSM_D_INNER, SSM_HEAD_DIM, SSM_GROUPS, SSM_STATE, SSM_CONV_W = 1024, 64, 2, 128, 5
SSM_HEADS = SSM_D_INNER // SSM_HEAD_DIM
N_EXPERTS, TOP_K = 8, 2
EPS = 1e-6
NEG_INF = -1e30

LANES = 128
SUBLANES = 8
VMEM_LIMIT_BYTES = 56 * 1024 * 1024
SSD_CHUNK = 128
MOE_ROWS = 1024
NA_ROW_UNROLL = 4
DA_KEY_CHUNK = 512
LOG2E = math.log2(math.e)


def _cparams(sem):
    return pltpu.CompilerParams(dimension_semantics=sem, vmem_limit_bytes=VMEM_LIMIT_BYTES)


def _dot(a, b):
    return jnp.dot(a, b, preferred_element_type=F32)


def _dot_nt(a, b):
    return lax.dot_general(a, b, (((1,), (1,)), ((), ())), preferred_element_type=F32)


def _silu(x):
    return x * jax.nn.sigmoid(x)


def _gmm_body(bexp_ref, neww_ref, mod_ref, nv_ref, x_ref, *rest, nk, n_w, epi, res_first_blocks, w_transposed):
    del bexp_ref, mod_ref
    w_refs, rest = rest[:n_w], rest[n_w:]
    if epi == "bias":
        bias_ref, rest = rest[0], rest[1:]
    elif epi == "resid":
        res_ref, rest = rest[0], rest[1:]
        if res_first_blocks is not None:
            res2_ref, rest = rest[0], rest[1:]
        gate_ref, rest = rest[0], rest[1:]
    o_ref, rest = rest[0], rest[1:]
    wbf_refs, acc_refs = rest[:n_w], rest[n_w:]
    i = pl.program_id(1)
    k = pl.program_id(2)

    def finalize(vals):
        if epi == "swiglu":
            o_ref[...] = (_silu(vals[0]) * vals[1]).astype(o_ref.dtype)
        elif epi == "bias":
            o_ref[...] = (vals[0] + bias_ref[...]).astype(o_ref.dtype)
        elif epi == "resid":
            res = res_ref[...]
            if res_first_blocks is not None:
                res = jnp.where(i < res_first_blocks, res, res2_ref[...])
            o_ref[...] = (res + gate_ref[...] * vals[0]).astype(o_ref.dtype)
        elif epi == "sigmoid":
            o_ref[...] = jax.nn.sigmoid(vals[0]).astype(o_ref.dtype)
        else:
            o_ref[...] = vals[0].astype(o_ref.dtype)

    @pl.when(jnp.logical_and(i >= nv_ref[0], k == nk - 1))
    def _():
        o_ref[...] = jnp.zeros_like(o_ref)

    @pl.when(i < nv_ref[0])
    def _():
        @pl.when(neww_ref[i] == 1)
        def _():
            for w_ref, wbf in zip(w_refs, wbf_refs):
                w = w_ref[...]
                wbf[k] = (w.T if w_transposed else w).astype(BF16)

        x = x_ref[...]
        parts = [_dot(x, wbf[k]) for wbf in wbf_refs]
        if nk == 1:
            finalize(parts)
        else:
            @pl.when(k == 0)
            def _():
                for acc, p in zip(acc_refs, parts):
                    acc[...] = p

            @pl.when(k > 0)
            def _():
                for acc, p in zip(acc_refs, parts):
                    acc[...] += p

            @pl.when(k == nk - 1)
            def _():
                finalize([acc[...] for acc in acc_refs])


def _gmm(x, ws, bexp, *, tm, tn, nk=1, n_cols=None, epi="plain", out_dtype=F32, bias=None, res=None,
         res2=None, mod=None, mod_idx=None, mod_part=0, n_rows=None, n_used=None, w_buffers=2, w_rows=None):
    m_rows = x.shape[0] if n_rows is None else n_rows
    k_dim = x.shape[1]
    n_dim = ws[0].shape[2] if n_cols is None else n_cols
    n_w = len(ws)
    tk = k_dim // nk
    nb = m_rows // tm
    res_first_blocks = None
    assert m_rows % tm == 0 and n_dim % tn == 0 and k_dim % nk == 0
    bexp = bexp.astype(jnp.int32)
    neww = jnp.concatenate([jnp.ones((1,), jnp.int32), (bexp[1:] != bexp[:-1]).astype(jnp.int32)])
    if mod_idx is None:
        mod_idx = jnp.zeros((nb,), jnp.int32)
    if n_used is None:
        n_used = jnp.full((1,), nb, jnp.int32)

    rb = lambda i, nv: jnp.minimum(i, nv[0] - 1)
    x_spec = pl.BlockSpec((tm, tk), lambda j, i, k, be, nw, md, nv: (rb(i, nv), k))
    kb_of = lambda i, k, nw, nv: jnp.where(nw[rb(i, nv)] == 1, k, nk - 1)
    if w_rows is None:
        w_block = (None, tk, tn)
        w_map = lambda j, i, k, be, nw, md, nv: (be[rb(i, nv)], kb_of(i, k, nw, nv), j)
    else:
        n_total = ws[0].shape[1]
        assert n_total % SUBLANES == 0 and w_rows % SUBLANES == 0 and tn % SUBLANES == 0
        ws = [w.reshape(-1, k_dim) for w in ws]
        w_block = (pl.Element(tn), pl.Element(tk))
        w_map = lambda j, i, k, be, nw, md, nv: (
            pl.multiple_of(be[rb(i, nv)] * n_total + w_rows + j * tn, SUBLANES),
            pl.multiple_of(kb_of(i, k, nw, nv) * tk, LANES))
    if w_buffers == 2:
        w_spec = pl.BlockSpec(w_block, w_map)
    else:
        w_spec = pl.BlockSpec(w_block, w_map, pipeline_mode=pl.Buffered(w_buffers))
    o_spec = pl.BlockSpec((tm, tn), lambda j, i, k, be, nw, md, nv: (i, j))
    in_specs = [x_spec] + [w_spec] * n_w
    args = [x] + list(ws)
    if epi == "bias":
        in_specs.append(pl.BlockSpec((1, tn), lambda j, i, k, be, nw, md, nv: (0, j)))
        args.append(bias)
    elif epi == "resid":
        part_off = mod_part * (n_dim // tn)
        if res2 is None:
            in_specs.append(o_spec)
            args.append(res)
        else:
            res_first_blocks = res.shape[0] // tm
            in_specs.append(pl.BlockSpec(
                (tm, tn), lambda j, i, k, be, nw, md, nv: (jnp.minimum(i, res_first_blocks - 1), j)))
            in_specs.append(pl.BlockSpec(
                (tm, tn), lambda j, i, k, be, nw, md, nv: (jnp.maximum(i - res_first_blocks, 0), j)))
            args += [res, res2]
        in_specs.append(pl.BlockSpec((None, 1, tn),
                                     lambda j, i, k, be, nw, md, nv: (md[rb(i, nv)], 0, part_off + j)))
        args.append(mod)
    scratch = [pltpu.VMEM((nk, tk, tn), BF16) for _ in range(n_w)]
    if nk > 1:
        scratch += [pltpu.VMEM((tm, tn), F32) for _ in range(n_w)]
    grid_spec = pltpu.PrefetchScalarGridSpec(
        num_scalar_prefetch=4, grid=(n_dim // tn, nb, nk), in_specs=in_specs, out_specs=o_spec,
        scratch_shapes=scratch)
    return pl.pallas_call(
        functools.partial(_gmm_body, nk=nk, n_w=n_w, epi=epi, res_first_blocks=res_first_blocks,
                          w_transposed=w_rows is not None),
        grid_spec=grid_spec,
        out_shape=jax.ShapeDtypeStruct((m_rows, n_dim), out_dtype),
        compiler_params=_cparams(("arbitrary", "arbitrary", "arbitrary")),
        name="gmm_" + epi,
    )(bexp, neww, mod_idx.astype(jnp.int32), n_used.astype(jnp.int32), *args)


def _split2(v):
    hi = v.astype(BF16)
    lo = (v - hi.astype(F32)).astype(BF16)
    return hi, lo


def _norm_body(x_ref, *rest, modulate, router, first_blocks):
    if first_blocks is not None:
        x2_ref, rest = rest[0], rest[1:]
    g_ref, rest = rest[0], rest[1:]
    if modulate:
        shift_ref, scale_ref, rest = rest[0], rest[1], rest[2:]
    if router:
        rw_ref, rb_ref, rest = rest[0], rest[1], rest[2:]
    o_ref = rest[0]
    x = x_ref[...]
    if first_blocks is not None:
        x = jnp.where(pl.program_id(0) < first_blocks, x, x2_ref[...])
    y = x * lax.rsqrt(jnp.mean(x * x, axis=-1, keepdims=True) + EPS) * g_ref[...]
    if modulate:
        y = y * (1.0 + scale_ref[...]) + shift_ref[...]
    o_ref[...] = y.astype(o_ref.dtype)
    if router:
        rt_ref = rest[1]
        y_hi, y_lo = _split2(y)
        w_hi, w_lo = _split2(rw_ref[...])
        logits = _dot(y_hi, w_hi) + _dot(y_lo, w_hi) + _dot(y_hi, w_lo) + rb_ref[...]
        lane = lax.broadcasted_iota(jnp.int32, logits.shape, 1).astype(F32)
        lg = jnp.where(lane < N_EXPERTS, logits, -jnp.inf)
        m1 = jnp.max(lg, axis=-1, keepdims=True)
        i1 = jnp.min(jnp.where(lg == m1, lane, float(LANES)), axis=-1, keepdims=True)
        lg2 = jnp.where(lane == i1, -jnp.inf, lg)
        m2 = jnp.max(lg2, axis=-1, keepdims=True)
        i2 = jnp.min(jnp.where(lg2 == m2, lane, float(LANES)), axis=-1, keepdims=True)
        e2 = jnp.exp(m2 - m1)
        g1 = 1.0 / (1.0 + e2)
        rt = jnp.where(lane == 0, i1, jnp.where(lane == 1, i2, jnp.where(lane == 2, g1, jnp.where(lane == 3, e2 * g1, 0.0))))
        rt_ref[...] = rt


def _norm(x, g, *, tm, n_rows, mod=None, mod_idx_fn=None, parts=(0, 1), out_dtype=BF16, router=None, x2=None):
    d = x.shape[1]
    nb = n_rows // tm
    row = pl.BlockSpec((tm, d), lambda i: (i, 0))
    first_blocks = None if x2 is None else x.shape[0] // tm
    if x2 is None:
        in_specs, args = [row], [x]
    else:
        in_specs, args = list(_two_source_specs((tm, d), first_blocks, lambda i: i)), [x, x2]
    in_specs.append(pl.BlockSpec((1, d), lambda i: (0, 0)))
    args.append(g.reshape(1, d))
    modulate = mod is not None
    if modulate:
        for p in parts:
            in_specs.append(pl.BlockSpec((None, 1, d), lambda i, p=p: (mod_idx_fn(i), 0, p)))
            args.append(mod)
    out_shape = [jax.ShapeDtypeStruct((n_rows, d), out_dtype)]
    out_specs = [row]
    if router is not None:
        rw, rb = router
        in_specs += [pl.BlockSpec((d, LANES), lambda i: (0, 0)), pl.BlockSpec((1, LANES), lambda i: (0, 0))]
        args += [rw, rb]
        out_shape.append(jax.ShapeDtypeStruct((n_rows, LANES), F32))
        out_specs.append(pl.BlockSpec((tm, LANES), lambda i: (i, 0)))
    res = pl.pallas_call(
        functools.partial(_norm_body, modulate=modulate, router=router is not None, first_blocks=first_blocks),
        grid=(nb,), in_specs=in_specs, out_specs=out_specs, out_shape=out_shape,
        compiler_params=_cparams(("arbitrary",)), name="rmsnorm",
    )(*args)
    return res if router is not None else res[0]


def _rope_body(q_ref, k_ref, c_ref, sm_ref, sp_ref, qo_ref, ko_ref, *, q_scale):
    c, sm, sp = c_ref[...], sm_ref[...], sp_ref[...]
    width = q_ref.shape[1]
    for src, dst, scale in ((q_ref, qo_ref, q_scale), (k_ref, ko_ref, 1.0)):
        for g in range(width // LANES):
            sl = slice(g * LANES, (g + 1) * LANES)
            x = src[:, sl]
            y = x * c + pltpu.roll(x, LANES - 16, 1) * sm + pltpu.roll(x, 16, 1) * sp
            dst[:, sl] = (y * scale).astype(dst.dtype)


def _rope_tables(seq, n_id_rows):
    t = jnp.arange(seq)
    pos = jnp.stack([t // GRID_W, t % GRID_W], axis=-1).astype(F32)
    n_freq = DA_HEAD_DIM // 4
    inv_freq = ROPE_BASE ** (-jnp.arange(n_freq, dtype=F32) / n_freq)
    ang = pos[:, :, None] * inv_freq
    d = np.arange(LANES) % DA_HEAD_DIM
    kind, which, f = d // 32, (d % 32) // 16, d % 16
    a = ang[:, kind, f]
    cos, sin = jnp.cos(a), jnp.sin(a)
    sm = jnp.where(which == 0, -sin, 0.0)
    sp = jnp.where(which == 1, sin, 0.0)
    pad = lambda v, fill: jnp.concatenate([v, jnp.full((n_id_rows, LANES), fill, F32)], axis=0)
    return pad(cos, 1.0), pad(sm, 0.0), pad(sp, 0.0)


def _rope(proj, tabs, *, tm, n_rows, n_lat_rows, seq, q_col, k_col, width):
    lat_blocks = n_lat_rows // tm
    per_seq = seq // tm
    tab_idx = lambda i: (jnp.where(i < lat_blocks, i % per_seq, per_seq), 0)
    out = jax.ShapeDtypeStruct((n_rows, width), BF16)
    return pl.pallas_call(
        functools.partial(_rope_body, q_scale=DA_HEAD_DIM ** -0.5 * LOG2E),
        grid=(n_rows // tm,),
        in_specs=[pl.BlockSpec((tm, width), lambda i: (i, q_col // width)),
                  pl.BlockSpec((tm, width), lambda i: (i, k_col // width)),
                  pl.BlockSpec((tm, LANES), tab_idx), pl.BlockSpec((tm, LANES), tab_idx),
                  pl.BlockSpec((tm, LANES), tab_idx)],
        out_specs=[pl.BlockSpec((tm, width), lambda i: (i, 0))] * 2,
        out_shape=[out, out], compiler_params=_cparams(("arbitrary",)), name="rope",
    )(proj, proj, *tabs)


def _na_body(q_ref, k_ref, v_ref, kc_ref, vc_ref, tb_ref, o_ref, kb, vb, kcb, vcb, *, rows, wh):
    kb[...] = k_ref[...].astype(BF16)
    vb[...] = v_ref[...].astype(BF16)
    kcb[...] = kc_ref[...].astype(BF16)
    vcb[...] = vc_ref[...].astype(BF16)
    lane = lax.broadcasted_iota(jnp.int32, (GRID_W, LANES), 1)
    first = lane < NA_HEAD_DIM
    scale = NA_HEAD_DIM ** -0.5

    def body(grp, carry):
        rr = [grp * NA_ROW_UNROLL + j for j in range(NA_ROW_UNROLL)]
        rs = [jnp.clip(r - wh // 2, 0, rows - wh) for r in rr]
        row_of = lambda r: pl.ds(pl.multiple_of(r * GRID_W, GRID_W), GRID_W)
        win_of = lambda r0: pl.ds(pl.multiple_of(r0 * GRID_W, GRID_W), wh * GRID_W)
        qq = []
        for r in rr:
            q = q_ref[row_of(r), :] * scale
            qq.append(jnp.concatenate([jnp.where(first, q, 0.0), jnp.where(first, 0.0, q)], axis=0).astype(BF16))
        s_l = [_dot_nt(x, kb[win_of(r0), :]) + tb_ref[r0 - r + (NA_WIN_H - 1)] for x, r, r0 in zip(qq, rr, rs)]
        s_c = [_dot_nt(x, kcb[...]) for x in qq]
        m = [jnp.maximum(jnp.max(a, axis=-1, keepdims=True), jnp.max(b, axis=-1, keepdims=True))
             for a, b in zip(s_l, s_c)]
        p_l = [jnp.exp(a - mm) for a, mm in zip(s_l, m)]
        p_c = [jnp.exp(b - mm) for b, mm in zip(s_c, m)]
        den = [jnp.sum(a, axis=-1, keepdims=True) + jnp.sum(b, axis=-1, keepdims=True) for a, b in zip(p_l, p_c)]
        o = [_dot(a.astype(BF16), vb[win_of(r0), :]) + _dot(b.astype(BF16), vcb[...])
             for a, b, r0 in zip(p_l, p_c, rs)]
        for r, oo, dd in zip(rr, o, den):
            oo = oo / dd
            o_ref[row_of(r), :] = jnp.where(first, oo[:GRID_W], oo[GRID_W:]).astype(o_ref.dtype)
        return carry

    assert rows % NA_ROW_UNROLL == 0
    lax.fori_loop(0, rows // NA_ROW_UNROLL, body, 0, unroll=2)


def _na_bias_table(rpb, rows):
    wh = min(NA_WIN_H, rows)
    qc = np.arange(GRID_W)[:, None]
    kc = np.arange(GRID_W)[None, :]
    ws = np.clip(qc - NA_WIN_W // 2, 0, GRID_W - NA_WIN_W)
    ok = (kc >= ws) & (kc < ws + NA_WIN_W)
    n_drow, n_dcol = 2 * NA_WIN_H - 1, 2 * NA_WIN_W - 1
    off = GRID_W - NA_WIN_W
    line = jnp.full((NA_HEADS, n_drow, 2 * GRID_W), NEG_INF, F32).at[:, :, off:off + n_dcol].set(rpb.astype(F32))
    span = 2 * GRID_W - 1
    t = jnp.tile(line, (1, 1, GRID_W))[:, :, :GRID_W * span].reshape(NA_HEADS, n_drow, GRID_W, span)
    t = jnp.where(ok, t[:, :, :, GRID_W - 1:], NEG_INF)
    n_d0 = NA_WIN_H
    t = jnp.stack([t[:, d0:d0 + wh] for d0 in range(n_d0)], axis=1)
    t = t.transpose(0, 1, 3, 2, 4).reshape(NA_HEADS // 2, 2, n_d0, GRID_W, wh * GRID_W)
    return t.transpose(0, 2, 1, 3, 4).reshape(NA_HEADS // 2, n_d0, 2 * GRID_W, wh * GRID_W)


def _na(proj, rpb, *, batch, seq, ctx_len, q_col, k_col, v_col):
    rows = seq // GRID_W
    wh = min(NA_WIN_H, rows)
    pairs = NA_HEADS // 2
    tb = _na_bias_table(rpb, rows)
    ctx0 = batch * seq // ctx_len
    lat = lambda col: pl.BlockSpec((seq, LANES), lambda b, p: (b, col // LANES + p))
    ctx = lambda col: pl.BlockSpec((ctx_len, LANES), lambda b, p: (ctx0 + b, col // LANES + p))
    return pl.pallas_call(
        functools.partial(_na_body, rows=rows, wh=wh),
        grid=(batch, pairs),
        in_specs=[lat(q_col), lat(k_col), lat(v_col), ctx(k_col), ctx(v_col),
                  pl.BlockSpec((None, NA_WIN_H, 2 * GRID_W, wh * GRID_W), lambda b, p: (p, 0, 0, 0))],
        out_specs=pl.BlockSpec((seq, LANES), lambda b, p: (b, p)),
        out_shape=jax.ShapeDtypeStruct((batch * seq, pairs * LANES), BF16),
        scratch_shapes=[pltpu.VMEM((seq, LANES), BF16), pltpu.VMEM((seq, LANES), BF16),
                        pltpu.VMEM((ctx_len, LANES), BF16), pltpu.VMEM((ctx_len, LANES), BF16)],
        compiler_params=_cparams(("arbitrary", "arbitrary")), name="na_attn",
    )(proj, proj, proj, proj, proj, tb)


def _ctx_attn_body(q_ref, k_ref, v_ref, o_ref):
    kb = k_ref[...].astype(BF16)
    vb = v_ref[...].astype(BF16)
    q = q_ref[...] * (NA_HEAD_DIM ** -0.5)
    lane = lax.broadcasted_iota(jnp.int32, q.shape, 1)
    first = lane < NA_HEAD_DIM
    outs = []
    for hh in range(2):
        qm = jnp.where(first if hh == 0 else jnp.logical_not(first), q, 0.0).astype(BF16)
        s = _dot_nt(qm, kb)
        p = jnp.exp(s - jnp.max(s, axis=-1, keepdims=True))
        outs.append(_dot(p.astype(BF16), vb) / jnp.sum(p, axis=-1, keepdims=True))
    o_ref[...] = jnp.where(first, outs[0], outs[1]).astype(o_ref.dtype)


def _ctx_attn(proj, *, batch, seq, ctx_len, q_col, k_col, v_col):
    pairs = NA_HEADS // 2
    ctx0 = batch * seq // ctx_len
    ctx = lambda col: pl.BlockSpec((ctx_len, LANES), lambda b, p: (ctx0 + b, col // LANES + p))
    return pl.pallas_call(
        _ctx_attn_body, grid=(batch, pairs),
        in_specs=[ctx(q_col), ctx(k_col), ctx(v_col)],
        out_specs=pl.BlockSpec((ctx_len, LANES), lambda b, p: (b, p)),
        out_shape=jax.ShapeDtypeStruct((batch * ctx_len, pairs * LANES), BF16),
        compiler_params=_cparams(("arbitrary", "arbitrary")), name="ctx_attn",
    )(proj, proj, proj)


def _da_two_streams(q_ref, kl_ref, kc_ref, vlb, vcb, g_ref, o_ref, s_ref, lam, lam_init):
    hq = q_ref.shape[0] // 2
    ck = DA_KEY_CHUNK
    chunks = [(kl_ref, vlb, off, ck, off) for off in range(0, kl_ref.shape[0], ck)]
    chunks.append((kc_ref, vcb, 0, kc_ref.shape[0], kl_ref.shape[0]))
    lane = lax.broadcasted_iota(jnp.int32, (hq, LANES), 1)
    lane_tiles = lambda x: [x[:, i * LANES:(i + 1) * LANES] for i in range(x.shape[1] // LANES)]

    def stacked_q(st):
        q = q_ref[st * hq:(st + 1) * hq, :]
        zero = jnp.zeros_like(q)
        return jnp.concatenate([jnp.where(lane < DA_HEAD_DIM, q, zero), jnp.where(lane >= DA_HEAD_DIM, q, zero)], axis=0)

    def scores(st, qq, chunk, m_run):
        k_ref, _, off, n, c0 = chunk
        s = _dot_nt(qq, k_ref[off:off + n, :])
        s_ref[st, :, c0:c0 + n] = s
        return functools.reduce(jnp.maximum, lane_tiles(s), m_run)

    def numerators(st, chunk, m, l_run):
        _, _, _, n, c0 = chunk
        p = jnp.exp2(s_ref[st, :, c0:c0 + n] - m)
        s_ref[st, :, c0:c0 + n] = p
        return functools.reduce(jnp.add, lane_tiles(p), l_run)

    def values(st, chunk, ratio, acc):
        _, v_ref, off, n, c0 = chunk
        a = (s_ref[st, 0:hq, c0:c0 + n] - ratio * s_ref[st, hq:2 * hq, c0:c0 + n]).astype(BF16)
        return acc + _dot(a, v_ref[off:off + n, :])

    def finish(st, acc, den):
        o = acc / den[:hq]
        o = o * lax.rsqrt(jnp.mean(o * o, axis=-1, keepdims=True) + EPS) * g_ref[...]
        o_ref[st * hq:(st + 1) * hq, :] = (o * (1.0 - lam_init)).astype(o_ref.dtype)

    neg = jnp.full((2 * hq, LANES), -jnp.inf, F32)
    zero_l = jnp.zeros((2 * hq, LANES), F32)
    zero_o = jnp.zeros((hq, LANES), F32)
    row_max = lambda m_run: jnp.max(m_run, axis=-1, keepdims=True)
    row_sum = lambda l_run: jnp.sum(l_run, axis=-1, keepdims=True)
    qq0, qq1 = stacked_q(0), stacked_q(1)
    m0 = neg
    for ch in chunks:
        m0 = scores(0, qq0, ch, m0)
    m0 = row_max(m0)
    l0, m1 = zero_l, neg
    for ch in chunks:
        l0 = numerators(0, ch, m0, l0)
        m1 = scores(1, qq1, ch, m1)
    den0, m1 = row_sum(l0), row_max(m1)
    ratio0 = lam * den0[:hq] / den0[hq:]
    acc0, l1 = zero_o, zero_l
    for ch in chunks:
        acc0 = values(0, ch, ratio0, acc0)
        l1 = numerators(1, ch, m1, l1)
    finish(0, acc0, den0)
    den1 = row_sum(l1)
    ratio1 = lam * den1[:hq] / den1[hq:]
    acc1 = zero_o
    for ch in chunks:
        acc1 = values(1, ch, ratio1, acc1)
    finish(1, acc1, den1)


def _da_body(lv_ref, q_ref, *rest, lam_init, has_lat):
    if has_lat:
        kl_ref, vl_ref, rest = rest[0], rest[1], rest[2:]
    kc_ref, vc_ref, g_ref, o_ref = rest[:4]
    scr = rest[4:]
    t = pl.program_id(2)
    if has_lat:
        vlb, vcb, s_ref = scr

        @pl.when(t == 0)
        def _():
            vlb[...] = vl_ref[...].astype(BF16)
            vcb[...] = vc_ref[...].astype(BF16)
    else:
        (vcb,) = scr
        vcb[...] = vc_ref[...].astype(BF16)

    lv = lv_ref[...]
    lam = (jnp.exp(jnp.sum(lv[0:1] * lv[1:2], axis=-1, keepdims=True))
           - jnp.exp(jnp.sum(lv[2:3] * lv[3:4], axis=-1, keepdims=True)) + lam_init)
    if has_lat:
        _da_two_streams(q_ref, kl_ref, kc_ref, vlb, vcb, g_ref, o_ref, s_ref, lam, lam_init)
        return
    q = q_ref[...]
    tq = q.shape[0]
    lane = lax.broadcasted_iota(jnp.int32, q.shape, 1)
    zero = jnp.zeros_like(q)
    qq = jnp.concatenate([jnp.where(lane < DA_HEAD_DIM, q, zero), jnp.where(lane >= DA_HEAD_DIM, q, zero)], axis=0)
    s_c = _dot_nt(qq, kc_ref[...])
    m = jnp.max(s_c, axis=-1, keepdims=True)
    if has_lat:
        s_l = _dot_nt(qq, kl_ref[...])
        m = jnp.maximum(m, jnp.max(s_l, axis=-1, keepdims=True))
        p_l = jnp.exp2(s_l - m)
    p_c = jnp.exp2(s_c - m)
    den = jnp.sum(p_c, axis=-1, keepdims=True)
    if has_lat:
        den = den + jnp.sum(p_l, axis=-1, keepdims=True)
    ratio = lam * den[:tq] / den[tq:]
    o = _dot((p_c[:tq] - ratio * p_c[tq:]).astype(BF16), vcb[...])
    if has_lat:
        o = o + _dot((p_l[:tq] - ratio * p_l[tq:]).astype(BF16), vlb[...])
    o = o / den[:tq]
    o = o * lax.rsqrt(jnp.mean(o * o, axis=-1, keepdims=True) + EPS) * g_ref[...]
    o_ref[...] = (o * (1.0 - lam_init)).astype(o_ref.dtype)


def _da(q_r, k_r, proj, lam_vec, subln_g, lam_init, *, batch, seq, ctx_len, v_col, tq, latent):
    ctx0 = batch * seq // ctx_len
    vcb0 = v_col // LANES
    g2 = subln_g.reshape(1, 2 * DA_HEAD_DIM).astype(F32)
    small = [pl.BlockSpec((4, DA_HEAD_DIM), lambda b, h, t: (0, 0))]
    gspec = pl.BlockSpec((1, LANES), lambda b, h, t: (0, 0))
    kc = pl.BlockSpec((ctx_len, LANES), lambda b, h, t: (ctx0 + b, h))
    vc = pl.BlockSpec((ctx_len, LANES), lambda b, h, t: (ctx0 + b, vcb0 + h))
    if latent:
        nq = seq // tq
        qs = pl.BlockSpec((tq, LANES), lambda b, h, t: (b * nq + t, h))
        kl = pl.BlockSpec((seq, LANES), lambda b, h, t: (b, h))
        vl = pl.BlockSpec((seq, LANES), lambda b, h, t: (b, vcb0 + h))
        in_specs = small + [qs, kl, vl, kc, vc, gspec]
        args = (lam_vec, q_r, k_r, proj, k_r, proj, g2)
        assert seq % DA_KEY_CHUNK == 0 and tq % 2 == 0
        scratch = [pltpu.VMEM((seq, LANES), BF16), pltpu.VMEM((ctx_len, LANES), BF16),
                   pltpu.VMEM((2, tq, seq + ctx_len), F32)]
        n_out = batch * seq
    else:
        nq = ctx_len // tq
        cq0 = batch * seq // tq
        qs = pl.BlockSpec((tq, LANES), lambda b, h, t: (cq0 + b * nq + t, h))
        in_specs = small + [qs, kc, vc, gspec]
        args = (lam_vec, q_r, k_r, proj, g2)
        scratch = [pltpu.VMEM((ctx_len, LANES), BF16)]
        n_out = batch * ctx_len
    return pl.pallas_call(
        functools.partial(_da_body, lam_init=lam_init, has_lat=latent),
        grid=(batch, DA_HEADS, nq), in_specs=in_specs,
        out_specs=pl.BlockSpec((tq, LANES), lambda b, h, t: (b * nq + t, h)),
        out_shape=jax.ShapeDtypeStruct((n_out, DA_HEADS * LANES), BF16),
        scratch_shapes=scratch,
        compiler_params=_cparams(("arbitrary", "arbitrary", "arbitrary")),
        name="diff_attn_lat" if latent else "diff_attn_ctx",
    )(*args)


def _ssd_body(x0_ref, x1_ref, bc_ref, x0p_ref, x1p_ref, bcp_ref, x0n_ref, x1n_ref, bcn_ref, dt_ref,
              cw_ref, cb_ref, prm_ref, dsk_ref, ex_ref, init_ref, y_ref, sout_ref, st_ref, u_ref, *, nc):
    q_len = SSD_CHUNK
    dirn = pl.program_id(1)
    z = pl.program_id(2)
    zz = jnp.where(dirn == 0, z, nc - 1 - z)
    fwd = dirn == 0
    half = SSM_D_INNER // SSM_GROUPS
    heads_per_group = SSM_HEADS // SSM_GROUPS

    @pl.when(z == 0)
    def _():
        st_ref[...] = init_ref[...]

    def conv_piece(main_ref, prev_ref, next_ref, c0):
        width = main_ref.shape[1]
        prev = jnp.where(zz == 0, 0.0, prev_ref[...])
        nxt = jnp.where(zz == nc - 1, 0.0, next_ref[...])
        ext = jnp.concatenate([prev, main_ref[...], nxt], axis=0)
        n_ext = ext.shape[0]
        acc = jnp.zeros((q_len, width), F32) + cb_ref[:, c0:c0 + width]
        for tap in range(SSM_CONV_W):
            sh = (SSM_CONV_W // 2 - tap) % n_ext
            e = ext if sh == 0 else pltpu.roll(ext, sh, 0)
            acc = acc + e[SUBLANES:SUBLANES + q_len] * cw_ref[tap:tap + 1, c0:c0 + width]
        return _silu(acc)

    @pl.when(fwd)
    def _():
        u_ref[zz, :, 0:half] = conv_piece(x0_ref, x0p_ref, x0n_ref, 0)
        u_ref[zz, :, half:2 * half] = conv_piece(x1_ref, x1p_ref, x1n_ref, half)
        u_ref[zz, :, SSM_D_INNER:] = conv_piece(bc_ref, bcp_ref, bcn_ref, SSM_D_INNER)

    xs = (u_ref[zz, :, 0:half], u_ref[zz, :, half:2 * half])
    bcm = u_ref[zz, :, SSM_D_INNER:]

    raw = dt_ref[...]
    raw = jnp.where(fwd, raw, pltpu.roll(raw, LANES - SSM_HEADS, 1))
    xb = raw + prm_ref[0:1, :]
    dtv = jnp.maximum(xb, 0.0) + jnp.log1p(jnp.exp(-jnp.abs(xb)))
    a = dtv * (-jnp.exp(prm_ref[1:2, :]))
    rowi = lax.broadcasted_iota(jnp.int32, (q_len, LANES), 0)
    coli = lax.broadcasted_iota(jnp.int32, (q_len, LANES), 1)
    cum = a
    sft = 1
    while sft < q_len:
        cum = cum + jnp.where(rowi >= sft, pltpu.roll(cum, sft, 0), 0.0)
        sft *= 2
    tot = cum[q_len - 1:q_len, :]
    g = jnp.where(fwd, cum, cum - a)
    e_g = jnp.exp(g)
    e_tg = jnp.exp(tot - g)
    w_state = jnp.where(fwd, e_tg, e_g)
    w_yoff = jnp.where(fwd, e_g, e_tg)
    e_tot = jnp.exp(tot)
    gs = g * jnp.where(fwd, LOG2E, -LOG2E)
    gs_t = gs.T
    tri = jnp.where(fwd, rowi - coli, coli - rowi) >= 0

    stack = jnp.concatenate([dtv, dtv * w_state, w_yoff, jnp.broadcast_to(e_tot, (SUBLANES, LANES))], axis=0)
    s_hi, s_lo = _split2(stack)
    ex = ex_ref[...]
    expd = _dot(s_hi, ex) + _dot(s_lo, ex)
    dt_e, dts_e, wy_e = expd[0:q_len], expd[q_len:2 * q_len], expd[2 * q_len:3 * q_len]
    tot_e = expd[3 * q_len:3 * q_len + 1]

    lane = lax.broadcasted_iota(jnp.int32, (q_len, LANES), 1)
    first = lane < SSM_HEAD_DIM
    for grp in range(SSM_GROUPS):
        x_g = xs[grp]
        csl = slice(grp * half, (grp + 1) * half)
        b_g = bcm[:, grp * SSM_STATE:(grp + 1) * SSM_STATE]
        c_g = bcm[:, (SSM_GROUPS + grp) * SSM_STATE:(SSM_GROUPS + grp + 1) * SSM_STATE].astype(BF16)
        xdt = (x_g * dt_e[:, csl]).astype(BF16)
        xdts = (x_g * dts_e[:, csl]).astype(BF16)
        cb = _dot_nt(c_g, b_g.astype(BF16))
        st_g = st_ref[:, csl]
        y_g = _dot(c_g, st_g.astype(BF16)) * wy_e[:, csl] + dsk_ref[:, csl] * x_g
        for pr in range(heads_per_group // 2):
            outs = []
            for hh in range(2):
                col = grp * heads_per_group + 2 * pr + hh
                seg = gs[:, col:col + 1] - gs_t[col:col + 1, :]
                mat = (cb * jnp.where(tri, jnp.exp2(seg), 0.0)).astype(BF16)
                outs.append(_dot(mat, xdt[:, pr * LANES:(pr + 1) * LANES]))
            lo = grp * half + pr * LANES
            y_ref[:, lo:lo + LANES] = y_g[:, pr * LANES:(pr + 1) * LANES] + jnp.where(first, outs[0], outs[1])
        st_ref[:, csl] = tot_e[:, csl] * st_g + _dot(b_g.T.astype(BF16), xdts)

    @pl.when(z == nc - 1)
    def _():
        sout_ref[...] = st_ref[...]


def _ssd(proj, conv_w, conv_b, dt_bias, a_log, d_skip, init, *, batch, n_tok, row0, x_col, dt_col):
    q_len = SSD_CHUNK
    nc = n_tok // q_len
    half = SSM_D_INNER // SSM_GROUPS
    conv_dim = SSM_D_INNER + 2 * SSM_GROUPS * SSM_STATE
    rb0 = row0 // q_len
    hb = q_len // SUBLANES
    n_halo = proj.shape[0] // SUBLANES
    zz = lambda d, z: jnp.where(d == 0, z, nc - 1 - z)
    rb = lambda b, d, z: rb0 + b * nc + zz(d, z)
    rbx = lambda b, d, z: rb0 + b * nc + jnp.where(d == 0, z, nc - 1)
    main = lambda c: pl.BlockSpec((q_len, half), lambda b, d, z: (rbx(b, d, z), c))
    prev = lambda c: pl.BlockSpec((SUBLANES, half), lambda b, d, z: (jnp.maximum(rbx(b, d, z) * hb - 1, 0), c))
    nxt = lambda c: pl.BlockSpec((SUBLANES, half),
                                 lambda b, d, z: (jnp.minimum(rbx(b, d, z) * hb + hb, n_halo - 1), c))
    cols = [x_col // half + j for j in range(conv_dim // half)]
    const2 = lambda shape: pl.BlockSpec(shape, lambda b, d, z: (0, 0))
    prm = jnp.zeros((2, SUBLANES, LANES), F32)
    prm = prm.at[:, 0, :SSM_HEADS].set(dt_bias.astype(F32)).at[:, 1, :SSM_HEADS].set(a_log.astype(F32))
    dsk = jnp.repeat(d_skip.astype(F32), SSM_HEAD_DIM, axis=-1).reshape(2, 1, SSM_D_INNER)
    ex = (np.arange(LANES)[:, None] == (np.arange(SSM_D_INNER) // SSM_HEAD_DIM)[None, :])
    ex = jnp.asarray(ex, BF16)
    cw = jnp.zeros((SUBLANES, conv_dim), F32).at[:SSM_CONV_W].set(conv_w.astype(F32))
    state_spec = pl.BlockSpec((None, None, SSM_STATE, SSM_D_INNER), lambda b, d, z: (b, d, 0, 0))
    y, s_out = pl.pallas_call(
        functools.partial(_ssd_body, nc=nc),
        grid=(batch, 2, nc),
        in_specs=[main(cols[0]), main(cols[1]), main(cols[2]), prev(cols[0]), prev(cols[1]), prev(cols[2]),
                  nxt(cols[0]), nxt(cols[1]), nxt(cols[2]),
                  pl.BlockSpec((q_len, LANES), lambda b, d, z: (rb(b, d, z), dt_col // LANES)),
                  const2((SUBLANES, conv_dim)), const2((1, conv_dim)),
                  pl.BlockSpec((None, SUBLANES, LANES), lambda b, d, z: (d, 0, 0)),
                  pl.BlockSpec((None, 1, SSM_D_INNER), lambda b, d, z: (d, 0, 0)),
                  const2((LANES, SSM_D_INNER)), state_spec],
        out_specs=[pl.BlockSpec((None, q_len, SSM_D_INNER), lambda b, d, z: (d, b * nc + zz(d, z), 0)),
                   state_spec],
        out_shape=[jax.ShapeDtypeStruct((2, batch * n_tok, SSM_D_INNER), F32),
                   jax.ShapeDtypeStruct((batch, 2, SSM_STATE, SSM_D_INNER), F32)],
        scratch_shapes=[pltpu.VMEM((SSM_STATE, SSM_D_INNER), F32), pltpu.VMEM((nc, q_len, conv_dim), F32)],
        compiler_params=_cparams(("arbitrary", "arbitrary", "arbitrary")), name="ssd_scan",
    )(proj, proj, proj, proj, proj, proj, proj, proj, proj, proj, cw, conv_b.reshape(1, conv_dim).astype(F32),
      prm, dsk, ex, init)
    return y, s_out


def _ssm_norm_body(y_ref, z_ref, g_ref, o_ref):
    yz = (y_ref[0] + y_ref[1]) * _silu(z_ref[...])
    gw = SSM_D_INNER // SSM_GROUPS
    for grp in range(SSM_GROUPS):
        v = yz[:, grp * gw:(grp + 1) * gw]
        v = v * lax.rsqrt(jnp.mean(v * v, axis=-1, keepdims=True) + EPS)
        o_ref[:, grp * gw:(grp + 1) * gw] = (v * g_ref[:, grp * gw:(grp + 1) * gw]).astype(o_ref.dtype)


def _ssm_norm(y, proj, g, *, tm, row0, z_col):
    n = y.shape[1]
    zb0 = row0 // tm
    return pl.pallas_call(
        _ssm_norm_body, grid=(n // tm,),
        in_specs=[pl.BlockSpec((2, tm, SSM_D_INNER), lambda i: (0, i, 0)),
                  pl.BlockSpec((tm, SSM_D_INNER), lambda i: (zb0 + i, z_col // SSM_D_INNER)),
                  pl.BlockSpec((1, SSM_D_INNER), lambda i: (0, 0))],
        out_specs=pl.BlockSpec((tm, SSM_D_INNER), lambda i: (i, 0)),
        out_shape=jax.ShapeDtypeStruct((n, SSM_D_INNER), BF16),
        compiler_params=_cparams(("arbitrary",)), name="ssm_gated_norm",
    )(y, proj, g.reshape(1, SSM_D_INNER).astype(F32))


def _two_source_specs(block, first_blocks, row_block_of):
    first = pl.BlockSpec(block, lambda *g: (jnp.minimum(row_block_of(*g), first_blocks - 1), 0))
    second = pl.BlockSpec(block, lambda *g: (jnp.maximum(row_block_of(*g) - first_blocks, 0), 0))
    return first, second


def _merge_body(*refs, lat_blocks, has_ctx):
    n_src = 6 if has_ctx else 3
    src, (ga_ref, gb_ref, gc_ref, wa_ref, wb_ref, wc_ref, o_ref, wab, wbb, wcb) = refs[:n_src], refs[n_src:]
    i = pl.program_id(1)

    @pl.when(i == 0)
    def _():
        wab[...] = wa_ref[...].astype(BF16)
        wbb[...] = wb_ref[...].astype(BF16)
        wcb[...] = wc_ref[...].astype(BF16)

    if has_ctx:
        oa, ob, oc = (jnp.where(i < lat_blocks, src[2 * n][...], src[2 * n + 1][...]) for n in range(3))
    else:
        oa, ob, oc = (r[...] for r in src)
    acc = ga_ref[...].astype(F32) * _dot(oa, wab[...])
    acc = acc + gb_ref[...].astype(F32) * _dot(ob, wbb[...])
    acc = acc + gc_ref[...].astype(F32) * _dot(oc, wcb[...])
    o_ref[...] = acc.astype(o_ref.dtype)


def _merge(o_lat, o_ctx, gates, w_a, w_b, w_c, layer, *, tm, tn, n_rows):
    d = w_a.shape[2]
    nj = d // tn
    lat_blocks = o_lat[0].shape[0] // tm
    gate = lambda x: pl.BlockSpec((tm, tn), lambda j, i: (i, x * nj + j))
    wsp = lambda kk: pl.BlockSpec((None, kk, tn), lambda j, i: (layer, 0, j))
    widths = (w_a.shape[1], w_b.shape[1], w_c.shape[1])
    src_specs, src = [], []
    for n, kk in enumerate(widths):
        if o_ctx is None:
            src_specs.append(pl.BlockSpec((tm, kk), lambda j, i: (i, 0)))
            src.append(o_lat[n])
        else:
            src_specs += _two_source_specs((tm, kk), lat_blocks, lambda j, i: i)
            src += [o_lat[n], o_ctx[n]]
    return pl.pallas_call(
        functools.partial(_merge_body, lat_blocks=lat_blocks, has_ctx=o_ctx is not None), grid=(nj, n_rows // tm),
        in_specs=src_specs + [gate(0), gate(1), gate(2)] + [wsp(kk) for kk in widths],
        out_specs=pl.BlockSpec((tm, tn), lambda j, i: (i, j)),
        out_shape=jax.ShapeDtypeStruct((n_rows, d), BF16),
        scratch_shapes=[pltpu.VMEM((kk, tn), BF16) for kk in widths],
        compiler_params=_cparams(("arbitrary", "arbitrary")), name="branch_merge",
    )(*src, gates, gates, gates, w_a, w_b, w_c)


def _route(top_idx, blk):
    n = top_idx.shape[0]
    e_flat = top_idx.reshape(-1)
    onehot = (e_flat[:, None] == jnp.arange(N_EXPERTS, dtype=jnp.int32)[None, :]).astype(jnp.int32)
    counts = jnp.sum(onehot, axis=0)
    rank = jnp.sum((jnp.cumsum(onehot, axis=0) - onehot) * onehot, axis=1)
    padded = (counts + blk - 1) // blk * blk
    pad_end = jnp.cumsum(padded)
    pad_start = pad_end - padded
    dest = jnp.sum(onehot * pad_start[None, :], axis=1) + rank
    n_blocks = -(-(n * TOP_K) // blk) + N_EXPERTS
    cap = n_blocks * blk
    tok_flat = jnp.repeat(jnp.arange(n, dtype=jnp.int32), TOP_K)
    slot_tok = jnp.zeros((cap,), jnp.int32).at[dest].set(tok_flat)
    blk_start = jnp.arange(n_blocks, dtype=jnp.int32) * blk
    block_expert = jnp.clip(jnp.sum((pad_end[None, :] <= blk_start[:, None]).astype(jnp.int32), axis=1),
                            0, N_EXPERTS - 1)
    n_used = (pad_end[-1] // blk).astype(jnp.int32).reshape(1)
    return dest.astype(jnp.int32), slot_tok, block_expert, n_used


GATHER_UNROLL = 8


def _row_copy(src_hbm, row, buf, slot, sem):
    return pltpu.make_async_copy(src_hbm.at[pl.ds(row, 1), :], buf.at[pl.ds(slot, 1), :], sem)


def _gather_body(nv_ref, idx_ref, src_hbm, o_ref, buf, sem, *, rows):
    @pl.when(pl.program_id(0) >= nv_ref[0])
    def _():
        o_ref[...] = jnp.zeros_like(o_ref)

    @pl.when(pl.program_id(0) < nv_ref[0])
    def _():
        def issue(g, c):
            for u in range(GATHER_UNROLL):
                r = g * GATHER_UNROLL + u
                _row_copy(src_hbm, idx_ref[0, r], buf, r, sem).start(priority=u % 2)
            return c

        def drain(g, c):
            for u in range(GATHER_UNROLL):
                r = g * GATHER_UNROLL + u
                _row_copy(src_hbm, idx_ref[0, r], buf, r, sem).wait()
            return c

        lax.fori_loop(0, rows // GATHER_UNROLL, issue, 0)
        lax.fori_loop(0, rows // GATHER_UNROLL, drain, 0)
        o_ref[...] = buf[...].astype(o_ref.dtype)


def _gather_rows(src, idx, n_used, *, rows, out_dtype):
    n, d = src.shape
    nb = idx.shape[0] // rows
    assert rows % GATHER_UNROLL == 0
    blk = lambda i, nv: jnp.minimum(i, nv[0] - 1)
    grid_spec = pltpu.PrefetchScalarGridSpec(
        num_scalar_prefetch=1, grid=(nb,),
        in_specs=[pl.BlockSpec((None, 1, rows), lambda i, nv: (blk(i, nv), 0, 0), memory_space=pltpu.SMEM),
                  pl.BlockSpec(memory_space=pl.ANY)],
        out_specs=pl.BlockSpec((rows, d), lambda i, nv: (i, 0)),
        scratch_shapes=[pltpu.VMEM((rows, d), src.dtype), pltpu.SemaphoreType.DMA])
    return pl.pallas_call(
        functools.partial(_gather_body, rows=rows), grid_spec=grid_spec,
        out_shape=jax.ShapeDtypeStruct((nb * rows, d), out_dtype),
        compiler_params=_cparams(("arbitrary",)), name="moe_dispatch",
    )(n_used, idx.reshape(nb, 1, rows), src)


def _combine_body(dest_ref, h_ref, rt_ref, gate_ref, g_ref, y_hbm, o_ref, buf, sem, *, rows, norm):
    def issue(g, c):
        for u in range(GATHER_UNROLL):
            r = g * GATHER_UNROLL + u
            for k in range(TOP_K):
                _row_copy(y_hbm, dest_ref[0, r * TOP_K + k], buf.at[k], r, sem).start(priority=k)
        return c

    def drain(g, c):
        for u in range(GATHER_UNROLL):
            r = g * GATHER_UNROLL + u
            for k in range(TOP_K):
                _row_copy(y_hbm, dest_ref[0, r * TOP_K + k], buf.at[k], r, sem).wait()
        return c

    lax.fori_loop(0, rows // GATHER_UNROLL, issue, 0)
    lax.fori_loop(0, rows // GATHER_UNROLL, drain, 0)
    rt = rt_ref[...]
    f = buf[0] * rt[:, TOP_K:TOP_K + 1] + buf[1] * rt[:, TOP_K + 1:TOP_K + 2]
    x = h_ref[...] + gate_ref[...] * f
    if norm:
        x = x * lax.rsqrt(jnp.mean(x * x, axis=-1, keepdims=True) + EPS) * g_ref[...]
    o_ref[...] = x


def _combine(h, y_sorted, dest, rt, mod, mod_idx_fn, part, g, *, rows, n_rows, norm):
    d = h.shape[1]
    nb = n_rows // rows
    row = pl.BlockSpec((rows, d), lambda i: (i, 0))
    return pl.pallas_call(
        functools.partial(_combine_body, rows=rows, norm=norm), grid=(nb,),
        in_specs=[pl.BlockSpec((None, 1, rows * TOP_K), lambda i: (i, 0, 0), memory_space=pltpu.SMEM),
                  row, pl.BlockSpec((rows, LANES), lambda i: (i, 0)),
                  pl.BlockSpec((None, 1, d), lambda i: (mod_idx_fn(i), 0, part)),
                  pl.BlockSpec((1, d), lambda i: (0, 0)), pl.BlockSpec(memory_space=pl.ANY)],
        out_specs=row, out_shape=jax.ShapeDtypeStruct((n_rows, d), F32),
        scratch_shapes=[pltpu.VMEM((TOP_K, rows, d), y_sorted.dtype), pltpu.SemaphoreType.DMA],
        compiler_params=_cparams(("arbitrary",)), name="moe_combine",
    )(dest.reshape(nb, 1, rows * TOP_K), h, rt, mod, g.reshape(1, d), y_sorted)


def kernel(x, c, ctx, c_ctx, ada_w, ada_b, norm_mix_g, norm_ffn_g, w_in, na_rpb, da_lambda, da_subln_g,
           ssm_conv_w, ssm_conv_b, ssm_dt_bias, ssm_a_log, ssm_d, ssm_norm_g, w_branch_a, w_branch_b,
           w_branch_c, w_out, ffn_w1, ffn_w3, ffn_w2, moe_router_w, moe_router_b, moe_w1, moe_w3, moe_w2,
           final_norm_g):
    batch, seq, d = x.shape
    ctx_len = ctx.shape[1]
    depth = ada_w.shape[0]
    n_lat, n_ctx = batch * seq, batch * ctx_len
    n_all = n_lat + n_ctx
    tm = next(t for t in (1024, 512, 256) if n_ctx % t == 0 and seq % t == 0)
    n_mod = 16
    assert batch + 1 <= n_mod
    lat_blocks = n_lat // tm
    per_seq = seq // tm
    te = next(t for t in (512, 256) if n_ctx % t == 0 and seq % t == 0)
    mod_idx_fn = lambda i: jnp.where(i < n_lat // te, i // (seq // te), batch)
    mod_idx_all = jnp.where(jnp.arange(n_all // tm) < lat_blocks, jnp.arange(n_all // tm) // per_seq, batch)
    layer_blocks = lambda nb, idx: jnp.full((nb,), idx, jnp.int32)

    na_w, da_w = NA_HEADS * NA_HEAD_DIM, DA_HEADS * 2 * DA_HEAD_DIM
    conv_dim = SSM_D_INNER + 2 * SSM_GROUPS * SSM_STATE
    col_qa, col_ka, col_va = 0, na_w, 2 * na_w
    col_qb, col_kb, col_vb = 3 * na_w, 3 * na_w + da_w, 3 * na_w + 2 * da_w
    col_z = 3 * na_w + 3 * da_w
    col_x = col_z + SSM_D_INNER
    col_dt = col_x + conv_dim
    col_gate = col_dt + 2 * SSM_HEADS
    n_main = 6144
    assert col_dt + LANES <= n_main and n_main % 512 == 0 and col_dt % LANES == 0

    h, h_ctx = x.reshape(n_lat, d), ctx.reshape(n_ctx, d)
    cond =jnp.zeros((n_mod, d), F32).at[:batch].set(jax.nn.silu(c)).at[batch].set(jax.nn.silu(c_ctx))
    cond = cond.astype(BF16)
    rope_tabs = _rope_tables(seq, tm)
    zero_state = jnp.zeros((batch, 2, SSM_STATE, SSM_D_INNER), F32)
    w_in_t = jnp.swapaxes(w_in, 1, 2)

    out = None
    for li in range(depth):
        last = li == depth - 1
        lam_init = 0.8 - 0.6 * math.exp(-0.3 * li)
        n_out = n_lat if last else n_all
        mod = _gmm(cond, [ada_w], layer_blocks(1, li), tm=n_mod, tn=512, epi="bias",
                   bias=ada_b[li].reshape(1, 6 * d)).reshape(n_mod, 1, 6 * d)
        u = _norm(h, norm_mix_g[li], tm=te, n_rows=n_all, mod=mod, mod_idx_fn=mod_idx_fn, parts=(0, 1), x2=h_ctx)
        proj = _gmm(u, [w_in_t], layer_blocks(n_all // tm, li), tm=tm, tn=1024, n_cols=n_main, w_rows=0)
        gates = _gmm(u, [w_in_t], layer_blocks(n_out // tm, li), tm=tm, tn=1024, n_rows=n_out, epi="sigmoid",
                     out_dtype=BF16, n_cols=3 * d, w_rows=col_gate)

        o_a = _na(proj, na_rpb[li], batch=batch, seq=seq, ctx_len=ctx_len, q_col=col_qa, k_col=col_ka, v_col=col_va)
        q_r, k_r = _rope(proj, rope_tabs, tm=tm, n_rows=n_all, n_lat_rows=n_lat, seq=seq,
                         q_col=col_qb, k_col=col_kb, width=da_w)
        o_b = _da(q_r, k_r, proj, da_lambda[li].astype(F32), da_subln_g[li], lam_init, batch=batch, seq=seq,
                  ctx_len=ctx_len, v_col=col_vb, tq=256, latent=True)
        ssd_args = (ssm_conv_w[li], ssm_conv_b[li], ssm_dt_bias[li], ssm_a_log[li], ssm_d[li])
        y_ctx, s_ctx = _ssd(proj, *ssd_args, zero_state, batch=batch, n_tok=ctx_len, row0=n_lat,
                            x_col=col_x, dt_col=col_dt)
        y_lat, _ = _ssd(proj, *ssd_args, s_ctx, batch=batch, n_tok=seq, row0=0, x_col=col_x, dt_col=col_dt)
        o_c = _ssm_norm(y_lat, proj, ssm_norm_g[li], tm=te, row0=0, z_col=col_z)
        if not last:
            o_a_c = _ctx_attn(proj, batch=batch, seq=seq, ctx_len=ctx_len, q_col=col_qa, k_col=col_ka, v_col=col_va)
            o_b_c = _da(q_r, k_r, proj, da_lambda[li].astype(F32), da_subln_g[li], lam_init, batch=batch, seq=seq,
                        ctx_len=ctx_len, v_col=col_vb, tq=ctx_len, latent=False)
            o_c_c = _ssm_norm(y_ctx, proj, ssm_norm_g[li], tm=te, row0=n_lat, z_col=col_z)
        merged = _merge((o_a, o_b, o_c), None if last else (o_a_c, o_b_c, o_c_c), gates,
                        w_branch_a, w_branch_b, w_branch_c, li, tm=te, tn=1024, n_rows=n_out)
        h = _gmm(merged, [w_out], layer_blocks(n_out // tm, li), tm=tm, tn=1024, epi="resid", res=h, res2=h_ctx,
                 mod=mod, mod_idx=mod_idx_all[:n_out // tm], mod_part=2, n_rows=n_out)
        h_ctx = None

        j = li // 2
        if li % 2 == 0:
            tokens = _norm(h, norm_ffn_g[li], tm=te, n_rows=n_out, mod=mod, mod_idx_fn=mod_idx_fn, parts=(3, 4))
            hid = _gmm(tokens, [ffn_w1, ffn_w3], layer_blocks(n_out // tm, j), tm=tm, tn=512,
                       epi="swiglu", out_dtype=BF16)
            h = _gmm(hid, [ffn_w2], layer_blocks(n_out // tm, j), tm=tm, tn=512, epi="resid", res=h,
                     mod=mod, mod_idx=mod_idx_all[:n_out // tm], mod_part=5, n_rows=n_out, w_buffers=1)
            if last:
                out = _norm(h, final_norm_g, tm=te, n_rows=n_lat, out_dtype=F32)
        else:
            rw = jnp.zeros((d, LANES), F32).at[:, :N_EXPERTS].set(moe_router_w[j].astype(F32))
            rb = jnp.zeros((1, LANES), F32).at[0, :N_EXPERTS].set(moe_router_b[j].astype(F32))
            tokens, rt = _norm(h, norm_ffn_g[li], tm=te, n_rows=n_out, mod=mod, mod_idx_fn=mod_idx_fn,
                               parts=(3, 4), router=(rw, rb), out_dtype=F32)
            dest, slot_tok, block_expert, n_used = _route(rt[:, :TOP_K].astype(jnp.int32), MOE_ROWS)
            block_expert = block_expert + j * N_EXPERTS
            stack = lambda w: w.reshape((-1,) + w.shape[2:])
            x_sorted = _gather_rows(tokens, slot_tok, n_used, rows=MOE_ROWS, out_dtype=BF16)
            hid = _gmm(x_sorted, [stack(moe_w1), stack(moe_w3)], block_expert, tm=MOE_ROWS, tn=512, epi="swiglu",
                       out_dtype=BF16, n_used=n_used)
            y_sorted = _gmm(hid, [stack(moe_w2)], block_expert, tm=MOE_ROWS, tn=512, n_used=n_used, w_buffers=1)
            h = _combine(h, y_sorted, dest, rt, mod, mod_idx_fn, 5, final_norm_g, rows=te, n_rows=n_out, norm=last)
            if last:
                out = h
    return out.reshape(batch, seq, d)
```

```python
import functools
import math

import numpy as np
import jax
import jax.numpy as jnp
from jax import lax
from jax.experimental import pallas as pl
from jax.experimental.pallas import tpu as pltpu

F32 = jnp.float32
BF16 = jnp.bfloat16

GRID_W = 64
NA_HEADS, NA_HEAD_DIM, NA_WIN_H, NA_WIN_W = 8, 64, 8, 16
DA_HEADS, DA_HEAD_DIM = 4, 64
ROPE_BASE = 10000.0
SSM_D_INNER, SSM_HEAD_DIM, SSM_GROUPS, SSM_STATE, SSM_CONV_W = 1024, 64, 2, 128, 5
SSM_HEADS = SSM_D_INNER // SSM_HEAD_DIM
N_EXPERTS, TOP_K = 8, 2
EPS = 1e-6
NEG_INF = -1e30

LANES = 128
SUBLANES = 8
VMEM_LIMIT_BYTES = 56 * 1024 * 1024
SSD_CHUNK = 128
MOE_ROWS = 1024
NA_ROW_UNROLL = 4
DA_KEY_CHUNK = 512
LOG2E = math.log2(math.e)


def _cparams(sem):
    return pltpu.CompilerParams(dimension_semantics=sem, vmem_limit_bytes=VMEM_LIMIT_BYTES)


def _dot(a, b):
    return jnp.dot(a, b, preferred_element_type=F32)


def _dot_nt(a, b):
    return lax.dot_general(a, b, (((1,), (1,)), ((), ())), preferred_element_type=F32)


def _silu(x):
    return x * jax.nn.sigmoid(x)


def _gmm_body(bexp_ref, neww_ref, mod_ref, nv_ref, x_ref, *rest, nk, n_w, epi, res_first_blocks, w_transposed):
    del bexp_ref, mod_ref
    w_refs, rest = rest[:n_w], rest[n_w:]
    if epi == "bias":
        bias_ref, rest = rest[0], rest[1:]
    elif epi == "resid":
        res_ref, rest = rest[0], rest[1:]
        if res_first_blocks is not None:
            res2_ref, rest = rest[0], rest[1:]
        gate_ref, rest = rest[0], rest[1:]
    o_ref, rest = rest[0], rest[1:]
    wbf_refs, acc_refs = rest[:n_w], rest[n_w:]
    i = pl.program_id(1)
    k = pl.program_id(2)

    def finalize(vals):
        if epi == "swiglu":
            o_ref[...] = (_silu(vals[0]) * vals[1]).astype(o_ref.dtype)
        elif epi == "bias":
            o_ref[...] = (vals[0] + bias_ref[...]).astype(o_ref.dtype)
        elif epi == "resid":
            res = res_ref[...]
            if res_first_blocks is not None:
                res = jnp.where(i < res_first_blocks, res, res2_ref[...])
            o_ref[...] = (res + gate_ref[...] * vals[0]).astype(o_ref.dtype)
        elif epi == "sigmoid":
            o_ref[...] = jax.nn.sigmoid(vals[0]).astype(o_ref.dtype)
        else:
            o_ref[...] = vals[0].astype(o_ref.dtype)

    @pl.when(jnp.logical_and(i >= nv_ref[0], k == nk - 1))
    def _():
        o_ref[...] = jnp.zeros_like(o_ref)

    @pl.when(i < nv_ref[0])
    def _():
        @pl.when(neww_ref[i] == 1)
        def _():
            for w_ref, wbf in zip(w_refs, wbf_refs):
                w = w_ref[...]
                wbf[k] = (w.T if w_transposed else w).astype(BF16)

        x = x_ref[...]
        parts = [_dot(x, wbf[k]) for wbf in wbf_refs]
        if nk == 1:
            finalize(parts)
        else:
            @pl.when(k == 0)
            def _():
                for acc, p in zip(acc_refs, parts):
                    acc[...] = p

            @pl.when(k > 0)
            def _():
                for acc, p in zip(acc_refs, parts):
                    acc[...] += p

            @pl.when(k == nk - 1)
            def _():
                finalize([acc[...] for acc in acc_refs])


def _gmm(x, ws, bexp, *, tm, tn, nk=1, n_cols=None, epi="plain", out_dtype=F32, bias=None, res=None,
         res2=None, mod=None, mod_idx=None, mod_part=0, n_rows=None, n_used=None, w_buffers=2, w_rows=None):
    m_rows = x.shape[0] if n_rows is None else n_rows
    k_dim = x.shape[1]
    n_dim = ws[0].shape[2] if n_cols is None else n_cols
    n_w = len(ws)
    tk = k_dim // nk
    nb = m_rows // tm
    res_first_blocks = None
    assert m_rows % tm == 0 and n_dim % tn == 0 and k_dim % nk == 0
    bexp = bexp.astype(jnp.int32)
    neww = jnp.concatenate([jnp.ones((1,), jnp.int32), (bexp[1:] != bexp[:-1]).astype(jnp.int32)])
    if mod_idx is None:
        mod_idx = jnp.zeros((nb,), jnp.int32)
    if n_used is None:
        n_used = jnp.full((1,), nb, jnp.int32)

    rb = lambda i, nv: jnp.minimum(i, nv[0] - 1)
    x_spec = pl.BlockSpec((tm, tk), lambda j, i, k, be, nw, md, nv: (rb(i, nv), k))
    kb_of = lambda i, k, nw, nv: jnp.where(nw[rb(i, nv)] == 1, k, nk - 1)
    if w_rows is None:
        w_block = (None, tk, tn)
        w_map = lambda j, i, k, be, nw, md, nv: (be[rb(i, nv)], kb_of(i, k, nw, nv), j)
    else:
        n_total = ws[0].shape[1]
        assert n_total % SUBLANES == 0 and w_rows % SUBLANES == 0 and tn % SUBLANES == 0
        ws = [w.reshape(-1, k_dim) for w in ws]
        w_block = (pl.Element(tn), pl.Element(tk))
        w_map = lambda j, i, k, be, nw, md, nv: (
            pl.multiple_of(be[rb(i, nv)] * n_total + w_rows + j * tn, SUBLANES),
            pl.multiple_of(kb_of(i, k, nw, nv) * tk, LANES))
    if w_buffers == 2:
        w_spec = pl.BlockSpec(w_block, w_map)
    else:
        w_spec = pl.BlockSpec(w_block, w_map, pipeline_mode=pl.Buffered(w_buffers))
    o_spec = pl.BlockSpec((tm, tn), lambda j, i, k, be, nw, md, nv: (i, j))
    in_specs = [x_spec] + [w_spec] * n_w
    args = [x] + list(ws)
    if epi == "bias":
        in_specs.append(pl.BlockSpec((1, tn), lambda j, i, k, be, nw, md, nv: (0, j)))
        args.append(bias)
    elif epi == "resid":
        part_off = mod_part * (n_dim // tn)
        if res2 is None:
            in_specs.append(o_spec)
            args.append(res)
        else:
            res_first_blocks = res.shape[0] // tm
            in_specs.append(pl.BlockSpec(
                (tm, tn), lambda j, i, k, be, nw, md, nv: (jnp.minimum(i, res_first_blocks - 1), j)))
            in_specs.append(pl.BlockSpec(
                (tm, tn), lambda j, i, k, be, nw, md, nv: (jnp.maximum(i - res_first_blocks, 0), j)))
            args += [res, res2]
        in_specs.append(pl.BlockSpec((None, 1, tn),
                                     lambda j, i, k, be, nw, md, nv: (md[rb(i, nv)], 0, part_off + j)))
        args.append(mod)
    scratch = [pltpu.VMEM((nk, tk, tn), BF16) for _ in range(n_w)]
    if nk > 1:
        scratch += [pltpu.VMEM((tm, tn), F32) for _ in range(n_w)]
    grid_spec = pltpu.PrefetchScalarGridSpec(
        num_scalar_prefetch=4, grid=(n_dim // tn, nb, nk), in_specs=in_specs, out_specs=o_spec,
        scratch_shapes=scratch)
    return pl.pallas_call(
        functools.partial(_gmm_body, nk=nk, n_w=n_w, epi=epi, res_first_blocks=res_first_blocks,
                          w_transposed=w_rows is not None),
        grid_spec=grid_spec,
        out_shape=jax.ShapeDtypeStruct((m_rows, n_dim), out_dtype),
        compiler_params=_cparams(("arbitrary", "arbitrary", "arbitrary")),
        name="gmm_" + epi,
    )(bexp, neww, mod_idx.astype(jnp.int32), n_used.astype(jnp.int32), *args)


def _split2(v):
    hi = v.astype(BF16)
    lo = (v - hi.astype(F32)).astype(BF16)
    return hi, lo


def _norm_body(x_ref, *rest, modulate, router, first_blocks):
    if first_blocks is not None:
        x2_ref, rest = rest[0], rest[1:]
    g_ref, rest = rest[0], rest[1:]
    if modulate:
        shift_ref, scale_ref, rest = rest[0], rest[1], rest[2:]
    if router:
        rw_ref, rb_ref, rest = rest[0], rest[1], rest[2:]
    o_ref = rest[0]
    x = x_ref[...]
    if first_blocks is not None:
        x = jnp.where(pl.program_id(0) < first_blocks, x, x2_ref[...])
    y = x * lax.rsqrt(jnp.mean(x * x, axis=-1, keepdims=True) + EPS) * g_ref[...]
    if modulate:
        y = y * (1.0 + scale_ref[...]) + shift_ref[...]
    o_ref[...] = y.astype(o_ref.dtype)
    if router:
        rt_ref = rest[1]
        y_hi, y_lo = _split2(y)
        w_hi, w_lo = _split2(rw_ref[...])
        logits = _dot(y_hi, w_hi) + _dot(y_lo, w_hi) + _dot(y_hi, w_lo) + rb_ref[...]
        lane = lax.broadcasted_iota(jnp.int32, logits.shape, 1).astype(F32)
        lg = jnp.where(lane < N_EXPERTS, logits, -jnp.inf)
        m1 = jnp.max(lg, axis=-1, keepdims=True)
        i1 = jnp.min(jnp.where(lg == m1, lane, float(LANES)), axis=-1, keepdims=True)
        lg2 = jnp.where(lane == i1, -jnp.inf, lg)
        m2 = jnp.max(lg2, axis=-1, keepdims=True)
        i2 = jnp.min(jnp.where(lg2 == m2, lane, float(LANES)), axis=-1, keepdims=True)
        e2 = jnp.exp(m2 - m1)
        g1 = 1.0 / (1.0 + e2)
        rt = jnp.where(lane == 0, i1, jnp.where(lane == 1, i2, jnp.where(lane == 2, g1, jnp.where(lane == 3, e2 * g1, 0.0))))
        rt_ref[...] = rt


def _norm(x, g, *, tm, n_rows, mod=None, mod_idx_fn=None, parts=(0, 1), out_dtype=BF16, router=None, x2=None):
    d = x.shape[1]
    nb = n_rows // tm
    row = pl.BlockSpec((tm, d), lambda i: (i, 0))
    first_blocks = None if x2 is None else x.shape[0] // tm
    if x2 is None:
        in_specs, args = [row], [x]
    else:
        in_specs, args = list(_two_source_specs((tm, d), first_blocks, lambda i: i)), [x, x2]
    in_specs.append(pl.BlockSpec((1, d), lambda i: (0, 0)))
    args.append(g.reshape(1, d))
    modulate = mod is not None
    if modulate:
        for p in parts:
            in_specs.append(pl.BlockSpec((None, 1, d), lambda i, p=p: (mod_idx_fn(i), 0, p)))
            args.append(mod)
    out_shape = [jax.ShapeDtypeStruct((n_rows, d), out_dtype)]
    out_specs = [row]
    if router is not None:
        rw, rb = router
        in_specs += [pl.BlockSpec((d, LANES), lambda i: (0, 0)), pl.BlockSpec((1, LANES), lambda i: (0, 0))]
        args += [rw, rb]
        out_shape.append(jax.ShapeDtypeStruct((n_rows, LANES), F32))
        out_specs.append(pl.BlockSpec((tm, LANES), lambda i: (i, 0)))
    res = pl.pallas_call(
        functools.partial(_norm_body, modulate=modulate, router=router is not None, first_blocks=first_blocks),
        grid=(nb,), in_specs=in_specs, out_specs=out_specs, out_shape=out_shape,
        compiler_params=_cparams(("arbitrary",)), name="rmsnorm",
    )(*args)
    return res if router is not None else res[0]


def _rope_body(q_ref, k_ref, c_ref, sm_ref, sp_ref, qo_ref, ko_ref, *, q_scale):
    c, sm, sp = c_ref[...], sm_ref[...], sp_ref[...]
    width = q_ref.shape[1]
    for src, dst, scale in ((q_ref, qo_ref, q_scale), (k_ref, ko_ref, 1.0)):
        for g in range(width // LANES):
            sl = slice(g * LANES, (g + 1) * LANES)
            x = src[:, sl]
            y = x * c + pltpu.roll(x, LANES - 16, 1) * sm + pltpu.roll(x, 16, 1) * sp
            dst[:, sl] = (y * scale).astype(dst.dtype)


def _rope_tables(seq, n_id_rows):
    t = jnp.arange(seq)
    pos = jnp.stack([t // GRID_W, t % GRID_W], axis=-1).astype(F32)
    n_freq = DA_HEAD_DIM // 4
    inv_freq = ROPE_BASE ** (-jnp.arange(n_freq, dtype=F32) / n_freq)
    ang = pos[:, :, None] * inv_freq
    d = np.arange(LANES) % DA_HEAD_DIM
    kind, which, f = d // 32, (d % 32) // 16, d % 16
    a = ang[:, kind, f]
    cos, sin = jnp.cos(a), jnp.sin(a)
    sm = jnp.where(which == 0, -sin, 0.0)
    sp = jnp.where(which == 1, sin, 0.0)
    pad = lambda v, fill: jnp.concatenate([v, jnp.full((n_id_rows, LANES), fill, F32)], axis=0)
    return pad(cos, 1.0), pad(sm, 0.0), pad(sp, 0.0)


def _rope(proj, tabs, *, tm, n_rows, n_lat_rows, seq, q_col, k_col, width):
    lat_blocks = n_lat_rows // tm
    per_seq = seq // tm
    tab_idx = lambda i: (jnp.where(i < lat_blocks, i % per_seq, per_seq), 0)
    out = jax.ShapeDtypeStruct((n_rows, width), BF16)
    return pl.pallas_call(
        functools.partial(_rope_body, q_scale=DA_HEAD_DIM ** -0.5 * LOG2E),
        grid=(n_rows // tm,),
        in_specs=[pl.BlockSpec((tm, width), lambda i: (i, q_col // width)),
                  pl.BlockSpec((tm, width), lambda i: (i, k_col // width)),
                  pl.BlockSpec((tm, LANES), tab_idx), pl.BlockSpec((tm, LANES), tab_idx),
                  pl.BlockSpec((tm, LANES), tab_idx)],
        out_specs=[pl.BlockSpec((tm, width), lambda i: (i, 0))] * 2,
        out_shape=[out, out], compiler_params=_cparams(("arbitrary",)), name="rope",
    )(proj, proj, *tabs)


def _na_body(q_ref, k_ref, v_ref, kc_ref, vc_ref, tb_ref, o_ref, kb, vb, kcb, vcb, *, rows, wh):
    kb[...] = k_ref[...].astype(BF16)
    vb[...] = v_ref[...].astype(BF16)
    kcb[...] = kc_ref[...].astype(BF16)
    vcb[...] = vc_ref[...].astype(BF16)
    lane = lax.broadcasted_iota(jnp.int32, (GRID_W, LANES), 1)
    first = lane < NA_HEAD_DIM
    scale = NA_HEAD_DIM ** -0.5

    def body(grp, carry):
        rr = [grp * NA_ROW_UNROLL + j for j in range(NA_ROW_UNROLL)]
        rs = [jnp.clip(r - wh // 2, 0, rows - wh) for r in rr]
        row_of = lambda r: pl.ds(pl.multiple_of(r * GRID_W, GRID_W), GRID_W)
        win_of = lambda r0: pl.ds(pl.multiple_of(r0 * GRID_W, GRID_W), wh * GRID_W)
        qq = []
        for r in rr:
            q = q_ref[row_of(r), :] * scale
            qq.append(jnp.concatenate([jnp.where(first, q, 0.0), jnp.where(first, 0.0, q)], axis=0).astype(BF16))
        s_l = [_dot_nt(x, kb[win_of(r0), :]) + tb_ref[r0 - r + (NA_WIN_H - 1)] for x, r, r0 in zip(qq, rr, rs)]
        s_c = [_dot_nt(x, kcb[...]) for x in qq]
        m = [jnp.maximum(jnp.max(a, axis=-1, keepdims=True), jnp.max(b, axis=-1, keepdims=True))
             for a, b in zip(s_l, s_c)]
        p_l = [jnp.exp(a - mm) for a, mm in zip(s_l, m)]
        p_c = [jnp.exp(b - mm) for b, mm in zip(s_c, m)]
        den = [jnp.sum(a, axis=-1, keepdims=True) + jnp.sum(b, axis=-1, keepdims=True) for a, b in zip(p_l, p_c)]
        o = [_dot(a.astype(BF16), vb[win_of(r0), :]) + _dot(b.astype(BF16), vcb[...])
             for a, b, r0 in zip(p_l, p_c, rs)]
        for r, oo, dd in zip(rr, o, den):
            oo = oo / dd
            o_ref[row_of(r), :] = jnp.where(first, oo[:GRID_W], oo[GRID_W:]).astype(o_ref.dtype)
        return carry

    assert rows % NA_ROW_UNROLL == 0
    lax.fori_loop(0, rows // NA_ROW_UNROLL, body, 0, unroll=2)


def _na_bias_table(rpb, rows):
    wh = min(NA_WIN_H, rows)
    qc = np.arange(GRID_W)[:, None]
    kc = np.arange(GRID_W)[None, :]
    ws = np.clip(qc - NA_WIN_W // 2, 0, GRID_W - NA_WIN_W)
    ok = (kc >= ws) & (kc < ws + NA_WIN_W)
    n_drow, n_dcol = 2 * NA_WIN_H - 1, 2 * NA_WIN_W - 1
    off = GRID_W - NA_WIN_W
    line = jnp.full((NA_HEADS, n_drow, 2 * GRID_W), NEG_INF, F32).at[:, :, off:off + n_dcol].set(rpb.astype(F32))
    span = 2 * GRID_W - 1
    t = jnp.tile(line, (1, 1, GRID_W))[:, :, :GRID_W * span].reshape(NA_HEADS, n_drow, GRID_W, span)
    t = jnp.where(ok, t[:, :, :, GRID_W - 1:], NEG_INF)
    n_d0 = NA_WIN_H
    t = jnp.stack([t[:, d0:d0 + wh] for d0 in range(n_d0)], axis=1)
    t = t.transpose(0, 1, 3, 2, 4).reshape(NA_HEADS // 2, 2, n_d0, GRID_W, wh * GRID_W)
    return t.transpose(0, 2, 1, 3, 4).reshape(NA_HEADS // 2, n_d0, 2 * GRID_W, wh * GRID_W)


def _na(proj, rpb, *, batch, seq, ctx_len, q_col, k_col, v_col):
    rows = seq // GRID_W
    wh = min(NA_WIN_H, rows)
    pairs = NA_HEADS // 2
    tb = _na_bias_table(rpb, rows)
    ctx0 = batch * seq // ctx_len
    lat = lambda col: pl.BlockSpec((seq, LANES), lambda b, p: (b, col // LANES + p))
    ctx = lambda col: pl.BlockSpec((ctx_len, LANES), lambda b, p: (ctx0 + b, col // LANES + p))
    return pl.pallas_call(
        functools.partial(_na_body, rows=rows, wh=wh),
        grid=(batch, pairs),
        in_specs=[lat(q_col), lat(k_col), lat(v_col), ctx(k_col), ctx(v_col),
                  pl.BlockSpec((None, NA_WIN_H, 2 * GRID_W, wh * GRID_W), lambda b, p: (p, 0, 0, 0))],
        out_specs=pl.BlockSpec((seq, LANES), lambda b, p: (b, p)),
        out_shape=jax.ShapeDtypeStruct((batch * seq, pairs * LANES), BF16),
        scratch_shapes=[pltpu.VMEM((seq, LANES), BF16), pltpu.VMEM((seq, LANES), BF16),
                        pltpu.VMEM((ctx_len, LANES), BF16), pltpu.VMEM((ctx_len, LANES), BF16)],
        compiler_params=_cparams(("arbitrary", "arbitrary")), name="na_attn",
    )(proj, proj, proj, proj, proj, tb)


def _ctx_attn_body(q_ref, k_ref, v_ref, o_ref):
    kb = k_ref[...].astype(BF16)
    vb = v_ref[...].astype(BF16)
    q = q_ref[...] * (NA_HEAD_DIM ** -0.5)
    lane = lax.broadcasted_iota(jnp.int32, q.shape, 1)
    first = lane < NA_HEAD_DIM
    outs = []
    for hh in range(2):
        qm = jnp.where(first if hh == 0 else jnp.logical_not(first), q, 0.0).astype(BF16)
        s = _dot_nt(qm, kb)
        p = jnp.exp(s - jnp.max(s, axis=-1, keepdims=True))
        outs.append(_dot(p.astype(BF16), vb) / jnp.sum(p, axis=-1, keepdims=True))
    o_ref[...] = jnp.where(first, outs[0], outs[1]).astype(o_ref.dtype)


def _ctx_attn(proj, *, batch, seq, ctx_len, q_col, k_col, v_col):
    pairs = NA_HEADS // 2
    ctx0 = batch * seq // ctx_len
    ctx = lambda col: pl.BlockSpec((ctx_len, LANES), lambda b, p: (ctx0 + b, col // LANES + p))
    return pl.pallas_call(
        _ctx_attn_body, grid=(batch, pairs),
        in_specs=[ctx(q_col), ctx(k_col), ctx(v_col)],
        out_specs=pl.BlockSpec((ctx_len, LANES), lambda b, p: (b, p)),
        out_shape=jax.ShapeDtypeStruct((batch * ctx_len, pairs * LANES), BF16),
        compiler_params=_cparams(("arbitrary", "arbitrary")), name="ctx_attn",
    )(proj, proj, proj)


def _da_two_streams(q_ref, kl_ref, kc_ref, vlb, vcb, g_ref, o_ref, s_ref, lam, lam_init):
    hq = q_ref.shape[0] // 2
    ck = DA_KEY_CHUNK
    chunks = [(kl_ref, vlb, off, ck, off) for off in range(0, kl_ref.shape[0], ck)]
    chunks.append((kc_ref, vcb, 0, kc_ref.shape[0], kl_ref.shape[0]))
    lane = lax.broadcasted_iota(jnp.int32, (hq, LANES), 1)
    lane_tiles = lambda x: [x[:, i * LANES:(i + 1) * LANES] for i in range(x.shape[1] // LANES)]

    def stacked_q(st):
        q = q_ref[st * hq:(st + 1) * hq, :]
        zero = jnp.zeros_like(q)
        return jnp.concatenate([jnp.where(lane < DA_HEAD_DIM, q, zero), jnp.where(lane >= DA_HEAD_DIM, q, zero)], axis=0)

    def scores(st, qq, chunk, m_run):
        k_ref, _, off, n, c0 = chunk
        s = _dot_nt(qq, k_ref[off:off + n, :])
        s_ref[st, :, c0:c0 + n] = s
        return functools.reduce(jnp.maximum, lane_tiles(s), m_run)

    def numerators(st, chunk, m, l_run):
        _, _, _, n, c0 = chunk
        p = jnp.exp2(s_ref[st, :, c0:c0 + n] - m)
        s_ref[st, :, c0:c0 + n] = p
        return functools.reduce(jnp.add, lane_tiles(p), l_run)

    def values(st, chunk, ratio, acc):
        _, v_ref, off, n, c0 = chunk
        a = (s_ref[st, 0:hq, c0:c0 + n] - ratio * s_ref[st, hq:2 * hq, c0:c0 + n]).astype(BF16)
        return acc + _dot(a, v_ref[off:off + n, :])

    def finish(st, acc, den):
        o = acc / den[:hq]
        o = o * lax.rsqrt(jnp.mean(o * o, axis=-1, keepdims=True) + EPS) * g_ref[...]
        o_ref[st * hq:(st + 1) * hq, :] = (o * (1.0 - lam_init)).astype(o_ref.dtype)

    neg = jnp.full((2 * hq, LANES), -jnp.inf, F32)
    zero_l = jnp.zeros((2 * hq, LANES), F32)
    zero_o = jnp.zeros((hq, LANES), F32)
    row_max = lambda m_run: jnp.max(m_run, axis=-1, keepdims=True)
    row_sum = lambda l_run: jnp.sum(l_run, axis=-1, keepdims=True)
    qq0, qq1 = stacked_q(0), stacked_q(1)
    m0 = neg
    for ch in chunks:
        m0 = scores(0, qq0, ch, m0)
    m0 = row_max(m0)
    l0, m1 = zero_l, neg
    for ch in chunks:
        l0 = numerators(0, ch, m0, l0)
        m1 = scores(1, qq1, ch, m1)
    den0, m1 = row_sum(l0), row_max(m1)
    ratio0 = lam * den0[:hq] / den0[hq:]
    acc0, l1 = zero_o, zero_l
    for ch in chunks:
        acc0 = values(0, ch, ratio0, acc0)
        l1 = numerators(1, ch, m1, l1)
    finish(0, acc0, den0)
    den1 = row_sum(l1)
    ratio1 = lam * den1[:hq] / den1[hq:]
    acc1 = zero_o
    for ch in chunks:
        acc1 = values(1, ch, ratio1, acc1)
    finish(1, acc1, den1)


def _da_body(lv_ref, q_ref, *rest, lam_init, has_lat):
    if has_lat:
        kl_ref, vl_ref, rest = rest[0], rest[1], rest[2:]
    kc_ref, vc_ref, g_ref, o_ref = rest[:4]
    scr = rest[4:]
    t = pl.program_id(2)
    if has_lat:
        vlb, vcb, s_ref = scr

        @pl.when(t == 0)
        def _():
            vlb[...] = vl_ref[...].astype(BF16)
            vcb[...] = vc_ref[...].astype(BF16)
    else:
        (vcb,) = scr
        vcb[...] = vc_ref[...].astype(BF16)

    lv = lv_ref[...]
    lam = (jnp.exp(jnp.sum(lv[0:1] * lv[1:2], axis=-1, keepdims=True))
           - jnp.exp(jnp.sum(lv[2:3] * lv[3:4], axis=-1, keepdims=True)) + lam_init)
    if has_lat:
        _da_two_streams(q_ref, kl_ref, kc_ref, vlb, vcb, g_ref, o_ref, s_ref, lam, lam_init)
        return
    q = q_ref[...]
    tq = q.shape[0]
    lane = lax.broadcasted_iota(jnp.int32, q.shape, 1)
    zero = jnp.zeros_like(q)
    qq = jnp.concatenate([jnp.where(lane < DA_HEAD_DIM, q, zero), jnp.where(lane >= DA_HEAD_DIM, q, zero)], axis=0)
    s_c = _dot_nt(qq, kc_ref[...])
    m = jnp.max(s_c, axis=-1, keepdims=True)
    if has_lat:
        s_l = _dot_nt(qq, kl_ref[...])
        m = jnp.maximum(m, jnp.max(s_l, axis=-1, keepdims=True))
        p_l = jnp.exp2(s_l - m)
    p_c = jnp.exp2(s_c - m)
    den = jnp.sum(p_c, axis=-1, keepdims=True)
    if has_lat:
        den = den + jnp.sum(p_l, axis=-1, keepdims=True)
    ratio = lam * den[:tq] / den[tq:]
    o = _dot((p_c[:tq] - ratio * p_c[tq:]).astype(BF16), vcb[...])
    if has_lat:
        o = o + _dot((p_l[:tq] - ratio * p_l[tq:]).astype(BF16), vlb[...])
    o = o / den[:tq]
    o = o * lax.rsqrt(jnp.mean(o * o, axis=-1, keepdims=True) + EPS) * g_ref[...]
    o_ref[...] = (o * (1.0 - lam_init)).astype(o_ref.dtype)


def _da(q_r, k_r, proj, lam_vec, subln_g, lam_init, *, batch, seq, ctx_len, v_col, tq, latent):
    ctx0 = batch * seq // ctx_len
    vcb0 = v_col // LANES
    g2 = subln_g.reshape(1, 2 * DA_HEAD_DIM).astype(F32)
    small = [pl.BlockSpec((4, DA_HEAD_DIM), lambda b, h, t: (0, 0))]
    gspec = pl.BlockSpec((1, LANES), lambda b, h, t: (0, 0))
    kc = pl.BlockSpec((ctx_len, LANES), lambda b, h, t: (ctx0 + b, h))
    vc = pl.BlockSpec((ctx_len, LANES), lambda b, h, t: (ctx0 + b, vcb0 + h))
    if latent:
        nq = seq // tq
        qs = pl.BlockSpec((tq, LANES), lambda b, h, t: (b * nq + t, h))
        kl = pl.BlockSpec((seq, LANES), lambda b, h, t: (b, h))
        vl = pl.BlockSpec((seq, LANES), lambda b, h, t: (b, vcb0 + h))
        in_specs = small + [qs, kl, vl, kc, vc, gspec]
        args = (lam_vec, q_r, k_r, proj, k_r, proj, g2)
        assert seq % DA_KEY_CHUNK == 0 and tq % 2 == 0
        scratch = [pltpu.VMEM((seq, LANES), BF16), pltpu.VMEM((ctx_len, LANES), BF16),
                   pltpu.VMEM((2, tq, seq + ctx_len), F32)]
        n_out = batch * seq
    else:
        nq = ctx_len // tq
        cq0 = batch * seq // tq
        qs = pl.BlockSpec((tq, LANES), lambda b, h, t: (cq0 + b * nq + t, h))
        in_specs = small + [qs, kc, vc, gspec]
        args = (lam_vec, q_r, k_r, proj, g2)
        scratch = [pltpu.VMEM((ctx_len, LANES), BF16)]
        n_out = batch * ctx_len
    return pl.pallas_call(
        functools.partial(_da_body, lam_init=lam_init, has_lat=latent),
        grid=(batch, DA_HEADS, nq), in_specs=in_specs,
        out_specs=pl.BlockSpec((tq, LANES), lambda b, h, t: (b * nq + t, h)),
        out_shape=jax.ShapeDtypeStruct((n_out, DA_HEADS * LANES), BF16),
        scratch_shapes=scratch,
        compiler_params=_cparams(("arbitrary", "arbitrary", "arbitrary")),
        name="diff_attn_lat" if latent else "diff_attn_ctx",
    )(*args)


def _ssd_body(x0_ref, x1_ref, bc_ref, x0p_ref, x1p_ref, bcp_ref, x0n_ref, x1n_ref, bcn_ref, dt_ref,
              cw_ref, cb_ref, prm_ref, dsk_ref, ex_ref, init_ref, y_ref, sout_ref, st_ref, u_ref, *, nc):
    q_len = SSD_CHUNK
    dirn = pl.program_id(1)
    z = pl.program_id(2)
    zz = jnp.where(dirn == 0, z, nc - 1 - z)
    fwd = dirn == 0
    half = SSM_D_INNER // SSM_GROUPS
    heads_per_group = SSM_HEADS // SSM_GROUPS

    @pl.when(z == 0)
    def _():
        st_ref[...] = init_ref[...]

    def conv_piece(main_ref, prev_ref, next_ref, c0):
        width = main_ref.shape[1]
        prev = jnp.where(zz == 0, 0.0, prev_ref[...])
        nxt = jnp.where(zz == nc - 1, 0.0, next_ref[...])
        ext = jnp.concatenate([prev, main_ref[...], nxt], axis=0)
        n_ext = ext.shape[0]
        acc = jnp.zeros((q_len, width), F32) + cb_ref[:, c0:c0 + width]
        for tap in range(SSM_CONV_W):
            sh = (SSM_CONV_W // 2 - tap) % n_ext
            e = ext if sh == 0 else pltpu.roll(ext, sh, 0)
            acc = acc + e[SUBLANES:SUBLANES + q_len] * cw_ref[tap:tap + 1, c0:c0 + width]
        return _silu(acc)

    @pl.when(fwd)
    def _():
        u_ref[zz, :, 0:half] = conv_piece(x0_ref, x0p_ref, x0n_ref, 0)
        u_ref[zz, :, half:2 * half] = conv_piece(x1_ref, x1p_ref, x1n_ref, half)
        u_ref[zz, :, SSM_D_INNER:] = conv_piece(bc_ref, bcp_ref, bcn_ref, SSM_D_INNER)

    xs = (u_ref[zz, :, 0:half], u_ref[zz, :, half:2 * half])
    bcm = u_ref[zz, :, SSM_D_INNER:]

    raw = dt_ref[...]
    raw = jnp.where(fwd, raw, pltpu.roll(raw, LANES - SSM_HEADS, 1))
    xb = raw + prm_ref[0:1, :]
    dtv = jnp.maximum(xb, 0.0) + jnp.log1p(jnp.exp(-jnp.abs(xb)))
    a = dtv * (-jnp.exp(prm_ref[1:2, :]))
    rowi = lax.broadcasted_iota(jnp.int32, (q_len, LANES), 0)
    coli = lax.broadcasted_iota(jnp.int32, (q_len, LANES), 1)
    cum = a
    sft = 1
    while sft < q_len:
        cum = cum + jnp.where(rowi >= sft, pltpu.roll(cum, sft, 0), 0.0)
        sft *= 2
    tot = cum[q_len - 1:q_len, :]
    g = jnp.where(fwd, cum, cum - a)
    e_g = jnp.exp(g)
    e_tg = jnp.exp(tot - g)
    w_state = jnp.where(fwd, e_tg, e_g)
    w_yoff = jnp.where(fwd, e_g, e_tg)
    e_tot = jnp.exp(tot)
    gs = g * jnp.where(fwd, LOG2E, -LOG2E)
    gs_t = gs.T
    tri = jnp.where(fwd, rowi - coli, coli - rowi) >= 0

    stack = jnp.concatenate([dtv, dtv * w_state, w_yoff, jnp.broadcast_to(e_tot, (SUBLANES, LANES))], axis=0)
    s_hi, s_lo = _split2(stack)
    ex = ex_ref[...]
    expd = _dot(s_hi, ex) + _dot(s_lo, ex)
    dt_e, dts_e, wy_e = expd[0:q_len], expd[q_len:2 * q_len], expd[2 * q_len:3 * q_len]
    tot_e = expd[3 * q_len:3 * q_len + 1]

    lane = lax.broadcasted_iota(jnp.int32, (q_len, LANES), 1)
    first = lane < SSM_HEAD_DIM
    for grp in range(SSM_GROUPS):
        x_g = xs[grp]
        csl = slice(grp * half, (grp + 1) * half)
        b_g = bcm[:, grp * SSM_STATE:(grp + 1) * SSM_STATE]
        c_g = bcm[:, (SSM_GROUPS + grp) * SSM_STATE:(SSM_GROUPS + grp + 1) * SSM_STATE].astype(BF16)
        xdt = (x_g * dt_e[:, csl]).astype(BF16)
        xdts = (x_g * dts_e[:, csl]).astype(BF16)
        cb = _dot_nt(c_g, b_g.astype(BF16))
        st_g = st_ref[:, csl]
        y_g = _dot(c_g, st_g.astype(BF16)) * wy_e[:, csl] + dsk_ref[:, csl] * x_g
        for pr in range(heads_per_group // 2):
            outs = []
            for hh in range(2):
                col = grp * heads_per_group + 2 * pr + hh
                seg = gs[:, col:col + 1] - gs_t[col:col + 1, :]
                mat = (cb * jnp.where(tri, jnp.exp2(seg), 0.0)).astype(BF16)
                outs.append(_dot(mat, xdt[:, pr * LANES:(pr + 1) * LANES]))
            lo = grp * half + pr * LANES
            y_ref[:, lo:lo + LANES] = y_g[:, pr * LANES:(pr + 1) * LANES] + jnp.where(first, outs[0], outs[1])
        st_ref[:, csl] = tot_e[:, csl] * st_g + _dot(b_g.T.astype(BF16), xdts)

    @pl.when(z == nc - 1)
    def _():
        sout_ref[...] = st_ref[...]


def _ssd(proj, tail, conv_w, conv_b, dt_bias, a_log, d_skip, init, *, batch, n_tok, row0, x_col, dt_col, tail_col):
    q_len = SSD_CHUNK
    nc = n_tok // q_len
    half = SSM_D_INNER // SSM_GROUPS
    conv_dim = SSM_D_INNER + 2 * SSM_GROUPS * SSM_STATE
    rb0 = row0 // q_len
    hb = q_len // SUBLANES
    n_halo = proj.shape[0] // SUBLANES
    zz = lambda d, z: jnp.where(d == 0, z, nc - 1 - z)
    rb = lambda b, d, z: rb0 + b * nc + zz(d, z)
    rbx = lambda b, d, z: rb0 + b * nc + jnp.where(d == 0, z, nc - 1)
    main = lambda c: pl.BlockSpec((q_len, half), lambda b, d, z: (rbx(b, d, z), c))
    prev = lambda c: pl.BlockSpec((SUBLANES, half), lambda b, d, z: (jnp.maximum(rbx(b, d, z) * hb - 1, 0), c))
    nxt = lambda c: pl.BlockSpec((SUBLANES, half),
                                 lambda b, d, z: (jnp.minimum(rbx(b, d, z) * hb + hb, n_halo - 1), c))
    bc_col = x_col + SSM_D_INNER
    assert bc_col == tail_col and conv_dim - SSM_D_INNER == half and (dt_col - tail_col) % LANES == 0
    cols = [x_col // half, x_col // half + 1, (bc_col - tail_col) // half]
    const2 = lambda shape: pl.BlockSpec(shape, lambda b, d, z: (0, 0))
    prm = jnp.zeros((2, SUBLANES, LANES), F32)
    prm = prm.at[:, 0, :SSM_HEADS].set(dt_bias.astype(F32)).at[:, 1, :SSM_HEADS].set(a_log.astype(F32))
    dsk = jnp.repeat(d_skip.astype(F32), SSM_HEAD_DIM, axis=-1).reshape(2, 1, SSM_D_INNER)
    ex = (np.arange(LANES)[:, None] == (np.arange(SSM_D_INNER) // SSM_HEAD_DIM)[None, :])
    ex = jnp.asarray(ex, BF16)
    cw = jnp.zeros((SUBLANES, conv_dim), F32).at[:SSM_CONV_W].set(conv_w.astype(F32))
    state_spec = pl.BlockSpec((None, None, SSM_STATE, SSM_D_INNER), lambda b, d, z: (b, d, 0, 0))
    y, s_out = pl.pallas_call(
        functools.partial(_ssd_body, nc=nc),
        grid=(batch, 2, nc),
        in_specs=[main(cols[0]), main(cols[1]), main(cols[2]), prev(cols[0]), prev(cols[1]), prev(cols[2]),
                  nxt(cols[0]), nxt(cols[1]), nxt(cols[2]),
                  pl.BlockSpec((q_len, LANES), lambda b, d, z: (rb(b, d, z), (dt_col - tail_col) // LANES)),
                  const2((SUBLANES, conv_dim)), const2((1, conv_dim)),
                  pl.BlockSpec((None, SUBLANES, LANES), lambda b, d, z: (d, 0, 0)),
                  pl.BlockSpec((None, 1, SSM_D_INNER), lambda b, d, z: (d, 0, 0)),
                  const2((LANES, SSM_D_INNER)), state_spec],
        out_specs=[pl.BlockSpec((None, q_len, SSM_D_INNER), lambda b, d, z: (d, b * nc + zz(d, z), 0)),
                   state_spec],
        out_shape=[jax.ShapeDtypeStruct((2, batch * n_tok, SSM_D_INNER), F32),
                   jax.ShapeDtypeStruct((batch, 2, SSM_STATE, SSM_D_INNER), F32)],
        scratch_shapes=[pltpu.VMEM((SSM_STATE, SSM_D_INNER), F32), pltpu.VMEM((nc, q_len, conv_dim), F32)],
        compiler_params=_cparams(("arbitrary", "arbitrary", "arbitrary")), name="ssd_scan",
    )(proj, proj, tail, proj, proj, tail, proj, proj, tail, tail, cw, conv_b.reshape(1, conv_dim).astype(F32),
      prm, dsk, ex, init)
    return y, s_out


def _ssm_norm_body(y_ref, z_ref, g_ref, o_ref):
    yz = (y_ref[0] + y_ref[1]) * _silu(z_ref[...])
    gw = SSM_D_INNER // SSM_GROUPS
    for grp in range(SSM_GROUPS):
        v = yz[:, grp * gw:(grp + 1) * gw]
        v = v * lax.rsqrt(jnp.mean(v * v, axis=-1, keepdims=True) + EPS)
        o_ref[:, grp * gw:(grp + 1) * gw] = (v * g_ref[:, grp * gw:(grp + 1) * gw]).astype(o_ref.dtype)


def _ssm_norm(y, proj, g, *, tm, row0, z_col):
    n = y.shape[1]
    zb0 = row0 // tm
    return pl.pallas_call(
        _ssm_norm_body, grid=(n // tm,),
        in_specs=[pl.BlockSpec((2, tm, SSM_D_INNER), lambda i: (0, i, 0)),
                  pl.BlockSpec((tm, SSM_D_INNER), lambda i: (zb0 + i, z_col // SSM_D_INNER)),
                  pl.BlockSpec((1, SSM_D_INNER), lambda i: (0, 0))],
        out_specs=pl.BlockSpec((tm, SSM_D_INNER), lambda i: (i, 0)),
        out_shape=jax.ShapeDtypeStruct((n, SSM_D_INNER), BF16),
        compiler_params=_cparams(("arbitrary",)), name="ssm_gated_norm",
    )(y, proj, g.reshape(1, SSM_D_INNER).astype(F32))


def _two_source_specs(block, first_blocks, row_block_of):
    first = pl.BlockSpec(block, lambda *g: (jnp.minimum(row_block_of(*g), first_blocks - 1), 0))
    second = pl.BlockSpec(block, lambda *g: (jnp.maximum(row_block_of(*g) - first_blocks, 0), 0))
    return first, second


def _merge_body(*refs, lat_blocks, has_ctx):
    n_src = 6 if has_ctx else 3
    src, (ga_ref, gb_ref, gc_ref, wa_ref, wb_ref, wc_ref, o_ref, wab, wbb, wcb) = refs[:n_src], refs[n_src:]
    i = pl.program_id(1)

    @pl.when(i == 0)
    def _():
        wab[...] = wa_ref[...].astype(BF16)
        wbb[...] = wb_ref[...].astype(BF16)
        wcb[...] = wc_ref[...].astype(BF16)

    if has_ctx:
        oa, ob, oc = (jnp.where(i < lat_blocks, src[2 * n][...], src[2 * n + 1][...]) for n in range(3))
    else:
        oa, ob, oc = (r[...] for r in src)
    acc = ga_ref[...].astype(F32) * _dot(oa, wab[...])
    acc = acc + gb_ref[...].astype(F32) * _dot(ob, wbb[...])
    acc = acc + gc_ref[...].astype(F32) * _dot(oc, wcb[...])
    o_ref[...] = acc.astype(o_ref.dtype)


def _merge(o_lat, o_ctx, gates, w_a, w_b, w_c, layer, *, tm, tn, n_rows):
    d = w_a.shape[2]
    nj = d // tn
    lat_blocks = o_lat[0].shape[0] // tm
    gate = lambda x: pl.BlockSpec((tm, tn), lambda j, i: (i, x * nj + j))
    wsp = lambda kk: pl.BlockSpec((None, kk, tn), lambda j, i: (layer, 0, j))
    widths = (w_a.shape[1], w_b.shape[1], w_c.shape[1])
    src_specs, src = [], []
    for n, kk in enumerate(widths):
        if o_ctx is None:
            src_specs.append(pl.BlockSpec((tm, kk), lambda j, i: (i, 0)))
            src.append(o_lat[n])
        else:
            src_specs += _two_source_specs((tm, kk), lat_blocks, lambda j, i: i)
            src += [o_lat[n], o_ctx[n]]
    return pl.pallas_call(
        functools.partial(_merge_body, lat_blocks=lat_blocks, has_ctx=o_ctx is not None), grid=(nj, n_rows // tm),
        in_specs=src_specs + [gate(0), gate(1), gate(2)] + [wsp(kk) for kk in widths],
        out_specs=pl.BlockSpec((tm, tn), lambda j, i: (i, j)),
        out_shape=jax.ShapeDtypeStruct((n_rows, d), BF16),
        scratch_shapes=[pltpu.VMEM((kk, tn), BF16) for kk in widths],
        compiler_params=_cparams(("arbitrary", "arbitrary")), name="branch_merge",
    )(*src, gates, gates, gates, w_a, w_b, w_c)


def _route(top_idx, blk):
    n = top_idx.shape[0]
    e_flat = top_idx.reshape(-1)
    onehot = (e_flat[:, None] == jnp.arange(N_EXPERTS, dtype=jnp.int32)[None, :]).astype(jnp.int32)
    counts = jnp.sum(onehot, axis=0)
    rank = jnp.sum((jnp.cumsum(onehot, axis=0) - onehot) * onehot, axis=1)
    padded = (counts + blk - 1) // blk * blk
    pad_end = jnp.cumsum(padded)
    pad_start = pad_end - padded
    dest = jnp.sum(onehot * pad_start[None, :], axis=1) + rank
    n_blocks = -(-(n * TOP_K) // blk) + N_EXPERTS
    cap = n_blocks * blk
    tok_flat = jnp.repeat(jnp.arange(n, dtype=jnp.int32), TOP_K)
    slot_tok = jnp.zeros((cap,), jnp.int32).at[dest].set(tok_flat)
    blk_start = jnp.arange(n_blocks, dtype=jnp.int32) * blk
    block_expert = jnp.clip(jnp.sum((pad_end[None, :] <= blk_start[:, None]).astype(jnp.int32), axis=1),
                            0, N_EXPERTS - 1)
    n_used = (pad_end[-1] // blk).astype(jnp.int32).reshape(1)
    return dest.astype(jnp.int32), slot_tok, block_expert, n_used


GATHER_UNROLL = 8


def _row_copy(src_hbm, row, buf, slot, sem):
    return pltpu.make_async_copy(src_hbm.at[pl.ds(row, 1), :], buf.at[pl.ds(slot, 1), :], sem)


def _gather_body(nv_ref, idx_ref, src_hbm, o_ref, buf, sem, *, rows):
    @pl.when(pl.program_id(0) >= nv_ref[0])
    def _():
        o_ref[...] = jnp.zeros_like(o_ref)

    @pl.when(pl.program_id(0) < nv_ref[0])
    def _():
        def issue(g, c):
            for u in range(GATHER_UNROLL):
                r = g * GATHER_UNROLL + u
                _row_copy(src_hbm, idx_ref[0, r], buf, r, sem).start(priority=u % 2)
            return c

        def drain(g, c):
            for u in range(GATHER_UNROLL):
                r = g * GATHER_UNROLL + u
                _row_copy(src_hbm, idx_ref[0, r], buf, r, sem).wait()
            return c

        lax.fori_loop(0, rows // GATHER_UNROLL, issue, 0)
        lax.fori_loop(0, rows // GATHER_UNROLL, drain, 0)
        o_ref[...] = buf[...].astype(o_ref.dtype)


def _gather_rows(src, idx, n_used, *, rows, out_dtype):
    n, d = src.shape
    nb = idx.shape[0] // rows
    assert rows % GATHER_UNROLL == 0
    blk = lambda i, nv: jnp.minimum(i, nv[0] - 1)
    grid_spec = pltpu.PrefetchScalarGridSpec(
        num_scalar_prefetch=1, grid=(nb,),
        in_specs=[pl.BlockSpec((None, 1, rows), lambda i, nv: (blk(i, nv), 0, 0), memory_space=pltpu.SMEM),
                  pl.BlockSpec(memory_space=pl.ANY)],
        out_specs=pl.BlockSpec((rows, d), lambda i, nv: (i, 0)),
        scratch_shapes=[pltpu.VMEM((rows, d), src.dtype), pltpu.SemaphoreType.DMA])
    return pl.pallas_call(
        functools.partial(_gather_body, rows=rows), grid_spec=grid_spec,
        out_shape=jax.ShapeDtypeStruct((nb * rows, d), out_dtype),
        compiler_params=_cparams(("arbitrary",)), name="moe_dispatch",
    )(n_used, idx.reshape(nb, 1, rows), src)


def _combine_body(dest_ref, h_ref, rt_ref, gate_ref, g_ref, y_hbm, o_ref, buf, sem, *, rows, norm):
    def issue(g, c):
        for u in range(GATHER_UNROLL):
            r = g * GATHER_UNROLL + u
            for k in range(TOP_K):
                _row_copy(y_hbm, dest_ref[0, r * TOP_K + k], buf.at[k], r, sem).start(priority=k)
        return c

    def drain(g, c):
        for u in range(GATHER_UNROLL):
            r = g * GATHER_UNROLL + u
            for k in range(TOP_K):
                _row_copy(y_hbm, dest_ref[0, r * TOP_K + k], buf.at[k], r, sem).wait()
        return c

    lax.fori_loop(0, rows // GATHER_UNROLL, issue, 0)
    lax.fori_loop(0, rows // GATHER_UNROLL, drain, 0)
    rt = rt_ref[...]
    f = buf[0] * rt[:, TOP_K:TOP_K + 1] + buf[1] * rt[:, TOP_K + 1:TOP_K + 2]
    x = h_ref[...] + gate_ref[...] * f
    if norm:
        x = x * lax.rsqrt(jnp.mean(x * x, axis=-1, keepdims=True) + EPS) * g_ref[...]
    o_ref[...] = x


def _combine(h, y_sorted, dest, rt, mod, mod_idx_fn, part, g, *, rows, n_rows, norm):
    d = h.shape[1]
    nb = n_rows // rows
    row = pl.BlockSpec((rows, d), lambda i: (i, 0))
    return pl.pallas_call(
        functools.partial(_combine_body, rows=rows, norm=norm), grid=(nb,),
        in_specs=[pl.BlockSpec((None, 1, rows * TOP_K), lambda i: (i, 0, 0), memory_space=pltpu.SMEM),
                  row, pl.BlockSpec((rows, LANES), lambda i: (i, 0)),
                  pl.BlockSpec((None, 1, d), lambda i: (mod_idx_fn(i), 0, part)),
                  pl.BlockSpec((1, d), lambda i: (0, 0)), pl.BlockSpec(memory_space=pl.ANY)],
        out_specs=row, out_shape=jax.ShapeDtypeStruct((n_rows, d), F32),
        scratch_shapes=[pltpu.VMEM((TOP_K, rows, d), y_sorted.dtype), pltpu.SemaphoreType.DMA],
        compiler_params=_cparams(("arbitrary",)), name="moe_combine",
    )(dest.reshape(nb, 1, rows * TOP_K), h, rt, mod, g.reshape(1, d), y_sorted)


def kernel(x, c, ctx, c_ctx, ada_w, ada_b, norm_mix_g, norm_ffn_g, w_in, na_rpb, da_lambda, da_subln_g,
           ssm_conv_w, ssm_conv_b, ssm_dt_bias, ssm_a_log, ssm_d, ssm_norm_g, w_branch_a, w_branch_b,
           w_branch_c, w_out, ffn_w1, ffn_w3, ffn_w2, moe_router_w, moe_router_b, moe_w1, moe_w3, moe_w2,
           final_norm_g):
    batch, seq, d = x.shape
    ctx_len = ctx.shape[1]
    depth = ada_w.shape[0]
    n_lat, n_ctx = batch * seq, batch * ctx_len
    n_all = n_lat + n_ctx
    tm = next(t for t in (1024, 512, 256) if n_ctx % t == 0 and seq % t == 0)
    n_mod = 16
    assert batch + 1 <= n_mod
    lat_blocks = n_lat // tm
    per_seq = seq // tm
    te = next(t for t in (512, 256) if n_ctx % t == 0 and seq % t == 0)
    tm2 = 2 * tm if n_ctx % (2 * tm) == 0 and seq % (2 * tm) == 0 else tm
    mod_idx_fn = lambda i: jnp.where(i < n_lat // te, i // (seq // te), batch)
    mod_idx_all = jnp.where(jnp.arange(n_all // tm) < lat_blocks, jnp.arange(n_all // tm) // per_seq, batch)
    layer_blocks = lambda nb, idx: jnp.full((nb,), idx, jnp.int32)

    na_w, da_w = NA_HEADS * NA_HEAD_DIM, DA_HEADS * 2 * DA_HEAD_DIM
    conv_dim = SSM_D_INNER + 2 * SSM_GROUPS * SSM_STATE
    col_qa, col_ka, col_va = 0, na_w, 2 * na_w
    col_qb, col_kb, col_vb = 3 * na_w, 3 * na_w + da_w, 3 * na_w + 2 * da_w
    col_z = 3 * na_w + 3 * da_w
    col_x = col_z + SSM_D_INNER
    col_dt = col_x + conv_dim
    col_gate = col_dt + 2 * SSM_HEADS
    n_main = col_x + SSM_D_INNER
    n_tail = conv_dim - SSM_D_INNER + LANES
    assert n_main % 1024 == 0 and col_dt == n_main + n_tail - LANES and 2 * SSM_HEADS <= LANES

    h, h_ctx = x.reshape(n_lat, d), ctx.reshape(n_ctx, d)
    cond =jnp.zeros((n_mod, d), F32).at[:batch].set(jax.nn.silu(c)).at[batch].set(jax.nn.silu(c_ctx))
    cond = cond.astype(BF16)
    rope_tabs = _rope_tables(seq, tm)
    zero_state = jnp.zeros((batch, 2, SSM_STATE, SSM_D_INNER), F32)
    w_in_t = jnp.swapaxes(w_in, 1, 2)

    out = None
    for li in range(depth):
        last = li == depth - 1
        lam_init = 0.8 - 0.6 * math.exp(-0.3 * li)
        n_out = n_lat if last else n_all
        mod = _gmm(cond, [ada_w], layer_blocks(1, li), tm=n_mod, tn=512, epi="bias",
                   bias=ada_b[li].reshape(1, 6 * d)).reshape(n_mod, 1, 6 * d)
        u = _norm(h, norm_mix_g[li], tm=te, n_rows=n_all, mod=mod, mod_idx_fn=mod_idx_fn, parts=(0, 1), x2=h_ctx)
        proj = _gmm(u, [w_in_t], layer_blocks(n_all // tm, li), tm=tm, tn=1024, n_cols=n_main, w_rows=0)
        tail = _gmm(u, [w_in_t], layer_blocks(n_all // tm, li), tm=tm, tn=n_tail, n_cols=n_tail, w_rows=n_main)
        gates = _gmm(u, [w_in_t], layer_blocks(n_out // tm2, li), tm=tm2, tn=1024, n_rows=n_out, epi="sigmoid",
                     out_dtype=BF16, n_cols=3 * d, w_rows=col_gate)

        o_a = _na(proj, na_rpb[li], batch=batch, seq=seq, ctx_len=ctx_len, q_col=col_qa, k_col=col_ka, v_col=col_va)
        q_r, k_r = _rope(proj, rope_tabs, tm=tm, n_rows=n_all, n_lat_rows=n_lat, seq=seq,
                         q_col=col_qb, k_col=col_kb, width=da_w)
        o_b = _da(q_r, k_r, proj, da_lambda[li].astype(F32), da_subln_g[li], lam_init, batch=batch, seq=seq,
                  ctx_len=ctx_len, v_col=col_vb, tq=256, latent=True)
        ssd_args = (ssm_conv_w[li], ssm_conv_b[li], ssm_dt_bias[li], ssm_a_log[li], ssm_d[li])
        ssd_cols = dict(x_col=col_x, dt_col=col_dt, tail_col=n_main)
        y_ctx, s_ctx = _ssd(proj, tail, *ssd_args, zero_state, batch=batch, n_tok=ctx_len, row0=n_lat, **ssd_cols)
        y_lat, _ = _ssd(proj, tail, *ssd_args, s_ctx, batch=batch, n_tok=seq, row0=0, **ssd_cols)
        o_c = _ssm_norm(y_lat, proj, ssm_norm_g[li], tm=te, row0=0, z_col=col_z)
        if not last:
            o_a_c = _ctx_attn(proj, batch=batch, seq=seq, ctx_len=ctx_len, q_col=col_qa, k_col=col_ka, v_col=col_va)
            o_b_c = _da(q_r, k_r, proj, da_lambda[li].astype(F32), da_subln_g[li], lam_init, batch=batch, seq=seq,
                        ctx_len=ctx_len, v_col=col_vb, tq=ctx_len, latent=False)
            o_c_c = _ssm_norm(y_ctx, proj, ssm_norm_g[li], tm=te, row0=n_lat, z_col=col_z)
        merged = _merge((o_a, o_b, o_c), None if last else (o_a_c, o_b_c, o_c_c), gates,
                        w_branch_a, w_branch_b, w_branch_c, li, tm=te, tn=1024, n_rows=n_out)
        h = _gmm(merged, [w_out], layer_blocks(n_out // tm, li), tm=tm, tn=1024, epi="resid", res=h, res2=h_ctx,
                 mod=mod, mod_idx=mod_idx_all[:n_out // tm], mod_part=2, n_rows=n_out)
        h_ctx = None

        j = li // 2
        if li % 2 == 0:
            tokens = _norm(h, norm_ffn_g[li], tm=te, n_rows=n_out, mod=mod, mod_idx_fn=mod_idx_fn, parts=(3, 4))
            hid = _gmm(tokens, [ffn_w1, ffn_w3], layer_blocks(n_out // tm2, j), tm=tm2, tn=512,
                       epi="swiglu", out_dtype=BF16)
            h = _gmm(hid, [ffn_w2], layer_blocks(n_out // tm, j), tm=tm, tn=512, epi="resid", res=h,
                     mod=mod, mod_idx=mod_idx_all[:n_out // tm], mod_part=5, n_rows=n_out, w_buffers=1)
            if last:
                out = _norm(h, final_norm_g, tm=te, n_rows=n_lat, out_dtype=F32)
        else:
            rw = jnp.zeros((d, LANES), F32).at[:, :N_EXPERTS].set(moe_router_w[j].astype(F32))
            rb = jnp.zeros((1, LANES), F32).at[0, :N_EXPERTS].set(moe_router_b[j].astype(F32))
            tokens, rt = _norm(h, norm_ffn_g[li], tm=te, n_rows=n_out, mod=mod, mod_idx_fn=mod_idx_fn,
                               parts=(3, 4), router=(rw, rb), out_dtype=F32)
            dest, slot_tok, block_expert, n_used = _route(rt[:, :TOP_K].astype(jnp.int32), MOE_ROWS)
            block_expert = block_expert + j * N_EXPERTS
            stack = lambda w: w.reshape((-1,) + w.shape[2:])
            x_sorted = _gather_rows(tokens, slot_tok, n_used, rows=MOE_ROWS, out_dtype=BF16)
            hid = _gmm(x_sorted, [stack(moe_w1), stack(moe_w3)], block_expert, tm=MOE_ROWS, tn=512, epi="swiglu",
                       out_dtype=BF16, n_used=n_used)
            y_sorted = _gmm(hid, [stack(moe_w2)], block_expert, tm=MOE_ROWS, tn=512, n_used=n_used, w_buffers=1)
            h = _combine(h, y_sorted, dest, rt, mod, mod_idx_fn, 5, final_norm_g, rows=te, n_rows=n_out, norm=last)
            if last:
                out = h
    return out.reshape(batch, seq, d)
```

```python
import functools
import math

import numpy as np
import jax
import jax.numpy as jnp
from jax import lax
from jax.experimental import pallas as pl
from jax.experimental.pallas import tpu as pltpu

F32 = jnp.float32
BF16 = jnp.bfloat16

GRID_W = 64
NA_HEADS, NA_HEAD_DIM, NA_WIN_H, NA_WIN_W = 8, 64, 8, 16
DA_HEADS, DA_HEAD_DIM = 4, 64
ROPE_BASE = 10000.0
SSM_D_INNER, SSM_HEAD_DIM, SSM_GROUPS, SSM_STATE, SSM_CONV_W = 1024, 64, 2, 128, 5
SSM_HEADS = SSM_D_INNER // SSM_HEAD_DIM
N_EXPERTS, TOP_K = 8, 2
EPS = 1e-6
NEG_INF = -1e30

LANES = 128
SUBLANES = 8
VMEM_LIMIT_BYTES = 56 * 1024 * 1024
SSD_CHUNK = 128
MOE_ROWS = 1024
NA_ROW_UNROLL = 4
DA_KEY_CHUNK = 512
DA_STREAMS = 4
LOG2E = math.log2(math.e)


def _cparams(sem):
    return pltpu.CompilerParams(dimension_semantics=sem, vmem_limit_bytes=VMEM_LIMIT_BYTES)


def _dot(a, b):
    return jnp.dot(a, b, preferred_element_type=F32)


def _dot_nt(a, b):
    return lax.dot_general(a, b, (((1,), (1,)), ((), ())), preferred_element_type=F32)


def _silu(x):
    return x * jax.nn.sigmoid(x)


def _gmm_body(bexp_ref, neww_ref, mod_ref, nv_ref, x_ref, *rest, nk, n_w, epi, res_first_blocks, w_transposed):
    del bexp_ref, mod_ref
    w_refs, rest = rest[:n_w], rest[n_w:]
    if epi == "bias":
        bias_ref, rest = rest[0], rest[1:]
    elif epi == "resid":
        res_ref, rest = rest[0], rest[1:]
        if res_first_blocks is not None:
            res2_ref, rest = rest[0], rest[1:]
        gate_ref, rest = rest[0], rest[1:]
    o_ref, rest = rest[0], rest[1:]
    wbf_refs, acc_refs = rest[:n_w], rest[n_w:]
    i = pl.program_id(1)
    k = pl.program_id(2)

    def finalize(vals):
        if epi == "swiglu":
            o_ref[...] = (_silu(vals[0]) * vals[1]).astype(o_ref.dtype)
        elif epi == "bias":
            o_ref[...] = (vals[0] + bias_ref[...]).astype(o_ref.dtype)
        elif epi == "resid":
            res = res_ref[...]
            if res_first_blocks is not None:
                res = jnp.where(i < res_first_blocks, res, res2_ref[...])
            o_ref[...] = (res + gate_ref[...] * vals[0]).astype(o_ref.dtype)
        elif epi == "sigmoid":
            o_ref[...] = jax.nn.sigmoid(vals[0]).astype(o_ref.dtype)
        else:
            o_ref[...] = vals[0].astype(o_ref.dtype)

    @pl.when(jnp.logical_and(i >= nv_ref[0], k == nk - 1))
    def _():
        o_ref[...] = jnp.zeros_like(o_ref)

    @pl.when(i < nv_ref[0])
    def _():
        @pl.when(neww_ref[i] == 1)
        def _():
            for w_ref, wbf in zip(w_refs, wbf_refs):
                w = w_ref[...]
                wbf[k] = (w.T if w_transposed else w).astype(BF16)

        x = x_ref[...]
        parts = [_dot(x, wbf[k]) for wbf in wbf_refs]
        if nk == 1:
            finalize(parts)
        else:
            @pl.when(k == 0)
            def _():
                for acc, p in zip(acc_refs, parts):
                    acc[...] = p

            @pl.when(k > 0)
            def _():
                for acc, p in zip(acc_refs, parts):
                    acc[...] += p

            @pl.when(k == nk - 1)
            def _():
                finalize([acc[...] for acc in acc_refs])


def _gmm(x, ws, bexp, *, tm, tn, nk=1, n_cols=None, epi="plain", out_dtype=F32, bias=None, res=None,
         res2=None, mod=None, mod_idx=None, mod_part=0, n_rows=None, n_used=None, w_buffers=2, w_rows=None):
    m_rows = x.shape[0] if n_rows is None else n_rows
    k_dim = x.shape[1]
    n_dim = ws[0].shape[2] if n_cols is None else n_cols
    n_w = len(ws)
    tk = k_dim // nk
    nb = m_rows // tm
    res_first_blocks = None
    assert m_rows % tm == 0 and n_dim % tn == 0 and k_dim % nk == 0
    bexp = bexp.astype(jnp.int32)
    neww = jnp.concatenate([jnp.ones((1,), jnp.int32), (bexp[1:] != bexp[:-1]).astype(jnp.int32)])
    if mod_idx is None:
        mod_idx = jnp.zeros((nb,), jnp.int32)
    if n_used is None:
        n_used = jnp.full((1,), nb, jnp.int32)

    rb = lambda i, nv: jnp.minimum(i, nv[0] - 1)
    x_spec = pl.BlockSpec((tm, tk), lambda j, i, k, be, nw, md, nv: (rb(i, nv), k))
    kb_of = lambda i, k, nw, nv: jnp.where(nw[rb(i, nv)] == 1, k, nk - 1)
    if w_rows is None:
        w_block = (None, tk, tn)
        w_map = lambda j, i, k, be, nw, md, nv: (be[rb(i, nv)], kb_of(i, k, nw, nv), j)
    else:
        n_total = ws[0].shape[1]
        assert n_total % SUBLANES == 0 and w_rows % SUBLANES == 0 and tn % SUBLANES == 0
        ws = [w.reshape(-1, k_dim) for w in ws]
        w_block = (pl.Element(tn), pl.Element(tk))
        w_map = lambda j, i, k, be, nw, md, nv: (
            pl.multiple_of(be[rb(i, nv)] * n_total + w_rows + j * tn, SUBLANES),
            pl.multiple_of(kb_of(i, k, nw, nv) * tk, LANES))
    if w_buffers == 2:
        w_spec = pl.BlockSpec(w_block, w_map)
    else:
        w_spec = pl.BlockSpec(w_block, w_map, pipeline_mode=pl.Buffered(w_buffers))
    o_spec = pl.BlockSpec((tm, tn), lambda j, i, k, be, nw, md, nv: (i, j))
    in_specs = [x_spec] + [w_spec] * n_w
    args = [x] + list(ws)
    if epi == "bias":
        in_specs.append(pl.BlockSpec((1, tn), lambda j, i, k, be, nw, md, nv: (0, j)))
        args.append(bias)
    elif epi == "resid":
        part_off = mod_part * (n_dim // tn)
        if res2 is None:
            in_specs.append(o_spec)
            args.append(res)
        else:
            res_first_blocks = res.shape[0] // tm
            in_specs.append(pl.BlockSpec(
                (tm, tn), lambda j, i, k, be, nw, md, nv: (jnp.minimum(i, res_first_blocks - 1), j)))
            in_specs.append(pl.BlockSpec(
                (tm, tn), lambda j, i, k, be, nw, md, nv: (jnp.maximum(i - res_first_blocks, 0), j)))
            args += [res, res2]
        in_specs.append(pl.BlockSpec((None, 1, tn),
                                     lambda j, i, k, be, nw, md, nv: (md[rb(i, nv)], 0, part_off + j)))
        args.append(mod)
    scratch = [pltpu.VMEM((nk, tk, tn), BF16) for _ in range(n_w)]
    if nk > 1:
        scratch += [pltpu.VMEM((tm, tn), F32) for _ in range(n_w)]
    grid_spec = pltpu.PrefetchScalarGridSpec(
        num_scalar_prefetch=4, grid=(n_dim // tn, nb, nk), in_specs=in_specs, out_specs=o_spec,
        scratch_shapes=scratch)
    return pl.pallas_call(
        functools.partial(_gmm_body, nk=nk, n_w=n_w, epi=epi, res_first_blocks=res_first_blocks,
                          w_transposed=w_rows is not None),
        grid_spec=grid_spec,
        out_shape=jax.ShapeDtypeStruct((m_rows, n_dim), out_dtype),
        compiler_params=_cparams(("arbitrary", "arbitrary", "arbitrary")),
        name="gmm_" + epi,
    )(bexp, neww, mod_idx.astype(jnp.int32), n_used.astype(jnp.int32), *args)


def _split2(v):
    hi = v.astype(BF16)
    lo = (v - hi.astype(F32)).astype(BF16)
    return hi, lo


def _norm_body(x_ref, *rest, modulate, router, first_blocks):
    if first_blocks is not None:
        x2_ref, rest = rest[0], rest[1:]
    g_ref, rest = rest[0], rest[1:]
    if modulate:
        shift_ref, scale_ref, rest = rest[0], rest[1], rest[2:]
    if router:
        rw_ref, rb_ref, rest = rest[0], rest[1], rest[2:]
    o_ref = rest[0]
    x = x_ref[...]
    if first_blocks is not None:
        x = jnp.where(pl.program_id(0) < first_blocks, x, x2_ref[...])
    y = x * lax.rsqrt(jnp.mean(x * x, axis=-1, keepdims=True) + EPS) * g_ref[...]
    if modulate:
        y = y * (1.0 + scale_ref[...]) + shift_ref[...]
    o_ref[...] = y.astype(o_ref.dtype)
    if router:
        rt_ref = rest[1]
        y_hi, y_lo = _split2(y)
        w_hi, w_lo = _split2(rw_ref[...])
        logits = _dot(y_hi, w_hi) + _dot(y_lo, w_hi) + _dot(y_hi, w_lo) + rb_ref[...]
        lane = lax.broadcasted_iota(jnp.int32, logits.shape, 1).astype(F32)
        lg = jnp.where(lane < N_EXPERTS, logits, -jnp.inf)
        m1 = jnp.max(lg, axis=-1, keepdims=True)
        i1 = jnp.min(jnp.where(lg == m1, lane, float(LANES)), axis=-1, keepdims=True)
        lg2 = jnp.where(lane == i1, -jnp.inf, lg)
        m2 = jnp.max(lg2, axis=-1, keepdims=True)
        i2 = jnp.min(jnp.where(lg2 == m2, lane, float(LANES)), axis=-1, keepdims=True)
        e2 = jnp.exp(m2 - m1)
        g1 = 1.0 / (1.0 + e2)
        rt = jnp.where(lane == 0, i1, jnp.where(lane == 1, i2, jnp.where(lane == 2, g1, jnp.where(lane == 3, e2 * g1, 0.0))))
        rt_ref[...] = rt


def _norm(x, g, *, tm, n_rows, mod=None, mod_idx_fn=None, parts=(0, 1), out_dtype=BF16, router=None, x2=None):
    d = x.shape[1]
    nb = n_rows // tm
    row = pl.BlockSpec((tm, d), lambda i: (i, 0))
    first_blocks = None if x2 is None else x.shape[0] // tm
    if x2 is None:
        in_specs, args = [row], [x]
    else:
        in_specs, args = list(_two_source_specs((tm, d), first_blocks, lambda i: i)), [x, x2]
    in_specs.append(pl.BlockSpec((1, d), lambda i: (0, 0)))
    args.append(g.reshape(1, d))
    modulate = mod is not None
    if modulate:
        for p in parts:
            in_specs.append(pl.BlockSpec((None, 1, d), lambda i, p=p: (mod_idx_fn(i), 0, p)))
            args.append(mod)
    out_shape = [jax.ShapeDtypeStruct((n_rows, d), out_dtype)]
    out_specs = [row]
    if router is not None:
        rw, rb = router
        in_specs += [pl.BlockSpec((d, LANES), lambda i: (0, 0)), pl.BlockSpec((1, LANES), lambda i: (0, 0))]
        args += [rw, rb]
        out_shape.append(jax.ShapeDtypeStruct((n_rows, LANES), F32))
        out_specs.append(pl.BlockSpec((tm, LANES), lambda i: (i, 0)))
    res = pl.pallas_call(
        functools.partial(_norm_body, modulate=modulate, router=router is not None, first_blocks=first_blocks),
        grid=(nb,), in_specs=in_specs, out_specs=out_specs, out_shape=out_shape,
        compiler_params=_cparams(("arbitrary",)), name="rmsnorm",
    )(*args)
    return res if router is not None else res[0]


def _rope_body(q_ref, k_ref, c_ref, sm_ref, sp_ref, qo_ref, ko_ref, *, q_scale):
    c, sm, sp = c_ref[...], sm_ref[...], sp_ref[...]
    width = q_ref.shape[1]
    for src, dst, scale in ((q_ref, qo_ref, q_scale), (k_ref, ko_ref, 1.0)):
        for g in range(width // LANES):
            sl = slice(g * LANES, (g + 1) * LANES)
            x = src[:, sl]
            y = x * c + pltpu.roll(x, LANES - 16, 1) * sm + pltpu.roll(x, 16, 1) * sp
            dst[:, sl] = (y * scale).astype(dst.dtype)


def _rope_tables(seq, n_id_rows):
    t = jnp.arange(seq)
    pos = jnp.stack([t // GRID_W, t % GRID_W], axis=-1).astype(F32)
    n_freq = DA_HEAD_DIM // 4
    inv_freq = ROPE_BASE ** (-jnp.arange(n_freq, dtype=F32) / n_freq)
    ang = pos[:, :, None] * inv_freq
    d = np.arange(LANES) % DA_HEAD_DIM
    kind, which, f = d // 32, (d % 32) // 16, d % 16
    a = ang[:, kind, f]
    cos, sin = jnp.cos(a), jnp.sin(a)
    sm = jnp.where(which == 0, -sin, 0.0)
    sp = jnp.where(which == 1, sin, 0.0)
    pad = lambda v, fill: jnp.concatenate([v, jnp.full((n_id_rows, LANES), fill, F32)], axis=0)
    return pad(cos, 1.0), pad(sm, 0.0), pad(sp, 0.0)


def _rope(proj, tabs, *, tm, n_rows, n_lat_rows, seq, q_col, k_col, width):
    lat_blocks = n_lat_rows // tm
    per_seq = seq // tm
    tab_idx = lambda i: (jnp.where(i < lat_blocks, i % per_seq, per_seq), 0)
    out = jax.ShapeDtypeStruct((n_rows, width), BF16)
    return pl.pallas_call(
        functools.partial(_rope_body, q_scale=DA_HEAD_DIM ** -0.5 * LOG2E),
        grid=(n_rows // tm,),
        in_specs=[pl.BlockSpec((tm, width), lambda i: (i, q_col // width)),
                  pl.BlockSpec((tm, width), lambda i: (i, k_col // width)),
                  pl.BlockSpec((tm, LANES), tab_idx), pl.BlockSpec((tm, LANES), tab_idx),
                  pl.BlockSpec((tm, LANES), tab_idx)],
        out_specs=[pl.BlockSpec((tm, width), lambda i: (i, 0))] * 2,
        out_shape=[out, out], compiler_params=_cparams(("arbitrary",)), name="rope",
    )(proj, proj, *tabs)


def _na_body(q_ref, k_ref, v_ref, kc_ref, vc_ref, tb_ref, o_ref, kb, vb, kcb, vcb, *, rows, wh):
    kb[...] = k_ref[...].astype(BF16)
    vb[...] = v_ref[...].astype(BF16)
    kcb[...] = kc_ref[...].astype(BF16)
    vcb[...] = vc_ref[...].astype(BF16)
    lane = lax.broadcasted_iota(jnp.int32, (GRID_W, LANES), 1)
    first = lane < NA_HEAD_DIM
    scale = NA_HEAD_DIM ** -0.5

    def body(grp, carry):
        rr = [grp * NA_ROW_UNROLL + j for j in range(NA_ROW_UNROLL)]
        rs = [jnp.clip(r - wh // 2, 0, rows - wh) for r in rr]
        row_of = lambda r: pl.ds(pl.multiple_of(r * GRID_W, GRID_W), GRID_W)
        win_of = lambda r0: pl.ds(pl.multiple_of(r0 * GRID_W, GRID_W), wh * GRID_W)
        qq = []
        for r in rr:
            q = q_ref[row_of(r), :] * scale
            qq.append(jnp.concatenate([jnp.where(first, q, 0.0), jnp.where(first, 0.0, q)], axis=0).astype(BF16))
        s_l = [_dot_nt(x, kb[win_of(r0), :]) + tb_ref[r0 - r + (NA_WIN_H - 1)] for x, r, r0 in zip(qq, rr, rs)]
        s_c = [_dot_nt(x, kcb[...]) for x in qq]
        m = [jnp.maximum(jnp.max(a, axis=-1, keepdims=True), jnp.max(b, axis=-1, keepdims=True))
             for a, b in zip(s_l, s_c)]
        p_l = [jnp.exp(a - mm) for a, mm in zip(s_l, m)]
        p_c = [jnp.exp(b - mm) for b, mm in zip(s_c, m)]
        den = [jnp.sum(a, axis=-1, keepdims=True) + jnp.sum(b, axis=-1, keepdims=True) for a, b in zip(p_l, p_c)]
        o = [_dot(a.astype(BF16), vb[win_of(r0), :]) + _dot(b.astype(BF16), vcb[...])
             for a, b, r0 in zip(p_l, p_c, rs)]
        for r, oo, dd in zip(rr, o, den):
            oo = oo / dd
            o_ref[row_of(r), :] = jnp.where(first, oo[:GRID_W], oo[GRID_W:]).astype(o_ref.dtype)
        return carry

    assert rows % NA_ROW_UNROLL == 0
    lax.fori_loop(0, rows // NA_ROW_UNROLL, body, 0, unroll=2)


def _na_bias_table(rpb, rows):
    wh = min(NA_WIN_H, rows)
    qc = np.arange(GRID_W)[:, None]
    kc = np.arange(GRID_W)[None, :]
    ws = np.clip(qc - NA_WIN_W // 2, 0, GRID_W - NA_WIN_W)
    ok = (kc >= ws) & (kc < ws + NA_WIN_W)
    n_drow, n_dcol = 2 * NA_WIN_H - 1, 2 * NA_WIN_W - 1
    off = GRID_W - NA_WIN_W
    line = jnp.full((NA_HEADS, n_drow, 2 * GRID_W), NEG_INF, F32).at[:, :, off:off + n_dcol].set(rpb.astype(F32))
    span = 2 * GRID_W - 1
    t = jnp.tile(line, (1, 1, GRID_W))[:, :, :GRID_W * span].reshape(NA_HEADS, n_drow, GRID_W, span)
    t = jnp.where(ok, t[:, :, :, GRID_W - 1:], NEG_INF)
    n_d0 = NA_WIN_H
    t = jnp.stack([t[:, d0:d0 + wh] for d0 in range(n_d0)], axis=1)
    t = t.transpose(0, 1, 3, 2, 4).reshape(NA_HEADS // 2, 2, n_d0, GRID_W, wh * GRID_W)
    return t.transpose(0, 2, 1, 3, 4).reshape(NA_HEADS // 2, n_d0, 2 * GRID_W, wh * GRID_W)


def _na(proj, rpb, *, batch, seq, ctx_len, q_col, k_col, v_col):
    rows = seq // GRID_W
    wh = min(NA_WIN_H, rows)
    pairs = NA_HEADS // 2
    tb = _na_bias_table(rpb, rows)
    ctx0 = batch * seq // ctx_len
    lat = lambda col: pl.BlockSpec((seq, LANES), lambda b, p: (b, col // LANES + p))
    ctx = lambda col: pl.BlockSpec((ctx_len, LANES), lambda b, p: (ctx0 + b, col // LANES + p))
    return pl.pallas_call(
        functools.partial(_na_body, rows=rows, wh=wh),
        grid=(batch, pairs),
        in_specs=[lat(q_col), lat(k_col), lat(v_col), ctx(k_col), ctx(v_col),
                  pl.BlockSpec((None, NA_WIN_H, 2 * GRID_W, wh * GRID_W), lambda b, p: (p, 0, 0, 0))],
        out_specs=pl.BlockSpec((seq, LANES), lambda b, p: (b, p)),
        out_shape=jax.ShapeDtypeStruct((batch * seq, pairs * LANES), BF16),
        scratch_shapes=[pltpu.VMEM((seq, LANES), BF16), pltpu.VMEM((seq, LANES), BF16),
                        pltpu.VMEM((ctx_len, LANES), BF16), pltpu.VMEM((ctx_len, LANES), BF16)],
        compiler_params=_cparams(("arbitrary", "arbitrary")), name="na_attn",
    )(proj, proj, proj, proj, proj, tb)


def _ctx_attn_body(q_ref, k_ref, v_ref, o_ref):
    kb = k_ref[...].astype(BF16)
    vb = v_ref[...].astype(BF16)
    q = q_ref[...] * (NA_HEAD_DIM ** -0.5)
    lane = lax.broadcasted_iota(jnp.int32, q.shape, 1)
    first = lane < NA_HEAD_DIM
    outs = []
    for hh in range(2):
        qm = jnp.where(first if hh == 0 else jnp.logical_not(first), q, 0.0).astype(BF16)
        s = _dot_nt(qm, kb)
        p = jnp.exp(s - jnp.max(s, axis=-1, keepdims=True))
        outs.append(_dot(p.astype(BF16), vb) / jnp.sum(p, axis=-1, keepdims=True))
    o_ref[...] = jnp.where(first, outs[0], outs[1]).astype(o_ref.dtype)


def _ctx_attn(proj, *, batch, seq, ctx_len, q_col, k_col, v_col):
    pairs = NA_HEADS // 2
    ctx0 = batch * seq // ctx_len
    ctx = lambda col: pl.BlockSpec((ctx_len, LANES), lambda b, p: (ctx0 + b, col // LANES + p))
    return pl.pallas_call(
        _ctx_attn_body, grid=(batch, pairs),
        in_specs=[ctx(q_col), ctx(k_col), ctx(v_col)],
        out_specs=pl.BlockSpec((ctx_len, LANES), lambda b, p: (b, p)),
        out_shape=jax.ShapeDtypeStruct((batch * ctx_len, pairs * LANES), BF16),
        compiler_params=_cparams(("arbitrary", "arbitrary")), name="ctx_attn",
    )(proj, proj, proj)


def _da_streams(q_ref, kl_ref, kc_ref, vlb, vcb, g_ref, o_ref, s_ref, lam, lam_init):
    n_st = DA_STREAMS
    hq = q_ref.shape[0] // n_st
    ck = DA_KEY_CHUNK
    chunks = [(kl_ref, vlb, off, ck, off) for off in range(0, kl_ref.shape[0], ck)]
    chunks.append((kc_ref, vcb, 0, kc_ref.shape[0], kl_ref.shape[0]))
    lane = lax.broadcasted_iota(jnp.int32, (hq, LANES), 1)
    lane_tiles = lambda x: [x[:, i * LANES:(i + 1) * LANES] for i in range(x.shape[1] // LANES)]

    def stacked_q(st):
        q = q_ref[st * hq:(st + 1) * hq, :]
        zero = jnp.zeros_like(q)
        return jnp.concatenate([jnp.where(lane < DA_HEAD_DIM, q, zero), jnp.where(lane >= DA_HEAD_DIM, q, zero)], axis=0)

    def scores(st, qq, chunk, m_run):
        k_ref, _, off, n, c0 = chunk
        s = _dot_nt(qq, k_ref[off:off + n, :])
        s_ref[st, :, c0:c0 + n] = s
        return functools.reduce(jnp.maximum, lane_tiles(s), m_run)

    def numerators(st, chunk, m, l_run):
        _, _, _, n, c0 = chunk
        p = jnp.exp2(s_ref[st, :, c0:c0 + n] - m)
        s_ref[st, :, c0:c0 + n] = p
        return functools.reduce(jnp.add, lane_tiles(p), l_run)

    def values(st, chunk, ratio, acc):
        _, v_ref, off, n, c0 = chunk
        a = (s_ref[st, 0:hq, c0:c0 + n] - ratio * s_ref[st, hq:2 * hq, c0:c0 + n]).astype(BF16)
        return acc + _dot(a, v_ref[off:off + n, :])

    def finish(st, acc, den):
        o = acc / den[:hq]
        o = o * lax.rsqrt(jnp.mean(o * o, axis=-1, keepdims=True) + EPS) * g_ref[...]
        o_ref[st * hq:(st + 1) * hq, :] = (o * (1.0 - lam_init)).astype(o_ref.dtype)

    neg = jnp.full((2 * hq, LANES), -jnp.inf, F32)
    zero_l = jnp.zeros((2 * hq, LANES), F32)
    zero_o = jnp.zeros((hq, LANES), F32)
    row_max = lambda m_run: jnp.max(m_run, axis=-1, keepdims=True)
    row_sum = lambda l_run: jnp.sum(l_run, axis=-1, keepdims=True)
    m, den, ratio = {}, {}, {}
    for p in range(n_st + 2):
        s1, s2, s3 = p, p - 1, p - 2
        qq = stacked_q(s1) if s1 < n_st else None
        m_run, l_run, acc = neg, zero_l, zero_o
        for ch in chunks:
            if 0 <= s3 < n_st:
                acc = values(s3, ch, ratio[s3], acc)
            if 0 <= s2 < n_st:
                l_run = numerators(s2, ch, m[s2], l_run)
            if s1 < n_st:
                m_run = scores(s1, qq, ch, m_run)
        if 0 <= s3 < n_st:
            finish(s3, acc, den[s3])
        if 0 <= s2 < n_st:
            den[s2] = row_sum(l_run)
            ratio[s2] = lam * den[s2][:hq] / den[s2][hq:]
        if s1 < n_st:
            m[s1] = row_max(m_run)


def _da_body(lv_ref, q_ref, *rest, lam_init, has_lat):
    if has_lat:
        kl_ref, vl_ref, rest = rest[0], rest[1], rest[2:]
    kc_ref, vc_ref, g_ref, o_ref = rest[:4]
    scr = rest[4:]
    t = pl.program_id(2)
    if has_lat:
        vlb, vcb, s_ref = scr

        @pl.when(t == 0)
        def _():
            vlb[...] = vl_ref[...].astype(BF16)
            vcb[...] = vc_ref[...].astype(BF16)
    else:
        (vcb,) = scr
        vcb[...] = vc_ref[...].astype(BF16)

    lv = lv_ref[...]
    lam = (jnp.exp(jnp.sum(lv[0:1] * lv[1:2], axis=-1, keepdims=True))
           - jnp.exp(jnp.sum(lv[2:3] * lv[3:4], axis=-1, keepdims=True)) + lam_init)
    if has_lat:
        _da_streams(q_ref, kl_ref, kc_ref, vlb, vcb, g_ref, o_ref, s_ref, lam, lam_init)
        return
    q = q_ref[...]
    tq = q.shape[0]
    lane = lax.broadcasted_iota(jnp.int32, q.shape, 1)
    zero = jnp.zeros_like(q)
    qq = jnp.concatenate([jnp.where(lane < DA_HEAD_DIM, q, zero), jnp.where(lane >= DA_HEAD_DIM, q, zero)], axis=0)
    s_c = _dot_nt(qq, kc_ref[...])
    m = jnp.max(s_c, axis=-1, keepdims=True)
    if has_lat:
        s_l = _dot_nt(qq, kl_ref[...])
        m = jnp.maximum(m, jnp.max(s_l, axis=-1, keepdims=True))
        p_l = jnp.exp2(s_l - m)
    p_c = jnp.exp2(s_c - m)
    den = jnp.sum(p_c, axis=-1, keepdims=True)
    if has_lat:
        den = den + jnp.sum(p_l, axis=-1, keepdims=True)
    ratio = lam * den[:tq] / den[tq:]
    o = _dot((p_c[:tq] - ratio * p_c[tq:]).astype(BF16), vcb[...])
    if has_lat:
        o = o + _dot((p_l[:tq] - ratio * p_l[tq:]).astype(BF16), vlb[...])
    o = o / den[:tq]
    o = o * lax.rsqrt(jnp.mean(o * o, axis=-1, keepdims=True) + EPS) * g_ref[...]
    o_ref[...] = (o * (1.0 - lam_init)).astype(o_ref.dtype)


def _da(q_r, k_r, proj, lam_vec, subln_g, lam_init, *, batch, seq, ctx_len, v_col, tq, latent):
    ctx0 = batch * seq // ctx_len
    vcb0 = v_col // LANES
    g2 = subln_g.reshape(1, 2 * DA_HEAD_DIM).astype(F32)
    small = [pl.BlockSpec((4, DA_HEAD_DIM), lambda b, h, t: (0, 0))]
    gspec = pl.BlockSpec((1, LANES), lambda b, h, t: (0, 0))
    kc = pl.BlockSpec((ctx_len, LANES), lambda b, h, t: (ctx0 + b, h))
    vc = pl.BlockSpec((ctx_len, LANES), lambda b, h, t: (ctx0 + b, vcb0 + h))
    if latent:
        nq = seq // tq
        qs = pl.BlockSpec((tq, LANES), lambda b, h, t: (b * nq + t, h))
        kl = pl.BlockSpec((seq, LANES), lambda b, h, t: (b, h))
        vl = pl.BlockSpec((seq, LANES), lambda b, h, t: (b, vcb0 + h))
        in_specs = small + [qs, kl, vl, kc, vc, gspec]
        args = (lam_vec, q_r, k_r, proj, k_r, proj, g2)
        assert seq % DA_KEY_CHUNK == 0 and tq % DA_STREAMS == 0
        scratch = [pltpu.VMEM((seq, LANES), BF16), pltpu.VMEM((ctx_len, LANES), BF16),
                   pltpu.VMEM((DA_STREAMS, 2 * tq // DA_STREAMS, seq + ctx_len), F32)]
        n_out = batch * seq
    else:
        nq = ctx_len // tq
        cq0 = batch * seq // tq
        qs = pl.BlockSpec((tq, LANES), lambda b, h, t: (cq0 + b * nq + t, h))
        in_specs = small + [qs, kc, vc, gspec]
        args = (lam_vec, q_r, k_r, proj, g2)
        scratch = [pltpu.VMEM((ctx_len, LANES), BF16)]
        n_out = batch * ctx_len
    return pl.pallas_call(
        functools.partial(_da_body, lam_init=lam_init, has_lat=latent),
        grid=(batch, DA_HEADS, nq), in_specs=in_specs,
        out_specs=pl.BlockSpec((tq, LANES), lambda b, h, t: (b * nq + t, h)),
        out_shape=jax.ShapeDtypeStruct((n_out, DA_HEADS * LANES), BF16),
        scratch_shapes=scratch,
        compiler_params=_cparams(("arbitrary", "arbitrary", "arbitrary")),
        name="diff_attn_lat" if latent else "diff_attn_ctx",
    )(*args)


def _ssd_body(x0_ref, x1_ref, bc_ref, x0p_ref, x1p_ref, bcp_ref, x0n_ref, x1n_ref, bcn_ref, dt_ref,
              cw_ref, cb_ref, prm_ref, dsk_ref, ex_ref, init_ref, y_ref, sout_ref, st_ref, u_ref, *, nc):
    q_len = SSD_CHUNK
    dirn = pl.program_id(1)
    z = pl.program_id(2)
    zz = jnp.where(dirn == 0, z, nc - 1 - z)
    fwd = dirn == 0
    half = SSM_D_INNER // SSM_GROUPS
    heads_per_group = SSM_HEADS // SSM_GROUPS

    @pl.when(z == 0)
    def _():
        st_ref[...] = init_ref[...]

    def conv_piece(main_ref, prev_ref, next_ref, c0):
        width = main_ref.shape[1]
        prev = jnp.where(zz == 0, 0.0, prev_ref[...])
        nxt = jnp.where(zz == nc - 1, 0.0, next_ref[...])
        ext = jnp.concatenate([prev, main_ref[...], nxt], axis=0)
        n_ext = ext.shape[0]
        acc = jnp.zeros((q_len, width), F32) + cb_ref[:, c0:c0 + width]
        for tap in range(SSM_CONV_W):
            sh = (SSM_CONV_W // 2 - tap) % n_ext
            e = ext if sh == 0 else pltpu.roll(ext, sh, 0)
            acc = acc + e[SUBLANES:SUBLANES + q_len] * cw_ref[tap:tap + 1, c0:c0 + width]
        return _silu(acc)

    @pl.when(fwd)
    def _():
        u_ref[zz, :, 0:half] = conv_piece(x0_ref, x0p_ref, x0n_ref, 0)
        u_ref[zz, :, half:2 * half] = conv_piece(x1_ref, x1p_ref, x1n_ref, half)
        u_ref[zz, :, SSM_D_INNER:] = conv_piece(bc_ref, bcp_ref, bcn_ref, SSM_D_INNER)

    xs = (u_ref[zz, :, 0:half], u_ref[zz, :, half:2 * half])
    bcm = u_ref[zz, :, SSM_D_INNER:]

    raw = dt_ref[...]
    raw = jnp.where(fwd, raw, pltpu.roll(raw, LANES - SSM_HEADS, 1))
    xb = raw + prm_ref[0:1, :]
    dtv = jnp.maximum(xb, 0.0) + jnp.log1p(jnp.exp(-jnp.abs(xb)))
    a = dtv * (-jnp.exp(prm_ref[1:2, :]))
    rowi = lax.broadcasted_iota(jnp.int32, (q_len, LANES), 0)
    coli = lax.broadcasted_iota(jnp.int32, (q_len, LANES), 1)
    cum = a
    sft = 1
    while sft < q_len:
        cum = cum + jnp.where(rowi >= sft, pltpu.roll(cum, sft, 0), 0.0)
        sft *= 2
    tot = cum[q_len - 1:q_len, :]
    g = jnp.where(fwd, cum, cum - a)
    e_g = jnp.exp(g)
    e_tg = jnp.exp(tot - g)
    w_state = jnp.where(fwd, e_tg, e_g)
    w_yoff = jnp.where(fwd, e_g, e_tg)
    e_tot = jnp.exp(tot)
    gs = g * jnp.where(fwd, LOG2E, -LOG2E)
    gs_t = gs.T
    tri = jnp.where(fwd, rowi - coli, coli - rowi) >= 0

    stack = jnp.concatenate([dtv, dtv * w_state, w_yoff, jnp.broadcast_to(e_tot, (SUBLANES, LANES))], axis=0)
    s_hi, s_lo = _split2(stack)
    ex = ex_ref[...]
    expd = _dot(s_hi, ex) + _dot(s_lo, ex)
    dt_e, dts_e, wy_e = expd[0:q_len], expd[q_len:2 * q_len], expd[2 * q_len:3 * q_len]
    tot_e = expd[3 * q_len:3 * q_len + 1]

    lane = lax.broadcasted_iota(jnp.int32, (q_len, LANES), 1)
    first = lane < SSM_HEAD_DIM
    for grp in range(SSM_GROUPS):
        x_g = xs[grp]
        csl = slice(grp * half, (grp + 1) * half)
        b_g = bcm[:, grp * SSM_STATE:(grp + 1) * SSM_STATE]
        c_g = bcm[:, (SSM_GROUPS + grp) * SSM_STATE:(SSM_GROUPS + grp + 1) * SSM_STATE].astype(BF16)
        xdt = (x_g * dt_e[:, csl]).astype(BF16)
        xdts = (x_g * dts_e[:, csl]).astype(BF16)
        cb = _dot_nt(c_g, b_g.astype(BF16))
        st_g = st_ref[:, csl]
        y_g = _dot(c_g, st_g.astype(BF16)) * wy_e[:, csl] + dsk_ref[:, csl] * x_g
        for pr in range(heads_per_group // 2):
            outs = []
            for hh in range(2):
                col = grp * heads_per_group + 2 * pr + hh
                seg = gs[:, col:col + 1] - gs_t[col:col + 1, :]
                mat = (cb * jnp.where(tri, jnp.exp2(seg), 0.0)).astype(BF16)
                outs.append(_dot(mat, xdt[:, pr * LANES:(pr + 1) * LANES]))
            lo = grp * half + pr * LANES
            y_ref[:, lo:lo + LANES] = y_g[:, pr * LANES:(pr + 1) * LANES] + jnp.where(first, outs[0], outs[1])
        st_ref[:, csl] = tot_e[:, csl] * st_g + _dot(b_g.T.astype(BF16), xdts)

    @pl.when(z == nc - 1)
    def _():
        sout_ref[...] = st_ref[...]


def _ssd(proj, tail, conv_w, conv_b, dt_bias, a_log, d_skip, init, *, batch, n_tok, row0, x_col, dt_col, tail_col):
    q_len = SSD_CHUNK
    nc = n_tok // q_len
    half = SSM_D_INNER // SSM_GROUPS
    conv_dim = SSM_D_INNER + 2 * SSM_GROUPS * SSM_STATE
    rb0 = row0 // q_len
    hb = q_len // SUBLANES
    n_halo = proj.shape[0] // SUBLANES
    zz = lambda d, z: jnp.where(d == 0, z, nc - 1 - z)
    rb = lambda b, d, z: rb0 + b * nc + zz(d, z)
    rbx = lambda b, d, z: rb0 + b * nc + jnp.where(d == 0, z, nc - 1)
    main = lambda c: pl.BlockSpec((q_len, half), lambda b, d, z: (rbx(b, d, z), c))
    prev = lambda c: pl.BlockSpec((SUBLANES, half), lambda b, d, z: (jnp.maximum(rbx(b, d, z) * hb - 1, 0), c))
    nxt = lambda c: pl.BlockSpec((SUBLANES, half),
                                 lambda b, d, z: (jnp.minimum(rbx(b, d, z) * hb + hb, n_halo - 1), c))
    bc_col = x_col + SSM_D_INNER
    assert bc_col == tail_col and conv_dim - SSM_D_INNER == half and (dt_col - tail_col) % LANES == 0
    cols = [x_col // half, x_col // half + 1, (bc_col - tail_col) // half]
    const2 = lambda shape: pl.BlockSpec(shape, lambda b, d, z: (0, 0))
    prm = jnp.zeros((2, SUBLANES, LANES), F32)
    prm = prm.at[:, 0, :SSM_HEADS].set(dt_bias.astype(F32)).at[:, 1, :SSM_HEADS].set(a_log.astype(F32))
    dsk = jnp.repeat(d_skip.astype(F32), SSM_HEAD_DIM, axis=-1).reshape(2, 1, SSM_D_INNER)
    ex = (np.arange(LANES)[:, None] == (np.arange(SSM_D_INNER) // SSM_HEAD_DIM)[None, :])
    ex = jnp.asarray(ex, BF16)
    cw = jnp.zeros((SUBLANES, conv_dim), F32).at[:SSM_CONV_W].set(conv_w.astype(F32))
    state_spec = pl.BlockSpec((None, None, SSM_STATE, SSM_D_INNER), lambda b, d, z: (b, d, 0, 0))
    y, s_out = pl.pallas_call(
        functools.partial(_ssd_body, nc=nc),
        grid=(batch, 2, nc),
        in_specs=[main(cols[0]), main(cols[1]), main(cols[2]), prev(cols[0]), prev(cols[1]), prev(cols[2]),
                  nxt(cols[0]), nxt(cols[1]), nxt(cols[2]),
                  pl.BlockSpec((q_len, LANES), lambda b, d, z: (rb(b, d, z), (dt_col - tail_col) // LANES)),
                  const2((SUBLANES, conv_dim)), const2((1, conv_dim)),
                  pl.BlockSpec((None, SUBLANES, LANES), lambda b, d, z: (d, 0, 0)),
                  pl.BlockSpec((None, 1, SSM_D_INNER), lambda b, d, z: (d, 0, 0)),
                  const2((LANES, SSM_D_INNER)), state_spec],
        out_specs=[pl.BlockSpec((None, q_len, SSM_D_INNER), lambda b, d, z: (d, b * nc + zz(d, z), 0)),
                   state_spec],
        out_shape=[jax.ShapeDtypeStruct((2, batch * n_tok, SSM_D_INNER), F32),
                   jax.ShapeDtypeStruct((batch, 2, SSM_STATE, SSM_D_INNER), F32)],
        scratch_shapes=[pltpu.VMEM((SSM_STATE, SSM_D_INNER), F32), pltpu.VMEM((nc, q_len, conv_dim), F32)],
        compiler_params=_cparams(("arbitrary", "arbitrary", "arbitrary")), name="ssd_scan",
    )(proj, proj, tail, proj, proj, tail, proj, proj, tail, tail, cw, conv_b.reshape(1, conv_dim).astype(F32),
      prm, dsk, ex, init)
    return y, s_out


def _ssm_norm_body(y_ref, z_ref, g_ref, o_ref):
    yz = (y_ref[0] + y_ref[1]) * _silu(z_ref[...])
    gw = SSM_D_INNER // SSM_GROUPS
    for grp in range(SSM_GROUPS):
        v = yz[:, grp * gw:(grp + 1) * gw]
        v = v * lax.rsqrt(jnp.mean(v * v, axis=-1, keepdims=True) + EPS)
        o_ref[:, grp * gw:(grp + 1) * gw] = (v * g_ref[:, grp * gw:(grp + 1) * gw]).astype(o_ref.dtype)


def _ssm_norm(y, proj, g, *, tm, row0, z_col):
    n = y.shape[1]
    zb0 = row0 // tm
    return pl.pallas_call(
        _ssm_norm_body, grid=(n // tm,),
        in_specs=[pl.BlockSpec((2, tm, SSM_D_INNER), lambda i: (0, i, 0)),
                  pl.BlockSpec((tm, SSM_D_INNER), lambda i: (zb0 + i, z_col // SSM_D_INNER)),
                  pl.BlockSpec((1, SSM_D_INNER), lambda i: (0, 0))],
        out_specs=pl.BlockSpec((tm, SSM_D_INNER), lambda i: (i, 0)),
        out_shape=jax.ShapeDtypeStruct((n, SSM_D_INNER), BF16),
        compiler_params=_cparams(("arbitrary",)), name="ssm_gated_norm",
    )(y, proj, g.reshape(1, SSM_D_INNER).astype(F32))


def _two_source_specs(block, first_blocks, row_block_of):
    first = pl.BlockSpec(block, lambda *g: (jnp.minimum(row_block_of(*g), first_blocks - 1), 0))
    second = pl.BlockSpec(block, lambda *g: (jnp.maximum(row_block_of(*g) - first_blocks, 0), 0))
    return first, second


def _merge_body(*refs, lat_blocks, has_ctx):
    n_src = 6 if has_ctx else 3
    src, (ga_ref, gb_ref, gc_ref, wa_ref, wb_ref, wc_ref, o_ref, wab, wbb, wcb) = refs[:n_src], refs[n_src:]
    i = pl.program_id(1)

    @pl.when(i == 0)
    def _():
        wab[...] = wa_ref[...].astype(BF16)
        wbb[...] = wb_ref[...].astype(BF16)
        wcb[...] = wc_ref[...].astype(BF16)

    if has_ctx:
        oa, ob, oc = (jnp.where(i < lat_blocks, src[2 * n][...], src[2 * n + 1][...]) for n in range(3))
    else:
        oa, ob, oc = (r[...] for r in src)
    acc = ga_ref[...].astype(F32) * _dot(oa, wab[...])
    acc = acc + gb_ref[...].astype(F32) * _dot(ob, wbb[...])
    acc = acc + gc_ref[...].astype(F32) * _dot(oc, wcb[...])
    o_ref[...] = acc.astype(o_ref.dtype)


def _merge(o_lat, o_ctx, gates, w_a, w_b, w_c, layer, *, tm, tn, n_rows):
    d = w_a.shape[2]
    nj = d // tn
    lat_blocks = o_lat[0].shape[0] // tm
    gate = lambda x: pl.BlockSpec((tm, tn), lambda j, i: (i, x * nj + j))
    wsp = lambda kk: pl.BlockSpec((None, kk, tn), lambda j, i: (layer, 0, j))
    widths = (w_a.shape[1], w_b.shape[1], w_c.shape[1])
    src_specs, src = [], []
    for n, kk in enumerate(widths):
        if o_ctx is None:
            src_specs.append(pl.BlockSpec((tm, kk), lambda j, i: (i, 0)))
            src.append(o_lat[n])
        else:
            src_specs += _two_source_specs((tm, kk), lat_blocks, lambda j, i: i)
            src += [o_lat[n], o_ctx[n]]
    return pl.pallas_call(
        functools.partial(_merge_body, lat_blocks=lat_blocks, has_ctx=o_ctx is not None), grid=(nj, n_rows // tm),
        in_specs=src_specs + [gate(0), gate(1), gate(2)] + [wsp(kk) for kk in widths],
        out_specs=pl.BlockSpec((tm, tn), lambda j, i: (i, j)),
        out_shape=jax.ShapeDtypeStruct((n_rows, d), BF16),
        scratch_shapes=[pltpu.VMEM((kk, tn), BF16) for kk in widths],
        compiler_params=_cparams(("arbitrary", "arbitrary")), name="branch_merge",
    )(*src, gates, gates, gates, w_a, w_b, w_c)


def _route(top_idx, blk):
    n = top_idx.shape[0]
    e_flat = top_idx.reshape(-1)
    onehot = (e_flat[:, None] == jnp.arange(N_EXPERTS, dtype=jnp.int32)[None, :]).astype(jnp.int32)
    counts = jnp.sum(onehot, axis=0)
    rank = jnp.sum((jnp.cumsum(onehot, axis=0) - onehot) * onehot, axis=1)
    padded = (counts + blk - 1) // blk * blk
    pad_end = jnp.cumsum(padded)
    pad_start = pad_end - padded
    dest = jnp.sum(onehot * pad_start[None, :], axis=1) + rank
    n_blocks = -(-(n * TOP_K) // blk) + N_EXPERTS
    cap = n_blocks * blk
    tok_flat = jnp.repeat(jnp.arange(n, dtype=jnp.int32), TOP_K)
    slot_tok = jnp.zeros((cap,), jnp.int32).at[dest].set(tok_flat)
    blk_start = jnp.arange(n_blocks, dtype=jnp.int32) * blk
    block_expert = jnp.clip(jnp.sum((pad_end[None, :] <= blk_start[:, None]).astype(jnp.int32), axis=1),
                            0, N_EXPERTS - 1)
    n_used = (pad_end[-1] // blk).astype(jnp.int32).reshape(1)
    return dest.astype(jnp.int32), slot_tok, block_expert, n_used


GATHER_UNROLL = 8


def _row_copy(src_hbm, row, buf, slot, sem):
    return pltpu.make_async_copy(src_hbm.at[pl.ds(row, 1), :], buf.at[pl.ds(slot, 1), :], sem)


def _gather_body(nv_ref, idx_ref, src_hbm, o_ref, buf, sem, *, rows):
    @pl.when(pl.program_id(0) >= nv_ref[0])
    def _():
        o_ref[...] = jnp.zeros_like(o_ref)

    @pl.when(pl.program_id(0) < nv_ref[0])
    def _():
        def issue(g, c):
            for u in range(GATHER_UNROLL):
                r = g * GATHER_UNROLL + u
                _row_copy(src_hbm, idx_ref[0, r], buf, r, sem).start(priority=u % 2)
            return c

        def drain(g, c):
            for u in range(GATHER_UNROLL):
                r = g * GATHER_UNROLL + u
                _row_copy(src_hbm, idx_ref[0, r], buf, r, sem).wait()
            return c

        lax.fori_loop(0, rows // GATHER_UNROLL, issue, 0)
        lax.fori_loop(0, rows // GATHER_UNROLL, drain, 0)
        o_ref[...] = buf[...].astype(o_ref.dtype)


def _gather_rows(src, idx, n_used, *, rows, out_dtype):
    n, d = src.shape
    nb = idx.shape[0] // rows
    assert rows % GATHER_UNROLL == 0
    blk = lambda i, nv: jnp.minimum(i, nv[0] - 1)
    grid_spec = pltpu.PrefetchScalarGridSpec(
        num_scalar_prefetch=1, grid=(nb,),
        in_specs=[pl.BlockSpec((None, 1, rows), lambda i, nv: (blk(i, nv), 0, 0), memory_space=pltpu.SMEM),
                  pl.BlockSpec(memory_space=pl.ANY)],
        out_specs=pl.BlockSpec((rows, d), lambda i, nv: (i, 0)),
        scratch_shapes=[pltpu.VMEM((rows, d), src.dtype), pltpu.SemaphoreType.DMA])
    return pl.pallas_call(
        functools.partial(_gather_body, rows=rows), grid_spec=grid_spec,
        out_shape=jax.ShapeDtypeStruct((nb * rows, d), out_dtype),
        compiler_params=_cparams(("arbitrary",)), name="moe_dispatch",
    )(n_used, idx.reshape(nb, 1, rows), src)


def _combine_body(dest_ref, h_ref, rt_ref, gate_ref, g_ref, y_hbm, o_ref, buf, sem, *, rows, norm):
    def issue(g, c):
        for u in range(GATHER_UNROLL):
            r = g * GATHER_UNROLL + u
            for k in range(TOP_K):
                _row_copy(y_hbm, dest_ref[0, r * TOP_K + k], buf.at[k], r, sem).start(priority=k)
        return c

    def drain(g, c):
        for u in range(GATHER_UNROLL):
            r = g * GATHER_UNROLL + u
            for k in range(TOP_K):
                _row_copy(y_hbm, dest_ref[0, r * TOP_K + k], buf.at[k], r, sem).wait()
        return c

    lax.fori_loop(0, rows // GATHER_UNROLL, issue, 0)
    lax.fori_loop(0, rows // GATHER_UNROLL, drain, 0)
    rt = rt_ref[...]
    f = buf[0] * rt[:, TOP_K:TOP_K + 1] + buf[1] * rt[:, TOP_K + 1:TOP_K + 2]
    x = h_ref[...] + gate_ref[...] * f
    if norm:
        x = x * lax.rsqrt(jnp.mean(x * x, axis=-1, keepdims=True) + EPS) * g_ref[...]
    o_ref[...] = x


def _combine(h, y_sorted, dest, rt, mod, mod_idx_fn, part, g, *, rows, n_rows, norm):
    d = h.shape[1]
    nb = n_rows // rows
    row = pl.BlockSpec((rows, d), lambda i: (i, 0))
    return pl.pallas_call(
        functools.partial(_combine_body, rows=rows, norm=norm), grid=(nb,),
        in_specs=[pl.BlockSpec((None, 1, rows * TOP_K), lambda i: (i, 0, 0), memory_space=pltpu.SMEM),
                  row, pl.BlockSpec((rows, LANES), lambda i: (i, 0)),
                  pl.BlockSpec((None, 1, d), lambda i: (mod_idx_fn(i), 0, part)),
                  pl.BlockSpec((1, d), lambda i: (0, 0)), pl.BlockSpec(memory_space=pl.ANY)],
        out_specs=row, out_shape=jax.ShapeDtypeStruct((n_rows, d), F32),
        scratch_shapes=[pltpu.VMEM((TOP_K, rows, d), y_sorted.dtype), pltpu.SemaphoreType.DMA],
        compiler_params=_cparams(("arbitrary",)), name="moe_combine",
    )(dest.reshape(nb, 1, rows * TOP_K), h, rt, mod, g.reshape(1, d), y_sorted)


def kernel(x, c, ctx, c_ctx, ada_w, ada_b, norm_mix_g, norm_ffn_g, w_in, na_rpb, da_lambda, da_subln_g,
           ssm_conv_w, ssm_conv_b, ssm_dt_bias, ssm_a_log, ssm_d, ssm_norm_g, w_branch_a, w_branch_b,
           w_branch_c, w_out, ffn_w1, ffn_w3, ffn_w2, moe_router_w, moe_router_b, moe_w1, moe_w3, moe_w2,
           final_norm_g):
    batch, seq, d = x.shape
    ctx_len = ctx.shape[1]
    depth = ada_w.shape[0]
    n_lat, n_ctx = batch * seq, batch * ctx_len
    n_all = n_lat + n_ctx
    tm = next(t for t in (1024, 512, 256) if n_ctx % t == 0 and seq % t == 0)
    n_mod = 16
    assert batch + 1 <= n_mod
    lat_blocks = n_lat // tm
    per_seq = seq // tm
    te = next(t for t in (512, 256) if n_ctx % t == 0 and seq % t == 0)
    mod_idx_fn = lambda i: jnp.where(i < n_lat // te, i // (seq // te), batch)
    mod_idx_all = jnp.where(jnp.arange(n_all // tm) < lat_blocks, jnp.arange(n_all // tm) // per_seq, batch)
    layer_blocks = lambda nb, idx: jnp.full((nb,), idx, jnp.int32)

    na_w, da_w = NA_HEADS * NA_HEAD_DIM, DA_HEADS * 2 * DA_HEAD_DIM
    conv_dim = SSM_D_INNER + 2 * SSM_GROUPS * SSM_STATE
    col_qa, col_ka, col_va = 0, na_w, 2 * na_w
    col_qb, col_kb, col_vb = 3 * na_w, 3 * na_w + da_w, 3 * na_w + 2 * da_w
    col_z = 3 * na_w + 3 * da_w
    col_x = col_z + SSM_D_INNER
    col_dt = col_x + conv_dim
    col_gate = col_dt + 2 * SSM_HEADS
    n_main = col_x + SSM_D_INNER
    n_tail = conv_dim - SSM_D_INNER + LANES
    assert n_main % 1024 == 0 and col_dt == n_main + n_tail - LANES and 2 * SSM_HEADS <= LANES

    h, h_ctx = x.reshape(n_lat, d), ctx.reshape(n_ctx, d)
    cond =jnp.zeros((n_mod, d), F32).at[:batch].set(jax.nn.silu(c)).at[batch].set(jax.nn.silu(c_ctx))
    cond = cond.astype(BF16)
    rope_tabs = _rope_tables(seq, tm)
    zero_state = jnp.zeros((batch, 2, SSM_STATE, SSM_D_INNER), F32)
    w_in_t = jnp.swapaxes(w_in, 1, 2)

    out = None
    for li in range(depth):
        last = li == depth - 1
        lam_init = 0.8 - 0.6 * math.exp(-0.3 * li)
        n_out = n_lat if last else n_all
        mod = _gmm(cond, [ada_w], layer_blocks(1, li), tm=n_mod, tn=512, epi="bias",
                   bias=ada_b[li].reshape(1, 6 * d)).reshape(n_mod, 1, 6 * d)
        u = _norm(h, norm_mix_g[li], tm=te, n_rows=n_all, mod=mod, mod_idx_fn=mod_idx_fn, parts=(0, 1), x2=h_ctx)
        proj = _gmm(u, [w_in_t], layer_blocks(n_all // tm, li), tm=tm, tn=1024, n_cols=n_main, w_rows=0)
        tail = _gmm(u, [w_in_t], layer_blocks(n_all // tm, li), tm=tm, tn=n_tail, n_cols=n_tail, w_rows=n_main)
        gates = _gmm(u, [w_in_t], layer_blocks(n_out // tm, li), tm=tm, tn=1024, n_rows=n_out, epi="sigmoid",
                     out_dtype=BF16, n_cols=3 * d, w_rows=col_gate)

        o_a = _na(proj, na_rpb[li], batch=batch, seq=seq, ctx_len=ctx_len, q_col=col_qa, k_col=col_ka, v_col=col_va)
        q_r, k_r = _rope(proj, rope_tabs, tm=tm, n_rows=n_all, n_lat_rows=n_lat, seq=seq,
                         q_col=col_qb, k_col=col_kb, width=da_w)
        o_b = _da(q_r, k_r, proj, da_lambda[li].astype(F32), da_subln_g[li], lam_init, batch=batch, seq=seq,
                  ctx_len=ctx_len, v_col=col_vb, tq=1024, latent=True)
        ssd_args = (ssm_conv_w[li], ssm_conv_b[li], ssm_dt_bias[li], ssm_a_log[li], ssm_d[li])
        ssd_cols = dict(x_col=col_x, dt_col=col_dt, tail_col=n_main)
        y_ctx, s_ctx = _ssd(proj, tail, *ssd_args, zero_state, batch=batch, n_tok=ctx_len, row0=n_lat, **ssd_cols)
        y_lat, _ = _ssd(proj, tail, *ssd_args, s_ctx, batch=batch, n_tok=seq, row0=0, **ssd_cols)
        o_c = _ssm_norm(y_lat, proj, ssm_norm_g[li], tm=te, row0=0, z_col=col_z)
        if not last:
            o_a_c = _ctx_attn(proj, batch=batch, seq=seq, ctx_len=ctx_len, q_col=col_qa, k_col=col_ka, v_col=col_va)
            o_b_c = _da(q_r, k_r, proj, da_lambda[li].astype(F32), da_subln_g[li], lam_init, batch=batch, seq=seq,
                        ctx_len=ctx_len, v_col=col_vb, tq=ctx_len, latent=False)
            o_c_c = _ssm_norm(y_ctx, proj, ssm_norm_g[li], tm=te, row0=n_lat, z_col=col_z)
        merged = _merge((o_a, o_b, o_c), None if last else (o_a_c, o_b_c, o_c_c), gates,
                        w_branch_a, w_branch_b, w_branch_c, li, tm=te, tn=1024, n_rows=n_out)
        h = _gmm(merged, [w_out], layer_blocks(n_out // tm, li), tm=tm, tn=1024, epi="resid", res=h, res2=h_ctx,
                 mod=mod, mod_idx=mod_idx_all[:n_out // tm], mod_part=2, n_rows=n_out)
        h_ctx = None

        j = li // 2
        if li % 2 == 0:
            tokens = _norm(h, norm_ffn_g[li], tm=te, n_rows=n_out, mod=mod, mod_idx_fn=mod_idx_fn, parts=(3, 4))
            hid = _gmm(tokens, [ffn_w1, ffn_w3], layer_blocks(n_out // tm, j), tm=tm, tn=512,
                       epi="swiglu", out_dtype=BF16)
            h = _gmm(hid, [ffn_w2], layer_blocks(n_out // tm, j), tm=tm, tn=512, epi="resid", res=h,
                     mod=mod, mod_idx=mod_idx_all[:n_out // tm], mod_part=5, n_rows=n_out, w_buffers=1)
            if last:
                out = _norm(h, final_norm_g, tm=te, n_rows=n_lat, out_dtype=F32)
        else:
            rw = jnp.zeros((d, LANES), F32).at[:, :N_EXPERTS].set(moe_router_w[j].astype(F32))
            rb = jnp.zeros((1, LANES), F32).at[0, :N_EXPERTS].set(moe_router_b[j].astype(F32))
            tokens, rt = _norm(h, norm_ffn_g[li], tm=te, n_rows=n_out, mod=mod, mod_idx_fn=mod_idx_fn,
                               parts=(3, 4), router=(rw, rb), out_dtype=F32)
            dest, slot_tok, block_expert, n_used = _route(rt[:, :TOP_K].astype(jnp.int32), MOE_ROWS)
            block_expert = block_expert + j * N_EXPERTS
            stack = lambda w: w.reshape((-1,) + w.shape[2:])
            x_sorted = _gather_rows(tokens, slot_tok, n_used, rows=MOE_ROWS, out_dtype=BF16)
            hid = _gmm(x_sorted, [stack(moe_w1), stack(moe_w3)], block_expert, tm=MOE_ROWS, tn=512, epi="swiglu",
                       out_dtype=BF16, n_used=n_used)
            y_sorted = _gmm(hid, [stack(moe_w2)], block_expert, tm=MOE_ROWS, tn=512, n_used=n_used, w_buffers=1)
            h = _combine(h, y_sorted, dest, rt, mod, mod_idx_fn, 5, final_norm_g, rows=te, n_rows=n_out, norm=last)
            if last:
                out = h
    return out.reshape(batch, seq, d)
```

```python
import functools
import math

import numpy as np
import jax
import jax.numpy as jnp
from jax import lax
from jax.experimental import pallas as pl
from jax.experimental.pallas import tpu as pltpu

F32 = jnp.float32
BF16 = jnp.bfloat16

GRID_W = 64
NA_HEADS, NA_HEAD_DIM, NA_WIN_H, NA_WIN_W = 8, 64, 8, 16
DA_HEADS, DA_HEAD_DIM = 4, 64
ROPE_BASE = 10000.0
SSM_D_INNER, SSM_HEAD_DIM, SSM_GROUPS, SSM_STATE, SSM_CONV_W = 1024, 64, 2, 128, 5
SSM_HEADS = SSM_D_INNER // SSM_HEAD_DIM
N_EXPERTS, TOP_K = 8, 2
EPS = 1e-6
NEG_INF = -1e30

LANES = 128
SUBLANES = 8
VMEM_LIMIT_BYTES = 56 * 1024 * 1024

ROW_TILES = (1024, 512, 256)
EW_ROW_TILES = (512, 256)
TN_WIDE = 1024
TN_FFN = 512
MOE_ROWS = 1024
SSD_CHUNK = 128
NA_ROW_UNROLL = 2
DA_Q_TILE = 1024
DA_KEY_CHUNK = 512
DA_STREAMS = 4
LOG2E = math.log2(math.e)


def _cparams(sem):
    return pltpu.CompilerParams(dimension_semantics=sem, vmem_limit_bytes=VMEM_LIMIT_BYTES)


def _dot(a, b):
    return jnp.dot(a, b, preferred_element_type=F32)


def _dot_nt(a, b):
    return lax.dot_general(a, b, (((1,), (1,)), ((), ())), preferred_element_type=F32)


def _silu(x):
    return x * jax.nn.sigmoid(x)


def _gmm_body(bexp_ref, neww_ref, mod_ref, nv_ref, x_ref, *rest, nk, n_w, epi, res_first_blocks, w_transposed):
    del bexp_ref, mod_ref
    w_refs, rest = rest[:n_w], rest[n_w:]
    if epi == "bias":
        bias_ref, rest = rest[0], rest[1:]
    elif epi == "resid":
        res_ref, rest = rest[0], rest[1:]
        if res_first_blocks is not None:
            res2_ref, rest = rest[0], rest[1:]
        gate_ref, rest = rest[0], rest[1:]
    o_ref, rest = rest[0], rest[1:]
    wbf_refs, acc_refs = rest[:n_w], rest[n_w:]
    i = pl.program_id(1)
    k = pl.program_id(2)

    def finalize(vals):
        if epi == "swiglu":
            o_ref[...] = (_silu(vals[0]) * vals[1]).astype(o_ref.dtype)
        elif epi == "bias":
            o_ref[...] = (vals[0] + bias_ref[...]).astype(o_ref.dtype)
        elif epi == "resid":
            res = res_ref[...]
            if res_first_blocks is not None:
                res = jnp.where(i < res_first_blocks, res, res2_ref[...])
            o_ref[...] = (res + gate_ref[...] * vals[0]).astype(o_ref.dtype)
        elif epi == "sigmoid":
            o_ref[...] = jax.nn.sigmoid(vals[0]).astype(o_ref.dtype)
        else:
            o_ref[...] = vals[0].astype(o_ref.dtype)

    @pl.when(jnp.logical_and(i >= nv_ref[0], k == nk - 1))
    def _():
        o_ref[...] = jnp.zeros_like(o_ref)

    @pl.when(i < nv_ref[0])
    def _():
        @pl.when(neww_ref[i] == 1)
        def _():
            for w_ref, wbf in zip(w_refs, wbf_refs):
                w = w_ref[...]
                wbf[k] = (w.T if w_transposed else w).astype(BF16)

        x = x_ref[...]
        parts = [_dot(x, wbf[k]) for wbf in wbf_refs]
        if nk == 1:
            finalize(parts)
        else:
            @pl.when(k == 0)
            def _():
                for acc, p in zip(acc_refs, parts):
                    acc[...] = p

            @pl.when(k > 0)
            def _():
                for acc, p in zip(acc_refs, parts):
                    acc[...] += p

            @pl.when(k == nk - 1)
            def _():
                finalize([acc[...] for acc in acc_refs])


def _gmm(x, ws, bexp, *, tm, tn, nk=1, n_cols=None, epi="plain", out_dtype=F32, bias=None, res=None,
         res2=None, mod=None, mod_idx=None, mod_part=0, n_rows=None, n_used=None, w_buffers=2, w_rows=None):
    m_rows = x.shape[0] if n_rows is None else n_rows
    k_dim = x.shape[1]
    n_dim = ws[0].shape[2] if n_cols is None else n_cols
    n_w = len(ws)
    tk = k_dim // nk
    nb = m_rows // tm
    res_first_blocks = None
    assert m_rows % tm == 0 and n_dim % tn == 0 and k_dim % nk == 0
    bexp = bexp.astype(jnp.int32)
    neww = jnp.concatenate([jnp.ones((1,), jnp.int32), (bexp[1:] != bexp[:-1]).astype(jnp.int32)])
    if mod_idx is None:
        mod_idx = jnp.zeros((nb,), jnp.int32)
    if n_used is None:
        n_used = jnp.full((1,), nb, jnp.int32)

    rb = lambda i, nv: jnp.minimum(i, nv[0] - 1)
    x_spec = pl.BlockSpec((tm, tk), lambda j, i, k, be, nw, md, nv: (rb(i, nv), k))
    kb_of = lambda i, k, nw, nv: jnp.where(nw[rb(i, nv)] == 1, k, nk - 1)
    if w_rows is None:
        w_block = (None, tk, tn)
        w_map = lambda j, i, k, be, nw, md, nv: (be[rb(i, nv)], kb_of(i, k, nw, nv), j)
    else:
        n_total = ws[0].shape[1]
        assert n_total % SUBLANES == 0 and w_rows % SUBLANES == 0 and tn % SUBLANES == 0
        ws = [w.reshape(-1, k_dim) for w in ws]
        w_block = (pl.Element(tn), pl.Element(tk))
        w_map = lambda j, i, k, be, nw, md, nv: (
            pl.multiple_of(be[rb(i, nv)] * n_total + w_rows + j * tn, SUBLANES),
            pl.multiple_of(kb_of(i, k, nw, nv) * tk, LANES))
    if w_buffers == 2:
        w_spec = pl.BlockSpec(w_block, w_map)
    else:
        w_spec = pl.BlockSpec(w_block, w_map, pipeline_mode=pl.Buffered(w_buffers))
    o_spec = pl.BlockSpec((tm, tn), lambda j, i, k, be, nw, md, nv: (i, j))
    in_specs = [x_spec] + [w_spec] * n_w
    args = [x] + list(ws)
    if epi == "bias":
        in_specs.append(pl.BlockSpec((1, tn), lambda j, i, k, be, nw, md, nv: (0, j)))
        args.append(bias)
    elif epi == "resid":
        part_off = mod_part * (n_dim // tn)
        if res2 is None:
            in_specs.append(o_spec)
            args.append(res)
        else:
            res_first_blocks = res.shape[0] // tm
            in_specs.append(pl.BlockSpec(
                (tm, tn), lambda j, i, k, be, nw, md, nv: (jnp.minimum(i, res_first_blocks - 1), j)))
            in_specs.append(pl.BlockSpec(
                (tm, tn), lambda j, i, k, be, nw, md, nv: (jnp.maximum(i - res_first_blocks, 0), j)))
            args += [res, res2]
        in_specs.append(pl.BlockSpec((None, 1, tn),
                                     lambda j, i, k, be, nw, md, nv: (md[rb(i, nv)], 0, part_off + j)))
        args.append(mod)
    scratch = [pltpu.VMEM((nk, tk, tn), BF16) for _ in range(n_w)]
    if nk > 1:
        scratch += [pltpu.VMEM((tm, tn), F32) for _ in range(n_w)]
    grid_spec = pltpu.PrefetchScalarGridSpec(
        num_scalar_prefetch=4, grid=(n_dim // tn, nb, nk), in_specs=in_specs, out_specs=o_spec,
        scratch_shapes=scratch)
    return pl.pallas_call(
        functools.partial(_gmm_body, nk=nk, n_w=n_w, epi=epi, res_first_blocks=res_first_blocks,
                          w_transposed=w_rows is not None),
        grid_spec=grid_spec,
        out_shape=jax.ShapeDtypeStruct((m_rows, n_dim), out_dtype),
        compiler_params=_cparams(("arbitrary", "arbitrary", "arbitrary")),
        name="gmm_" + epi,
    )(bexp, neww, mod_idx.astype(jnp.int32), n_used.astype(jnp.int32), *args)


def _split2(v):
    hi = v.astype(BF16)
    lo = (v - hi.astype(F32)).astype(BF16)
    return hi, lo


def _norm_body(x_ref, *rest, modulate, router, first_blocks):
    if first_blocks is not None:
        x2_ref, rest = rest[0], rest[1:]
    g_ref, rest = rest[0], rest[1:]
    if modulate:
        shift_ref, scale_ref, rest = rest[0], rest[1], rest[2:]
    if router:
        rw_ref, rb_ref, rest = rest[0], rest[1], rest[2:]
    o_ref = rest[0]
    x = x_ref[...]
    if first_blocks is not None:
        x = jnp.where(pl.program_id(0) < first_blocks, x, x2_ref[...])
    y = x * lax.rsqrt(jnp.mean(x * x, axis=-1, keepdims=True) + EPS) * g_ref[...]
    if modulate:
        y = y * (1.0 + scale_ref[...]) + shift_ref[...]
    o_ref[...] = y.astype(o_ref.dtype)
    if router:
        rt_ref = rest[1]
        y_hi, y_lo = _split2(y)
        w_hi, w_lo = _split2(rw_ref[...])
        logits = _dot(y_hi, w_hi) + _dot(y_lo, w_hi) + _dot(y_hi, w_lo) + rb_ref[...]
        lane = lax.broadcasted_iota(jnp.int32, logits.shape, 1).astype(F32)
        lg = jnp.where(lane < N_EXPERTS, logits, -jnp.inf)
        m1 = jnp.max(lg, axis=-1, keepdims=True)
        i1 = jnp.min(jnp.where(lg == m1, lane, float(LANES)), axis=-1, keepdims=True)
        lg2 = jnp.where(lane == i1, -jnp.inf, lg)
        m2 = jnp.max(lg2, axis=-1, keepdims=True)
        i2 = jnp.min(jnp.where(lg2 == m2, lane, float(LANES)), axis=-1, keepdims=True)
        e2 = jnp.exp(m2 - m1)
        g1 = 1.0 / (1.0 + e2)
        rt = jnp.where(lane == 0, i1, jnp.where(lane == 1, i2, jnp.where(lane == 2, g1, jnp.where(lane == 3, e2 * g1, 0.0))))
        rt_ref[...] = rt


def _norm(x, g, *, tm, n_rows, mod=None, mod_idx_fn=None, parts=(0, 1), out_dtype=BF16, router=None, x2=None):
    d = x.shape[1]
    nb = n_rows // tm
    row = pl.BlockSpec((tm, d), lambda i: (i, 0))
    first_blocks = None if x2 is None else x.shape[0] // tm
    if x2 is None:
        in_specs, args = [row], [x]
    else:
        in_specs, args = list(_two_source_specs((tm, d), first_blocks, lambda i: i)), [x, x2]
    in_specs.append(pl.BlockSpec((1, d), lambda i: (0, 0)))
    args.append(g.reshape(1, d))
    modulate = mod is not None
    if modulate:
        for p in parts:
            in_specs.append(pl.BlockSpec((None, 1, d), lambda i, p=p: (mod_idx_fn(i), 0, p)))
            args.append(mod)
    out_shape = [jax.ShapeDtypeStruct((n_rows, d), out_dtype)]
    out_specs = [row]
    if router is not None:
        rw, rb = router
        in_specs += [pl.BlockSpec((d, LANES), lambda i: (0, 0)), pl.BlockSpec((1, LANES), lambda i: (0, 0))]
        args += [rw, rb]
        out_shape.append(jax.ShapeDtypeStruct((n_rows, LANES), F32))
        out_specs.append(pl.BlockSpec((tm, LANES), lambda i: (i, 0)))
    res = pl.pallas_call(
        functools.partial(_norm_body, modulate=modulate, router=router is not None, first_blocks=first_blocks),
        grid=(nb,), in_specs=in_specs, out_specs=out_specs, out_shape=out_shape,
        compiler_params=_cparams(("arbitrary",)), name="rmsnorm",
    )(*args)
    return res if router is not None else res[0]


def _rope_body(q_ref, k_ref, c_ref, sm_ref, sp_ref, qo_ref, ko_ref, *, q_scale):
    c, sm, sp = c_ref[...], sm_ref[...], sp_ref[...]
    width = q_ref.shape[1]
    for src, dst, scale in ((q_ref, qo_ref, q_scale), (k_ref, ko_ref, 1.0)):
        for g in range(width // LANES):
            sl = slice(g * LANES, (g + 1) * LANES)
            x = src[:, sl]
            y = x * c + pltpu.roll(x, LANES - 16, 1) * sm + pltpu.roll(x, 16, 1) * sp
            dst[:, sl] = (y * scale).astype(dst.dtype)


def _rope_tables(seq, n_id_rows):
    t = jnp.arange(seq)
    pos = jnp.stack([t // GRID_W, t % GRID_W], axis=-1).astype(F32)
    n_freq = DA_HEAD_DIM // 4
    inv_freq = ROPE_BASE ** (-jnp.arange(n_freq, dtype=F32) / n_freq)
    ang = pos[:, :, None] * inv_freq
    d = np.arange(LANES) % DA_HEAD_DIM
    kind, which, f = d // 32, (d % 32) // 16, d % 16
    a = ang[:, kind, f]
    cos, sin = jnp.cos(a), jnp.sin(a)
    sm = jnp.where(which == 0, -sin, 0.0)
    sp = jnp.where(which == 1, sin, 0.0)
    pad = lambda v, fill: jnp.concatenate([v, jnp.full((n_id_rows, LANES), fill, F32)], axis=0)
    return pad(cos, 1.0), pad(sm, 0.0), pad(sp, 0.0)


def _rope(proj, tabs, *, tm, n_rows, n_lat_rows, seq, q_col, k_col, width):
    lat_blocks = n_lat_rows // tm
    per_seq = seq // tm
    tab_idx = lambda i: (jnp.where(i < lat_blocks, i % per_seq, per_seq), 0)
    out = jax.ShapeDtypeStruct((n_rows, width), BF16)
    return pl.pallas_call(
        functools.partial(_rope_body, q_scale=DA_HEAD_DIM ** -0.5 * LOG2E),
        grid=(n_rows // tm,),
        in_specs=[pl.BlockSpec((tm, width), lambda i: (i, q_col // width)),
                  pl.BlockSpec((tm, width), lambda i: (i, k_col // width)),
                  pl.BlockSpec((tm, LANES), tab_idx), pl.BlockSpec((tm, LANES), tab_idx),
                  pl.BlockSpec((tm, LANES), tab_idx)],
        out_specs=[pl.BlockSpec((tm, width), lambda i: (i, 0))] * 2,
        out_shape=[out, out], compiler_params=_cparams(("arbitrary",)), name="rope",
    )(proj, proj, *tabs)


def _na_body(q_ref, k_ref, v_ref, kc_ref, vc_ref, tb_ref, o_ref, kb, vb, kcb, vcb, *, rows, wh):
    kb[...] = k_ref[...].astype(BF16)
    vb[...] = v_ref[...].astype(BF16)
    kcb[...] = kc_ref[...].astype(BF16)
    vcb[...] = vc_ref[...].astype(BF16)
    lane = lax.broadcasted_iota(jnp.int32, (GRID_W, LANES), 1)
    first = lane < NA_HEAD_DIM
    scale = NA_HEAD_DIM ** -0.5

    row_of = lambda r: slice(r * GRID_W, (r + 1) * GRID_W)
    win_of = lambda r: slice(min(max(r - wh // 2, 0), rows - wh) * GRID_W,
                             (min(max(r - wh // 2, 0), rows - wh) + wh) * GRID_W)

    def scores(r):
        q = q_ref[row_of(r), :] * scale
        qq = jnp.concatenate([jnp.where(first, q, 0.0), jnp.where(first, 0.0, q)], axis=0).astype(BF16)
        d0 = min(max(r - wh // 2, 0), rows - wh) - r + (NA_WIN_H - 1)
        return _dot_nt(qq, kb[win_of(r), :]) + tb_ref[d0], _dot_nt(qq, kcb[...])

    def softmax(s_l, s_c):
        m = jnp.maximum(jnp.max(s_l, axis=-1, keepdims=True), jnp.max(s_c, axis=-1, keepdims=True))
        p_l, p_c = jnp.exp(s_l - m), jnp.exp(s_c - m)
        den = jnp.sum(p_l, axis=-1, keepdims=True) + jnp.sum(p_c, axis=-1, keepdims=True)
        return p_l.astype(BF16), p_c.astype(BF16), den

    def values(r, p_l, p_c, den):
        o = (_dot(p_l, vb[win_of(r), :]) + _dot(p_c, vcb[...])) / den
        o_ref[row_of(r), :] = jnp.where(first, o[:GRID_W], o[GRID_W:]).astype(o_ref.dtype)

    assert rows % NA_ROW_UNROLL == 0
    n_grp = rows // NA_ROW_UNROLL
    sc, pr = {}, {}
    for p in range(n_grp + 2):
        for j in range(NA_ROW_UNROLL):
            r3, r2, r1 = ((p - 2) * NA_ROW_UNROLL + j, (p - 1) * NA_ROW_UNROLL + j, p * NA_ROW_UNROLL + j)
            if 0 <= p - 2 < n_grp:
                values(r3, *pr.pop(r3))
            if 0 <= p - 1 < n_grp:
                pr[r2] = softmax(*sc.pop(r2))
            if p < n_grp:
                sc[r1] = scores(r1)


def _na_bias_table(rpb, rows):
    wh = min(NA_WIN_H, rows)
    qc = np.arange(GRID_W)[:, None]
    kc = np.arange(GRID_W)[None, :]
    ws = np.clip(qc - NA_WIN_W // 2, 0, GRID_W - NA_WIN_W)
    ok = (kc >= ws) & (kc < ws + NA_WIN_W)
    n_drow, n_dcol = 2 * NA_WIN_H - 1, 2 * NA_WIN_W - 1
    off = GRID_W - NA_WIN_W
    line = jnp.full((NA_HEADS, n_drow, 2 * GRID_W), NEG_INF, F32).at[:, :, off:off + n_dcol].set(rpb.astype(F32))
    span = 2 * GRID_W - 1
    t = jnp.tile(line, (1, 1, GRID_W))[:, :, :GRID_W * span].reshape(NA_HEADS, n_drow, GRID_W, span)
    t = jnp.where(ok, t[:, :, :, GRID_W - 1:], NEG_INF)
    n_d0 = NA_WIN_H
    t = jnp.stack([t[:, d0:d0 + wh] for d0 in range(n_d0)], axis=1)
    t = t.transpose(0, 1, 3, 2, 4).reshape(NA_HEADS // 2, 2, n_d0, GRID_W, wh * GRID_W)
    return t.transpose(0, 2, 1, 3, 4).reshape(NA_HEADS // 2, n_d0, 2 * GRID_W, wh * GRID_W)


def _na(proj, rpb, *, batch, seq, ctx_len, q_col, k_col, v_col):
    rows = seq // GRID_W
    wh = min(NA_WIN_H, rows)
    pairs = NA_HEADS // 2
    tb = _na_bias_table(rpb, rows)
    ctx0 = batch * seq // ctx_len
    lat = lambda col: pl.BlockSpec((seq, LANES), lambda b, p: (b, col // LANES + p))
    ctx = lambda col: pl.BlockSpec((ctx_len, LANES), lambda b, p: (ctx0 + b, col // LANES + p))
    return pl.pallas_call(
        functools.partial(_na_body, rows=rows, wh=wh),
        grid=(batch, pairs),
        in_specs=[lat(q_col), lat(k_col), lat(v_col), ctx(k_col), ctx(v_col),
                  pl.BlockSpec((None, NA_WIN_H, 2 * GRID_W, wh * GRID_W), lambda b, p: (p, 0, 0, 0))],
        out_specs=pl.BlockSpec((seq, LANES), lambda b, p: (b, p)),
        out_shape=jax.ShapeDtypeStruct((batch * seq, pairs * LANES), BF16),
        scratch_shapes=[pltpu.VMEM((seq, LANES), BF16), pltpu.VMEM((seq, LANES), BF16),
                        pltpu.VMEM((ctx_len, LANES), BF16), pltpu.VMEM((ctx_len, LANES), BF16)],
        compiler_params=_cparams(("arbitrary", "arbitrary")), name="na_attn",
    )(proj, proj, proj, proj, proj, tb)


def _ctx_attn_body(q_ref, k_ref, v_ref, o_ref):
    kb = k_ref[...].astype(BF16)
    vb = v_ref[...].astype(BF16)
    q = q_ref[...] * (NA_HEAD_DIM ** -0.5)
    lane = lax.broadcasted_iota(jnp.int32, q.shape, 1)
    first = lane < NA_HEAD_DIM
    outs = []
    for hh in range(2):
        qm = jnp.where(first if hh == 0 else jnp.logical_not(first), q, 0.0).astype(BF16)
        s = _dot_nt(qm, kb)
        p = jnp.exp(s - jnp.max(s, axis=-1, keepdims=True))
        outs.append(_dot(p.astype(BF16), vb) / jnp.sum(p, axis=-1, keepdims=True))
    o_ref[...] = jnp.where(first, outs[0], outs[1]).astype(o_ref.dtype)


def _ctx_attn(proj, *, batch, seq, ctx_len, q_col, k_col, v_col):
    pairs = NA_HEADS // 2
    ctx0 = batch * seq // ctx_len
    ctx = lambda col: pl.BlockSpec((ctx_len, LANES), lambda b, p: (ctx0 + b, col // LANES + p))
    return pl.pallas_call(
        _ctx_attn_body, grid=(batch, pairs),
        in_specs=[ctx(q_col), ctx(k_col), ctx(v_col)],
        out_specs=pl.BlockSpec((ctx_len, LANES), lambda b, p: (b, p)),
        out_shape=jax.ShapeDtypeStruct((batch * ctx_len, pairs * LANES), BF16),
        compiler_params=_cparams(("arbitrary", "arbitrary")), name="ctx_attn",
    )(proj, proj, proj)


def _da_streams(q_ref, kl_ref, kc_ref, vlb, vcb, g_ref, o_ref, s_ref, lam, lam_init):
    n_st = DA_STREAMS
    hq = q_ref.shape[0] // n_st
    ck = DA_KEY_CHUNK
    chunks = [(kl_ref, vlb, off, ck, off) for off in range(0, kl_ref.shape[0], ck)]
    chunks.append((kc_ref, vcb, 0, kc_ref.shape[0], kl_ref.shape[0]))
    lane = lax.broadcasted_iota(jnp.int32, (hq, LANES), 1)
    lane_tiles = lambda x: [x[:, i * LANES:(i + 1) * LANES] for i in range(x.shape[1] // LANES)]

    def stacked_q(st):
        q = q_ref[st * hq:(st + 1) * hq, :]
        zero = jnp.zeros_like(q)
        return jnp.concatenate([jnp.where(lane < DA_HEAD_DIM, q, zero), jnp.where(lane >= DA_HEAD_DIM, q, zero)], axis=0)

    def scores(st, qq, chunk, m_run):
        k_ref, _, off, n, c0 = chunk
        s = _dot_nt(qq, k_ref[off:off + n, :])
        s_ref[st, :, c0:c0 + n] = s
        return functools.reduce(jnp.maximum, lane_tiles(s), m_run)

    def numerators(st, chunk, m, l_run):
        _, _, _, n, c0 = chunk
        p = jnp.exp2(s_ref[st, :, c0:c0 + n] - m)
        s_ref[st, :, c0:c0 + n] = p
        return functools.reduce(jnp.add, lane_tiles(p), l_run)

    def values(st, chunk, ratio, acc):
        _, v_ref, off, n, c0 = chunk
        a = (s_ref[st, 0:hq, c0:c0 + n] - ratio * s_ref[st, hq:2 * hq, c0:c0 + n]).astype(BF16)
        return acc + _dot(a, v_ref[off:off + n, :])

    def finish(st, acc, den):
        o = acc / den[:hq]
        o = o * lax.rsqrt(jnp.mean(o * o, axis=-1, keepdims=True) + EPS) * g_ref[...]
        o_ref[st * hq:(st + 1) * hq, :] = (o * (1.0 - lam_init)).astype(o_ref.dtype)

    neg = jnp.full((2 * hq, LANES), -jnp.inf, F32)
    zero_l = jnp.zeros((2 * hq, LANES), F32)
    zero_o = jnp.zeros((hq, LANES), F32)
    row_max = lambda m_run: jnp.max(m_run, axis=-1, keepdims=True)
    row_sum = lambda l_run: jnp.sum(l_run, axis=-1, keepdims=True)
    m, den, ratio = {}, {}, {}
    for p in range(n_st + 2):
        s1, s2, s3 = p, p - 1, p - 2
        qq = stacked_q(s1) if s1 < n_st else None
        m_run, l_run, acc = neg, zero_l, zero_o
        for ch in chunks:
            if 0 <= s3 < n_st:
                acc = values(s3, ch, ratio[s3], acc)
            if 0 <= s2 < n_st:
                l_run = numerators(s2, ch, m[s2], l_run)
            if s1 < n_st:
                m_run = scores(s1, qq, ch, m_run)
        if 0 <= s3 < n_st:
            finish(s3, acc, den[s3])
        if 0 <= s2 < n_st:
            den[s2] = row_sum(l_run)
            ratio[s2] = lam * den[s2][:hq] / den[s2][hq:]
        if s1 < n_st:
            m[s1] = row_max(m_run)


def _da_body(lv_ref, q_ref, *rest, lam_init, has_lat):
    if has_lat:
        kl_ref, vl_ref, rest = rest[0], rest[1], rest[2:]
    kc_ref, vc_ref, g_ref, o_ref = rest[:4]
    scr = rest[4:]
    t = pl.program_id(2)
    if has_lat:
        vlb, vcb, s_ref = scr

        @pl.when(t == 0)
        def _():
            vlb[...] = vl_ref[...].astype(BF16)
            vcb[...] = vc_ref[...].astype(BF16)
    else:
        (vcb,) = scr
        vcb[...] = vc_ref[...].astype(BF16)

    lv = lv_ref[...]
    lam = (jnp.exp(jnp.sum(lv[0:1] * lv[1:2], axis=-1, keepdims=True))
           - jnp.exp(jnp.sum(lv[2:3] * lv[3:4], axis=-1, keepdims=True)) + lam_init)
    if has_lat:
        _da_streams(q_ref, kl_ref, kc_ref, vlb, vcb, g_ref, o_ref, s_ref, lam, lam_init)
        return
    q = q_ref[...]
    tq = q.shape[0]
    lane = lax.broadcasted_iota(jnp.int32, q.shape, 1)
    zero = jnp.zeros_like(q)
    qq = jnp.concatenate([jnp.where(lane < DA_HEAD_DIM, q, zero), jnp.where(lane >= DA_HEAD_DIM, q, zero)], axis=0)
    s_c = _dot_nt(qq, kc_ref[...])
    m = jnp.max(s_c, axis=-1, keepdims=True)
    if has_lat:
        s_l = _dot_nt(qq, kl_ref[...])
        m = jnp.maximum(m, jnp.max(s_l, axis=-1, keepdims=True))
        p_l = jnp.exp2(s_l - m)
    p_c = jnp.exp2(s_c - m)
    den = jnp.sum(p_c, axis=-1, keepdims=True)
    if has_lat:
        den = den + jnp.sum(p_l, axis=-1, keepdims=True)
    ratio = lam * den[:tq] / den[tq:]
    o = _dot((p_c[:tq] - ratio * p_c[tq:]).astype(BF16), vcb[...])
    if has_lat:
        o = o + _dot((p_l[:tq] - ratio * p_l[tq:]).astype(BF16), vlb[...])
    o = o / den[:tq]
    o = o * lax.rsqrt(jnp.mean(o * o, axis=-1, keepdims=True) + EPS) * g_ref[...]
    o_ref[...] = (o * (1.0 - lam_init)).astype(o_ref.dtype)


def _da(q_r, k_r, proj, lam_vec, subln_g, lam_init, *, batch, seq, ctx_len, v_col, tq, latent):
    ctx0 = batch * seq // ctx_len
    vcb0 = v_col // LANES
    g2 = subln_g.reshape(1, 2 * DA_HEAD_DIM).astype(F32)
    small = [pl.BlockSpec((4, DA_HEAD_DIM), lambda b, h, t: (0, 0))]
    gspec = pl.BlockSpec((1, LANES), lambda b, h, t: (0, 0))
    kc = pl.BlockSpec((ctx_len, LANES), lambda b, h, t: (ctx0 + b, h))
    vc = pl.BlockSpec((ctx_len, LANES), lambda b, h, t: (ctx0 + b, vcb0 + h))
    if latent:
        nq = seq // tq
        qs = pl.BlockSpec((tq, LANES), lambda b, h, t: (b * nq + t, h))
        kl = pl.BlockSpec((seq, LANES), lambda b, h, t: (b, h))
        vl = pl.BlockSpec((seq, LANES), lambda b, h, t: (b, vcb0 + h))
        in_specs = small + [qs, kl, vl, kc, vc, gspec]
        args = (lam_vec, q_r, k_r, proj, k_r, proj, g2)
        assert seq % DA_KEY_CHUNK == 0 and tq % DA_STREAMS == 0
        scratch = [pltpu.VMEM((seq, LANES), BF16), pltpu.VMEM((ctx_len, LANES), BF16),
                   pltpu.VMEM((DA_STREAMS, 2 * tq // DA_STREAMS, seq + ctx_len), F32)]
        n_out = batch * seq
    else:
        nq = ctx_len // tq
        cq0 = batch * seq // tq
        qs = pl.BlockSpec((tq, LANES), lambda b, h, t: (cq0 + b * nq + t, h))
        in_specs = small + [qs, kc, vc, gspec]
        args = (lam_vec, q_r, k_r, proj, g2)
        scratch = [pltpu.VMEM((ctx_len, LANES), BF16)]
        n_out = batch * ctx_len
    return pl.pallas_call(
        functools.partial(_da_body, lam_init=lam_init, has_lat=latent),
        grid=(batch, DA_HEADS, nq), in_specs=in_specs,
        out_specs=pl.BlockSpec((tq, LANES), lambda b, h, t: (b * nq + t, h)),
        out_shape=jax.ShapeDtypeStruct((n_out, DA_HEADS * LANES), BF16),
        scratch_shapes=scratch,
        compiler_params=_cparams(("arbitrary", "arbitrary", "arbitrary")),
        name="diff_attn_lat" if latent else "diff_attn_ctx",
    )(*args)


def _ssd_body(x0_ref, x1_ref, bc_ref, x0p_ref, x1p_ref, bcp_ref, x0n_ref, x1n_ref, bcn_ref, dt_ref,
              cw_ref, cb_ref, prm_ref, dsk_ref, ex_ref, init_ref, y_ref, sout_ref, st_ref, u_ref, *, nc):
    q_len = SSD_CHUNK
    dirn = pl.program_id(1)
    z = pl.program_id(2)
    zz = jnp.where(dirn == 0, z, nc - 1 - z)
    fwd = dirn == 0
    half = SSM_D_INNER // SSM_GROUPS
    heads_per_group = SSM_HEADS // SSM_GROUPS

    @pl.when(z == 0)
    def _():
        st_ref[...] = init_ref[...]

    def conv_piece(main_ref, prev_ref, next_ref, c0):
        width = main_ref.shape[1]
        prev = jnp.where(zz == 0, 0.0, prev_ref[...])
        nxt = jnp.where(zz == nc - 1, 0.0, next_ref[...])
        ext = jnp.concatenate([prev, main_ref[...], nxt], axis=0)
        n_ext = ext.shape[0]
        acc = jnp.zeros((q_len, width), F32) + cb_ref[:, c0:c0 + width]
        for tap in range(SSM_CONV_W):
            sh = (SSM_CONV_W // 2 - tap) % n_ext
            e = ext if sh == 0 else pltpu.roll(ext, sh, 0)
            acc = acc + e[SUBLANES:SUBLANES + q_len] * cw_ref[tap:tap + 1, c0:c0 + width]
        return _silu(acc)

    @pl.when(fwd)
    def _():
        u_ref[zz, :, 0:half] = conv_piece(x0_ref, x0p_ref, x0n_ref, 0)
        u_ref[zz, :, half:2 * half] = conv_piece(x1_ref, x1p_ref, x1n_ref, half)
        u_ref[zz, :, SSM_D_INNER:] = conv_piece(bc_ref, bcp_ref, bcn_ref, SSM_D_INNER)

    xs = (u_ref[zz, :, 0:half], u_ref[zz, :, half:2 * half])
    bcm = u_ref[zz, :, SSM_D_INNER:]

    raw = dt_ref[...]
    raw = jnp.where(fwd, raw, pltpu.roll(raw, LANES - SSM_HEADS, 1))
    xb = raw + prm_ref[0:1, :]
    dtv = jnp.maximum(xb, 0.0) + jnp.log1p(jnp.exp(-jnp.abs(xb)))
    a = dtv * (-jnp.exp(prm_ref[1:2, :]))
    rowi = lax.broadcasted_iota(jnp.int32, (q_len, LANES), 0)
    coli = lax.broadcasted_iota(jnp.int32, (q_len, LANES), 1)
    cum = a
    sft = 1
    while sft < q_len:
        cum = cum + jnp.where(rowi >= sft, pltpu.roll(cum, sft, 0), 0.0)
        sft *= 2
    tot = cum[q_len - 1:q_len, :]
    g = jnp.where(fwd, cum, cum - a)
    e_g = jnp.exp(g)
    e_tg = jnp.exp(tot - g)
    w_state = jnp.where(fwd, e_tg, e_g)
    w_yoff = jnp.where(fwd, e_g, e_tg)
    e_tot = jnp.exp(tot)
    gs = g * jnp.where(fwd, LOG2E, -LOG2E)
    gs_t = gs.T
    tri = jnp.where(fwd, rowi - coli, coli - rowi) >= 0

    stack = jnp.concatenate([dtv, dtv * w_state, w_yoff, jnp.broadcast_to(e_tot, (SUBLANES, LANES))], axis=0)
    s_hi, s_lo = _split2(stack)
    ex = ex_ref[...]
    expd = _dot(s_hi, ex) + _dot(s_lo, ex)
    dt_e, dts_e, wy_e = expd[0:q_len], expd[q_len:2 * q_len], expd[2 * q_len:3 * q_len]
    tot_e = expd[3 * q_len:3 * q_len + 1]

    lane = lax.broadcasted_iota(jnp.int32, (q_len, LANES), 1)
    first = lane < SSM_HEAD_DIM
    for grp in range(SSM_GROUPS):
        x_g = xs[grp]
        csl = slice(grp * half, (grp + 1) * half)
        b_g = bcm[:, grp * SSM_STATE:(grp + 1) * SSM_STATE]
        c_g = bcm[:, (SSM_GROUPS + grp) * SSM_STATE:(SSM_GROUPS + grp + 1) * SSM_STATE].astype(BF16)
        xdt = (x_g * dt_e[:, csl]).astype(BF16)
        xdts = (x_g * dts_e[:, csl]).astype(BF16)
        cb = _dot_nt(c_g, b_g.astype(BF16))
        st_g = st_ref[:, csl]
        y_g = _dot(c_g, st_g.astype(BF16)) * wy_e[:, csl] + dsk_ref[:, csl] * x_g
        for pr in range(heads_per_group // 2):
            outs = []
            for hh in range(2):
                col = grp * heads_per_group + 2 * pr + hh
                seg = gs[:, col:col + 1] - gs_t[col:col + 1, :]
                mat = (cb * jnp.where(tri, jnp.exp2(seg), 0.0)).astype(BF16)
                outs.append(_dot(mat, xdt[:, pr * LANES:(pr + 1) * LANES]))
            lo = grp * half + pr * LANES
            y_ref[:, lo:lo + LANES] = y_g[:, pr * LANES:(pr + 1) * LANES] + jnp.where(first, outs[0], outs[1])
        st_ref[:, csl] = tot_e[:, csl] * st_g + _dot(b_g.T.astype(BF16), xdts)

    @pl.when(z == nc - 1)
    def _():
        sout_ref[...] = st_ref[...]


def _ssd(proj, tail, conv_w, conv_b, dt_bias, a_log, d_skip, init, *, batch, n_tok, row0, x_col, dt_col, tail_col):
    q_len = SSD_CHUNK
    nc = n_tok // q_len
    half = SSM_D_INNER // SSM_GROUPS
    conv_dim = SSM_D_INNER + 2 * SSM_GROUPS * SSM_STATE
    rb0 = row0 // q_len
    hb = q_len // SUBLANES
    n_halo = proj.shape[0] // SUBLANES
    zz = lambda d, z: jnp.where(d == 0, z, nc - 1 - z)
    rb = lambda b, d, z: rb0 + b * nc + zz(d, z)
    rbx = lambda b, d, z: rb0 + b * nc + jnp.where(d == 0, z, nc - 1)
    main = lambda c: pl.BlockSpec((q_len, half), lambda b, d, z: (rbx(b, d, z), c))
    prev = lambda c: pl.BlockSpec((SUBLANES, half), lambda b, d, z: (jnp.maximum(rbx(b, d, z) * hb - 1, 0), c))
    nxt = lambda c: pl.BlockSpec((SUBLANES, half),
                                 lambda b, d, z: (jnp.minimum(rbx(b, d, z) * hb + hb, n_halo - 1), c))
    bc_col = x_col + SSM_D_INNER
    assert bc_col == tail_col and conv_dim - SSM_D_INNER == half and (dt_col - tail_col) % LANES == 0
    cols = [x_col // half, x_col // half + 1, (bc_col - tail_col) // half]
    const2 = lambda shape: pl.BlockSpec(shape, lambda b, d, z: (0, 0))
    prm = jnp.zeros((2, SUBLANES, LANES), F32)
    prm = prm.at[:, 0, :SSM_HEADS].set(dt_bias.astype(F32)).at[:, 1, :SSM_HEADS].set(a_log.astype(F32))
    dsk = jnp.repeat(d_skip.astype(F32), SSM_HEAD_DIM, axis=-1).reshape(2, 1, SSM_D_INNER)
    ex = (np.arange(LANES)[:, None] == (np.arange(SSM_D_INNER) // SSM_HEAD_DIM)[None, :])
    ex = jnp.asarray(ex, BF16)
    cw = jnp.zeros((SUBLANES, conv_dim), F32).at[:SSM_CONV_W].set(conv_w.astype(F32))
    state_spec = pl.BlockSpec((None, None, SSM_STATE, SSM_D_INNER), lambda b, d, z: (b, d, 0, 0))
    y, s_out = pl.pallas_call(
        functools.partial(_ssd_body, nc=nc),
        grid=(batch, 2, nc),
        in_specs=[main(cols[0]), main(cols[1]), main(cols[2]), prev(cols[0]), prev(cols[1]), prev(cols[2]),
                  nxt(cols[0]), nxt(cols[1]), nxt(cols[2]),
                  pl.BlockSpec((q_len, LANES), lambda b, d, z: (rb(b, d, z), (dt_col - tail_col) // LANES)),
                  const2((SUBLANES, conv_dim)), const2((1, conv_dim)),
                  pl.BlockSpec((None, SUBLANES, LANES), lambda b, d, z: (d, 0, 0)),
                  pl.BlockSpec((None, 1, SSM_D_INNER), lambda b, d, z: (d, 0, 0)),
                  const2((LANES, SSM_D_INNER)), state_spec],
        out_specs=[pl.BlockSpec((None, q_len, SSM_D_INNER), lambda b, d, z: (d, b * nc + zz(d, z), 0)),
                   state_spec],
        out_shape=[jax.ShapeDtypeStruct((2, batch * n_tok, SSM_D_INNER), F32),
                   jax.ShapeDtypeStruct((batch, 2, SSM_STATE, SSM_D_INNER), F32)],
        scratch_shapes=[pltpu.VMEM((SSM_STATE, SSM_D_INNER), F32), pltpu.VMEM((nc, q_len, conv_dim), F32)],
        compiler_params=_cparams(("arbitrary", "arbitrary", "arbitrary")), name="ssd_scan",
    )(proj, proj, tail, proj, proj, tail, proj, proj, tail, tail, cw, conv_b.reshape(1, conv_dim).astype(F32),
      prm, dsk, ex, init)
    return y, s_out


def _ssm_norm_body(y_ref, z_ref, g_ref, o_ref):
    yz = (y_ref[0] + y_ref[1]) * _silu(z_ref[...])
    gw = SSM_D_INNER // SSM_GROUPS
    for grp in range(SSM_GROUPS):
        v = yz[:, grp * gw:(grp + 1) * gw]
        v = v * lax.rsqrt(jnp.mean(v * v, axis=-1, keepdims=True) + EPS)
        o_ref[:, grp * gw:(grp + 1) * gw] = (v * g_ref[:, grp * gw:(grp + 1) * gw]).astype(o_ref.dtype)


def _ssm_norm(y, proj, g, *, tm, row0, z_col):
    n = y.shape[1]
    zb0 = row0 // tm
    return pl.pallas_call(
        _ssm_norm_body, grid=(n // tm,),
        in_specs=[pl.BlockSpec((2, tm, SSM_D_INNER), lambda i: (0, i, 0)),
                  pl.BlockSpec((tm, SSM_D_INNER), lambda i: (zb0 + i, z_col // SSM_D_INNER)),
                  pl.BlockSpec((1, SSM_D_INNER), lambda i: (0, 0))],
        out_specs=pl.BlockSpec((tm, SSM_D_INNER), lambda i: (i, 0)),
        out_shape=jax.ShapeDtypeStruct((n, SSM_D_INNER), BF16),
        compiler_params=_cparams(("arbitrary",)), name="ssm_gated_norm",
    )(y, proj, g.reshape(1, SSM_D_INNER).astype(F32))


def _two_source_specs(block, first_blocks, row_block_of):
    first = pl.BlockSpec(block, lambda *g: (jnp.minimum(row_block_of(*g), first_blocks - 1), 0))
    second = pl.BlockSpec(block, lambda *g: (jnp.maximum(row_block_of(*g) - first_blocks, 0), 0))
    return first, second


def _merge_body(*refs, lat_blocks, has_ctx):
    n_src = 6 if has_ctx else 3
    src, (ga_ref, gb_ref, gc_ref, wa_ref, wb_ref, wc_ref, o_ref, wab, wbb, wcb) = refs[:n_src], refs[n_src:]
    i = pl.program_id(1)

    @pl.when(i == 0)
    def _():
        wab[...] = wa_ref[...].astype(BF16)
        wbb[...] = wb_ref[...].astype(BF16)
        wcb[...] = wc_ref[...].astype(BF16)

    if has_ctx:
        oa, ob, oc = (jnp.where(i < lat_blocks, src[2 * n][...], src[2 * n + 1][...]) for n in range(3))
    else:
        oa, ob, oc = (r[...] for r in src)
    acc = ga_ref[...].astype(F32) * _dot(oa, wab[...])
    acc = acc + gb_ref[...].astype(F32) * _dot(ob, wbb[...])
    acc = acc + gc_ref[...].astype(F32) * _dot(oc, wcb[...])
    o_ref[...] = acc.astype(o_ref.dtype)


def _merge(o_lat, o_ctx, gates, w_a, w_b, w_c, layer, *, tm, tn, n_rows):
    d = w_a.shape[2]
    nj = d // tn
    lat_blocks = o_lat[0].shape[0] // tm
    gate = lambda x: pl.BlockSpec((tm, tn), lambda j, i: (i, x * nj + j))
    wsp = lambda kk: pl.BlockSpec((None, kk, tn), lambda j, i: (layer, 0, j))
    widths = (w_a.shape[1], w_b.shape[1], w_c.shape[1])
    src_specs, src = [], []
    for n, kk in enumerate(widths):
        if o_ctx is None:
            src_specs.append(pl.BlockSpec((tm, kk), lambda j, i: (i, 0)))
            src.append(o_lat[n])
        else:
            src_specs += _two_source_specs((tm, kk), lat_blocks, lambda j, i: i)
            src += [o_lat[n], o_ctx[n]]
    return pl.pallas_call(
        functools.partial(_merge_body, lat_blocks=lat_blocks, has_ctx=o_ctx is not None), grid=(nj, n_rows // tm),
        in_specs=src_specs + [gate(0), gate(1), gate(2)] + [wsp(kk) for kk in widths],
        out_specs=pl.BlockSpec((tm, tn), lambda j, i: (i, j)),
        out_shape=jax.ShapeDtypeStruct((n_rows, d), BF16),
        scratch_shapes=[pltpu.VMEM((kk, tn), BF16) for kk in widths],
        compiler_params=_cparams(("arbitrary", "arbitrary")), name="branch_merge",
    )(*src, gates, gates, gates, w_a, w_b, w_c)


def _route(top_idx, blk):
    n = top_idx.shape[0]
    e_flat = top_idx.reshape(-1)
    onehot = (e_flat[:, None] == jnp.arange(N_EXPERTS, dtype=jnp.int32)[None, :]).astype(jnp.int32)
    counts = jnp.sum(onehot, axis=0)
    rank = jnp.sum((jnp.cumsum(onehot, axis=0) - onehot) * onehot, axis=1)
    padded = (counts + blk - 1) // blk * blk
    pad_end = jnp.cumsum(padded)
    pad_start = pad_end - padded
    dest = jnp.sum(onehot * pad_start[None, :], axis=1) + rank
    n_blocks = -(-(n * TOP_K) // blk) + N_EXPERTS
    cap = n_blocks * blk
    tok_flat = jnp.repeat(jnp.arange(n, dtype=jnp.int32), TOP_K)
    slot_tok = jnp.zeros((cap,), jnp.int32).at[dest].set(tok_flat)
    blk_start = jnp.arange(n_blocks, dtype=jnp.int32) * blk
    block_expert = jnp.clip(jnp.sum((pad_end[None, :] <= blk_start[:, None]).astype(jnp.int32), axis=1),
                            0, N_EXPERTS - 1)
    n_used = (pad_end[-1] // blk).astype(jnp.int32).reshape(1)
    return dest.astype(jnp.int32), slot_tok, block_expert, n_used


GATHER_UNROLL = 8


def _row_copy(src_hbm, row, buf, slot, sem):
    return pltpu.make_async_copy(src_hbm.at[pl.ds(row, 1), :], buf.at[pl.ds(slot, 1), :], sem)


def _gather_body(nv_ref, idx_ref, src_hbm, o_ref, buf, sem, *, rows):
    @pl.when(pl.program_id(0) >= nv_ref[0])
    def _():
        o_ref[...] = jnp.zeros_like(o_ref)

    @pl.when(pl.program_id(0) < nv_ref[0])
    def _():
        def issue(g, c):
            for u in range(GATHER_UNROLL):
                r = g * GATHER_UNROLL + u
                _row_copy(src_hbm, idx_ref[0, r], buf, r, sem).start(priority=u % 2)
            return c

        def drain(g, c):
            for u in range(GATHER_UNROLL):
                r = g * GATHER_UNROLL + u
                _row_copy(src_hbm, idx_ref[0, r], buf, r, sem).wait()
            return c

        lax.fori_loop(0, rows // GATHER_UNROLL, issue, 0)
        lax.fori_loop(0, rows // GATHER_UNROLL, drain, 0)
        o_ref[...] = buf[...].astype(o_ref.dtype)


def _gather_rows(src, idx, n_used, *, rows, out_dtype):
    n, d = src.shape
    nb = idx.shape[0] // rows
    assert rows % GATHER_UNROLL == 0
    blk = lambda i, nv: jnp.minimum(i, nv[0] - 1)
    grid_spec = pltpu.PrefetchScalarGridSpec(
        num_scalar_prefetch=1, grid=(nb,),
        in_specs=[pl.BlockSpec((None, 1, rows), lambda i, nv: (blk(i, nv), 0, 0), memory_space=pltpu.SMEM),
                  pl.BlockSpec(memory_space=pl.ANY)],
        out_specs=pl.BlockSpec((rows, d), lambda i, nv: (i, 0)),
        scratch_shapes=[pltpu.VMEM((rows, d), src.dtype), pltpu.SemaphoreType.DMA])
    return pl.pallas_call(
        functools.partial(_gather_body, rows=rows), grid_spec=grid_spec,
        out_shape=jax.ShapeDtypeStruct((nb * rows, d), out_dtype),
        compiler_params=_cparams(("arbitrary",)), name="moe_dispatch",
    )(n_used, idx.reshape(nb, 1, rows), src)


def _combine_body(dest_ref, h_ref, rt_ref, gate_ref, g_ref, y_hbm, o_ref, buf, sem, *, rows, norm):
    def issue(g, c):
        for u in range(GATHER_UNROLL):
            r = g * GATHER_UNROLL + u
            for k in range(TOP_K):
                _row_copy(y_hbm, dest_ref[0, r * TOP_K + k], buf.at[k], r, sem).start(priority=k)
        return c

    def drain(g, c):
        for u in range(GATHER_UNROLL):
            r = g * GATHER_UNROLL + u
            for k in range(TOP_K):
                _row_copy(y_hbm, dest_ref[0, r * TOP_K + k], buf.at[k], r, sem).wait()
        return c

    lax.fori_loop(0, rows // GATHER_UNROLL, issue, 0)
    lax.fori_loop(0, rows // GATHER_UNROLL, drain, 0)
    rt = rt_ref[...]
    f = buf[0] * rt[:, TOP_K:TOP_K + 1] + buf[1] * rt[:, TOP_K + 1:TOP_K + 2]
    x = h_ref[...] + gate_ref[...] * f
    if norm:
        x = x * lax.rsqrt(jnp.mean(x * x, axis=-1, keepdims=True) + EPS) * g_ref[...]
    o_ref[...] = x


def _combine(h, y_sorted, dest, rt, mod, mod_idx_fn, part, g, *, rows, n_rows, norm):
    d = h.shape[1]
    nb = n_rows // rows
    row = pl.BlockSpec((rows, d), lambda i: (i, 0))
    return pl.pallas_call(
        functools.partial(_combine_body, rows=rows, norm=norm), grid=(nb,),
        in_specs=[pl.BlockSpec((None, 1, rows * TOP_K), lambda i: (i, 0, 0), memory_space=pltpu.SMEM),
                  row, pl.BlockSpec((rows, LANES), lambda i: (i, 0)),
                  pl.BlockSpec((None, 1, d), lambda i: (mod_idx_fn(i), 0, part)),
                  pl.BlockSpec((1, d), lambda i: (0, 0)), pl.BlockSpec(memory_space=pl.ANY)],
        out_specs=row, out_shape=jax.ShapeDtypeStruct((n_rows, d), F32),
        scratch_shapes=[pltpu.VMEM((TOP_K, rows, d), y_sorted.dtype), pltpu.SemaphoreType.DMA],
        compiler_params=_cparams(("arbitrary",)), name="moe_combine",
    )(dest.reshape(nb, 1, rows * TOP_K), h, rt, mod, g.reshape(1, d), y_sorted)


def kernel(x, c, ctx, c_ctx, ada_w, ada_b, norm_mix_g, norm_ffn_g, w_in, na_rpb, da_lambda, da_subln_g,
           ssm_conv_w, ssm_conv_b, ssm_dt_bias, ssm_a_log, ssm_d, ssm_norm_g, w_branch_a, w_branch_b,
           w_branch_c, w_out, ffn_w1, ffn_w3, ffn_w2, moe_router_w, moe_router_b, moe_w1, moe_w3, moe_w2,
           final_norm_g):
    batch, seq, d = x.shape
    ctx_len = ctx.shape[1]
    depth = ada_w.shape[0]
    n_lat, n_ctx = batch * seq, batch * ctx_len
    n_all = n_lat + n_ctx
    tm = next(t for t in ROW_TILES if n_ctx % t == 0 and seq % t == 0)
    n_mod = 16
    assert batch + 1 <= n_mod
    lat_blocks = n_lat // tm
    per_seq = seq // tm
    te = next(t for t in EW_ROW_TILES if n_ctx % t == 0 and seq % t == 0)
    mod_idx_fn = lambda i: jnp.where(i < n_lat // te, i // (seq // te), batch)
    mod_idx_all = jnp.where(jnp.arange(n_all // tm) < lat_blocks, jnp.arange(n_all // tm) // per_seq, batch)
    layer_blocks = lambda nb, idx: jnp.full((nb,), idx, jnp.int32)

    na_w, da_w = NA_HEADS * NA_HEAD_DIM, DA_HEADS * 2 * DA_HEAD_DIM
    conv_dim = SSM_D_INNER + 2 * SSM_GROUPS * SSM_STATE
    col_qa, col_ka, col_va = 0, na_w, 2 * na_w
    col_qb, col_kb, col_vb = 3 * na_w, 3 * na_w + da_w, 3 * na_w + 2 * da_w
    col_z = 3 * na_w + 3 * da_w
    col_x = col_z + SSM_D_INNER
    col_dt = col_x + conv_dim
    col_gate = col_dt + 2 * SSM_HEADS
    n_main = col_x + SSM_D_INNER
    n_tail = conv_dim - SSM_D_INNER + LANES
    assert n_main % TN_WIDE == 0 and col_dt == n_main + n_tail - LANES and 2 * SSM_HEADS <= LANES

    h, h_ctx = x.reshape(n_lat, d), ctx.reshape(n_ctx, d)
    cond =jnp.zeros((n_mod, d), F32).at[:batch].set(jax.nn.silu(c)).at[batch].set(jax.nn.silu(c_ctx))
    cond = cond.astype(BF16)
    rope_tabs = _rope_tables(seq, tm)
    zero_state = jnp.zeros((batch, 2, SSM_STATE, SSM_D_INNER), F32)
    w_in_t = jnp.swapaxes(w_in, 1, 2)

    out = None
    for li in range(depth):
        last = li == depth - 1
        lam_init = 0.8 - 0.6 * math.exp(-0.3 * li)
        n_out = n_lat if last else n_all
        mod = _gmm(cond, [ada_w], layer_blocks(1, li), tm=n_mod, tn=TN_FFN, epi="bias",
                   bias=ada_b[li].reshape(1, 6 * d)).reshape(n_mod, 1, 6 * d)
        u = _norm(h, norm_mix_g[li], tm=te, n_rows=n_all, mod=mod, mod_idx_fn=mod_idx_fn, parts=(0, 1), x2=h_ctx)
        proj = _gmm(u, [w_in_t], layer_blocks(n_all // tm, li), tm=tm, tn=TN_WIDE, n_cols=n_main, w_rows=0)
        tail = _gmm(u, [w_in_t], layer_blocks(n_all // tm, li), tm=tm, tn=n_tail, n_cols=n_tail, w_rows=n_main)
        gates = _gmm(u, [w_in_t], layer_blocks(n_out // tm, li), tm=tm, tn=TN_WIDE, n_rows=n_out, epi="sigmoid",
                     out_dtype=BF16, n_cols=3 * d, w_rows=col_gate)

        o_a = _na(proj, na_rpb[li], batch=batch, seq=seq, ctx_len=ctx_len, q_col=col_qa, k_col=col_ka, v_col=col_va)
        q_r, k_r = _rope(proj, rope_tabs, tm=tm, n_rows=n_all, n_lat_rows=n_lat, seq=seq,
                         q_col=col_qb, k_col=col_kb, width=da_w)
        o_b = _da(q_r, k_r, proj, da_lambda[li].astype(F32), da_subln_g[li], lam_init, batch=batch, seq=seq,
                  ctx_len=ctx_len, v_col=col_vb, tq=DA_Q_TILE, latent=True)
        ssd_args = (ssm_conv_w[li], ssm_conv_b[li], ssm_dt_bias[li], ssm_a_log[li], ssm_d[li])
        ssd_cols = dict(x_col=col_x, dt_col=col_dt, tail_col=n_main)
        y_ctx, s_ctx = _ssd(proj, tail, *ssd_args, zero_state, batch=batch, n_tok=ctx_len, row0=n_lat, **ssd_cols)
        y_lat, _ = _ssd(proj, tail, *ssd_args, s_ctx, batch=batch, n_tok=seq, row0=0, **ssd_cols)
        o_c = _ssm_norm(y_lat, proj, ssm_norm_g[li], tm=te, row0=0, z_col=col_z)
        if not last:
            o_a_c = _ctx_attn(proj, batch=batch, seq=seq, ctx_len=ctx_len, q_col=col_qa, k_col=col_ka, v_col=col_va)
            o_b_c = _da(q_r, k_r, proj, da_lambda[li].astype(F32), da_subln_g[li], lam_init, batch=batch, seq=seq,
                        ctx_len=ctx_len, v_col=col_vb, tq=ctx_len, latent=False)
            o_c_c = _ssm_norm(y_ctx, proj, ssm_norm_g[li], tm=te, row0=n_lat, z_col=col_z)
        merged = _merge((o_a, o_b, o_c), None if last else (o_a_c, o_b_c, o_c_c), gates,
                        w_branch_a, w_branch_b, w_branch_c, li, tm=te, tn=TN_WIDE, n_rows=n_out)
        h = _gmm(merged, [w_out], layer_blocks(n_out // tm, li), tm=tm, tn=TN_WIDE, epi="resid", res=h, res2=h_ctx,
                 mod=mod, mod_idx=mod_idx_all[:n_out // tm], mod_part=2, n_rows=n_out)
        h_ctx = None

        j = li // 2
        if li % 2 == 0:
            tokens = _norm(h, norm_ffn_g[li], tm=te, n_rows=n_out, mod=mod, mod_idx_fn=mod_idx_fn, parts=(3, 4))
            hid = _gmm(tokens, [ffn_w1, ffn_w3], layer_blocks(n_out // tm, j), tm=tm, tn=TN_FFN,
                       epi="swiglu", out_dtype=BF16)
            h = _gmm(hid, [ffn_w2], layer_blocks(n_out // tm, j), tm=tm, tn=TN_FFN, epi="resid", res=h,
                     mod=mod, mod_idx=mod_idx_all[:n_out // tm], mod_part=5, n_rows=n_out, w_buffers=1)
            if last:
                out = _norm(h, final_norm_g, tm=te, n_rows=n_lat, out_dtype=F32)
        else:
            rw = jnp.zeros((d, LANES), F32).at[:, :N_EXPERTS].set(moe_router_w[j].astype(F32))
            rb = jnp.zeros((1, LANES), F32).at[0, :N_EXPERTS].set(moe_router_b[j].astype(F32))
            tokens, rt = _norm(h, norm_ffn_g[li], tm=te, n_rows=n_out, mod=mod, mod_idx_fn=mod_idx_fn,
                               parts=(3, 4), router=(rw, rb), out_dtype=F32)
            dest, slot_tok, block_expert, n_used = _route(rt[:, :TOP_K].astype(jnp.int32), MOE_ROWS)
            block_expert = block_expert + j * N_EXPERTS
            stack = lambda w: w.reshape((-1,) + w.shape[2:])
            x_sorted = _gather_rows(tokens, slot_tok, n_used, rows=MOE_ROWS, out_dtype=BF16)
            hid = _gmm(x_sorted, [stack(moe_w1), stack(moe_w3)], block_expert, tm=MOE_ROWS, tn=TN_FFN, epi="swiglu",
                       out_dtype=BF16, n_used=n_used)
            y_sorted = _gmm(hid, [stack(moe_w2)], block_expert, tm=MOE_ROWS, tn=TN_FFN, n_used=n_used, w_buffers=1)
            h = _combine(h, y_sorted, dest, rt, mod, mod_idx_fn, 5, final_norm_g, rows=te, n_rows=n_out, norm=last)
            if last:
                out = h
    return out.reshape(batch, seq, d)
```

```python
import functools
import math

import numpy as np
import jax
import jax.numpy as jnp
from jax import lax
from jax.experimental import pallas as pl
from jax.experimental.pallas import tpu as pltpu

F32 = jnp.float32
BF16 = jnp.bfloat16

GRID_W = 64
NA_HEADS, NA_HEAD_DIM, NA_WIN_H, NA_WIN_W = 8, 64, 8, 16
DA_HEADS, DA_HEAD_DIM = 4, 64
ROPE_BASE = 10000.0
SSM_D_INNER, SSM_HEAD_DIM, SSM_GROUPS, SSM_STATE, SSM_CONV_W = 1024, 64, 2, 128, 5
SSM_HEADS = SSM_D_INNER // SSM_HEAD_DIM
N_EXPERTS, TOP_K = 8, 2
EPS = 1e-6
NEG_INF = -1e30

LANES = 128
SUBLANES = 8
VMEM_LIMIT_BYTES = 56 * 1024 * 1024

ROW_TILES = (1024, 512, 256)
EW_ROW_TILES = (512, 256)
TN_WIDE = 1024
TN_FFN = 512
MOE_ROWS = 1024
SSD_CHUNK = 128
NA_ROW_UNROLL = 2
DA_Q_TILE = 1024
DA_KEY_CHUNK = 512
DA_STREAMS = 4
LOG2E = math.log2(math.e)


def _cparams(sem):
    return pltpu.CompilerParams(dimension_semantics=sem, vmem_limit_bytes=VMEM_LIMIT_BYTES)


def _dot(a, b):
    return jnp.dot(a, b, preferred_element_type=F32)


def _dot_nt(a, b):
    return lax.dot_general(a, b, (((1,), (1,)), ((), ())), preferred_element_type=F32)


def _sigmoid(x):
    return 0.5 * jnp.tanh(0.5 * x) + 0.5


def _silu(x):
    return x * _sigmoid(x)


def _gmm_body(bexp_ref, neww_ref, mod_ref, nv_ref, x_ref, *rest, nk, n_w, epi, res_first_blocks, w_transposed):
    del bexp_ref, mod_ref
    w_refs, rest = rest[:n_w], rest[n_w:]
    if epi == "bias":
        bias_ref, rest = rest[0], rest[1:]
    elif epi == "resid":
        res_ref, rest = rest[0], rest[1:]
        if res_first_blocks is not None:
            res2_ref, rest = rest[0], rest[1:]
        gate_ref, rest = rest[0], rest[1:]
    o_ref, rest = rest[0], rest[1:]
    wbf_refs, acc_refs = rest[:n_w], rest[n_w:]
    i = pl.program_id(1)
    k = pl.program_id(2)

    def finalize(vals):
        if epi == "swiglu":
            o_ref[...] = (_silu(vals[0]) * vals[1]).astype(o_ref.dtype)
        elif epi == "bias":
            o_ref[...] = (vals[0] + bias_ref[...]).astype(o_ref.dtype)
        elif epi == "resid":
            res = res_ref[...]
            if res_first_blocks is not None:
                res = jnp.where(i < res_first_blocks, res, res2_ref[...])
            o_ref[...] = (res + gate_ref[...] * vals[0]).astype(o_ref.dtype)
        elif epi == "sigmoid":
            o_ref[...] = _sigmoid(vals[0]).astype(o_ref.dtype)
        else:
            o_ref[...] = vals[0].astype(o_ref.dtype)

    @pl.when(jnp.logical_and(i >= nv_ref[0], k == nk - 1))
    def _():
        o_ref[...] = jnp.zeros_like(o_ref)

    @pl.when(i < nv_ref[0])
    def _():
        @pl.when(neww_ref[i] == 1)
        def _():
            for w_ref, wbf in zip(w_refs, wbf_refs):
                w = w_ref[...]
                wbf[k] = (w.T if w_transposed else w).astype(BF16)

        x = x_ref[...]
        parts = [_dot(x, wbf[k]) for wbf in wbf_refs]
        if nk == 1:
            finalize(parts)
        else:
            @pl.when(k == 0)
            def _():
                for acc, p in zip(acc_refs, parts):
                    acc[...] = p

            @pl.when(k > 0)
            def _():
                for acc, p in zip(acc_refs, parts):
                    acc[...] += p

            @pl.when(k == nk - 1)
            def _():
                finalize([acc[...] for acc in acc_refs])


def _gmm(x, ws, bexp, *, tm, tn, nk=1, n_cols=None, epi="plain", out_dtype=F32, bias=None, res=None,
         res2=None, mod=None, mod_idx=None, mod_part=0, n_rows=None, n_used=None, w_buffers=2, w_rows=None):
    m_rows = x.shape[0] if n_rows is None else n_rows
    k_dim = x.shape[1]
    n_dim = ws[0].shape[2] if n_cols is None else n_cols
    n_w = len(ws)
    tk = k_dim // nk
    nb = m_rows // tm
    res_first_blocks = None
    assert m_rows % tm == 0 and n_dim % tn == 0 and k_dim % nk == 0
    bexp = bexp.astype(jnp.int32)
    neww = jnp.concatenate([jnp.ones((1,), jnp.int32), (bexp[1:] != bexp[:-1]).astype(jnp.int32)])
    if mod_idx is None:
        mod_idx = jnp.zeros((nb,), jnp.int32)
    if n_used is None:
        n_used = jnp.full((1,), nb, jnp.int32)

    rb = lambda i, nv: jnp.minimum(i, nv[0] - 1)
    x_spec = pl.BlockSpec((tm, tk), lambda j, i, k, be, nw, md, nv: (rb(i, nv), k))
    kb_of = lambda i, k, nw, nv: jnp.where(nw[rb(i, nv)] == 1, k, nk - 1)
    if w_rows is None:
        w_block = (None, tk, tn)
        w_map = lambda j, i, k, be, nw, md, nv: (be[rb(i, nv)], kb_of(i, k, nw, nv), j)
    else:
        n_total = ws[0].shape[1]
        assert n_total % SUBLANES == 0 and w_rows % SUBLANES == 0 and tn % SUBLANES == 0
        ws = [w.reshape(-1, k_dim) for w in ws]
        w_block = (pl.Element(tn), pl.Element(tk))
        w_map = lambda j, i, k, be, nw, md, nv: (
            pl.multiple_of(be[rb(i, nv)] * n_total + w_rows + j * tn, SUBLANES),
            pl.multiple_of(kb_of(i, k, nw, nv) * tk, LANES))
    if w_buffers == 2:
        w_spec = pl.BlockSpec(w_block, w_map)
    else:
        w_spec = pl.BlockSpec(w_block, w_map, pipeline_mode=pl.Buffered(w_buffers))
    o_spec = pl.BlockSpec((tm, tn), lambda j, i, k, be, nw, md, nv: (i, j))
    in_specs = [x_spec] + [w_spec] * n_w
    args = [x] + list(ws)
    if epi == "bias":
        in_specs.append(pl.BlockSpec((1, tn), lambda j, i, k, be, nw, md, nv: (0, j)))
        args.append(bias)
    elif epi == "resid":
        part_off = mod_part * (n_dim // tn)
        if res2 is None:
            in_specs.append(o_spec)
            args.append(res)
        else:
            res_first_blocks = res.shape[0] // tm
            in_specs.append(pl.BlockSpec(
                (tm, tn), lambda j, i, k, be, nw, md, nv: (jnp.minimum(i, res_first_blocks - 1), j)))
            in_specs.append(pl.BlockSpec(
                (tm, tn), lambda j, i, k, be, nw, md, nv: (jnp.maximum(i - res_first_blocks, 0), j)))
            args += [res, res2]
        in_specs.append(pl.BlockSpec((None, 1, tn),
                                     lambda j, i, k, be, nw, md, nv: (md[rb(i, nv)], 0, part_off + j)))
        args.append(mod)
    scratch = [pltpu.VMEM((nk, tk, tn), BF16) for _ in range(n_w)]
    if nk > 1:
        scratch += [pltpu.VMEM((tm, tn), F32) for _ in range(n_w)]
    grid_spec = pltpu.PrefetchScalarGridSpec(
        num_scalar_prefetch=4, grid=(n_dim // tn, nb, nk), in_specs=in_specs, out_specs=o_spec,
        scratch_shapes=scratch)
    return pl.pallas_call(
        functools.partial(_gmm_body, nk=nk, n_w=n_w, epi=epi, res_first_blocks=res_first_blocks,
                          w_transposed=w_rows is not None),
        grid_spec=grid_spec,
        out_shape=jax.ShapeDtypeStruct((m_rows, n_dim), out_dtype),
        compiler_params=_cparams(("arbitrary", "arbitrary", "arbitrary")),
        name="gmm_" + epi,
    )(bexp, neww, mod_idx.astype(jnp.int32), n_used.astype(jnp.int32), *args)


def _split2(v):
    hi = v.astype(BF16)
    lo = (v - hi.astype(F32)).astype(BF16)
    return hi, lo


def _norm_body(x_ref, *rest, modulate, router, first_blocks):
    if first_blocks is not None:
        x2_ref, rest = rest[0], rest[1:]
    g_ref, rest = rest[0], rest[1:]
    if modulate:
        shift_ref, scale_ref, rest = rest[0], rest[1], rest[2:]
    if router:
        rw_ref, rb_ref, rest = rest[0], rest[1], rest[2:]
    o_ref = rest[0]
    x = x_ref[...]
    if first_blocks is not None:
        x = jnp.where(pl.program_id(0) < first_blocks, x, x2_ref[...])
    y = x * lax.rsqrt(jnp.mean(x * x, axis=-1, keepdims=True) + EPS) * g_ref[...]
    if modulate:
        y = y * (1.0 + scale_ref[...]) + shift_ref[...]
    o_ref[...] = y.astype(o_ref.dtype)
    if router:
        rt_ref = rest[1]
        y_hi, y_lo = _split2(y)
        w_hi, w_lo = _split2(rw_ref[...])
        logits = _dot(y_hi, w_hi) + _dot(y_lo, w_hi) + _dot(y_hi, w_lo) + rb_ref[...]
        lane = lax.broadcasted_iota(jnp.int32, logits.shape, 1).astype(F32)
        lg = jnp.where(lane < N_EXPERTS, logits, -jnp.inf)
        m1 = jnp.max(lg, axis=-1, keepdims=True)
        i1 = jnp.min(jnp.where(lg == m1, lane, float(LANES)), axis=-1, keepdims=True)
        lg2 = jnp.where(lane == i1, -jnp.inf, lg)
        m2 = jnp.max(lg2, axis=-1, keepdims=True)
        i2 = jnp.min(jnp.where(lg2 == m2, lane, float(LANES)), axis=-1, keepdims=True)
        e2 = jnp.exp(m2 - m1)
        g1 = 1.0 / (1.0 + e2)
        rt = jnp.where(lane == 0, i1, jnp.where(lane == 1, i2, jnp.where(lane == 2, g1, jnp.where(lane == 3, e2 * g1, 0.0))))
        rt_ref[...] = rt


def _norm(x, g, *, tm, n_rows, mod=None, mod_idx_fn=None, parts=(0, 1), out_dtype=BF16, router=None, x2=None):
    d = x.shape[1]
    nb = n_rows // tm
    row = pl.BlockSpec((tm, d), lambda i: (i, 0))
    first_blocks = None if x2 is None else x.shape[0] // tm
    if x2 is None:
        in_specs, args = [row], [x]
    else:
        in_specs, args = list(_two_source_specs((tm, d), first_blocks, lambda i: i)), [x, x2]
    in_specs.append(pl.BlockSpec((1, d), lambda i: (0, 0)))
    args.append(g.reshape(1, d))
    modulate = mod is not None
    if modulate:
        for p in parts:
            in_specs.append(pl.BlockSpec((None, 1, d), lambda i, p=p: (mod_idx_fn(i), 0, p)))
            args.append(mod)
    out_shape = [jax.ShapeDtypeStruct((n_rows, d), out_dtype)]
    out_specs = [row]
    if router is not None:
        rw, rb = router
        in_specs += [pl.BlockSpec((d, LANES), lambda i: (0, 0)), pl.BlockSpec((1, LANES), lambda i: (0, 0))]
        args += [rw, rb]
        out_shape.append(jax.ShapeDtypeStruct((n_rows, LANES), F32))
        out_specs.append(pl.BlockSpec((tm, LANES), lambda i: (i, 0)))
    res = pl.pallas_call(
        functools.partial(_norm_body, modulate=modulate, router=router is not None, first_blocks=first_blocks),
        grid=(nb,), in_specs=in_specs, out_specs=out_specs, out_shape=out_shape,
        compiler_params=_cparams(("arbitrary",)), name="rmsnorm",
    )(*args)
    return res if router is not None else res[0]


def _rope_body(q_ref, k_ref, c_ref, sm_ref, sp_ref, qo_ref, ko_ref, *, q_scale):
    c, sm, sp = c_ref[...], sm_ref[...], sp_ref[...]
    width = q_ref.shape[1]
    for src, dst, scale in ((q_ref, qo_ref, q_scale), (k_ref, ko_ref, 1.0)):
        for g in range(width // LANES):
            sl = slice(g * LANES, (g + 1) * LANES)
            x = src[:, sl]
            y = x * c + pltpu.roll(x, LANES - 16, 1) * sm + pltpu.roll(x, 16, 1) * sp
            dst[:, sl] = (y * scale).astype(dst.dtype)


def _rope_tables(seq, n_id_rows):
    t = jnp.arange(seq)
    pos = jnp.stack([t // GRID_W, t % GRID_W], axis=-1).astype(F32)
    n_freq = DA_HEAD_DIM // 4
    inv_freq = ROPE_BASE ** (-jnp.arange(n_freq, dtype=F32) / n_freq)
    ang = pos[:, :, None] * inv_freq
    d = np.arange(LANES) % DA_HEAD_DIM
    kind, which, f = d // 32, (d % 32) // 16, d % 16
    a = ang[:, kind, f]
    cos, sin = jnp.cos(a), jnp.sin(a)
    sm = jnp.where(which == 0, -sin, 0.0)
    sp = jnp.where(which == 1, sin, 0.0)
    pad = lambda v, fill: jnp.concatenate([v, jnp.full((n_id_rows, LANES), fill, F32)], axis=0)
    return pad(cos, 1.0), pad(sm, 0.0), pad(sp, 0.0)


def _rope(proj, tabs, *, tm, n_rows, n_lat_rows, seq, q_col, k_col, width):
    lat_blocks = n_lat_rows // tm
    per_seq = seq // tm
    tab_idx = lambda i: (jnp.where(i < lat_blocks, i % per_seq, per_seq), 0)
    out = jax.ShapeDtypeStruct((n_rows, width), BF16)
    return pl.pallas_call(
        functools.partial(_rope_body, q_scale=DA_HEAD_DIM ** -0.5 * LOG2E),
        grid=(n_rows // tm,),
        in_specs=[pl.BlockSpec((tm, width), lambda i: (i, q_col // width)),
                  pl.BlockSpec((tm, width), lambda i: (i, k_col // width)),
                  pl.BlockSpec((tm, LANES), tab_idx), pl.BlockSpec((tm, LANES), tab_idx),
                  pl.BlockSpec((tm, LANES), tab_idx)],
        out_specs=[pl.BlockSpec((tm, width), lambda i: (i, 0))] * 2,
        out_shape=[out, out], compiler_params=_cparams(("arbitrary",)), name="rope",
    )(proj, proj, *tabs)


def _na_body(q_ref, k_ref, v_ref, kc_ref, vc_ref, tb_ref, o_ref, kb, vb, kcb, vcb, *, rows, wh):
    kb[...] = k_ref[...].astype(BF16)
    vb[...] = v_ref[...].astype(BF16)
    kcb[...] = kc_ref[...].astype(BF16)
    vcb[...] = vc_ref[...].astype(BF16)
    lane = lax.broadcasted_iota(jnp.int32, (GRID_W, LANES), 1)
    first = lane < NA_HEAD_DIM
    scale = NA_HEAD_DIM ** -0.5

    row_of = lambda r: slice(r * GRID_W, (r + 1) * GRID_W)
    win_of = lambda r: slice(min(max(r - wh // 2, 0), rows - wh) * GRID_W,
                             (min(max(r - wh // 2, 0), rows - wh) + wh) * GRID_W)

    def scores(r):
        q = q_ref[row_of(r), :] * scale
        qq = jnp.concatenate([jnp.where(first, q, 0.0), jnp.where(first, 0.0, q)], axis=0).astype(BF16)
        d0 = min(max(r - wh // 2, 0), rows - wh) - r + (NA_WIN_H - 1)
        return _dot_nt(qq, kb[win_of(r), :]) + tb_ref[d0], _dot_nt(qq, kcb[...])

    def softmax(s_l, s_c):
        m = jnp.maximum(jnp.max(s_l, axis=-1, keepdims=True), jnp.max(s_c, axis=-1, keepdims=True))
        p_l, p_c = jnp.exp(s_l - m), jnp.exp(s_c - m)
        den = jnp.sum(p_l, axis=-1, keepdims=True) + jnp.sum(p_c, axis=-1, keepdims=True)
        return p_l.astype(BF16), p_c.astype(BF16), den

    def values(r, p_l, p_c, den):
        o = (_dot(p_l, vb[win_of(r), :]) + _dot(p_c, vcb[...])) / den
        o_ref[row_of(r), :] = jnp.where(first, o[:GRID_W], o[GRID_W:]).astype(o_ref.dtype)

    assert rows % NA_ROW_UNROLL == 0
    n_grp = rows // NA_ROW_UNROLL
    sc, pr = {}, {}
    for p in range(n_grp + 2):
        for j in range(NA_ROW_UNROLL):
            r3, r2, r1 = ((p - 2) * NA_ROW_UNROLL + j, (p - 1) * NA_ROW_UNROLL + j, p * NA_ROW_UNROLL + j)
            if 0 <= p - 2 < n_grp:
                values(r3, *pr.pop(r3))
            if 0 <= p - 1 < n_grp:
                pr[r2] = softmax(*sc.pop(r2))
            if p < n_grp:
                sc[r1] = scores(r1)


def _na_bias_table(rpb, rows):
    wh = min(NA_WIN_H, rows)
    qc = np.arange(GRID_W)[:, None]
    kc = np.arange(GRID_W)[None, :]
    ws = np.clip(qc - NA_WIN_W // 2, 0, GRID_W - NA_WIN_W)
    ok = (kc >= ws) & (kc < ws + NA_WIN_W)
    n_drow, n_dcol = 2 * NA_WIN_H - 1, 2 * NA_WIN_W - 1
    off = GRID_W - NA_WIN_W
    line = jnp.full((NA_HEADS, n_drow, 2 * GRID_W), NEG_INF, F32).at[:, :, off:off + n_dcol].set(rpb.astype(F32))
    span = 2 * GRID_W - 1
    t = jnp.tile(line, (1, 1, GRID_W))[:, :, :GRID_W * span].reshape(NA_HEADS, n_drow, GRID_W, span)
    t = jnp.where(ok, t[:, :, :, GRID_W - 1:], NEG_INF)
    n_d0 = NA_WIN_H
    t = jnp.stack([t[:, d0:d0 + wh] for d0 in range(n_d0)], axis=1)
    t = t.transpose(0, 1, 3, 2, 4).reshape(NA_HEADS // 2, 2, n_d0, GRID_W, wh * GRID_W)
    return t.transpose(0, 2, 1, 3, 4).reshape(NA_HEADS // 2, n_d0, 2 * GRID_W, wh * GRID_W)


def _na(proj, rpb, *, batch, seq, ctx_len, q_col, k_col, v_col):
    rows = seq // GRID_W
    wh = min(NA_WIN_H, rows)
    pairs = NA_HEADS // 2
    tb = _na_bias_table(rpb, rows)
    ctx0 = batch * seq // ctx_len
    lat = lambda col: pl.BlockSpec((seq, LANES), lambda b, p: (b, col // LANES + p))
    ctx = lambda col: pl.BlockSpec((ctx_len, LANES), lambda b, p: (ctx0 + b, col // LANES + p))
    return pl.pallas_call(
        functools.partial(_na_body, rows=rows, wh=wh),
        grid=(batch, pairs),
        in_specs=[lat(q_col), lat(k_col), lat(v_col), ctx(k_col), ctx(v_col),
                  pl.BlockSpec((None, NA_WIN_H, 2 * GRID_W, wh * GRID_W), lambda b, p: (p, 0, 0, 0))],
        out_specs=pl.BlockSpec((seq, LANES), lambda b, p: (b, p)),
        out_shape=jax.ShapeDtypeStruct((batch * seq, pairs * LANES), BF16),
        scratch_shapes=[pltpu.VMEM((seq, LANES), BF16), pltpu.VMEM((seq, LANES), BF16),
                        pltpu.VMEM((ctx_len, LANES), BF16), pltpu.VMEM((ctx_len, LANES), BF16)],
        compiler_params=_cparams(("arbitrary", "arbitrary")), name="na_attn",
    )(proj, proj, proj, proj, proj, tb)


def _ctx_attn_body(q_ref, k_ref, v_ref, o_ref):
    kb = k_ref[...].astype(BF16)
    vb = v_ref[...].astype(BF16)
    q = q_ref[...] * (NA_HEAD_DIM ** -0.5)
    lane = lax.broadcasted_iota(jnp.int32, q.shape, 1)
    first = lane < NA_HEAD_DIM
    outs = []
    for hh in range(2):
        qm = jnp.where(first if hh == 0 else jnp.logical_not(first), q, 0.0).astype(BF16)
        s = _dot_nt(qm, kb)
        p = jnp.exp(s - jnp.max(s, axis=-1, keepdims=True))
        outs.append(_dot(p.astype(BF16), vb) / jnp.sum(p, axis=-1, keepdims=True))
    o_ref[...] = jnp.where(first, outs[0], outs[1]).astype(o_ref.dtype)


def _ctx_attn(proj, *, batch, seq, ctx_len, q_col, k_col, v_col):
    pairs = NA_HEADS // 2
    ctx0 = batch * seq // ctx_len
    ctx = lambda col: pl.BlockSpec((ctx_len, LANES), lambda b, p: (ctx0 + b, col // LANES + p))
    return pl.pallas_call(
        _ctx_attn_body, grid=(batch, pairs),
        in_specs=[ctx(q_col), ctx(k_col), ctx(v_col)],
        out_specs=pl.BlockSpec((ctx_len, LANES), lambda b, p: (b, p)),
        out_shape=jax.ShapeDtypeStruct((batch * ctx_len, pairs * LANES), BF16),
        compiler_params=_cparams(("arbitrary", "arbitrary")), name="ctx_attn",
    )(proj, proj, proj)


def _da_streams(q_ref, kl_ref, kc_ref, vlb, vcb, g_ref, o_ref, s_ref, lam, lam_init):
    n_st = DA_STREAMS
    hq = q_ref.shape[0] // n_st
    ck = DA_KEY_CHUNK
    chunks = [(kl_ref, vlb, off, ck, off) for off in range(0, kl_ref.shape[0], ck)]
    chunks.append((kc_ref, vcb, 0, kc_ref.shape[0], kl_ref.shape[0]))
    lane = lax.broadcasted_iota(jnp.int32, (hq, LANES), 1)
    lane_tiles = lambda x: [x[:, i * LANES:(i + 1) * LANES] for i in range(x.shape[1] // LANES)]

    def stacked_q(st):
        q = q_ref[st * hq:(st + 1) * hq, :]
        zero = jnp.zeros_like(q)
        return jnp.concatenate([jnp.where(lane < DA_HEAD_DIM, q, zero), jnp.where(lane >= DA_HEAD_DIM, q, zero)], axis=0)

    def scores(st, qq, chunk, m_run):
        k_ref, _, off, n, c0 = chunk
        s = _dot_nt(qq, k_ref[off:off + n, :])
        s_ref[st, :, c0:c0 + n] = s
        return functools.reduce(jnp.maximum, lane_tiles(s), m_run)

    def numerators(st, chunk, m, l_run):
        _, _, _, n, c0 = chunk
        p = jnp.exp2(s_ref[st, :, c0:c0 + n] - m)
        s_ref[st, :, c0:c0 + n] = p
        return functools.reduce(jnp.add, lane_tiles(p), l_run)

    def values(st, chunk, ratio, acc):
        _, v_ref, off, n, c0 = chunk
        a = (s_ref[st, 0:hq, c0:c0 + n] - ratio * s_ref[st, hq:2 * hq, c0:c0 + n]).astype(BF16)
        return acc + _dot(a, v_ref[off:off + n, :])

    def finish(st, acc, den):
        o = acc / den[:hq]
        o = o * lax.rsqrt(jnp.mean(o * o, axis=-1, keepdims=True) + EPS) * g_ref[...]
        o_ref[st * hq:(st + 1) * hq, :] = (o * (1.0 - lam_init)).astype(o_ref.dtype)

    neg = jnp.full((2 * hq, LANES), -jnp.inf, F32)
    zero_l = jnp.zeros((2 * hq, LANES), F32)
    zero_o = jnp.zeros((hq, LANES), F32)
    row_max = lambda m_run: jnp.max(m_run, axis=-1, keepdims=True)
    row_sum = lambda l_run: jnp.sum(l_run, axis=-1, keepdims=True)
    m, den, ratio = {}, {}, {}
    for p in range(n_st + 2):
        s1, s2, s3 = p, p - 1, p - 2
        qq = stacked_q(s1) if s1 < n_st else None
        m_run, l_run, acc = neg, zero_l, zero_o
        for ch in chunks:
            if 0 <= s3 < n_st:
                acc = values(s3, ch, ratio[s3], acc)
            if 0 <= s2 < n_st:
                l_run = numerators(s2, ch, m[s2], l_run)
            if s1 < n_st:
                m_run = scores(s1, qq, ch, m_run)
        if 0 <= s3 < n_st:
            finish(s3, acc, den[s3])
        if 0 <= s2 < n_st:
            den[s2] = row_sum(l_run)
            ratio[s2] = lam * den[s2][:hq] / den[s2][hq:]
        if s1 < n_st:
            m[s1] = row_max(m_run)


def _da_body(lv_ref, q_ref, *rest, lam_init, has_lat):
    if has_lat:
        kl_ref, vl_ref, rest = rest[0], rest[1], rest[2:]
    kc_ref, vc_ref, g_ref, o_ref = rest[:4]
    scr = rest[4:]
    t = pl.program_id(2)
    if has_lat:
        vlb, vcb, s_ref = scr

        @pl.when(t == 0)
        def _():
            vlb[...] = vl_ref[...].astype(BF16)
            vcb[...] = vc_ref[...].astype(BF16)
    else:
        (vcb,) = scr
        vcb[...] = vc_ref[...].astype(BF16)

    lv = lv_ref[...]
    lam = (jnp.exp(jnp.sum(lv[0:1] * lv[1:2], axis=-1, keepdims=True))
           - jnp.exp(jnp.sum(lv[2:3] * lv[3:4], axis=-1, keepdims=True)) + lam_init)
    if has_lat:
        _da_streams(q_ref, kl_ref, kc_ref, vlb, vcb, g_ref, o_ref, s_ref, lam, lam_init)
        return
    q = q_ref[...]
    tq = q.shape[0]
    lane = lax.broadcasted_iota(jnp.int32, q.shape, 1)
    zero = jnp.zeros_like(q)
    qq = jnp.concatenate([jnp.where(lane < DA_HEAD_DIM, q, zero), jnp.where(lane >= DA_HEAD_DIM, q, zero)], axis=0)
    s_c = _dot_nt(qq, kc_ref[...])
    m = jnp.max(s_c, axis=-1, keepdims=True)
    if has_lat:
        s_l = _dot_nt(qq, kl_ref[...])
        m = jnp.maximum(m, jnp.max(s_l, axis=-1, keepdims=True))
        p_l = jnp.exp2(s_l - m)
    p_c = jnp.exp2(s_c - m)
    den = jnp.sum(p_c, axis=-1, keepdims=True)
    if has_lat:
        den = den + jnp.sum(p_l, axis=-1, keepdims=True)
    ratio = lam * den[:tq] / den[tq:]
    o = _dot((p_c[:tq] - ratio * p_c[tq:]).astype(BF16), vcb[...])
    if has_lat:
        o = o + _dot((p_l[:tq] - ratio * p_l[tq:]).astype(BF16), vlb[...])
    o = o / den[:tq]
    o = o * lax.rsqrt(jnp.mean(o * o, axis=-1, keepdims=True) + EPS) * g_ref[...]
    o_ref[...] = (o * (1.0 - lam_init)).astype(o_ref.dtype)


def _da(q_r, k_r, proj, lam_vec, subln_g, lam_init, *, batch, seq, ctx_len, v_col, tq, latent):
    ctx0 = batch * seq // ctx_len
    vcb0 = v_col // LANES
    g2 = subln_g.reshape(1, 2 * DA_HEAD_DIM).astype(F32)
    small = [pl.BlockSpec((4, DA_HEAD_DIM), lambda b, h, t: (0, 0))]
    gspec = pl.BlockSpec((1, LANES), lambda b, h, t: (0, 0))
    kc = pl.BlockSpec((ctx_len, LANES), lambda b, h, t: (ctx0 + b, h))
    vc = pl.BlockSpec((ctx_len, LANES), lambda b, h, t: (ctx0 + b, vcb0 + h))
    if latent:
        nq = seq // tq
        qs = pl.BlockSpec((tq, LANES), lambda b, h, t: (b * nq + t, h))
        kl = pl.BlockSpec((seq, LANES), lambda b, h, t: (b, h))
        vl = pl.BlockSpec((seq, LANES), lambda b, h, t: (b, vcb0 + h))
        in_specs = small + [qs, kl, vl, kc, vc, gspec]
        args = (lam_vec, q_r, k_r, proj, k_r, proj, g2)
        assert seq % DA_KEY_CHUNK == 0 and tq % DA_STREAMS == 0
        scratch = [pltpu.VMEM((seq, LANES), BF16), pltpu.VMEM((ctx_len, LANES), BF16),
                   pltpu.VMEM((DA_STREAMS, 2 * tq // DA_STREAMS, seq + ctx_len), F32)]
        n_out = batch * seq
    else:
        nq = ctx_len // tq
        cq0 = batch * seq // tq
        qs = pl.BlockSpec((tq, LANES), lambda b, h, t: (cq0 + b * nq + t, h))
        in_specs = small + [qs, kc, vc, gspec]
        args = (lam_vec, q_r, k_r, proj, g2)
        scratch = [pltpu.VMEM((ctx_len, LANES), BF16)]
        n_out = batch * ctx_len
    return pl.pallas_call(
        functools.partial(_da_body, lam_init=lam_init, has_lat=latent),
        grid=(batch, DA_HEADS, nq), in_specs=in_specs,
        out_specs=pl.BlockSpec((tq, LANES), lambda b, h, t: (b * nq + t, h)),
        out_shape=jax.ShapeDtypeStruct((n_out, DA_HEADS * LANES), BF16),
        scratch_shapes=scratch,
        compiler_params=_cparams(("arbitrary", "arbitrary", "arbitrary")),
        name="diff_attn_lat" if latent else "diff_attn_ctx",
    )(*args)


def _ssd_body(x0_ref, x1_ref, bc_ref, x0p_ref, x1p_ref, bcp_ref, x0n_ref, x1n_ref, bcn_ref, dt_ref,
              cw_ref, cb_ref, prm_ref, dsk_ref, ex_ref, init_ref, y_ref, sout_ref, st_ref, u_ref, *, nc):
    q_len = SSD_CHUNK
    dirn = pl.program_id(1)
    z = pl.program_id(2)
    zz = jnp.where(dirn == 0, z, nc - 1 - z)
    fwd = dirn == 0
    half = SSM_D_INNER // SSM_GROUPS
    heads_per_group = SSM_HEADS // SSM_GROUPS

    @pl.when(z == 0)
    def _():
        st_ref[...] = init_ref[...]

    def conv_piece(main_ref, prev_ref, next_ref, c0):
        width = main_ref.shape[1]
        prev = jnp.where(zz == 0, 0.0, prev_ref[...])
        nxt = jnp.where(zz == nc - 1, 0.0, next_ref[...])
        ext = jnp.concatenate([prev, main_ref[...], nxt], axis=0)
        n_ext = ext.shape[0]
        acc = jnp.zeros((q_len, width), F32) + cb_ref[:, c0:c0 + width]
        for tap in range(SSM_CONV_W):
            sh = (SSM_CONV_W // 2 - tap) % n_ext
            e = ext if sh == 0 else pltpu.roll(ext, sh, 0)
            acc = acc + e[SUBLANES:SUBLANES + q_len] * cw_ref[tap:tap + 1, c0:c0 + width]
        return _silu(acc)

    @pl.when(fwd)
    def _():
        u_ref[zz, :, 0:half] = conv_piece(x0_ref, x0p_ref, x0n_ref, 0)
        u_ref[zz, :, half:2 * half] = conv_piece(x1_ref, x1p_ref, x1n_ref, half)
        u_ref[zz, :, SSM_D_INNER:] = conv_piece(bc_ref, bcp_ref, bcn_ref, SSM_D_INNER)

    xs = (u_ref[zz, :, 0:half], u_ref[zz, :, half:2 * half])
    bcm = u_ref[zz, :, SSM_D_INNER:]

    raw = dt_ref[...]
    raw = jnp.where(fwd, raw, pltpu.roll(raw, LANES - SSM_HEADS, 1))
    xb = raw + prm_ref[0:1, :]
    dtv = jnp.maximum(xb, 0.0) + jnp.log1p(jnp.exp(-jnp.abs(xb)))
    a = dtv * (-jnp.exp(prm_ref[1:2, :]))
    rowi = lax.broadcasted_iota(jnp.int32, (q_len, LANES), 0)
    coli = lax.broadcasted_iota(jnp.int32, (q_len, LANES), 1)
    cum = a
    sft = 1
    while sft < q_len:
        cum = cum + jnp.where(rowi >= sft, pltpu.roll(cum, sft, 0), 0.0)
        sft *= 2
    tot = cum[q_len - 1:q_len, :]
    g = jnp.where(fwd, cum, cum - a)
    e_g = jnp.exp(g)
    e_tg = jnp.exp(tot - g)
    w_state = jnp.where(fwd, e_tg, e_g)
    w_yoff = jnp.where(fwd, e_g, e_tg)
    e_tot = jnp.exp(tot)
    gs = g * jnp.where(fwd, LOG2E, -LOG2E)
    gs_t = gs.T
    tri = jnp.where(fwd, rowi - coli, coli - rowi) >= 0

    stack = jnp.concatenate([dtv, dtv * w_state, w_yoff, jnp.broadcast_to(e_tot, (SUBLANES, LANES))], axis=0)
    s_hi, s_lo = _split2(stack)
    ex = ex_ref[...]
    expd = _dot(s_hi, ex) + _dot(s_lo, ex)
    dt_e, dts_e, wy_e = expd[0:q_len], expd[q_len:2 * q_len], expd[2 * q_len:3 * q_len]
    tot_e = expd[3 * q_len:3 * q_len + 1]

    lane = lax.broadcasted_iota(jnp.int32, (q_len, LANES), 1)
    first = lane < SSM_HEAD_DIM
    for grp in range(SSM_GROUPS):
        x_g = xs[grp]
        csl = slice(grp * half, (grp + 1) * half)
        b_g = bcm[:, grp * SSM_STATE:(grp + 1) * SSM_STATE]
        c_g = bcm[:, (SSM_GROUPS + grp) * SSM_STATE:(SSM_GROUPS + grp + 1) * SSM_STATE].astype(BF16)
        xdt = (x_g * dt_e[:, csl]).astype(BF16)
        xdts = (x_g * dts_e[:, csl]).astype(BF16)
        cb = _dot_nt(c_g, b_g.astype(BF16))
        st_g = st_ref[:, csl]
        y_g = _dot(c_g, st_g.astype(BF16)) * wy_e[:, csl] + dsk_ref[:, csl] * x_g
        for pr in range(heads_per_group // 2):
            outs = []
            for hh in range(2):
                col = grp * heads_per_group + 2 * pr + hh
                seg = gs[:, col:col + 1] - gs_t[col:col + 1, :]
                mat = (cb * jnp.where(tri, jnp.exp2(seg), 0.0)).astype(BF16)
                outs.append(_dot(mat, xdt[:, pr * LANES:(pr + 1) * LANES]))
            lo = grp * half + pr * LANES
            y_ref[:, lo:lo + LANES] = y_g[:, pr * LANES:(pr + 1) * LANES] + jnp.where(first, outs[0], outs[1])
        st_ref[:, csl] = tot_e[:, csl] * st_g + _dot(b_g.T.astype(BF16), xdts)

    @pl.when(z == nc - 1)
    def _():
        sout_ref[...] = st_ref[...]


def _ssd(proj, tail, conv_w, conv_b, dt_bias, a_log, d_skip, init, *, batch, n_tok, row0, x_col, dt_col, tail_col):
    q_len = SSD_CHUNK
    nc = n_tok // q_len
    half = SSM_D_INNER // SSM_GROUPS
    conv_dim = SSM_D_INNER + 2 * SSM_GROUPS * SSM_STATE
    rb0 = row0 // q_len
    hb = q_len // SUBLANES
    n_halo = proj.shape[0] // SUBLANES
    zz = lambda d, z: jnp.where(d == 0, z, nc - 1 - z)
    rb = lambda b, d, z: rb0 + b * nc + zz(d, z)
    rbx = lambda b, d, z: rb0 + b * nc + jnp.where(d == 0, z, nc - 1)
    main = lambda c: pl.BlockSpec((q_len, half), lambda b, d, z: (rbx(b, d, z), c))
    prev = lambda c: pl.BlockSpec((SUBLANES, half), lambda b, d, z: (jnp.maximum(rbx(b, d, z) * hb - 1, 0), c))
    nxt = lambda c: pl.BlockSpec((SUBLANES, half),
                                 lambda b, d, z: (jnp.minimum(rbx(b, d, z) * hb + hb, n_halo - 1), c))
    bc_col = x_col + SSM_D_INNER
    assert bc_col == tail_col and conv_dim - SSM_D_INNER == half and (dt_col - tail_col) % LANES == 0
    cols = [x_col // half, x_col // half + 1, (bc_col - tail_col) // half]
    const2 = lambda shape: pl.BlockSpec(shape, lambda b, d, z: (0, 0))
    prm = jnp.zeros((2, SUBLANES, LANES), F32)
    prm = prm.at[:, 0, :SSM_HEADS].set(dt_bias.astype(F32)).at[:, 1, :SSM_HEADS].set(a_log.astype(F32))
    dsk = jnp.repeat(d_skip.astype(F32), SSM_HEAD_DIM, axis=-1).reshape(2, 1, SSM_D_INNER)
    ex = (np.arange(LANES)[:, None] == (np.arange(SSM_D_INNER) // SSM_HEAD_DIM)[None, :])
    ex = jnp.asarray(ex, BF16)
    cw = jnp.zeros((SUBLANES, conv_dim), F32).at[:SSM_CONV_W].set(conv_w.astype(F32))
    state_spec = pl.BlockSpec((None, None, SSM_STATE, SSM_D_INNER), lambda b, d, z: (b, d, 0, 0))
    y, s_out = pl.pallas_call(
        functools.partial(_ssd_body, nc=nc),
        grid=(batch, 2, nc),
        in_specs=[main(cols[0]), main(cols[1]), main(cols[2]), prev(cols[0]), prev(cols[1]), prev(cols[2]),
                  nxt(cols[0]), nxt(cols[1]), nxt(cols[2]),
                  pl.BlockSpec((q_len, LANES), lambda b, d, z: (rb(b, d, z), (dt_col - tail_col) // LANES)),
                  const2((SUBLANES, conv_dim)), const2((1, conv_dim)),
                  pl.BlockSpec((None, SUBLANES, LANES), lambda b, d, z: (d, 0, 0)),
                  pl.BlockSpec((None, 1, SSM_D_INNER), lambda b, d, z: (d, 0, 0)),
                  const2((LANES, SSM_D_INNER)), state_spec],
        out_specs=[pl.BlockSpec((None, q_len, SSM_D_INNER), lambda b, d, z: (d, b * nc + zz(d, z), 0)),
                   state_spec],
        out_shape=[jax.ShapeDtypeStruct((2, batch * n_tok, SSM_D_INNER), F32),
                   jax.ShapeDtypeStruct((batch, 2, SSM_STATE, SSM_D_INNER), F32)],
        scratch_shapes=[pltpu.VMEM((SSM_STATE, SSM_D_INNER), F32), pltpu.VMEM((nc, q_len, conv_dim), F32)],
        compiler_params=_cparams(("arbitrary", "arbitrary", "arbitrary")), name="ssd_scan",
    )(proj, proj, tail, proj, proj, tail, proj, proj, tail, tail, cw, conv_b.reshape(1, conv_dim).astype(F32),
      prm, dsk, ex, init)
    return y, s_out


def _ssm_norm_body(y_ref, z_ref, g_ref, o_ref):
    yz = (y_ref[0] + y_ref[1]) * _silu(z_ref[...])
    gw = SSM_D_INNER // SSM_GROUPS
    for grp in range(SSM_GROUPS):
        v = yz[:, grp * gw:(grp + 1) * gw]
        v = v * lax.rsqrt(jnp.mean(v * v, axis=-1, keepdims=True) + EPS)
        o_ref[:, grp * gw:(grp + 1) * gw] = (v * g_ref[:, grp * gw:(grp + 1) * gw]).astype(o_ref.dtype)


def _ssm_norm(y, proj, g, *, tm, row0, z_col):
    n = y.shape[1]
    zb0 = row0 // tm
    return pl.pallas_call(
        _ssm_norm_body, grid=(n // tm,),
        in_specs=[pl.BlockSpec((2, tm, SSM_D_INNER), lambda i: (0, i, 0)),
                  pl.BlockSpec((tm, SSM_D_INNER), lambda i: (zb0 + i, z_col // SSM_D_INNER)),
                  pl.BlockSpec((1, SSM_D_INNER), lambda i: (0, 0))],
        out_specs=pl.BlockSpec((tm, SSM_D_INNER), lambda i: (i, 0)),
        out_shape=jax.ShapeDtypeStruct((n, SSM_D_INNER), BF16),
        compiler_params=_cparams(("arbitrary",)), name="ssm_gated_norm",
    )(y, proj, g.reshape(1, SSM_D_INNER).astype(F32))


def _two_source_specs(block, first_blocks, row_block_of):
    first = pl.BlockSpec(block, lambda *g: (jnp.minimum(row_block_of(*g), first_blocks - 1), 0))
    second = pl.BlockSpec(block, lambda *g: (jnp.maximum(row_block_of(*g) - first_blocks, 0), 0))
    return first, second


def _merge_body(*refs, lat_blocks, has_ctx):
    n_src = 6 if has_ctx else 3
    src, (ga_ref, gb_ref, gc_ref, wa_ref, wb_ref, wc_ref, o_ref, wab, wbb, wcb) = refs[:n_src], refs[n_src:]
    i = pl.program_id(1)

    @pl.when(i == 0)
    def _():
        wab[...] = wa_ref[...].astype(BF16)
        wbb[...] = wb_ref[...].astype(BF16)
        wcb[...] = wc_ref[...].astype(BF16)

    if has_ctx:
        oa, ob, oc = (jnp.where(i < lat_blocks, src[2 * n][...], src[2 * n + 1][...]) for n in range(3))
    else:
        oa, ob, oc = (r[...] for r in src)
    acc = ga_ref[...].astype(F32) * _dot(oa, wab[...])
    acc = acc + gb_ref[...].astype(F32) * _dot(ob, wbb[...])
    acc = acc + gc_ref[...].astype(F32) * _dot(oc, wcb[...])
    o_ref[...] = acc.astype(o_ref.dtype)


def _merge(o_lat, o_ctx, gates, w_a, w_b, w_c, layer, *, tm, tn, n_rows):
    d = w_a.shape[2]
    nj = d // tn
    lat_blocks = o_lat[0].shape[0] // tm
    gate = lambda x: pl.BlockSpec((tm, tn), lambda j, i: (i, x * nj + j))
    wsp = lambda kk: pl.BlockSpec((None, kk, tn), lambda j, i: (layer, 0, j))
    widths = (w_a.shape[1], w_b.shape[1], w_c.shape[1])
    src_specs, src = [], []
    for n, kk in enumerate(widths):
        if o_ctx is None:
            src_specs.append(pl.BlockSpec((tm, kk), lambda j, i: (i, 0)))
            src.append(o_lat[n])
        else:
            src_specs += _two_source_specs((tm, kk), lat_blocks, lambda j, i: i)
            src += [o_lat[n], o_ctx[n]]
    return pl.pallas_call(
        functools.partial(_merge_body, lat_blocks=lat_blocks, has_ctx=o_ctx is not None), grid=(nj, n_rows // tm),
        in_specs=src_specs + [gate(0), gate(1), gate(2)] + [wsp(kk) for kk in widths],
        out_specs=pl.BlockSpec((tm, tn), lambda j, i: (i, j)),
        out_shape=jax.ShapeDtypeStruct((n_rows, d), BF16),
        scratch_shapes=[pltpu.VMEM((kk, tn), BF16) for kk in widths],
        compiler_params=_cparams(("arbitrary", "arbitrary")), name="branch_merge",
    )(*src, gates, gates, gates, w_a, w_b, w_c)


def _route(top_idx, blk):
    n = top_idx.shape[0]
    e_flat = top_idx.reshape(-1)
    onehot = (e_flat[:, None] == jnp.arange(N_EXPERTS, dtype=jnp.int32)[None, :]).astype(jnp.int32)
    counts = jnp.sum(onehot, axis=0)
    rank = jnp.sum((jnp.cumsum(onehot, axis=0) - onehot) * onehot, axis=1)
    padded = (counts + blk - 1) // blk * blk
    pad_end = jnp.cumsum(padded)
    pad_start = pad_end - padded
    dest = jnp.sum(onehot * pad_start[None, :], axis=1) + rank
    n_blocks = -(-(n * TOP_K) // blk) + N_EXPERTS
    cap = n_blocks * blk
    tok_flat = jnp.repeat(jnp.arange(n, dtype=jnp.int32), TOP_K)
    slot_tok = jnp.zeros((cap,), jnp.int32).at[dest].set(tok_flat)
    blk_start = jnp.arange(n_blocks, dtype=jnp.int32) * blk
    block_expert = jnp.clip(jnp.sum((pad_end[None, :] <= blk_start[:, None]).astype(jnp.int32), axis=1),
                            0, N_EXPERTS - 1)
    n_used = (pad_end[-1] // blk).astype(jnp.int32).reshape(1)
    return dest.astype(jnp.int32), slot_tok, block_expert, n_used


GATHER_UNROLL = 8


def _row_copy(src_hbm, row, buf, slot, sem):
    return pltpu.make_async_copy(src_hbm.at[pl.ds(row, 1), :], buf.at[pl.ds(slot, 1), :], sem)


def _gather_body(nv_ref, idx_ref, src_hbm, o_ref, buf, sem, *, rows):
    @pl.when(pl.program_id(0) >= nv_ref[0])
    def _():
        o_ref[...] = jnp.zeros_like(o_ref)

    @pl.when(pl.program_id(0) < nv_ref[0])
    def _():
        def issue(g, c):
            for u in range(GATHER_UNROLL):
                r = g * GATHER_UNROLL + u
                _row_copy(src_hbm, idx_ref[0, r], buf, r, sem).start(priority=u % 2)
            return c

        def drain(g, c):
            for u in range(GATHER_UNROLL):
                r = g * GATHER_UNROLL + u
                _row_copy(src_hbm, idx_ref[0, r], buf, r, sem).wait()
            return c

        lax.fori_loop(0, rows // GATHER_UNROLL, issue, 0)
        lax.fori_loop(0, rows // GATHER_UNROLL, drain, 0)
        o_ref[...] = buf[...].astype(o_ref.dtype)


def _gather_rows(src, idx, n_used, *, rows, out_dtype):
    n, d = src.shape
    nb = idx.shape[0] // rows
    assert rows % GATHER_UNROLL == 0
    blk = lambda i, nv: jnp.minimum(i, nv[0] - 1)
    grid_spec = pltpu.PrefetchScalarGridSpec(
        num_scalar_prefetch=1, grid=(nb,),
        in_specs=[pl.BlockSpec((None, 1, rows), lambda i, nv: (blk(i, nv), 0, 0), memory_space=pltpu.SMEM),
                  pl.BlockSpec(memory_space=pl.ANY)],
        out_specs=pl.BlockSpec((rows, d), lambda i, nv: (i, 0)),
        scratch_shapes=[pltpu.VMEM((rows, d), src.dtype), pltpu.SemaphoreType.DMA])
    return pl.pallas_call(
        functools.partial(_gather_body, rows=rows), grid_spec=grid_spec,
        out_shape=jax.ShapeDtypeStruct((nb * rows, d), out_dtype),
        compiler_params=_cparams(("arbitrary",)), name="moe_dispatch",
    )(n_used, idx.reshape(nb, 1, rows), src)


def _combine_body(dest_ref, h_ref, rt_ref, gate_ref, g_ref, y_hbm, o_ref, buf, sem, *, rows, norm):
    def issue(g, c):
        for u in range(GATHER_UNROLL):
            r = g * GATHER_UNROLL + u
            for k in range(TOP_K):
                _row_copy(y_hbm, dest_ref[0, r * TOP_K + k], buf.at[k], r, sem).start(priority=k)
        return c

    def drain(g, c):
        for u in range(GATHER_UNROLL):
            r = g * GATHER_UNROLL + u
            for k in range(TOP_K):
                _row_copy(y_hbm, dest_ref[0, r * TOP_K + k], buf.at[k], r, sem).wait()
        return c

    lax.fori_loop(0, rows // GATHER_UNROLL, issue, 0)
    lax.fori_loop(0, rows // GATHER_UNROLL, drain, 0)
    rt = rt_ref[...]
    f = buf[0] * rt[:, TOP_K:TOP_K + 1] + buf[1] * rt[:, TOP_K + 1:TOP_K + 2]
    x = h_ref[...] + gate_ref[...] * f
    if norm:
        x = x * lax.rsqrt(jnp.mean(x * x, axis=-1, keepdims=True) + EPS) * g_ref[...]
    o_ref[...] = x


def _combine(h, y_sorted, dest, rt, mod, mod_idx_fn, part, g, *, rows, n_rows, norm):
    d = h.shape[1]
    nb = n_rows // rows
    row = pl.BlockSpec((rows, d), lambda i: (i, 0))
    return pl.pallas_call(
        functools.partial(_combine_body, rows=rows, norm=norm), grid=(nb,),
        in_specs=[pl.BlockSpec((None, 1, rows * TOP_K), lambda i: (i, 0, 0), memory_space=pltpu.SMEM),
                  row, pl.BlockSpec((rows, LANES), lambda i: (i, 0)),
                  pl.BlockSpec((None, 1, d), lambda i: (mod_idx_fn(i), 0, part)),
                  pl.BlockSpec((1, d), lambda i: (0, 0)), pl.BlockSpec(memory_space=pl.ANY)],
        out_specs=row, out_shape=jax.ShapeDtypeStruct((n_rows, d), F32),
        scratch_shapes=[pltpu.VMEM((TOP_K, rows, d), y_sorted.dtype), pltpu.SemaphoreType.DMA],
        compiler_params=_cparams(("arbitrary",)), name="moe_combine",
    )(dest.reshape(nb, 1, rows * TOP_K), h, rt, mod, g.reshape(1, d), y_sorted)


def kernel(x, c, ctx, c_ctx, ada_w, ada_b, norm_mix_g, norm_ffn_g, w_in, na_rpb, da_lambda, da_subln_g,
           ssm_conv_w, ssm_conv_b, ssm_dt_bias, ssm_a_log, ssm_d, ssm_norm_g, w_branch_a, w_branch_b,
           w_branch_c, w_out, ffn_w1, ffn_w3, ffn_w2, moe_router_w, moe_router_b, moe_w1, moe_w3, moe_w2,
           final_norm_g):
    batch, seq, d = x.shape
    ctx_len = ctx.shape[1]
    depth = ada_w.shape[0]
    n_lat, n_ctx = batch * seq, batch * ctx_len
    n_all = n_lat + n_ctx
    tm = next(t for t in ROW_TILES if n_ctx % t == 0 and seq % t == 0)
    n_mod = 16
    assert batch + 1 <= n_mod
    lat_blocks = n_lat // tm
    per_seq = seq // tm
    te = next(t for t in EW_ROW_TILES if n_ctx % t == 0 and seq % t == 0)
    mod_idx_fn = lambda i: jnp.where(i < n_lat // te, i // (seq // te), batch)
    mod_idx_all = jnp.where(jnp.arange(n_all // tm) < lat_blocks, jnp.arange(n_all // tm) // per_seq, batch)
    layer_blocks = lambda nb, idx: jnp.full((nb,), idx, jnp.int32)

    na_w, da_w = NA_HEADS * NA_HEAD_DIM, DA_HEADS * 2 * DA_HEAD_DIM
    conv_dim = SSM_D_INNER + 2 * SSM_GROUPS * SSM_STATE
    col_qa, col_ka, col_va = 0, na_w, 2 * na_w
    col_qb, col_kb, col_vb = 3 * na_w, 3 * na_w + da_w, 3 * na_w + 2 * da_w
    col_z = 3 * na_w + 3 * da_w
    col_x = col_z + SSM_D_INNER
    col_dt = col_x + conv_dim
    col_gate = col_dt + 2 * SSM_HEADS
    n_main = col_x + SSM_D_INNER
    n_tail = conv_dim - SSM_D_INNER + LANES
    assert n_main % TN_WIDE == 0 and col_dt == n_main + n_tail - LANES and 2 * SSM_HEADS <= LANES

    h, h_ctx = x.reshape(n_lat, d), ctx.reshape(n_ctx, d)
    cond =jnp.zeros((n_mod, d), F32).at[:batch].set(jax.nn.silu(c)).at[batch].set(jax.nn.silu(c_ctx))
    cond = cond.astype(BF16)
    rope_tabs = _rope_tables(seq, tm)
    zero_state = jnp.zeros((batch, 2, SSM_STATE, SSM_D_INNER), F32)
    w_in_t = jnp.swapaxes(w_in, 1, 2)

    out = None
    for li in range(depth):
        last = li == depth - 1
        lam_init = 0.8 - 0.6 * math.exp(-0.3 * li)
        n_out = n_lat if last else n_all
        mod = _gmm(cond, [ada_w], layer_blocks(1, li), tm=n_mod, tn=TN_FFN, epi="bias",
                   bias=ada_b[li].reshape(1, 6 * d)).reshape(n_mod, 1, 6 * d)
        u = _norm(h, norm_mix_g[li], tm=te, n_rows=n_all, mod=mod, mod_idx_fn=mod_idx_fn, parts=(0, 1), x2=h_ctx)
        proj = _gmm(u, [w_in_t], layer_blocks(n_all // tm, li), tm=tm, tn=TN_WIDE, n_cols=n_main, w_rows=0)
        tail = _gmm(u, [w_in_t], layer_blocks(n_all // tm, li), tm=tm, tn=n_tail, n_cols=n_tail, w_rows=n_main)
        gates = _gmm(u, [w_in_t], layer_blocks(n_out // tm, li), tm=tm, tn=TN_WIDE, n_rows=n_out, epi="sigmoid",
                     out_dtype=BF16, n_cols=3 * d, w_rows=col_gate)

        o_a = _na(proj, na_rpb[li], batch=batch, seq=seq, ctx_len=ctx_len, q_col=col_qa, k_col=col_ka, v_col=col_va)
        q_r, k_r = _rope(proj, rope_tabs, tm=tm, n_rows=n_all, n_lat_rows=n_lat, seq=seq,
                         q_col=col_qb, k_col=col_kb, width=da_w)
        o_b = _da(q_r, k_r, proj, da_lambda[li].astype(F32), da_subln_g[li], lam_init, batch=batch, seq=seq,
                  ctx_len=ctx_len, v_col=col_vb, tq=DA_Q_TILE, latent=True)
        ssd_args = (ssm_conv_w[li], ssm_conv_b[li], ssm_dt_bias[li], ssm_a_log[li], ssm_d[li])
        ssd_cols = dict(x_col=col_x, dt_col=col_dt, tail_col=n_main)
        y_ctx, s_ctx = _ssd(proj, tail, *ssd_args, zero_state, batch=batch, n_tok=ctx_len, row0=n_lat, **ssd_cols)
        y_lat, _ = _ssd(proj, tail, *ssd_args, s_ctx, batch=batch, n_tok=seq, row0=0, **ssd_cols)
        o_c = _ssm_norm(y_lat, proj, ssm_norm_g[li], tm=te, row0=0, z_col=col_z)
        if not last:
            o_a_c = _ctx_attn(proj, batch=batch, seq=seq, ctx_len=ctx_len, q_col=col_qa, k_col=col_ka, v_col=col_va)
            o_b_c = _da(q_r, k_r, proj, da_lambda[li].astype(F32), da_subln_g[li], lam_init, batch=batch, seq=seq,
                        ctx_len=ctx_len, v_col=col_vb, tq=ctx_len, latent=False)
            o_c_c = _ssm_norm(y_ctx, proj, ssm_norm_g[li], tm=te, row0=n_lat, z_col=col_z)
        merged = _merge((o_a, o_b, o_c), None if last else (o_a_c, o_b_c, o_c_c), gates,
                        w_branch_a, w_branch_b, w_branch_c, li, tm=te, tn=TN_WIDE, n_rows=n_out)
        h = _gmm(merged, [w_out], layer_blocks(n_out // tm, li), tm=tm, tn=TN_WIDE, epi="resid", res=h, res2=h_ctx,
                 mod=mod, mod_idx=mod_idx_all[:n_out // tm], mod_part=2, n_rows=n_out)
        h_ctx = None

        j = li // 2
        if li % 2 == 0:
            tokens = _norm(h, norm_ffn_g[li], tm=te, n_rows=n_out, mod=mod, mod_idx_fn=mod_idx_fn, parts=(3, 4))
            hid = _gmm(tokens, [ffn_w1, ffn_w3], layer_blocks(n_out // tm, j), tm=tm, tn=TN_FFN,
                       epi="swiglu", out_dtype=BF16)
            h = _gmm(hid, [ffn_w2], layer_blocks(n_out // tm, j), tm=tm, tn=TN_FFN, epi="resid", res=h,
                     mod=mod, mod_idx=mod_idx_all[:n_out // tm], mod_part=5, n_rows=n_out, w_buffers=1)
            if last:
                out = _norm(h, final_norm_g, tm=te, n_rows=n_lat, out_dtype=F32)
        else:
            rw = jnp.zeros((d, LANES), F32).at[:, :N_EXPERTS].set(moe_router_w[j].astype(F32))
            rb = jnp.zeros((1, LANES), F32).at[0, :N_EXPERTS].set(moe_router_b[j].astype(F32))
            tokens, rt = _norm(h, norm_ffn_g[li], tm=te, n_rows=n_out, mod=mod, mod_idx_fn=mod_idx_fn,
                               parts=(3, 4), router=(rw, rb), out_dtype=F32)
            dest, slot_tok, block_expert, n_used = _route(rt[:, :TOP_K].astype(jnp.int32), MOE_ROWS)
            block_expert = block_expert + j * N_EXPERTS
            stack = lambda w: w.reshape((-1,) + w.shape[2:])
            x_sorted = _gather_rows(tokens, slot_tok, n_used, rows=MOE_ROWS, out_dtype=BF16)
            hid = _gmm(x_sorted, [stack(moe_w1), stack(moe_w3)], block_expert, tm=MOE_ROWS, tn=TN_FFN, epi="swiglu",
                       out_dtype=BF16, n_used=n_used)
            y_sorted = _gmm(hid, [stack(moe_w2)], block_expert, tm=MOE_ROWS, tn=TN_FFN, n_used=n_used, w_buffers=1)
            h = _combine(h, y_sorted, dest, rt, mod, mod_idx_fn, 5, final_norm_g, rows=te, n_rows=n_out, norm=last)
            if last:
                out = h
    return out.reshape(batch, seq, d)
```
